```python
import math
import jax
import jax.numpy as jnp
from jax import lax
import numpy as np

D_MODEL = 2048
BATCH = 4
SEQ = 2048
DEPTH = 4
DEC_BATCH = 8
DEC_SEQ = 8
PAST_LEN = 16384
PAGE_SIZE = 128

F32 = jnp.float32
N_MIXERS = 4
RMS_EPS = 1e-6
ROPE_THETA = 10000.0
NEG_BIG = -1e30
CHUNK = 64

GLA_HEADS = 4
GLA_DK = D_MODEL // 2 // GLA_HEADS
GLA_DV = D_MODEL // GLA_HEADS
GLA_RANK = 16
GLA_TEMP = 16.0

HGRN_DK = 128
HGRN_HEADS = D_MODEL // HGRN_DK

NSA_HEADS = 16
NSA_DH = D_MODEL // NSA_HEADS
NSA_KV = 4
NSA_GROUP = NSA_HEADS // NSA_KV
NSA_BLOCK = 64
NSA_TOP_N = 16
NSA_WINDOW = 512
NSA_SEL_QBLOCK = 32
NSA_WIN_QBLOCK = 128

S5_GROUP_CH = 16
S5_GROUPS = D_MODEL // S5_GROUP_CH
S5_STATE = 64

D_FF = 5632
CONV_W = 3

N_GLA = (DEPTH + 3) // N_MIXERS
N_HGRN = (DEPTH + 2) // N_MIXERS
N_NSA = (DEPTH + 1) // N_MIXERS
N_S5 = DEPTH // N_MIXERS

kernel_name = 'hybrid_gla_hgrn2_nsa_s5_decode_step'


def rmsnorm(x, g):
    xf = x.astype(F32)
    y = xf * lax.rsqrt(jnp.mean(xf * xf, axis=-1, keepdims=True) + RMS_EPS) * g.astype(F32)
    return y.astype(x.dtype)


def gated_head_norm(o, gain, gate):
    b, t = o.shape[0], o.shape[1]
    of = o.astype(F32)
    of = of * lax.rsqrt(jnp.mean(of * of, axis=-1, keepdims=True) + RMS_EPS) * gain.astype(F32)
    return of.reshape(b, t, -1) * jax.nn.silu(gate.astype(F32))


def rope(x, pos):
    half = x.shape[-1] // 2
    inv = ROPE_THETA ** (-jnp.arange(half, dtype=F32) / half)
    ang = pos.astype(F32)[:, None] * inv[None, :]
    cos, sin = jnp.cos(ang)[None, :, None, :], jnp.sin(ang)[None, :, None, :]
    xf = x.astype(F32)
    x1, x2 = xf[..., :half], xf[..., half:]
    return jnp.concatenate([x1 * cos - x2 * sin, x2 * cos + x1 * sin], axis=-1).astype(x.dtype)


def masked_softmax(s, mask):
    s = jnp.where(mask, s, NEG_BIG)
    m = jnp.max(s, axis=-1, keepdims=True)
    e = jnp.where(mask, jnp.exp(s - m), 0.0)
    return e / jnp.maximum(jnp.sum(e, axis=-1, keepdims=True), 1e-30)


def gated_linear_recurrence(q, k, v, log_g, s0):
    b, h, t, dk = q.shape
    dv = v.shape[-1]
    c = math.gcd(t, CHUNK)
    n = t // c

    def chunks(a):
        return jnp.moveaxis(a.astype(F32).reshape(b, h, n, c, a.shape[-1]), 2, 0)

    causal = jnp.tril(jnp.ones((c, c), dtype=bool))[None, None, :, :, None]

    def step(state, blk):
        qc, kc, vc, gc = blk
        cum = jnp.cumsum(gc, axis=2)
        inter = jnp.einsum('bhtk,bhkv->bhtv', qc * jnp.exp(cum), state)
        rel = jnp.where(causal, cum[:, :, :, None, :] - cum[:, :, None, :, :], -jnp.inf)
        att = jnp.einsum('bhtk,bhsk,bhtsk->bhts', qc, kc, jnp.exp(rel))
        intra = jnp.einsum('bhts,bhsv->bhtv', att, vc)
        last = cum[:, :, -1:, :]
        k_to_end = kc * jnp.exp(last - cum)
        state = jnp.exp(last[:, :, 0, :])[..., None] * state + jnp.einsum('bhsk,bhsv->bhkv', k_to_end, vc)
        return state, inter + intra

    s_fin, o = lax.scan(step, s0.astype(F32), (chunks(q), chunks(k), chunks(v), chunks(log_g)))
    o = jnp.moveaxis(o, 0, 2).reshape(b, h, t, dv)
    return o, s_fin


def to_heads(a, heads, d):
    b, t = a.shape[0], a.shape[1]
    return a.reshape(b, t, heads, d).transpose(0, 2, 1, 3)


def gla_mixer(h, w_in, w_alpha, b_alpha, head_norm, w_out, s0):
    b, t, _ = h.shape
    dk_tot, dv_tot = GLA_HEADS * GLA_DK, GLA_HEADS * GLA_DV
    q, k, v, r, lr = jnp.split(h @ w_in, [dk_tot, 2 * dk_tot, 2 * dk_tot + dv_tot, 2 * dk_tot + 2 * dv_tot], axis=-1)
    log_alpha = jax.nn.log_sigmoid((lr @ w_alpha + b_alpha).astype(F32)) / GLA_TEMP
    o, s_new = gated_linear_recurrence(to_heads(q, GLA_HEADS, GLA_DK) * GLA_DK ** -0.5,
                                       to_heads(k, GLA_HEADS, GLA_DK),
                                       to_heads(v, GLA_HEADS, GLA_DV),
                                       to_heads(log_alpha, GLA_HEADS, GLA_DK), s0)
    o = gated_head_norm(o.transpose(0, 2, 1, 3), head_norm, r)
    return o.astype(h.dtype) @ w_out, s_new


def hgrn2_mixer(h, w_in, lower_bound, head_norm, w_out, s0):
    q, fz, i_in, g = jnp.split(h @ w_in, 4, axis=-1)
    f = lower_bound + (1.0 - lower_bound) * jax.nn.sigmoid(fz.astype(F32))
    o, s_new = gated_linear_recurrence(to_heads(jax.nn.silu(q), HGRN_HEADS, HGRN_DK),
                                       to_heads(1.0 - f, HGRN_HEADS, HGRN_DK),
                                       to_heads(i_in, HGRN_HEADS, HGRN_DK),
                                       to_heads(jnp.log(f), HGRN_HEADS, HGRN_DK), s0)
    o = gated_head_norm(o.transpose(0, 2, 1, 3), head_norm, g)
    return o.astype(h.dtype) @ w_out, s_new


def nsa_mixer(h, start, w_in, pool_w_k, pool_w_v, w_out, past):
    b, t, _ = h.shape
    qw, kvw = NSA_HEADS * NSA_DH, NSA_KV * NSA_DH
    cuts = [qw + m * kvw for m in range(7)]
    q, kc, vc, ks, vs, kw, vw, gates = jnp.split(h @ w_in, cuts, axis=-1)
    qpos = start + jnp.arange(t, dtype=jnp.int32)
    q = q.reshape(b, t, NSA_HEADS, NSA_DH)
    kc, vc, ks, vs, kw, vw = (a.reshape(b, t, NSA_KV, NSA_DH) for a in (kc, vc, ks, vs, kw, vw))
    ks, kw = rope(ks, qpos), rope(kw, qpos)
    q_rot = rope(q, qpos).reshape(b, t, NSA_KV, NSA_GROUP, NSA_DH)
    q_cmp = q.reshape(b, t, NSA_KV, NSA_GROUP, NSA_DH)
    scale = NSA_DH ** -0.5
    bi = jnp.arange(b)[:, None, None, None]
    gi = jnp.arange(NSA_KV)[None, :, None, None]
    ks_t, vs_t = ks.transpose(0, 2, 1, 3), vs.transpose(0, 2, 1, 3)

    if past is None:
        kc_all, vc_all = kc, vc
        prev_kw = jnp.zeros((b, 0, NSA_KV, NSA_DH), kw.dtype)
        prev_vw = jnp.zeros((b, 0, NSA_KV, NSA_DH), vw.dtype)
        keep = min(NSA_WINDOW, t)

        def fetch(kpos):
            return ks_t[bi, gi, kpos], vs_t[bi, gi, kpos]
    else:
        pool_ck, pool_cv, pool_sk, pool_sv, page_table, prev_kw, prev_vw = past
        past_len = page_table.shape[1] * PAGE_SIZE

        def gather_pages(pool):
            return pool[page_table].reshape(b, past_len, NSA_KV, NSA_DH)

        kc_all = jnp.concatenate([gather_pages(pool_ck).astype(kc.dtype), kc], axis=1)
        vc_all = jnp.concatenate([gather_pages(pool_cv).astype(vc.dtype), vc], axis=1)
        keep = prev_kw.shape[1]

        def fetch(kpos):
            in_past = (kpos < past_len)[..., None]
            pc = jnp.minimum(kpos, past_len - 1)
            phys = page_table[bi, pc // PAGE_SIZE]
            off = pc % PAGE_SIZE
            nc = jnp.clip(kpos - past_len, 0, t - 1)
            kg = jnp.where(in_past, pool_sk[phys, off, gi].astype(ks.dtype), ks_t[bi, gi, nc])
            vg = jnp.where(in_past, pool_sv[phys, off, gi].astype(vs.dtype), vs_t[bi, gi, nc])
            return kg, vg

    total = kc_all.shape[1]
    n_cb = total // NSA_BLOCK
    k_cmp = jnp.einsum('bnlgd,l->bngd', kc_all[:, :n_cb * NSA_BLOCK].reshape(b, n_cb, NSA_BLOCK, NSA_KV, NSA_DH), pool_w_k)
    v_cmp = jnp.einsum('bnlgd,l->bngd', vc_all[:, :n_cb * NSA_BLOCK].reshape(b, n_cb, NSA_BLOCK, NSA_KV, NSA_DH), pool_w_v)
    blk_end = (jnp.arange(n_cb) + 1) * NSA_BLOCK - 1
    s_cmp = jnp.einsum('btgrd,bngd->bgrtn', q_cmp, k_cmp).astype(F32) * scale
    p_cmp = masked_softmax(s_cmp, blk_end[None, :] <= qpos[:, None])
    o_cmp = jnp.einsum('bgrtn,bngd->btgrd', p_cmp.astype(v_cmp.dtype), v_cmp)

    n_blk = -(-total // NSA_BLOCK)
    imp = jnp.pad(jnp.sum(p_cmp, axis=2), ((0, 0), (0, 0), (0, 0), (0, n_blk - n_cb)))
    blk = jnp.arange(n_blk)[None, :]
    cur = (qpos // NSA_BLOCK)[:, None]
    forced = (blk == cur) | (blk == 0)
    score = jnp.where(blk > cur, -1.0, jnp.where(forced, NSA_GROUP + 1.0, imp))
    n_sel = min(NSA_TOP_N, n_blk)
    _, sel = lax.top_k(score, n_sel)

    qb = math.gcd(t, NSA_SEL_QBLOCK)
    nqb = t // qb
    offs = jnp.arange(NSA_BLOCK)

    def sel_block(blk_in):
        qq, ii, pp = blk_in
        kpos = (ii[..., None] * NSA_BLOCK + offs).reshape(b, NSA_KV, qb, n_sel * NSA_BLOCK)
        kg, vg = fetch(kpos)
        s = jnp.einsum('bqgrd,bgqsd->bgrqs', qq, kg).astype(F32) * scale
        p = masked_softmax(s, (kpos <= pp[None, None, :, None])[:, :, None])
        return jnp.einsum('bgrqs,bgqsd->bqgrd', p.astype(vg.dtype), vg)

    o_sel = lax.map(sel_block, (jnp.moveaxis(q_rot.reshape(b, nqb, qb, NSA_KV, NSA_GROUP, NSA_DH), 1, 0),
                                jnp.moveaxis(sel.reshape(b, NSA_KV, nqb, qb, n_sel), 2, 0),
                                qpos.reshape(nqb, qb)))
    o_sel = jnp.moveaxis(o_sel, 0, 1).reshape(b, t, NSA_KV, NSA_GROUP, NSA_DH)

    pad = NSA_WINDOW - prev_kw.shape[1]
    zpad = jnp.zeros((b, pad, NSA_KV, NSA_DH), kw.dtype)
    kw_ext = jnp.concatenate([zpad, prev_kw.astype(kw.dtype), kw], axis=1)
    vw_ext = jnp.concatenate([zpad.astype(vw.dtype), prev_vw.astype(vw.dtype), vw], axis=1)
    kpos_ext = start - NSA_WINDOW + jnp.arange(NSA_WINDOW + t)
    qbw = math.gcd(t, NSA_WIN_QBLOCK)
    nqw = t // qbw
    band = jnp.arange(nqw)[:, None] * qbw + jnp.arange(NSA_WINDOW + qbw)[None, :]
    kb, vb = kw_ext[:, band], vw_ext[:, band]
    kpb = kpos_ext[band]
    dist = qpos.reshape(nqw, qbw)[:, :, None] - kpb[:, None, :]
    mask_w = (dist >= 0) & (dist < NSA_WINDOW) & (kpb[:, None, :] >= 0)
    s_w = jnp.einsum('bnqgrd,bnkgd->bngrqk', q_rot.reshape(b, nqw, qbw, NSA_KV, NSA_GROUP, NSA_DH), kb).astype(F32) * scale
    p_w = masked_softmax(s_w, mask_w[None, :, None, None])
    o_win = jnp.einsum('bngrqk,bnkgd->bnqgrd', p_w.astype(vb.dtype), vb).reshape(b, t, NSA_KV, NSA_GROUP, NSA_DH)

    g = jax.nn.sigmoid(gates.astype(F32)).reshape(b, t, NSA_KV, NSA_GROUP, 3)
    o = g[..., 0:1] * o_cmp + g[..., 1:2] * o_sel + g[..., 2:3] * o_win
    y = o.reshape(b, t, qw).astype(h.dtype) @ w_out
    return y, (kc, vc, ks, vs, kw_ext[:, -keep:], vw_ext[:, -keep:])


def _ssm_combine(e1, e2):
    a1, b1 = e1
    a2, b2 = e2
    return a2 * a1, a2 * b1 + b2


def s5_mixer(h, a_re, a_im, log_dt, b_re, b_im, c_re, c_im, d_skip, w_glu, s_re, s_im):
    b, t, _ = h.shape
    u = h.astype(F32).reshape(b, t, S5_GROUPS, S5_GROUP_CH)
    a = lax.complex(a_re.astype(F32), a_im.astype(F32))
    dt = jnp.exp(log_dt.astype(F32))[:, None]
    a_bar = jnp.exp(a * dt)
    b_bar = ((a_bar - 1.0) / a)[..., None] * lax.complex(b_re.astype(F32), b_im.astype(F32))
    c_mat = lax.complex(c_re.astype(F32), c_im.astype(F32))
    bu = jnp.einsum('btgc,gpc->btgp', u.astype(jnp.complex64), b_bar)
    a_seq = jnp.broadcast_to(a_bar, (1, t) + a_bar.shape)
    a_cum, states = lax.associative_scan(_ssm_combine, (a_seq, bu), axis=1)
    states = states + a_cum * lax.complex(s_re.astype(F32), s_im.astype(F32))[:, None]
    y = jnp.einsum('gcp,btgp->btgc', c_mat, states).real + d_skip.astype(F32).reshape(S5_GROUPS, S5_GROUP_CH) * u
    z = jax.nn.gelu(y.reshape(b, t, D_MODEL)).astype(h.dtype)
    gl = z @ w_glu
    y_out = gl[..., :D_MODEL] * jax.nn.sigmoid(gl[..., D_MODEL:])
    last = states[:, -1]
    return y_out, (last.real, last.imag)


def conv_ffn(h, w_in, conv_w, conv_b, w_out, buf):
    t = h.shape[1]
    up = h @ w_in
    ext = jnp.concatenate([buf.astype(up.dtype), up], axis=1)
    mixed = sum(conv_w[j] * ext[:, j:j + t] for j in range(CONV_W)) + conv_b
    gate, val = jnp.split(mixed, 2, axis=-1)
    return (jax.nn.silu(gate) * val) @ w_out, ext[:, t:]


def setup_inputs(seed: int = 0) -> dict:
    key = jax.random.key(seed)
    kit = iter(jax.random.split(key, 64))

    def nrm(shape, scale):
        return jax.random.normal(next(kit), shape, F32) * scale

    def gain(shape):
        return 1.0 + nrm(shape, 0.02)

    n_pages = PAST_LEN // PAGE_SIZE
    n_used = DEC_BATCH * n_pages
    n_phys = n_used + max(1, n_used // 4)
    w_buf = min(NSA_WINDOW, PAST_LEN)
    gla_in = 2 * GLA_HEADS * GLA_DK + 2 * GLA_HEADS * GLA_DV + GLA_RANK
    nsa_in = NSA_HEADS * NSA_DH + 6 * NSA_KV * NSA_DH + 3 * NSA_HEADS
    hgrn_w = HGRN_HEADS * HGRN_DK
    pool_shape = (N_NSA, n_phys, PAGE_SIZE, NSA_KV, NSA_DH)
    win_shape = (N_NSA, DEC_BATCH, w_buf, NSA_KV, NSA_DH)
    s5_shape = (N_S5, DEC_BATCH, S5_GROUPS, S5_STATE)
    page_table = jax.random.permutation(next(kit), n_phys)[:n_used].reshape(DEC_BATCH, n_pages).astype(jnp.int32)
    return {
        'x_prompt': nrm((BATCH, SEQ, D_MODEL), 1.0),
        'x_sample': nrm((DEC_BATCH, DEC_SEQ, D_MODEL), 1.0),
        'state_gla': nrm((N_GLA, DEC_BATCH, GLA_HEADS, GLA_DK, GLA_DV), 1.0),
        'state_hgrn': nrm((N_HGRN, DEC_BATCH, HGRN_HEADS, HGRN_DK, HGRN_DK), 1.0),
        'cache_nsa_cmp_k': nrm(pool_shape, 1.0),
        'cache_nsa_cmp_v': nrm(pool_shape, 1.0),
        'cache_nsa_sel_k': nrm(pool_shape, 1.0),
        'cache_nsa_sel_v': nrm(pool_shape, 1.0),
        'cache_nsa_win_k': nrm(win_shape, 1.0),
        'cache_nsa_win_v': nrm(win_shape, 1.0),
        'state_s5_re': nrm(s5_shape, 0.5),
        'state_s5_im': nrm(s5_shape, 0.5),
        'state_ffn_conv': nrm((DEPTH, DEC_BATCH, CONV_W - 1, 2 * D_FF), 1.0),
        'page_table': page_table,
        'norm_mix': gain((DEPTH, D_MODEL)),
        'norm_ffn': gain((DEPTH, D_MODEL)),
        'final_norm': gain((D_MODEL,)),
        'gla_w_in': nrm((N_GLA, D_MODEL, gla_in), D_MODEL ** -0.5),
        'gla_w_alpha': nrm((N_GLA, GLA_RANK, GLA_HEADS * GLA_DK), GLA_RANK ** -0.5),
        'gla_b_alpha': nrm((N_GLA, GLA_HEADS * GLA_DK), 0.1),
        'gla_head_norm': gain((N_GLA, GLA_DV)),
        'gla_w_out': nrm((N_GLA, GLA_HEADS * GLA_DV, D_MODEL), (GLA_HEADS * GLA_DV) ** -0.5),
        'hgrn_w_in': nrm((N_HGRN, D_MODEL, 4 * hgrn_w), D_MODEL ** -0.5),
        'hgrn_lower_bound': nrm((DEPTH, hgrn_w), 0.1),
        'hgrn_head_norm': gain((N_HGRN, HGRN_DK)),
        'hgrn_w_out': nrm((N_HGRN, hgrn_w, D_MODEL), hgrn_w ** -0.5),
        'nsa_w_in': nrm((N_NSA, D_MODEL, nsa_in), D_MODEL ** -0.5),
        'nsa_pool_k': NSA_BLOCK ** -0.5 * (1.0 + nrm((N_NSA, NSA_BLOCK), 0.1)),
        'nsa_pool_v': NSA_BLOCK ** -0.5 * (1.0 + nrm((N_NSA, NSA_BLOCK), 0.1)),
        'nsa_w_out': nrm((N_NSA, NSA_HEADS * NSA_DH, D_MODEL), (NSA_HEADS * NSA_DH) ** -0.5),
        's5_a_re': -0.5 + nrm((N_S5, S5_GROUPS, S5_STATE), 0.01),
        's5_a_im': jnp.pi * jnp.arange(S5_STATE, dtype=F32) + nrm((N_S5, S5_GROUPS, S5_STATE), 0.01),
        's5_log_dt': jax.random.uniform(next(kit), (N_S5, S5_GROUPS), F32, math.log(1e-3), math.log(1e-1)),
        's5_b_re': nrm((N_S5, S5_GROUPS, S5_STATE, S5_GROUP_CH), (0.5 / S5_GROUP_CH) ** 0.5),
        's5_b_im': nrm((N_S5, S5_GROUPS, S5_STATE, S5_GROUP_CH), (0.5 / S5_GROUP_CH) ** 0.5),
        's5_c_re': nrm((N_S5, S5_GROUPS, S5_GROUP_CH, S5_STATE), (0.5 / S5_STATE) ** 0.5),
        's5_c_im': nrm((N_S5, S5_GROUPS, S5_GROUP_CH, S5_STATE), (0.5 / S5_STATE) ** 0.5),
        's5_d': nrm((N_S5, D_MODEL), 1.0),
        's5_w_glu': nrm((N_S5, D_MODEL, 2 * D_MODEL), D_MODEL ** -0.5),
        'ffn_w_in': nrm((DEPTH, D_MODEL, 2 * D_FF), D_MODEL ** -0.5),
        'ffn_conv_w': nrm((DEPTH, CONV_W, 2 * D_FF), CONV_W ** -0.5),
        'ffn_conv_b': nrm((DEPTH, 2 * D_FF), 0.02),
        'ffn_w_out': nrm((DEPTH, D_FF, D_MODEL), D_FF ** -0.5),
    }


def reference(x_prompt, x_sample, state_gla, state_hgrn, cache_nsa_cmp_k, cache_nsa_cmp_v,
              cache_nsa_sel_k, cache_nsa_sel_v, cache_nsa_win_k, cache_nsa_win_v,
              state_s5_re, state_s5_im, state_ffn_conv, page_table,
              norm_mix, norm_ffn, final_norm,
              gla_w_in, gla_w_alpha, gla_b_alpha, gla_head_norm, gla_w_out,
              hgrn_w_in, hgrn_lower_bound, hgrn_head_norm, hgrn_w_out,
              nsa_w_in, nsa_pool_k, nsa_pool_v, nsa_w_out,
              s5_a_re, s5_a_im, s5_log_dt, s5_b_re, s5_b_im, s5_c_re, s5_c_im, s5_d, s5_w_glu,
              ffn_w_in, ffn_conv_w, ffn_conv_b, ffn_w_out):
    bp, bs = x_prompt.shape[0], x_sample.shape[0]
    lb_cum = jnp.cumsum(jax.nn.softmax(hgrn_lower_bound.astype(F32), axis=0), axis=0)
    lower_bounds = lb_cum - lb_cum[:1]
    hp, hs = x_prompt, x_sample
    gla_p, gla_s, hgrn_p, hgrn_s, nsa_p, nsa_s, s5_p, s5_s, conv_p, conv_s = ([] for _ in range(10))
    for i in range(DEPTH):
        kind, j = i % N_MIXERS, i // N_MIXERS
        up, us = rmsnorm(hp, norm_mix[i]), rmsnorm(hs, norm_mix[i])
        if kind == 0:
            w = (gla_w_in[j], gla_w_alpha[j], gla_b_alpha[j], gla_head_norm[j], gla_w_out[j])
            yp, st_p = gla_mixer(up, *w, jnp.zeros((bp, GLA_HEADS, GLA_DK, GLA_DV), F32))
            ys, st_s = gla_mixer(us, *w, state_gla[j])
            gla_p.append(st_p)
            gla_s.append(st_s)
        elif kind == 1:
            w = (hgrn_w_in[j], lower_bounds[i], hgrn_head_norm[j], hgrn_w_out[j])
            yp, st_p = hgrn2_mixer(up, *w, jnp.zeros((bp, HGRN_HEADS, HGRN_DK, HGRN_DK), F32))
            ys, st_s = hgrn2_mixer(us, *w, state_hgrn[j])
            hgrn_p.append(st_p)
            hgrn_s.append(st_s)
        elif kind == 2:
            w = (nsa_w_in[j], nsa_pool_k[j], nsa_pool_v[j], nsa_w_out[j])
            yp, st_p = nsa_mixer(up, 0, *w, None)
            ys, st_s = nsa_mixer(us, PAST_LEN, *w, (cache_nsa_cmp_k[j], cache_nsa_cmp_v[j], cache_nsa_sel_k[j],
                                                   cache_nsa_sel_v[j], page_table, cache_nsa_win_k[j], cache_nsa_win_v[j]))
            nsa_p.append(st_p)
            nsa_s.append(st_s)
        else:
            w = (s5_a_re[j], s5_a_im[j], s5_log_dt[j], s5_b_re[j], s5_b_im[j], s5_c_re[j], s5_c_im[j], s5_d[j], s5_w_glu[j])
            z0 = jnp.zeros((bp, S5_GROUPS, S5_STATE), F32)
            yp, st_p = s5_mixer(up, *w, z0, z0)
            ys, st_s = s5_mixer(us, *w, state_s5_re[j], state_s5_im[j])
            s5_p.append(st_p)
            s5_s.append(st_s)
        hp = hp + yp.astype(hp.dtype)
        hs = hs + ys.astype(hs.dtype)
        fw = (ffn_w_in[i], ffn_conv_w[i], ffn_conv_b[i], ffn_w_out[i])
        fp, cb_p = conv_ffn(rmsnorm(hp, norm_ffn[i]), *fw, jnp.zeros((bp, CONV_W - 1, 2 * D_FF), hp.dtype))
        fs, cb_s = conv_ffn(rmsnorm(hs, norm_ffn[i]), *fw, state_ffn_conv[i])
        hp = hp + fp.astype(hp.dtype)
        hs = hs + fs.astype(hs.dtype)
        conv_p.append(cb_p)
        conv_s.append(cb_s)
    y_prompt = rmsnorm(hp, final_norm)
    y_sample = rmsnorm(hs, final_norm)
    gla_state_p, gla_state_s = jnp.stack(gla_p), jnp.stack(gla_s)
    hgrn_state_p, hgrn_state_s = jnp.stack(hgrn_p), jnp.stack(hgrn_s)
    cmp_k_p, cmp_k_s = jnp.stack([r[0] for r in nsa_p]), jnp.stack([r[0] for r in nsa_s])
    cmp_v_p, cmp_v_s = jnp.stack([r[1] for r in nsa_p]), jnp.stack([r[1] for r in nsa_s])
    sel_k_p, sel_k_s = jnp.stack([r[2] for r in nsa_p]), jnp.stack([r[2] for r in nsa_s])
    sel_v_p, sel_v_s = jnp.stack([r[3] for r in nsa_p]), jnp.stack([r[3] for r in nsa_s])
    win_k_p, win_k_s = jnp.stack([r[4] for r in nsa_p]), jnp.stack([r[4] for r in nsa_s])
    win_v_p, win_v_s = jnp.stack([r[5] for r in nsa_p]), jnp.stack([r[5] for r in nsa_s])
    s5_re_p, s5_re_s = jnp.stack([r[0] for r in s5_p]), jnp.stack([r[0] for r in s5_s])
    s5_im_p, s5_im_s = jnp.stack([r[1] for r in s5_p]), jnp.stack([r[1] for r in s5_s])
    conv_state_p, conv_state_s = jnp.stack(conv_p), jnp.stack(conv_s)
    return (y_prompt, y_sample, gla_state_p, gla_state_s, hgrn_state_p, hgrn_state_s,
            cmp_k_p, cmp_k_s, cmp_v_p, cmp_v_s, sel_k_p, sel_k_s, sel_v_p, sel_v_s,
            win_k_p, win_k_s, win_v_p, win_v_s, s5_re_p, s5_re_s, s5_im_p, s5_im_s,
            conv_state_p, conv_state_s)
```

```python
import functools
import math

import jax
import jax.numpy as jnp
from jax import lax
from jax.experimental import pallas as pl
from jax.experimental.pallas import tpu as pltpu

F32 = jnp.float32
BF16 = jnp.bfloat16
HIGHEST = lax.Precision.HIGHEST

RMS_EPS = 1e-6
ROPE_THETA = 10000.0
NEG = -1e30
CHUNK = 64
SUBCHUNK = 16
GLA_TEMP = 16.0
NSA_BLOCK = 64
NSA_TOP_N = 16
NSA_WINDOW = 512
NSA_GROUP = 4
NSA_DH = 128
S5_CH = 16
S5_STATE = 64
CONV_W = 3
LANES = 128
SUBLANES = 8
VMEM_LIMIT = 48 * 1024 * 1024


def _cp(*sem):
    return pltpu.CompilerParams(dimension_semantics=sem, vmem_limit_bytes=VMEM_LIMIT)


def _mm(a, b):
    return jnp.dot(a.astype(BF16), b.astype(BF16), preferred_element_type=F32)


def _mm_nt(a, b):
    return lax.dot_general(a.astype(BF16), b.astype(BF16), (((1,), (1,)), ((), ())),
                           preferred_element_type=F32)


def _sigmoid(x):
    return 1.0 / (1.0 + jnp.exp(-x))


def _rms(x, g):
    return x * lax.rsqrt(jnp.mean(x * x, axis=-1, keepdims=True) + RMS_EPS) * g


def _norm_body(x_ref, g_ref, o_ref):
    o_ref[...] = _rms(x_ref[...], g_ref[...])


def rmsnorm_rows(x, gain, tm):
    m, d = x.shape
    return pl.pallas_call(
        _norm_body, grid=(m // tm,),
        in_specs=[pl.BlockSpec((tm, d), lambda i: (i, 0)), pl.BlockSpec((1, d), lambda i: (0, 0))],
        out_specs=pl.BlockSpec((tm, d), lambda i: (i, 0)),
        out_shape=jax.ShapeDtypeStruct((m, d), F32),
        compiler_params=_cp("arbitrary"), name="rmsnorm")(x, gain.reshape(1, d))


def _proj_body(x_ref, g_ref, w_ref, o_ref, xn_ref):
    @pl.when(pl.program_id(1) == 0)
    def _():
        xn_ref[...] = _rms(x_ref[...], g_ref[...]).astype(BF16)
    o_ref[...] = jnp.dot(xn_ref[...], w_ref[...], preferred_element_type=F32)


def norm_proj(x, gain, w, n_out, tm, tn):
    m, d = x.shape
    return pl.pallas_call(
        _proj_body, grid=(m // tm, n_out // tn),
        in_specs=[pl.BlockSpec((tm, d), lambda i, j: (i, 0)),
                  pl.BlockSpec((1, d), lambda i, j: (0, 0)),
                  pl.BlockSpec((d, tn), lambda i, j: (0, j))],
        out_specs=pl.BlockSpec((tm, tn), lambda i, j: (i, j)),
        out_shape=jax.ShapeDtypeStruct((m, n_out), F32),
        scratch_shapes=[pltpu.VMEM((tm, d), BF16)],
        compiler_params=_cp("arbitrary", "arbitrary"), name="norm_proj")(x, gain.reshape(1, d), w)


def _out_body(a_ref, w_ref, r_ref, o_ref):
    o_ref[...] = r_ref[...] + jnp.dot(a_ref[...].astype(BF16), w_ref[...], preferred_element_type=F32)


def out_proj(a, w, res, tm, tn):
    m, k = a.shape
    n = w.shape[1]
    return pl.pallas_call(
        _out_body, grid=(m // tm, n // tn),
        in_specs=[pl.BlockSpec((tm, k), lambda i, j: (i, 0)),
                  pl.BlockSpec((k, tn), lambda i, j: (0, j)),
                  pl.BlockSpec((tm, tn), lambda i, j: (i, j))],
        out_specs=pl.BlockSpec((tm, tn), lambda i, j: (i, j)),
        out_shape=jax.ShapeDtypeStruct((m, n), F32),
        compiler_params=_cp("arbitrary", "arbitrary"), name="out_proj")(a, w, res)


def _out_glu_body(a_ref, w1_ref, w2_ref, r_ref, o_ref):
    a = a_ref[...]
    g1 = jnp.dot(a, w1_ref[...], preferred_element_type=F32)
    g2 = jnp.dot(a, w2_ref[...], preferred_element_type=F32)
    o_ref[...] = r_ref[...] + g1 * _sigmoid(g2)


def out_glu(a, w, res, tm, tn):
    m, k = a.shape
    n = w.shape[1] // 2
    nj = n // tn
    return pl.pallas_call(
        _out_glu_body, grid=(m // tm, nj),
        in_specs=[pl.BlockSpec((tm, k), lambda i, j: (i, 0)),
                  pl.BlockSpec((k, tn), lambda i, j: (0, j)),
                  pl.BlockSpec((k, tn), lambda i, j: (0, nj + j)),
                  pl.BlockSpec((tm, tn), lambda i, j: (i, j))],
        out_specs=pl.BlockSpec((tm, tn), lambda i, j: (i, j)),
        out_shape=jax.ShapeDtypeStruct((m, n), F32),
        compiler_params=_cp("arbitrary", "arbitrary"), name="out_glu")(a, w, w, res)


def _ffn_in_body(*refs, seg, tiles_per_seq, tail_rows, has_state):
    if has_state:
        (x_ref, g_ref, wg_ref, wv_ref, cwg_ref, cwv_ref, cbg_ref, cbv_ref,
         p1g_ref, p2g_ref, p1v_ref, p2v_ref,
         act_ref, tg_ref, tv_ref, xn_ref, carry_ref) = refs
    else:
        (x_ref, g_ref, wg_ref, wv_ref, cwg_ref, cwv_ref, cbg_ref, cbv_ref,
         act_ref, tg_ref, tv_ref, xn_ref, carry_ref) = refs
    i = pl.program_id(0)
    f = pl.program_id(1)
    tm = x_ref.shape[0]

    @pl.when(f == 0)
    def _():
        xn_ref[...] = _rms(x_ref[...], g_ref[...]).astype(BF16)

    xn = xn_ref[...]
    row = lax.broadcasted_iota(jnp.int32, (tm, wg_ref.shape[1]), 0)
    rowm = row % seg
    fresh = (i % tiles_per_seq) == 0

    def conv(w_ref, cw_ref, cb_ref, kind, p1_ref, p2_ref):
        u = jnp.dot(xn, w_ref[...], preferred_element_type=F32)
        if has_state:
            p1 = p1_ref[...]
            p2 = p2_ref[...]
        else:
            prev = carry_ref[kind, f]
            prev = jnp.where(fresh, 0.0, prev)
            prev0 = prev[SUBLANES - 2:SUBLANES - 1, :]
            prev1 = prev[SUBLANES - 1:SUBLANES, :]
            p1 = jnp.broadcast_to(prev1, u.shape)
            p2 = jnp.where(row == 0, prev0, prev1)
            carry_ref[kind, f] = u[tm - SUBLANES:, :]
        u1 = jnp.where(rowm < 1, p1, pltpu.roll(u, 1, 0))
        u2 = jnp.where(rowm < 2, p2, pltpu.roll(u, 2, 0))
        cw = cw_ref[...]
        mixed = cw[0:1, :] * u2 + cw[1:2, :] * u1 + cw[2:3, :] * u + cb_ref[...]
        return mixed, u[tm - tail_rows:, :]

    mg, tg = conv(wg_ref, cwg_ref, cbg_ref, 0, p1g_ref if has_state else None, p2g_ref if has_state else None)
    mv, tv = conv(wv_ref, cwv_ref, cbv_ref, 1, p1v_ref if has_state else None, p2v_ref if has_state else None)
    tg_ref[0] = tg
    tv_ref[0] = tv
    act_ref[...] = (mg * _sigmoid(mg) * mv).astype(BF16)


def ffn_in(x, gain, w_in, conv_w, conv_b, tm, tf, seq_len, hist=None):
    m, d = x.shape
    ff = w_in.shape[1] // 2
    nf = ff // tf
    nb = m // tm
    has_state = hist is not None
    if has_state:
        seg, tiles_per_seq, tail_rows = seq_len, 1, tm
    else:
        seg, tiles_per_seq, tail_rows = tm, seq_len // tm, SUBLANES
    wspec_g = pl.BlockSpec((d, tf), lambda i, f: (0, f))
    wspec_v = pl.BlockSpec((d, tf), lambda i, f: (0, nf + f))
    cspec_g = lambda r: pl.BlockSpec((r, tf), lambda i, f: (0, f))
    cspec_v = lambda r: pl.BlockSpec((r, tf), lambda i, f: (0, nf + f))
    in_specs = [pl.BlockSpec((tm, d), lambda i, f: (i, 0)), pl.BlockSpec((1, d), lambda i, f: (0, 0)),
                wspec_g, wspec_v, cspec_g(CONV_W), cspec_v(CONV_W), cspec_g(1), cspec_v(1)]
    args = [x, gain.reshape(1, d), w_in, w_in, conv_w, conv_w, conv_b.reshape(1, -1), conv_b.reshape(1, -1)]
    if has_state:
        p1, p2 = hist
        in_specs += [pl.BlockSpec((tm, tf), lambda i, f: (i, f)), pl.BlockSpec((tm, tf), lambda i, f: (i, f)),
                     pl.BlockSpec((tm, tf), lambda i, f: (i, nf + f)), pl.BlockSpec((tm, tf), lambda i, f: (i, nf + f))]
        args += [p1, p2, p1, p2]
    body = functools.partial(_ffn_in_body, seg=seg, tiles_per_seq=tiles_per_seq,
                             tail_rows=tail_rows, has_state=has_state)
    return pl.pallas_call(
        body, grid=(nb, nf), in_specs=in_specs,
        out_specs=[pl.BlockSpec((tm, tf), lambda i, f: (i, f)),
                   pl.BlockSpec((1, tail_rows, tf), lambda i, f: (i, 0, f)),
                   pl.BlockSpec((1, tail_rows, tf), lambda i, f: (i, 0, f))],
        out_shape=[jax.ShapeDtypeStruct((m, ff), BF16),
                   jax.ShapeDtypeStruct((nb, tail_rows, ff), F32),
                   jax.ShapeDtypeStruct((nb, tail_rows, ff), F32)],
        scratch_shapes=[pltpu.VMEM((tm, d), BF16), pltpu.VMEM((2, nf, SUBLANES, tf), F32)],
        compiler_params=_cp("arbitrary", "arbitrary"), name="ffn_in")(*args)


def _pad_rows(a, rows):
    if a.shape[0] == rows:
        return a
    return jnp.concatenate([a, jnp.zeros((rows - a.shape[0], a.shape[1]), a.dtype)], axis=0)


def _glr_chunk(q, k, v, g, st, c, sub):
    dk = q.shape[1]
    row = lax.broadcasted_iota(jnp.int32, (c, LANES), 0)
    col = lax.broadcasted_iota(jnp.int32, (c, LANES), 1)
    trow = lax.broadcasted_iota(jnp.int32, (c, c), 0)
    tcol = lax.broadcasted_iota(jnp.int32, (c, c), 1)
    tri = (trow >= tcol).astype(F32)
    cum = jnp.dot(tri, g, preferred_element_type=F32, precision=HIGHEST)
    last = cum[c - 1:c, :]
    inter = _mm_nt(q * jnp.exp(cum), st)

    rowk = lax.broadcasted_iota(jnp.int32, (c, dk), 0)
    rm = rowk % sub
    ones = jnp.ones((dk, LANES), BF16)
    att = jnp.zeros((c, LANES), F32)
    for d in range(sub):
        if d == 0:
            p = q * k
        else:
            ks = pltpu.roll(k, d, 0)
            cs = pltpu.roll(cum, d, 0)
            p = q * ks * jnp.exp(jnp.where(rm >= d, cum - cs, NEG))
        a = jnp.dot(p.astype(BF16), ones, preferred_element_type=F32)
        att = att + jnp.where(col == row - d, a, 0.0)
    if c > sub:
        blocks = [jnp.zeros((sub, LANES), F32)]
        for i in range(1, c // sub):
            cs = cum[i * sub - 1:i * sub, :]
            qi = q[i * sub:(i + 1) * sub, :] * jnp.exp(cum[i * sub:(i + 1) * sub, :] - cs)
            kj = k * jnp.exp(jnp.where(rowk < i * sub, cs - cum, NEG))
            blocks.append(_mm_nt(qi, _pad_rows(kj, LANES)))
        att = att + jnp.concatenate(blocks, axis=0)
    vpad = _pad_rows(v, LANES)
    intra = _mm(att, vpad)
    kd = _pad_rows(k * jnp.exp(last - cum), LANES)
    st_new = st * jnp.exp(last) + _mm(vpad.T, kd)
    return inter + intra, st_new


def _rec_body(*refs, mode, c, sub, n_chunks, dk, dv, layer, has_state):
    refs = list(refs)
    if mode == "gla":
        q_ref, k_ref, v_ref, r_ref, lr_ref, wa_ref, ba_ref, hn_ref = refs[:8]
        rest = refs[8:]
    else:
        q_ref, k_ref, v_ref, r_ref, lb_ref, hn_ref = refs[:6]
        rest = refs[6:]
    if has_state:
        s0_ref, og_ref, sout_ref, st_ref = rest
    else:
        og_ref, sout_ref, st_ref = rest
    tstep = pl.program_id(2)

    @pl.when(tstep == 0)
    def _():
        if has_state:
            st_ref[...] = s0_ref[0, 0].T
        else:
            st_ref[...] = jnp.zeros(st_ref.shape, F32)

    if mode == "hgrn":
        lbx = lb_ref[...]
        e = jnp.exp(lbx - jnp.max(lbx, axis=0, keepdims=True))
        sm = e / jnp.sum(e, axis=0, keepdims=True)
        lbv = jnp.zeros((1, dk), F32)
        for li in range(1, layer + 1):
            lbv = lbv + sm[li:li + 1, :]

    for ci in range(n_chunks):
        sl = slice(ci * c, (ci + 1) * c)
        if mode == "gla":
            q = q_ref[0, sl, :] * (dk ** -0.5)
            k = k_ref[0, sl, :]
            z = _mm(lr_ref[0, sl, :], wa_ref[...]) + ba_ref[...]
            g = -(jnp.maximum(-z, 0.0) + jnp.log1p(jnp.exp(-jnp.abs(z)))) / GLA_TEMP
        else:
            qz = q_ref[0, sl, :]
            q = qz * _sigmoid(qz)
            fg = lbv + (1.0 - lbv) * _sigmoid(k_ref[0, sl, :])
            k = 1.0 - fg
            g = jnp.log(fg)
        v = v_ref[0, sl, :]
        o, st_new = _glr_chunk(q, k, v, g, st_ref[...], c, sub)
        st_ref[...] = st_new
        of = o * lax.rsqrt(jnp.mean(o * o, axis=-1, keepdims=True) + RMS_EPS) * hn_ref[...]
        gate = r_ref[0, sl, :]
        og_ref[0, sl, :] = (of * (gate * _sigmoid(gate))).astype(BF16)

    @pl.when(tstep == pl.num_programs(2) - 1)
    def _():
        sout_ref[0, 0] = st_ref[...].T


def recurrence(mode, proj, heads, dk, dv, hn, tb, c, extra, s0=None, layer=0):
    b, t, _ = proj.shape
    sub = min(SUBCHUNK, c)
    has_state = s0 is not None
    nq = (heads * dk) // dk
    if mode == "gla":
        lr, wa, ba = extra
        koff, voff, roff = heads, (2 * heads * dk) // dv, (2 * heads * dk) // dv + heads
        in_specs = [pl.BlockSpec((1, tb, dk), lambda i, h, s: (i, s, h)),
                    pl.BlockSpec((1, tb, dk), lambda i, h, s: (i, s, koff + h)),
                    pl.BlockSpec((1, tb, dv), lambda i, h, s: (i, s, voff + h)),
                    pl.BlockSpec((1, tb, dv), lambda i, h, s: (i, s, roff + h)),
                    pl.BlockSpec((1, tb, LANES), lambda i, h, s: (i, s, 0)),
                    pl.BlockSpec((LANES, dk), lambda i, h, s: (0, h)),
                    pl.BlockSpec((1, dk), lambda i, h, s: (0, h)),
                    pl.BlockSpec((1, dv), lambda i, h, s: (0, 0))]
        args = [proj, proj, proj, proj, lr, wa, ba, hn.reshape(1, dv)]
    else:
        (lb,) = extra
        in_specs = [pl.BlockSpec((1, tb, dk), lambda i, h, s: (i, s, h)),
                    pl.BlockSpec((1, tb, dk), lambda i, h, s: (i, s, heads + h)),
                    pl.BlockSpec((1, tb, dv), lambda i, h, s: (i, s, 2 * heads + h)),
                    pl.BlockSpec((1, tb, dv), lambda i, h, s: (i, s, 3 * heads + h)),
                    pl.BlockSpec((lb.shape[0], dk), lambda i, h, s: (0, h)),
                    pl.BlockSpec((1, dv), lambda i, h, s: (0, 0))]
        args = [proj, proj, proj, proj, lb, hn.reshape(1, dv)]
    if has_state:
        in_specs.append(pl.BlockSpec((1, 1, dk, dv), lambda i, h, s: (i, h, 0, 0)))
        args.append(s0)
    body = functools.partial(_rec_body, mode=mode, c=c, sub=sub, n_chunks=tb // c, dk=dk, dv=dv,
                             layer=layer, has_state=has_state)
    return pl.pallas_call(
        body, grid=(b, heads, t // tb), in_specs=in_specs,
        out_specs=[pl.BlockSpec((1, tb, dv), lambda i, h, s: (i, s, h)),
                   pl.BlockSpec((1, 1, dk, dv), lambda i, h, s: (i, h, 0, 0))],
        out_shape=[jax.ShapeDtypeStruct((b, t, heads * dv), BF16),
                   jax.ShapeDtypeStruct((b, heads, dk, dv), F32)],
        scratch_shapes=[pltpu.VMEM((dv, dk), F32)],
        compiler_params=_cp("arbitrary", "arbitrary", "arbitrary"), name="recurrence_" + mode)(*args)


def _rope_body(q_ref, ks_ref, kw_ref, cos_ref, sin_ref, qo_ref, kso_ref, kwo_ref):
    cos = cos_ref[...]
    sin = sin_ref[...]

    def rot(src, dst):
        for h in range(src.shape[1] // LANES):
            x = src[:, h * LANES:(h + 1) * LANES]
            dst[:, h * LANES:(h + 1) * LANES] = x * cos + pltpu.roll(x, LANES // 2, 1) * sin

    rot(q_ref, qo_ref)
    rot(ks_ref, kso_ref)
    rot(kw_ref, kwo_ref)


def nsa_rope(proj, cos, sin, tm, tiles_per_seq, qw, kvw):
    m = proj.shape[0]
    ks_blk = (qw + 2 * kvw) // kvw
    kw_blk = (qw + 4 * kvw) // kvw
    return pl.pallas_call(
        _rope_body, grid=(m // tm,),
        in_specs=[pl.BlockSpec((tm, qw), lambda i: (i, 0)),
                  pl.BlockSpec((tm, kvw), lambda i: (i, ks_blk)),
                  pl.BlockSpec((tm, kvw), lambda i: (i, kw_blk)),
                  pl.BlockSpec((tm, LANES), lambda i: (i % tiles_per_seq, 0)),
                  pl.BlockSpec((tm, LANES), lambda i: (i % tiles_per_seq, 0))],
        out_specs=[pl.BlockSpec((tm, qw), lambda i: (i, 0)),
                   pl.BlockSpec((tm, kvw), lambda i: (i, 0)),
                   pl.BlockSpec((tm, kvw), lambda i: (i, 0))],
        out_shape=[jax.ShapeDtypeStruct((m, qw), F32), jax.ShapeDtypeStruct((m, kvw), F32),
                   jax.ShapeDtypeStruct((m, kvw), F32)],
        compiler_params=_cp("arbitrary"), name="nsa_rope")(proj, proj, proj, cos, sin)


def _pool_rows(x, pw):
    n = x.shape[0] // NSA_BLOCK
    return jnp.sum(x.reshape(n, NSA_BLOCK, x.shape[1]) * pw[None], axis=1)


def _pool_body(kc_ref, vc_ref, pk_ref, pv_ref, ko_ref, vo_ref, *, n_cb):
    ko_ref[...] = jnp.zeros(ko_ref.shape, F32)
    vo_ref[...] = jnp.zeros(vo_ref.shape, F32)
    ko_ref[0, 0:n_cb, :] = _pool_rows(kc_ref[0, 0:n_cb * NSA_BLOCK, :], pk_ref[...])
    vo_ref[0, 0:n_cb, :] = _pool_rows(vc_ref[0, 0:n_cb * NSA_BLOCK, :], pv_ref[...])


def nsa_pool_prompt(proj3, pk, pv, qw, kvw, n_pad):
    b, t, _ = proj3.shape
    n_cb = t // NSA_BLOCK
    kc_blk = qw // kvw
    return pl.pallas_call(
        functools.partial(_pool_body, n_cb=n_cb), grid=(b,),
        in_specs=[pl.BlockSpec((1, t, kvw), lambda i: (i, 0, kc_blk)),
                  pl.BlockSpec((1, t, kvw), lambda i: (i, 0, kc_blk + 1)),
                  pl.BlockSpec((NSA_BLOCK, kvw), lambda i: (0, 0)),
                  pl.BlockSpec((NSA_BLOCK, kvw), lambda i: (0, 0))],
        out_specs=[pl.BlockSpec((1, n_pad, kvw), lambda i: (i, 0, 0)),
                   pl.BlockSpec((1, n_pad, kvw), lambda i: (i, 0, 0))],
        out_shape=[jax.ShapeDtypeStruct((b, n_pad, kvw), F32), jax.ShapeDtypeStruct((b, n_pad, kvw), F32)],
        compiler_params=_cp("arbitrary"), name="nsa_pool")(proj3, proj3, pk, pv)


def _pool_pages_body(pt_ref, *refs, pg):
    k_refs = refs[:pg]
    v_refs = refs[pg:2 * pg]
    pk_ref, pv_ref, ko_ref, vo_ref = refs[2 * pg:]
    per = k_refs[0].shape[1] // NSA_BLOCK
    for i in range(pg):
        ko_ref[0, i * per:(i + 1) * per, :] = _pool_rows(k_refs[i][0], pk_ref[...])
        vo_ref[0, i * per:(i + 1) * per, :] = _pool_rows(v_refs[i][0], pv_ref[...])


def nsa_pool_pages(pool_k, pool_v, page_table, pk, pv, pg):
    b, n_pages = page_table.shape
    _, page, kvw = pool_k.shape
    per = page // NSA_BLOCK
    page_spec = lambda i: pl.BlockSpec((1, page, kvw), lambda bi, s, pt, i=i: (pt[bi, s * pg + i], 0, 0))
    gs = pltpu.PrefetchScalarGridSpec(
        num_scalar_prefetch=1, grid=(b, n_pages // pg),
        in_specs=[page_spec(i) for i in range(pg)] + [page_spec(i) for i in range(pg)]
                 + [pl.BlockSpec((NSA_BLOCK, kvw), lambda bi, s, pt: (0, 0)),
                    pl.BlockSpec((NSA_BLOCK, kvw), lambda bi, s, pt: (0, 0))],
        out_specs=[pl.BlockSpec((1, pg * per, kvw), lambda bi, s, pt: (bi, s, 0)),
                   pl.BlockSpec((1, pg * per, kvw), lambda bi, s, pt: (bi, s, 0))])
    n_blk = n_pages * per
    return pl.pallas_call(
        functools.partial(_pool_pages_body, pg=pg), grid_spec=gs,
        out_shape=[jax.ShapeDtypeStruct((b, n_blk, kvw), F32), jax.ShapeDtypeStruct((b, n_blk, kvw), F32)],
        compiler_params=_cp("arbitrary", "arbitrary"), name="nsa_pool_pages")(
            page_table, *([pool_k] * pg), *([pool_v] * pg), pk, pv)


def _cmp_body(q_ref, kc_ref, vc_ref, o_ref, sel_ref, *, q_start, n_cb, n_blk, nb_pad):
    tq = q_ref.shape[1]
    ncp = kc_ref.shape[1]
    qt = pl.program_id(2)
    kc = kc_ref[0]
    vc = vc_ref[0]
    scale = NSA_DH ** -0.5
    colc = lax.broadcasted_iota(jnp.int32, (tq, ncp), 1)
    qposc = q_start + qt * tq + lax.broadcasted_iota(jnp.int32, (tq, ncp), 0)
    valid = ((colc + 1) * NSA_BLOCK - 1 <= qposc) & (colc < n_cb)
    imp = jnp.zeros((tq, ncp), F32)
    for r in range(NSA_GROUP):
        qr = q_ref[0, :, r * LANES:(r + 1) * LANES]
        s = jnp.where(valid, _mm_nt(qr, kc) * scale, NEG)
        m = jnp.max(s, axis=-1, keepdims=True)
        e = jnp.where(valid, jnp.exp(s - m), 0.0)
        p = e / jnp.maximum(jnp.sum(e, axis=-1, keepdims=True), 1e-30)
        o_ref[0, :, r * LANES:(r + 1) * LANES] = _mm(p, vc)
        imp = imp + p
    if nb_pad > ncp:
        imp = jnp.concatenate([imp, jnp.zeros((tq, nb_pad - ncp), F32)], axis=1)
    blk = lax.broadcasted_iota(jnp.int32, (tq, nb_pad), 1)
    qpos = q_start + qt * tq + lax.broadcasted_iota(jnp.int32, (tq, nb_pad), 0)
    cur = qpos // NSA_BLOCK
    forced = (blk == cur) | (blk == 0)
    score = jnp.where(blk > cur, -1.0, jnp.where(forced, NSA_GROUP + 1.0, imp))
    score = jnp.where(blk < n_blk, score, -2.0)
    blkf = blk.astype(F32)

    def pick(_, carry):
        sc, sel = carry
        mx = jnp.max(sc, axis=-1, keepdims=True)
        first = jnp.min(jnp.where(sc == mx, blkf, 1e9), axis=-1, keepdims=True)
        hit = blkf == first
        return jnp.where(hit, -3.0, sc), jnp.where(hit, 1.0, sel)

    _, sel = lax.fori_loop(0, min(NSA_TOP_N, n_blk), pick, (score, jnp.zeros((tq, nb_pad), F32)))
    sel_ref[0, 0] = sel


def nsa_cmp(q3, kcmp, vcmp, tq, q_start, n_cb, n_blk, nb_pad, kvw):
    b, t = q3.shape[0], q3.shape[1]
    ncp = kcmp.shape[1]
    kv = kvw // LANES
    gw = NSA_GROUP * LANES
    body = functools.partial(_cmp_body, q_start=q_start, n_cb=n_cb, n_blk=n_blk, nb_pad=nb_pad)
    return pl.pallas_call(
        body, grid=(b, kv, t // tq),
        in_specs=[pl.BlockSpec((1, tq, gw), lambda i, g, s: (i, s, g)),
                  pl.BlockSpec((1, ncp, LANES), lambda i, g, s: (i, 0, g)),
                  pl.BlockSpec((1, ncp, LANES), lambda i, g, s: (i, 0, g))],
        out_specs=[pl.BlockSpec((1, tq, gw), lambda i, g, s: (i, s, g)),
                   pl.BlockSpec((1, 1, tq, nb_pad), lambda i, g, s: (i, g, s, 0))],
        out_shape=[jax.ShapeDtypeStruct((b, t, kv * gw), F32),
                   jax.ShapeDtypeStruct((b, kv, t, nb_pad), F32)],
        compiler_params=_cp("arbitrary", "arbitrary", "arbitrary"), name="nsa_cmp")(q3, kcmp, vcmp)


def _softmax_step(carry, s, valid, vv):
    m, l, acc = carry
    s = jnp.where(valid, s, NEG)
    m_new = jnp.maximum(m, jnp.max(s, axis=-1, keepdims=True))
    p = jnp.where(valid, jnp.exp(s - m_new), 0.0)
    alpha = jnp.exp(m - m_new)
    l = alpha * l + jnp.sum(p, axis=-1, keepdims=True)
    acc = alpha * acc + _mm(p, vv)
    return m_new, l, acc


def _softmax_init(rows):
    return (jnp.full((rows, 1), NEG, F32), jnp.zeros((rows, 1), F32), jnp.zeros((rows, LANES), F32))


def _attn_body(*refs, tq, tk, n_kt, q_start, k_start, do_sel, do_win):
    refs = list(refs)
    q_ref = refs.pop(0)
    if do_sel:
        ks_ref, vs_ref, sel_ref = refs[:3]
        refs = refs[3:]
    if do_win:
        kw_ref, vw_ref = refs[:2]
        refs = refs[2:]
    outs = refs
    qt = pl.program_id(2)
    scale = NSA_DH ** -0.5
    rows = NSA_GROUP * tq
    q4 = jnp.concatenate([q_ref[0, :, r * LANES:(r + 1) * LANES] for r in range(NSA_GROUP)], axis=0).astype(BF16)
    q0 = q_start + qt * tq
    qpos = q0 + lax.broadcasted_iota(jnp.int32, (rows, tk), 0) % tq
    kcol = lax.broadcasted_iota(jnp.int32, (rows, tk), 1)

    def finish(carry, o_ref):
        m, l, acc = carry
        o = acc / jnp.maximum(l, 1e-30)
        for r in range(NSA_GROUP):
            o_ref[0, :, r * LANES:(r + 1) * LANES] = o[r * tq:(r + 1) * tq, :]

    if do_sel:
        selb = sel_ref[0, 0].astype(BF16)
        nbp = selb.shape[1]
        en = lax.broadcasted_iota(jnp.int32, (nbp, tk), 0)
        es = lax.broadcasted_iota(jnp.int32, (nbp, tk), 1)

        def sel_step(kt, carry):
            off = pl.multiple_of(kt * tk, tk)
            kk = ks_ref[0, pl.ds(off, tk), :]
            vv = vs_ref[0, pl.ds(off, tk), :]
            s = _mm_nt(q4, kk) * scale
            kp0 = k_start + kt * tk
            expand = ((kp0 + es) // NSA_BLOCK == en).astype(BF16)
            chosen = jnp.dot(selb, expand, preferred_element_type=F32)
            chosen = jnp.concatenate([chosen] * NSA_GROUP, axis=0)
            valid = (chosen > 0.5) & (kp0 + kcol <= qpos)
            return _softmax_step(carry, s, valid, vv)

        hi = jnp.minimum(n_kt, (q0 + tq - 1 - k_start) // tk + 1)
        finish(lax.fori_loop(0, hi, sel_step, _softmax_init(rows)), outs.pop(0))

    if do_win:
        def win_step(kt, carry):
            off = pl.multiple_of(kt * tk, tk)
            kk = kw_ref[0, pl.ds(off, tk), :]
            vv = vw_ref[0, pl.ds(off, tk), :]
            s = _mm_nt(q4, kk) * scale
            kpos = k_start + kt * tk + kcol
            dist = qpos - kpos
            valid = (dist >= 0) & (dist < NSA_WINDOW) & (kpos >= 0)
            return _softmax_step(carry, s, valid, vv)

        lo = jnp.maximum(0, (q0 - (NSA_WINDOW - 1) - k_start) // tk)
        hi = jnp.minimum(n_kt, (q0 + tq - 1 - k_start) // tk + 1)
        finish(lax.fori_loop(lo, hi, win_step, _softmax_init(rows)), outs.pop(0))


def nsa_attend(q_rot3, tq, tk, q_start, k_start, sel_args=None, win_args=None):
    b, t, qw = q_rot3.shape
    kv = qw // (NSA_GROUP * LANES)
    gw = NSA_GROUP * LANES
    in_specs = [pl.BlockSpec((1, tq, gw), lambda i, g, s: (i, s, g))]
    args = [q_rot3]
    n_out = 0
    t_k = None
    if sel_args is not None:
        k, v, sel = sel_args
        t_k = k.shape[1]
        nbp = sel.shape[-1]
        in_specs += [pl.BlockSpec((1, t_k, LANES), lambda i, g, s: (i, 0, g)),
                     pl.BlockSpec((1, t_k, LANES), lambda i, g, s: (i, 0, g)),
                     pl.BlockSpec((1, 1, tq, nbp), lambda i, g, s: (i, g, s, 0))]
        args += [k, v, sel]
        n_out += 1
    if win_args is not None:
        k, v = win_args
        t_k = k.shape[1]
        in_specs += [pl.BlockSpec((1, t_k, LANES), lambda i, g, s: (i, 0, g)),
                     pl.BlockSpec((1, t_k, LANES), lambda i, g, s: (i, 0, g))]
        args += [k, v]
        n_out += 1
    body = functools.partial(_attn_body, tq=tq, tk=tk, n_kt=t_k // tk, q_start=q_start, k_start=k_start,
                             do_sel=sel_args is not None, do_win=win_args is not None)
    return pl.pallas_call(
        body, grid=(b, kv, t // tq), in_specs=in_specs,
        out_specs=[pl.BlockSpec((1, tq, gw), lambda i, g, s: (i, s, g))] * n_out,
        out_shape=[jax.ShapeDtypeStruct((b, t, qw), F32)] * n_out,
        compiler_params=_cp("arbitrary", "arbitrary", "arbitrary"), name="nsa_attend")(*args)


def _paged_sel_body(pt_ref, *refs, pg, past_len, t_new):
    k_refs = refs[:pg]
    v_refs = refs[pg:2 * pg]
    q_ref, sel_ref, kn_ref, vn_ref, o_ref, m_ref, l_ref, acc_ref = refs[2 * pg:]
    step = pl.program_id(1)
    page = k_refs[0].shape[1]
    kv = kn_ref.shape[2] // LANES
    rows = NSA_GROUP * t_new
    scale = NSA_DH ** -0.5
    nbp = sel_ref.shape[-1]

    @pl.when(step == 0)
    def _():
        m_ref[...] = jnp.full(m_ref.shape, NEG, F32)
        l_ref[...] = jnp.zeros(l_ref.shape, F32)
        acc_ref[...] = jnp.zeros(acc_ref.shape, F32)

    tloc = lax.broadcasted_iota(jnp.int32, (rows, page), 0) % t_new
    kcol = lax.broadcasted_iota(jnp.int32, (rows, page), 1)
    en = lax.broadcasted_iota(jnp.int32, (nbp, page), 0)
    es = lax.broadcasted_iota(jnp.int32, (nbp, page), 1)

    def group_q(g):
        return jnp.concatenate([q_ref[0, :, (g * NSA_GROUP + r) * LANES:(g * NSA_GROUP + r + 1) * LANES]
                                for r in range(NSA_GROUP)], axis=0)

    def update(g, kk, vv, kp0):
        selb = sel_ref[0, g].astype(BF16)
        expand = ((kp0 + es) // NSA_BLOCK == en).astype(BF16)
        chosen = jnp.dot(selb, expand, preferred_element_type=F32)
        chosen = jnp.concatenate([chosen] * NSA_GROUP, axis=0)
        valid = (chosen > 0.5) & (kp0 + kcol <= past_len + tloc)
        s = _mm_nt(group_q(g), kk) * scale
        carry = (m_ref[g, :, 0:1], l_ref[g, :, 0:1], acc_ref[g])
        m, l, acc = _softmax_step(carry, s, valid, vv)
        m_ref[g] = jnp.broadcast_to(m, (rows, LANES))
        l_ref[g] = jnp.broadcast_to(l, (rows, LANES))
        acc_ref[g] = acc

    for i in range(pg):
        kp0 = (step * pg + i) * page
        for g in range(kv):
            update(g, k_refs[i][0, :, g * LANES:(g + 1) * LANES], v_refs[i][0, :, g * LANES:(g + 1) * LANES], kp0)

    @pl.when(step == pl.num_programs(1) - 1)
    def _():
        for g in range(kv):
            update(g, kn_ref[0, :, g * LANES:(g + 1) * LANES], vn_ref[0, :, g * LANES:(g + 1) * LANES], past_len)
            o = acc_ref[g] / jnp.maximum(l_ref[g, :, 0:1], 1e-30)
            for r in range(NSA_GROUP):
                hh = g * NSA_GROUP + r
                o_ref[0, :, hh * LANES:(hh + 1) * LANES] = o[r * t_new:(r + 1) * t_new, :]


def nsa_paged_sel(q_rot3, sel, pool_k, pool_v, page_table, k_new, v_new, pg):
    b, t_new, qw = q_rot3.shape
    n_pages = page_table.shape[1]
    _, page, kvw = pool_k.shape
    kv = kvw // LANES
    nbp = sel.shape[-1]
    rows = NSA_GROUP * t_new
    page_spec = lambda i: pl.BlockSpec((1, page, kvw), lambda bi, s, pt, i=i: (pt[bi, s * pg + i], 0, 0))
    gs = pltpu.PrefetchScalarGridSpec(
        num_scalar_prefetch=1, grid=(b, n_pages // pg),
        in_specs=[page_spec(i) for i in range(pg)] + [page_spec(i) for i in range(pg)]
                 + [pl.BlockSpec((1, t_new, qw), lambda bi, s, pt: (bi, 0, 0)),
                    pl.BlockSpec((1, kv, t_new, nbp), lambda bi, s, pt: (bi, 0, 0, 0)),
                    pl.BlockSpec((1, page, kvw), lambda bi, s, pt: (bi, 0, 0)),
                    pl.BlockSpec((1, page, kvw), lambda bi, s, pt: (bi, 0, 0))],
        out_specs=pl.BlockSpec((1, t_new, qw), lambda bi, s, pt: (bi, 0, 0)),
        scratch_shapes=[pltpu.VMEM((kv, rows, LANES), F32), pltpu.VMEM((kv, rows, LANES), F32),
                        pltpu.VMEM((kv, rows, LANES), F32)])
    body = functools.partial(_paged_sel_body, pg=pg, past_len=n_pages * page, t_new=t_new)
    return pl.pallas_call(
        body, grid_spec=gs, out_shape=jax.ShapeDtypeStruct((b, t_new, qw), F32),
        compiler_params=_cp("arbitrary", "arbitrary"), name="nsa_paged_sel")(
            page_table, *([pool_k] * pg), *([pool_v] * pg), q_rot3, sel, k_new, v_new)


def _combine_body(oc_ref, os_ref, ow_ref, gt_ref, a_ref):
    gs = _sigmoid(gt_ref[...])
    for hh in range(oc_ref.shape[1] // LANES):
        sl = slice(hh * LANES, (hh + 1) * LANES)
        a = (gs[:, 3 * hh:3 * hh + 1] * oc_ref[:, sl] + gs[:, 3 * hh + 1:3 * hh + 2] * os_ref[:, sl]
             + gs[:, 3 * hh + 2:3 * hh + 3] * ow_ref[:, sl])
        a_ref[:, sl] = a.astype(BF16)


def nsa_combine(o_cmp, o_sel, o_win, gates, tm):
    m, qw = o_cmp.shape
    spec = pl.BlockSpec((tm, qw), lambda i: (i, 0))
    return pl.pallas_call(
        _combine_body, grid=(m // tm,),
        in_specs=[spec, spec, spec, pl.BlockSpec((tm, LANES), lambda i: (i, 0))],
        out_specs=spec, out_shape=jax.ShapeDtypeStruct((m, qw), BF16),
        compiler_params=_cp("arbitrary"), name="nsa_combine")(o_cmp, o_sel, o_win, gates)


def _s5_body(*refs, seg, seq_len, has_state):
    if has_state:
        (u_ref, a1_ref, a2_ref, dt_ref, b1_ref, b2_ref, cm_ref, d_ref, s0_ref,
         z_ref, st_ref, x_ref, y_ref) = refs
    else:
        (u_ref, a1_ref, a2_ref, dt_ref, b1_ref, b2_ref, cm_ref, d_ref,
         z_ref, st_ref, x_ref, y_ref) = refs
    gq = pl.program_id(1)
    m = u_ref.shape[0]
    nseg = m // seg
    nb = m // seq_len
    half = LANES // 2
    lane = lax.broadcasted_iota(jnp.int32, (1, LANES), 1)
    sgn = jnp.where(lane < half, -1.0, 1.0)

    are, aim, dt = a1_ref[0], a2_ref[0], jnp.exp(dt_ref[0])
    er = jnp.exp(are * dt)
    abr, abi = er * jnp.cos(aim * dt), er * jnp.sin(aim * dt)
    nr, ni, den = abr - 1.0, abi, are * are + aim * aim
    cr, ci = (nr * are + ni * aim) / den, (ni * are - nr * aim) / den
    bcat = cr * b1_ref[0] + ci * b2_ref[0]

    def cmul(x, pr, pi):
        return x * pr + pltpu.roll(x, half, 1) * (pi * sgn)

    u = u_ref[...]
    x_ref[...] = _mm(u, bcat)
    x = jnp.zeros((nseg, LANES), F32)
    for s in range(seg):
        x = cmul(x, abr, abi) + x_ref[pl.ds(s, nseg, stride=seg), :]
        x_ref[pl.ds(s, nseg, stride=seg), :] = x
    if has_state:
        carry = s0_ref[0]
    else:
        spb = seq_len // seg
        pr, pi = abr, abi
        for _ in range(int(math.log2(seg))):
            pr, pi = pr * pr - pi * pi, 2.0 * pr * pi
        rown = lax.broadcasted_iota(jnp.int32, (nseg, LANES), 0) % spb
        inc = x
        sh = 1
        while sh < spb:
            inc = inc + jnp.where(rown >= sh, cmul(pltpu.roll(inc, sh, 0), pr, pi), 0.0)
            pr, pi = pr * pr - pi * pi, 2.0 * pr * pi
            sh *= 2
        carry = jnp.where(rown >= 1, pltpu.roll(inc, 1, 0), 0.0)
    pr, pi = abr, abi
    for s in range(seg):
        x_ref[pl.ds(s, nseg, stride=seg), :] = x_ref[pl.ds(s, nseg, stride=seg), :] + cmul(carry, pr, pi)
        pr, pi = pr * abr - pi * abi, pr * abi + pi * abr
    st_ref[0] = x_ref[pl.ds(seq_len - 1, nb, stride=seq_len), :]

    yg = _mm(x_ref[...], cm_ref[0])

    @pl.when(gq == 0)
    def _():
        y_ref[...] = yg

    @pl.when(gq > 0)
    def _():
        y_ref[...] = y_ref[...] + yg

    @pl.when(gq == pl.num_programs(1) - 1)
    def _():
        y = y_ref[...] + d_ref[...] * u
        z = 0.5 * y * (1.0 + jnp.tanh(math.sqrt(2.0 / math.pi) * (y + 0.044715 * (y * y * y))))
        z_ref[...] = z.astype(BF16)


def s5_scan(u, prm, seq_len, seg, s0=None):
    m, d = u.shape
    a1, a2, dtb, b1, b2, cm, dsk = prm
    groups = a1.shape[0]
    per_tile = LANES // S5_CH
    nb = m // seq_len
    has_state = s0 is not None
    gidx = lambda j, q: (j * per_tile + q, 0, 0)
    in_specs = [pl.BlockSpec((m, LANES), lambda j, q: (0, j)),
                pl.BlockSpec((1, 1, LANES), gidx), pl.BlockSpec((1, 1, LANES), gidx), pl.BlockSpec((1, 1, LANES), gidx),
                pl.BlockSpec((1, LANES, LANES), gidx), pl.BlockSpec((1, LANES, LANES), gidx),
                pl.BlockSpec((1, LANES, LANES), gidx),
                pl.BlockSpec((1, LANES), lambda j, q: (0, j))]
    args = [u, a1, a2, dtb, b1, b2, cm, dsk]
    if has_state:
        in_specs.append(pl.BlockSpec((1, nb, LANES), gidx))
        args.append(s0)
    body = functools.partial(_s5_body, seg=seg, seq_len=seq_len, has_state=has_state)
    return pl.pallas_call(
        body, grid=(d // LANES, per_tile), in_specs=in_specs,
        out_specs=[pl.BlockSpec((m, LANES), lambda j, q: (0, j)),
                   pl.BlockSpec((1, nb, LANES), gidx)],
        out_shape=[jax.ShapeDtypeStruct((m, d), BF16), jax.ShapeDtypeStruct((groups, nb, LANES), F32)],
        scratch_shapes=[pltpu.VMEM((m, LANES), F32), pltpu.VMEM((m, LANES), F32)],
        compiler_params=_cp("arbitrary", "arbitrary"), name="s5_scan")(*args)


def _s5_params(a_re, a_im, log_dt, b_re, b_im, c_re, c_im, d_skip):
    groups, p = a_re.shape
    per_tile = LANES // S5_CH
    dup = lambda a: jnp.concatenate([a, a], axis=-1)[:, None, :]
    a1, a2 = dup(a_re), dup(a_im)
    dtb = jnp.broadcast_to(log_dt[:, None, None], (groups, 1, LANES))
    slot = jax.nn.one_hot(jnp.arange(groups) % per_tile, per_tile, dtype=F32)

    def rows_in_tile(w):
        return (slot[:, :, None, None] * w[:, None]).reshape(groups, LANES, w.shape[-1])

    bre_t, bim_t = b_re.transpose(0, 2, 1), b_im.transpose(0, 2, 1)
    b1 = rows_in_tile(jnp.concatenate([bre_t, bim_t], axis=-1))
    b2 = rows_in_tile(jnp.concatenate([-bim_t, bre_t], axis=-1))
    cmat = jnp.concatenate([c_re, -c_im], axis=-1)
    cm = rows_in_tile(cmat).transpose(0, 2, 1)
    return a1, a2, dtb, b1, b2, cm, d_skip.reshape(1, -1)


def _tiles(m):
    if m >= 1024:
        return 1024, 512
    return m, m


def _gla_layer(h, b, t, gain, w_in, w_tail, w_alpha_pad, b_alpha, head_norm, w_out, s0, heads, dk, dv):
    tm, tmo = _tiles(h.shape[0])
    n_main = 2 * heads * dk + 2 * heads * dv
    proj = norm_proj(h, gain, w_in, n_main, tm, 512)
    lr = norm_proj(h, gain, w_tail, LANES, tm, LANES)
    c = math.gcd(t, CHUNK)
    tb = math.gcd(t, 256)
    og, st = recurrence("gla", proj.reshape(b, t, n_main), heads, dk, dv, head_norm, tb, c,
                        (lr.reshape(b, t, LANES), w_alpha_pad, b_alpha.reshape(1, -1)), s0)
    return out_proj(og.reshape(b * t, heads * dv), w_out, h, tmo, 512), st


def _hgrn_layer(h, b, t, gain, w_in, lower_bound, layer, head_norm, w_out, s0, heads, dk):
    tm, tmo = _tiles(h.shape[0])
    n = 4 * heads * dk
    proj = norm_proj(h, gain, w_in, n, tm, 512)
    c = math.gcd(t, CHUNK)
    tb = math.gcd(t, 256)
    og, st = recurrence("hgrn", proj.reshape(b, t, n), heads, dk, dk, head_norm, tb, c,
                        (lower_bound,), s0, layer=layer)
    return out_proj(og.reshape(b * t, heads * dk), w_out, h, tmo, 512), st


def _rope_tables(start, t):
    half = NSA_DH // 2
    inv = ROPE_THETA ** (-jnp.arange(half, dtype=F32) / half)
    ang = (start + jnp.arange(t, dtype=jnp.int32)).astype(F32)[:, None] * inv[None, :]
    cos, sin = jnp.cos(ang), jnp.sin(ang)
    return jnp.concatenate([cos, cos], axis=-1), jnp.concatenate([-sin, sin], axis=-1)


def _nsa_layer(h, b, t, start, gain, w_in, w_gates, pool_k, pool_v, w_out, past, heads, kv):
    tm, tmo = _tiles(h.shape[0])
    qw, kvw = heads * NSA_DH, kv * NSA_DH
    n_main = qw + 6 * kvw
    proj = norm_proj(h, gain, w_in, n_main, tm, 512)
    gates = norm_proj(h, gain, w_gates, LANES, tm, LANES)
    cos, sin = _rope_tables(start, t)
    pk = jnp.broadcast_to(pool_k[:, None], (NSA_BLOCK, kvw))
    pv = jnp.broadcast_to(pool_v[:, None], (NSA_BLOCK, kvw))
    proj3 = proj.reshape(b, t, n_main)
    col = lambda i: proj3[:, :, qw + i * kvw:qw + (i + 1) * kvw]
    kc, vc, vs, vw = col(0), col(1), col(3), col(5)

    if past is None:
        trope = min(256, t)
        q_rot, ks, kw = nsa_rope(proj, cos, sin, trope, t // trope, qw, kvw)
        q_rot3, ks3, kw3 = q_rot.reshape(b, t, qw), ks.reshape(b, t, kvw), kw.reshape(b, t, kvw)
        n_cb = t // NSA_BLOCK
        n_blk = -(-t // NSA_BLOCK)
        kcmp, vcmp = nsa_pool_prompt(proj3, pk, pv, qw, kvw, LANES)
        o_cmp, sel = nsa_cmp(proj3, kcmp, vcmp, min(t, 512), 0, n_cb, n_blk, LANES, kvw)
        tq = min(t, 128)
        o_sel, o_win = nsa_attend(q_rot3, tq, tq, 0, 0, sel_args=(ks3, vs, sel), win_args=(kw3, vw))
        keep = min(NSA_WINDOW, t)
        win_k, win_v = kw3[:, t - keep:], vw[:, t - keep:]
    else:
        pool_ck, pool_cv, pool_sk, pool_sv, page_table, prev_kw, prev_vw = past
        n_pages = page_table.shape[1]
        page = pool_ck.shape[1]
        past_len = n_pages * page
        cos_r, sin_r = jnp.tile(cos, (b, 1)), jnp.tile(sin, (b, 1))
        q_rot, ks, kw = nsa_rope(proj, cos_r, sin_r, b * t, 1, qw, kvw)
        q_rot3, ks3, kw3 = q_rot.reshape(b, t, qw), ks.reshape(b, t, kvw), kw.reshape(b, t, kvw)
        flat = lambda p_: p_.reshape(p_.shape[0], page, kvw)
        kcmp, vcmp = nsa_pool_pages(flat(pool_ck), flat(pool_cv), page_table, pk, pv, 4)
        total = past_len + t
        n_cb = total // NSA_BLOCK
        n_blk = -(-total // NSA_BLOCK)
        nb_pad = -(-n_blk // LANES) * LANES
        o_cmp, sel = nsa_cmp(proj3, kcmp, vcmp, t, past_len, n_cb, n_blk, nb_pad, kvw)
        padp = lambda a: jnp.concatenate([a, jnp.zeros((b, page - t, kvw), F32)], axis=1)
        o_sel = nsa_paged_sel(q_rot3, sel, flat(pool_sk), flat(pool_sv), page_table, padp(ks3), padp(vs), 4)
        keep = prev_kw.shape[1]
        kw_ext = jnp.concatenate([prev_kw.reshape(b, keep, kvw), kw3], axis=1)
        vw_ext = jnp.concatenate([prev_vw.reshape(b, keep, kvw), vw], axis=1)
        t_ext = keep + t
        t_pad = -(-t_ext // LANES) * LANES
        pade = lambda a: jnp.concatenate([a, jnp.zeros((b, t_pad - t_ext, kvw), F32)], axis=1)
        (o_win,) = nsa_attend(q_rot3, t, LANES, past_len, past_len - keep, win_args=(pade(kw_ext), pade(vw_ext)))
        win_k, win_v = kw_ext[:, t_ext - keep:], vw_ext[:, t_ext - keep:]

    a = nsa_combine(o_cmp.reshape(b * t, qw), o_sel.reshape(b * t, qw), o_win.reshape(b * t, qw), gates,
                    min(b * t, 512))
    y = out_proj(a, w_out, h, tmo, 512)
    shp = lambda x, n: x.reshape(b, n, kv, NSA_DH)
    return y, (shp(kc, t), shp(vc, t), shp(ks3, t), shp(vs, t), shp(win_k, keep), shp(win_v, keep))


def _s5_layer(h, b, t, gain, prm, w_glu, s_re, s_im):
    tm, tmo = _tiles(h.shape[0])
    u = rmsnorm_rows(h, gain, min(h.shape[0], 512))
    groups = prm[0].shape[0]
    if s_re is None:
        z, st = s5_scan(u, prm, t, math.gcd(t, 32))
    else:
        s0 = jnp.concatenate([s_re, s_im], axis=-1).transpose(1, 0, 2)
        z, st = s5_scan(u, prm, t, t, s0)
    y = out_glu(z, w_glu, h, tmo, 512)
    st = st.transpose(1, 0, 2)
    return y, (st[..., :S5_STATE], st[..., S5_STATE:])


def _ffn_layer(h, b, t, gain, w_in, conv_w, conv_b, w_out, buf):
    tm, tmo = _tiles(h.shape[0])
    ff2 = w_in.shape[1]
    if buf is None:
        tm = min(512, t)
        act, tg, tv = ffn_in(h, gain, w_in, conv_w, conv_b, tm, 512, t)
        per = t // tm
        last = lambda a: a[per - 1::per, SUBLANES - (CONV_W - 1):, :]
        state = jnp.concatenate([last(tg), last(tv)], axis=-1)
    else:
        zrow = jnp.zeros((b, t - 1, ff2), F32)
        p1 = jnp.concatenate([buf[:, 1:2], zrow], axis=1).reshape(b * t, ff2)
        p2 = jnp.concatenate([buf, zrow[:, 1:]], axis=1).reshape(b * t, ff2)
        act, tg, tv = ffn_in(h, gain, w_in, conv_w, conv_b, b * t, 512, t, hist=(p1, p2))
        up = jnp.concatenate([tg[0], tv[0]], axis=-1).reshape(b, t, ff2)
        state = jnp.concatenate([buf, up], axis=1)[:, t:]
    return out_proj(act, w_out, h, tmo, 512), state


def kernel(x_prompt, x_sample, state_gla, state_hgrn, cache_nsa_cmp_k, cache_nsa_cmp_v, cache_nsa_sel_k, cache_nsa_sel_v, cache_nsa_win_k, cache_nsa_win_v, state_s5_re, state_s5_im, state_ffn_conv, page_table, norm_mix, norm_ffn, final_norm, gla_w_in, gla_w_alpha, gla_b_alpha, gla_head_norm, gla_w_out, hgrn_w_in, hgrn_lower_bound, hgrn_head_norm, hgrn_w_out, nsa_w_in, nsa_pool_k, nsa_pool_v, nsa_w_out, s5_a_re, s5_a_im, s5_log_dt, s5_b_re, s5_b_im, s5_c_re, s5_c_im, s5_d, s5_w_glu, ffn_w_in, ffn_conv_w, ffn_conv_b, ffn_w_out):
    bp, tp, d = x_prompt.shape
    bs, ts, _ = x_sample.shape
    depth = norm_mix.shape[0]
    n_mixers = 4
    gla_heads, gla_dk, gla_dv = state_gla.shape[2], state_gla.shape[3], state_gla.shape[4]
    hgrn_heads, hgrn_dk = state_hgrn.shape[2], state_hgrn.shape[3]
    nsa_kv = cache_nsa_cmp_k.shape[3]
    nsa_heads = d // NSA_DH
    hp = x_prompt.reshape(bp * tp, d)
    hs = x_sample.reshape(bs * ts, d)
    bf = lambda w: w.astype(BF16)

    def pad_cols(w, n):
        return jnp.concatenate([w, jnp.zeros((w.shape[0], n - w.shape[1]), w.dtype)], axis=1)

    outs = {k: [] for k in ("gla_p", "gla_s", "hgrn_p", "hgrn_s", "nsa_p", "nsa_s", "s5_p", "s5_s", "conv_p", "conv_s")}
    for i in range(depth):
        kind, j = i % n_mixers, i // n_mixers
        if kind == 0:
            n_main = 2 * gla_heads * gla_dk + 2 * gla_heads * gla_dv
            w_in = bf(gla_w_in[j])
            w_tail = bf(pad_cols(gla_w_in[j][:, n_main:], LANES))
            rank = gla_w_alpha.shape[1]
            wa = bf(jnp.concatenate([gla_w_alpha[j], jnp.zeros((LANES - rank, gla_w_alpha.shape[2]), F32)], axis=0))
            common = (norm_mix[i], w_in, w_tail, wa, gla_b_alpha[j], gla_head_norm[j], bf(gla_w_out[j]))
            hp, st_p = _gla_layer(hp, bp, tp, *common, None, gla_heads, gla_dk, gla_dv)
            hs, st_s = _gla_layer(hs, bs, ts, *common, state_gla[j], gla_heads, gla_dk, gla_dv)
            outs["gla_p"].append(st_p)
            outs["gla_s"].append(st_s)
        elif kind == 1:
            common = (norm_mix[i], bf(hgrn_w_in[j]), hgrn_lower_bound, i, hgrn_head_norm[j], bf(hgrn_w_out[j]))
            hp, st_p = _hgrn_layer(hp, bp, tp, *common, None, hgrn_heads, hgrn_dk)
            hs, st_s = _hgrn_layer(hs, bs, ts, *common, state_hgrn[j], hgrn_heads, hgrn_dk)
            outs["hgrn_p"].append(st_p)
            outs["hgrn_s"].append(st_s)
        elif kind == 2:
            n_main = nsa_heads * NSA_DH + 6 * nsa_kv * NSA_DH
            w_in = bf(nsa_w_in[j])
            w_gates = bf(pad_cols(nsa_w_in[j][:, n_main:], LANES))
            common = (norm_mix[i], w_in, w_gates, nsa_pool_k[j], nsa_pool_v[j], bf(nsa_w_out[j]))
            hp, st_p = _nsa_layer(hp, bp, tp, 0, *common, None, nsa_heads, nsa_kv)
            past = (cache_nsa_cmp_k[j], cache_nsa_cmp_v[j], cache_nsa_sel_k[j], cache_nsa_sel_v[j],
                    page_table, cache_nsa_win_k[j], cache_nsa_win_v[j])
            hs, st_s = _nsa_layer(hs, bs, ts, page_table.shape[1] * cache_nsa_cmp_k.shape[2], *common, past,
                                  nsa_heads, nsa_kv)
            outs["nsa_p"].append(st_p)
            outs["nsa_s"].append(st_s)
        else:
            prm = _s5_params(s5_a_re[j], s5_a_im[j], s5_log_dt[j], s5_b_re[j], s5_b_im[j], s5_c_re[j],
                             s5_c_im[j], s5_d[j])
            w_glu = bf(s5_w_glu[j])
            hp, st_p = _s5_layer(hp, bp, tp, norm_mix[i], prm, w_glu, None, None)
            hs, st_s = _s5_layer(hs, bs, ts, norm_mix[i], prm, w_glu, state_s5_re[j], state_s5_im[j])
            outs["s5_p"].append(st_p)
            outs["s5_s"].append(st_s)
        fw = (norm_ffn[i], bf(ffn_w_in[i]), ffn_conv_w[i], ffn_conv_b[i], bf(ffn_w_out[i]))
        hp, cb_p = _ffn_layer(hp, bp, tp, *fw, None)
        hs, cb_s = _ffn_layer(hs, bs, ts, *fw, state_ffn_conv[i])
        outs["conv_p"].append(cb_p)
        outs["conv_s"].append(cb_s)

    y_prompt = rmsnorm_rows(hp, final_norm, min(hp.shape[0], 512)).reshape(bp, tp, d)
    y_sample = rmsnorm_rows(hs, final_norm, min(hs.shape[0], 512)).reshape(bs, ts, d)
    stack = lambda xs: jnp.stack(xs)
    pick = lambda key, r: stack([e[r] for e in outs[key]])
    res = [y_prompt, y_sample, stack(outs["gla_p"]), stack(outs["gla_s"]), stack(outs["hgrn_p"]), stack(outs["hgrn_s"])]
    for r in range(6):
        res += [pick("nsa_p", r), pick("nsa_s", r)]
    for r in range(2):
        res += [pick("s5_p", r), pick("s5_s", r)]
    res += [stack(outs["conv_p"]), stack(outs["conv_s"])]
    return tuple(res)
```

```python
import functools
import math

import jax
import jax.numpy as jnp
from jax import lax
from jax.experimental import pallas as pl
from jax.experimental.pallas import tpu as pltpu

F32 = jnp.float32
BF16 = jnp.bfloat16
HIGHEST = lax.Precision.HIGHEST

RMS_EPS = 1e-6
ROPE_THETA = 10000.0
NEG = -1e30
CHUNK = 64
SUBCHUNK = 16
GLA_TEMP = 16.0
NSA_BLOCK = 64
NSA_TOP_N = 16
NSA_WINDOW = 512
NSA_GROUP = 4
NSA_DH = 128
S5_CH = 16
S5_STATE = 64
CONV_W = 3
LANES = 128
SUBLANES = 8
VMEM_LIMIT = 48 * 1024 * 1024


def _cp(*sem):
    return pltpu.CompilerParams(dimension_semantics=sem, vmem_limit_bytes=VMEM_LIMIT)


def _mm(a, b):
    return jnp.dot(a.astype(BF16), b.astype(BF16), preferred_element_type=F32)


def _mm_nt(a, b):
    return lax.dot_general(a.astype(BF16), b.astype(BF16), (((1,), (1,)), ((), ())),
                           preferred_element_type=F32)


def _sigmoid(x):
    return 1.0 / (1.0 + jnp.exp(-x))


def _rms(x, g):
    return x * lax.rsqrt(jnp.mean(x * x, axis=-1, keepdims=True) + RMS_EPS) * g


CAST_BLOCK_BYTES = 4 * 1024 * 1024


def _cast_body(w_ref, o_ref):
    o_ref[...] = w_ref[0].astype(BF16)


def cast_bf16(w_stack, layer):
    _, k, n = w_stack.shape
    tk = 16
    while k % (2 * tk) == 0 and 2 * tk * n * 4 <= CAST_BLOCK_BYTES:
        tk *= 2
    return pl.pallas_call(
        _cast_body, grid=(k // tk,),
        in_specs=[pl.BlockSpec((1, tk, n), lambda i: (layer, i, 0))],
        out_specs=pl.BlockSpec((tk, n), lambda i: (i, 0)),
        out_shape=jax.ShapeDtypeStruct((k, n), BF16),
        compiler_params=_cp("arbitrary"), name="cast_bf16")(w_stack)


def _norm_body(x_ref, g_ref, o_ref):
    o_ref[...] = _rms(x_ref[...], g_ref[...])


def rmsnorm_rows(x, gain, tm):
    m, d = x.shape
    return pl.pallas_call(
        _norm_body, grid=(m // tm,),
        in_specs=[pl.BlockSpec((tm, d), lambda i: (i, 0)), pl.BlockSpec((1, d), lambda i: (0, 0))],
        out_specs=pl.BlockSpec((tm, d), lambda i: (i, 0)),
        out_shape=jax.ShapeDtypeStruct((m, d), F32),
        compiler_params=_cp("arbitrary"), name="rmsnorm")(x, gain.reshape(1, d))


def _proj_body(x_ref, g_ref, w_ref, o_ref, xn_ref):
    @pl.when(pl.program_id(1) == 0)
    def _():
        xn_ref[...] = _rms(x_ref[...], g_ref[...]).astype(BF16)
    o_ref[...] = jnp.dot(xn_ref[...], w_ref[...], preferred_element_type=F32)


def norm_proj(x, gain, w, n_out, tm, tn):
    m, d = x.shape
    return pl.pallas_call(
        _proj_body, grid=(m // tm, n_out // tn),
        in_specs=[pl.BlockSpec((tm, d), lambda i, j: (i, 0)),
                  pl.BlockSpec((1, d), lambda i, j: (0, 0)),
                  pl.BlockSpec((d, tn), lambda i, j: (0, j))],
        out_specs=pl.BlockSpec((tm, tn), lambda i, j: (i, j)),
        out_shape=jax.ShapeDtypeStruct((m, n_out), F32),
        scratch_shapes=[pltpu.VMEM((tm, d), BF16)],
        compiler_params=_cp("arbitrary", "arbitrary"), name="norm_proj")(x, gain.reshape(1, d), w)


def _out_body(a_ref, w_ref, r_ref, o_ref):
    o_ref[...] = r_ref[...] + jnp.dot(a_ref[...].astype(BF16), w_ref[...], preferred_element_type=F32)


def out_proj(a, w, res, tm, tn):
    m, k = a.shape
    n = w.shape[1]
    return pl.pallas_call(
        _out_body, grid=(m // tm, n // tn),
        in_specs=[pl.BlockSpec((tm, k), lambda i, j: (i, 0)),
                  pl.BlockSpec((k, tn), lambda i, j: (0, j)),
                  pl.BlockSpec((tm, tn), lambda i, j: (i, j))],
        out_specs=pl.BlockSpec((tm, tn), lambda i, j: (i, j)),
        out_shape=jax.ShapeDtypeStruct((m, n), F32),
        compiler_params=_cp("arbitrary", "arbitrary"), name="out_proj")(a, w, res)


def _out_glu_body(a_ref, w1_ref, w2_ref, r_ref, o_ref):
    a = a_ref[...]
    g1 = jnp.dot(a, w1_ref[...], preferred_element_type=F32)
    g2 = jnp.dot(a, w2_ref[...], preferred_element_type=F32)
    o_ref[...] = r_ref[...] + g1 * _sigmoid(g2)


def out_glu(a, w, res, tm, tn):
    m, k = a.shape
    n = w.shape[1] // 2
    nj = n // tn
    return pl.pallas_call(
        _out_glu_body, grid=(m // tm, nj),
        in_specs=[pl.BlockSpec((tm, k), lambda i, j: (i, 0)),
                  pl.BlockSpec((k, tn), lambda i, j: (0, j)),
                  pl.BlockSpec((k, tn), lambda i, j: (0, nj + j)),
                  pl.BlockSpec((tm, tn), lambda i, j: (i, j))],
        out_specs=pl.BlockSpec((tm, tn), lambda i, j: (i, j)),
        out_shape=jax.ShapeDtypeStruct((m, n), F32),
        compiler_params=_cp("arbitrary", "arbitrary"), name="out_glu")(a, w, w, res)


def _ffn_in_body(*refs, seg, tiles_per_seq, tail_rows, has_state):
    if has_state:
        (x_ref, g_ref, wg_ref, wv_ref, cwg_ref, cwv_ref, cbg_ref, cbv_ref,
         p1g_ref, p2g_ref, p1v_ref, p2v_ref,
         act_ref, tg_ref, tv_ref, xn_ref, carry_ref) = refs
    else:
        (x_ref, g_ref, wg_ref, wv_ref, cwg_ref, cwv_ref, cbg_ref, cbv_ref,
         act_ref, tg_ref, tv_ref, xn_ref, carry_ref) = refs
    i = pl.program_id(0)
    f = pl.program_id(1)
    tm = x_ref.shape[0]

    @pl.when(f == 0)
    def _():
        xn_ref[...] = _rms(x_ref[...], g_ref[...]).astype(BF16)

    xn = xn_ref[...]
    row = lax.broadcasted_iota(jnp.int32, (tm, wg_ref.shape[1]), 0)
    rowm = row % seg
    fresh = (i % tiles_per_seq) == 0

    def conv(w_ref, cw_ref, cb_ref, kind, p1_ref, p2_ref):
        u = jnp.dot(xn, w_ref[...], preferred_element_type=F32)
        if has_state:
            p1 = p1_ref[...]
            p2 = p2_ref[...]
        else:
            prev = carry_ref[kind, f]
            prev = jnp.where(fresh, 0.0, prev)
            prev0 = prev[SUBLANES - 2:SUBLANES - 1, :]
            prev1 = prev[SUBLANES - 1:SUBLANES, :]
            p1 = jnp.broadcast_to(prev1, u.shape)
            p2 = jnp.where(row == 0, prev0, prev1)
            carry_ref[kind, f] = u[tm - SUBLANES:, :]
        u1 = jnp.where(rowm < 1, p1, pltpu.roll(u, 1, 0))
        u2 = jnp.where(rowm < 2, p2, pltpu.roll(u, 2, 0))
        cw = cw_ref[...]
        mixed = cw[0:1, :] * u2 + cw[1:2, :] * u1 + cw[2:3, :] * u + cb_ref[...]
        return mixed, u[tm - tail_rows:, :]

    mg, tg = conv(wg_ref, cwg_ref, cbg_ref, 0, p1g_ref if has_state else None, p2g_ref if has_state else None)
    mv, tv = conv(wv_ref, cwv_ref, cbv_ref, 1, p1v_ref if has_state else None, p2v_ref if has_state else None)
    tg_ref[0] = tg
    tv_ref[0] = tv
    act_ref[...] = (mg * _sigmoid(mg) * mv).astype(BF16)


def ffn_in(x, gain, w_in, conv_w, conv_b, tm, tf, seq_len, hist=None):
    m, d = x.shape
    ff = w_in.shape[1] // 2
    nf = ff // tf
    nb = m // tm
    has_state = hist is not None
    if has_state:
        seg, tiles_per_seq, tail_rows = seq_len, 1, tm
    else:
        seg, tiles_per_seq, tail_rows = tm, seq_len // tm, SUBLANES
    wspec_g = pl.BlockSpec((d, tf), lambda i, f: (0, f))
    wspec_v = pl.BlockSpec((d, tf), lambda i, f: (0, nf + f))
    cspec_g = lambda r: pl.BlockSpec((r, tf), lambda i, f: (0, f))
    cspec_v = lambda r: pl.BlockSpec((r, tf), lambda i, f: (0, nf + f))
    in_specs = [pl.BlockSpec((tm, d), lambda i, f: (i, 0)), pl.BlockSpec((1, d), lambda i, f: (0, 0)),
                wspec_g, wspec_v, cspec_g(CONV_W), cspec_v(CONV_W), cspec_g(1), cspec_v(1)]
    args = [x, gain.reshape(1, d), w_in, w_in, conv_w, conv_w, conv_b.reshape(1, -1), conv_b.reshape(1, -1)]
    if has_state:
        p1, p2 = hist
        in_specs += [pl.BlockSpec((tm, tf), lambda i, f: (i, f)), pl.BlockSpec((tm, tf), lambda i, f: (i, f)),
                     pl.BlockSpec((tm, tf), lambda i, f: (i, nf + f)), pl.BlockSpec((tm, tf), lambda i, f: (i, nf + f))]
        args += [p1, p2, p1, p2]
    body = functools.partial(_ffn_in_body, seg=seg, tiles_per_seq=tiles_per_seq,
                             tail_rows=tail_rows, has_state=has_state)
    return pl.pallas_call(
        body, grid=(nb, nf), in_specs=in_specs,
        out_specs=[pl.BlockSpec((tm, tf), lambda i, f: (i, f)),
                   pl.BlockSpec((1, tail_rows, tf), lambda i, f: (i, 0, f)),
                   pl.BlockSpec((1, tail_rows, tf), lambda i, f: (i, 0, f))],
        out_shape=[jax.ShapeDtypeStruct((m, ff), BF16),
                   jax.ShapeDtypeStruct((nb, tail_rows, ff), F32),
                   jax.ShapeDtypeStruct((nb, tail_rows, ff), F32)],
        scratch_shapes=[pltpu.VMEM((tm, d), BF16), pltpu.VMEM((2, nf, SUBLANES, tf), F32)],
        compiler_params=_cp("arbitrary", "arbitrary"), name="ffn_in")(*args)


def _pad_rows(a, rows):
    if a.shape[0] == rows:
        return a
    return jnp.concatenate([a, jnp.zeros((rows - a.shape[0], a.shape[1]), a.dtype)], axis=0)


def _glr_chunk(q, k, v, g, st, c, sub):
    dk = q.shape[1]
    row = lax.broadcasted_iota(jnp.int32, (c, LANES), 0)
    col = lax.broadcasted_iota(jnp.int32, (c, LANES), 1)
    trow = lax.broadcasted_iota(jnp.int32, (c, c), 0)
    tcol = lax.broadcasted_iota(jnp.int32, (c, c), 1)
    tri = (trow >= tcol).astype(F32)
    cum = jnp.dot(tri, g, preferred_element_type=F32, precision=HIGHEST)
    last = cum[c - 1:c, :]
    inter = _mm_nt(q * jnp.exp(cum), st)

    rowk = lax.broadcasted_iota(jnp.int32, (c, dk), 0)
    rm = rowk % sub
    ones = jnp.ones((dk, LANES), BF16)
    att = jnp.zeros((c, LANES), F32)
    for d in range(sub):
        if d == 0:
            p = q * k
        else:
            ks = pltpu.roll(k, d, 0)
            cs = pltpu.roll(cum, d, 0)
            p = q * ks * jnp.exp(jnp.where(rm >= d, cum - cs, NEG))
        a = jnp.dot(p.astype(BF16), ones, preferred_element_type=F32)
        att = att + jnp.where(col == row - d, a, 0.0)
    if c > sub:
        blocks = [jnp.zeros((sub, LANES), F32)]
        for i in range(1, c // sub):
            cs = cum[i * sub - 1:i * sub, :]
            qi = q[i * sub:(i + 1) * sub, :] * jnp.exp(cum[i * sub:(i + 1) * sub, :] - cs)
            kj = k * jnp.exp(jnp.where(rowk < i * sub, cs - cum, NEG))
            blocks.append(_mm_nt(qi, _pad_rows(kj, LANES)))
        att = att + jnp.concatenate(blocks, axis=0)
    vpad = _pad_rows(v, LANES)
    intra = _mm(att, vpad)
    kd = _pad_rows(k * jnp.exp(last - cum), LANES)
    st_new = st * jnp.exp(last) + _mm(vpad.T, kd)
    return inter + intra, st_new


def _rec_body(*refs, mode, c, sub, n_chunks, dk, dv, layer, has_state):
    refs = list(refs)
    if mode == "gla":
        q_ref, k_ref, v_ref, r_ref, lr_ref, wa_ref, ba_ref, hn_ref = refs[:8]
        rest = refs[8:]
    else:
        q_ref, k_ref, v_ref, r_ref, lb_ref, hn_ref = refs[:6]
        rest = refs[6:]
    if has_state:
        s0_ref, og_ref, sout_ref, st_ref = rest
    else:
        og_ref, sout_ref, st_ref = rest
    tstep = pl.program_id(2)

    @pl.when(tstep == 0)
    def _():
        if has_state:
            st_ref[...] = s0_ref[0, 0].T
        else:
            st_ref[...] = jnp.zeros(st_ref.shape, F32)

    if mode == "hgrn":
        lbx = lb_ref[...]
        e = jnp.exp(lbx - jnp.max(lbx, axis=0, keepdims=True))
        sm = e / jnp.sum(e, axis=0, keepdims=True)
        lbv = jnp.zeros((1, dk), F32)
        for li in range(1, layer + 1):
            lbv = lbv + sm[li:li + 1, :]

    for ci in range(n_chunks):
        sl = slice(ci * c, (ci + 1) * c)
        if mode == "gla":
            q = q_ref[0, sl, :] * (dk ** -0.5)
            k = k_ref[0, sl, :]
            z = _mm(lr_ref[0, sl, :], wa_ref[...]) + ba_ref[...]
            g = -(jnp.maximum(-z, 0.0) + jnp.log1p(jnp.exp(-jnp.abs(z)))) / GLA_TEMP
        else:
            qz = q_ref[0, sl, :]
            q = qz * _sigmoid(qz)
            fg = lbv + (1.0 - lbv) * _sigmoid(k_ref[0, sl, :])
            k = 1.0 - fg
            g = jnp.log(fg)
        v = v_ref[0, sl, :]
        o, st_new = _glr_chunk(q, k, v, g, st_ref[...], c, sub)
        st_ref[...] = st_new
        of = o * lax.rsqrt(jnp.mean(o * o, axis=-1, keepdims=True) + RMS_EPS) * hn_ref[...]
        gate = r_ref[0, sl, :]
        og_ref[0, sl, :] = (of * (gate * _sigmoid(gate))).astype(BF16)

    @pl.when(tstep == pl.num_programs(2) - 1)
    def _():
        sout_ref[0, 0] = st_ref[...].T


def recurrence(mode, proj, heads, dk, dv, hn, tb, c, extra, s0=None, layer=0):
    b, t, _ = proj.shape
    sub = min(SUBCHUNK, c)
    has_state = s0 is not None
    nq = (heads * dk) // dk
    if mode == "gla":
        lr, wa, ba = extra
        koff, voff, roff = heads, (2 * heads * dk) // dv, (2 * heads * dk) // dv + heads
        in_specs = [pl.BlockSpec((1, tb, dk), lambda i, h, s: (i, s, h)),
                    pl.BlockSpec((1, tb, dk), lambda i, h, s: (i, s, koff + h)),
                    pl.BlockSpec((1, tb, dv), lambda i, h, s: (i, s, voff + h)),
                    pl.BlockSpec((1, tb, dv), lambda i, h, s: (i, s, roff + h)),
                    pl.BlockSpec((1, tb, LANES), lambda i, h, s: (i, s, 0)),
                    pl.BlockSpec((LANES, dk), lambda i, h, s: (0, h)),
                    pl.BlockSpec((1, dk), lambda i, h, s: (0, h)),
                    pl.BlockSpec((1, dv), lambda i, h, s: (0, 0))]
        args = [proj, proj, proj, proj, lr, wa, ba, hn.reshape(1, dv)]
    else:
        (lb,) = extra
        in_specs = [pl.BlockSpec((1, tb, dk), lambda i, h, s: (i, s, h)),
                    pl.BlockSpec((1, tb, dk), lambda i, h, s: (i, s, heads + h)),
                    pl.BlockSpec((1, tb, dv), lambda i, h, s: (i, s, 2 * heads + h)),
                    pl.BlockSpec((1, tb, dv), lambda i, h, s: (i, s, 3 * heads + h)),
                    pl.BlockSpec((lb.shape[0], dk), lambda i, h, s: (0, h)),
                    pl.BlockSpec((1, dv), lambda i, h, s: (0, 0))]
        args = [proj, proj, proj, proj, lb, hn.reshape(1, dv)]
    if has_state:
        in_specs.append(pl.BlockSpec((1, 1, dk, dv), lambda i, h, s: (i, h, 0, 0)))
        args.append(s0)
    body = functools.partial(_rec_body, mode=mode, c=c, sub=sub, n_chunks=tb // c, dk=dk, dv=dv,
                             layer=layer, has_state=has_state)
    return pl.pallas_call(
        body, grid=(b, heads, t // tb), in_specs=in_specs,
        out_specs=[pl.BlockSpec((1, tb, dv), lambda i, h, s: (i, s, h)),
                   pl.BlockSpec((1, 1, dk, dv), lambda i, h, s: (i, h, 0, 0))],
        out_shape=[jax.ShapeDtypeStruct((b, t, heads * dv), BF16),
                   jax.ShapeDtypeStruct((b, heads, dk, dv), F32)],
        scratch_shapes=[pltpu.VMEM((dv, dk), F32)],
        compiler_params=_cp("arbitrary", "arbitrary", "arbitrary"), name="recurrence_" + mode)(*args)


def _rope_body(q_ref, ks_ref, kw_ref, cos_ref, sin_ref, qo_ref, kso_ref, kwo_ref):
    cos = cos_ref[...]
    sin = sin_ref[...]

    def rot(src, dst):
        for h in range(src.shape[1] // LANES):
            x = src[:, h * LANES:(h + 1) * LANES]
            dst[:, h * LANES:(h + 1) * LANES] = x * cos + pltpu.roll(x, LANES // 2, 1) * sin

    rot(q_ref, qo_ref)
    rot(ks_ref, kso_ref)
    rot(kw_ref, kwo_ref)


def nsa_rope(proj, cos, sin, tm, tiles_per_seq, qw, kvw):
    m = proj.shape[0]
    ks_blk = (qw + 2 * kvw) // kvw
    kw_blk = (qw + 4 * kvw) // kvw
    return pl.pallas_call(
        _rope_body, grid=(m // tm,),
        in_specs=[pl.BlockSpec((tm, qw), lambda i: (i, 0)),
                  pl.BlockSpec((tm, kvw), lambda i: (i, ks_blk)),
                  pl.BlockSpec((tm, kvw), lambda i: (i, kw_blk)),
                  pl.BlockSpec((tm, LANES), lambda i: (i % tiles_per_seq, 0)),
                  pl.BlockSpec((tm, LANES), lambda i: (i % tiles_per_seq, 0))],
        out_specs=[pl.BlockSpec((tm, qw), lambda i: (i, 0)),
                   pl.BlockSpec((tm, kvw), lambda i: (i, 0)),
                   pl.BlockSpec((tm, kvw), lambda i: (i, 0))],
        out_shape=[jax.ShapeDtypeStruct((m, qw), F32), jax.ShapeDtypeStruct((m, kvw), F32),
                   jax.ShapeDtypeStruct((m, kvw), F32)],
        compiler_params=_cp("arbitrary"), name="nsa_rope")(proj, proj, proj, cos, sin)


def _pool_rows(x, pw):
    n = x.shape[0] // NSA_BLOCK
    return jnp.sum(x.reshape(n, NSA_BLOCK, x.shape[1]) * pw[None], axis=1)


def _pool_body(kc_ref, vc_ref, pk_ref, pv_ref, ko_ref, vo_ref, *, n_cb):
    ko_ref[...] = jnp.zeros(ko_ref.shape, F32)
    vo_ref[...] = jnp.zeros(vo_ref.shape, F32)
    ko_ref[0, 0:n_cb, :] = _pool_rows(kc_ref[0, 0:n_cb * NSA_BLOCK, :], pk_ref[...])
    vo_ref[0, 0:n_cb, :] = _pool_rows(vc_ref[0, 0:n_cb * NSA_BLOCK, :], pv_ref[...])


def nsa_pool_prompt(proj3, pk, pv, qw, kvw, n_pad):
    b, t, _ = proj3.shape
    n_cb = t // NSA_BLOCK
    kc_blk = qw // kvw
    return pl.pallas_call(
        functools.partial(_pool_body, n_cb=n_cb), grid=(b,),
        in_specs=[pl.BlockSpec((1, t, kvw), lambda i: (i, 0, kc_blk)),
                  pl.BlockSpec((1, t, kvw), lambda i: (i, 0, kc_blk + 1)),
                  pl.BlockSpec((NSA_BLOCK, kvw), lambda i: (0, 0)),
                  pl.BlockSpec((NSA_BLOCK, kvw), lambda i: (0, 0))],
        out_specs=[pl.BlockSpec((1, n_pad, kvw), lambda i: (i, 0, 0)),
                   pl.BlockSpec((1, n_pad, kvw), lambda i: (i, 0, 0))],
        out_shape=[jax.ShapeDtypeStruct((b, n_pad, kvw), F32), jax.ShapeDtypeStruct((b, n_pad, kvw), F32)],
        compiler_params=_cp("arbitrary"), name="nsa_pool")(proj3, proj3, pk, pv)


def _pool_pages_body(pt_ref, *refs, pg):
    k_refs = refs[:pg]
    v_refs = refs[pg:2 * pg]
    pk_ref, pv_ref, ko_ref, vo_ref = refs[2 * pg:]
    per = k_refs[0].shape[1] // NSA_BLOCK
    for i in range(pg):
        ko_ref[0, i * per:(i + 1) * per, :] = _pool_rows(k_refs[i][0], pk_ref[...])
        vo_ref[0, i * per:(i + 1) * per, :] = _pool_rows(v_refs[i][0], pv_ref[...])


def nsa_pool_pages(pool_k, pool_v, page_table, pk, pv, pg):
    b, n_pages = page_table.shape
    _, page, kvw = pool_k.shape
    per = page // NSA_BLOCK
    page_spec = lambda i: pl.BlockSpec((1, page, kvw), lambda bi, s, pt, i=i: (pt[bi, s * pg + i], 0, 0))
    gs = pltpu.PrefetchScalarGridSpec(
        num_scalar_prefetch=1, grid=(b, n_pages // pg),
        in_specs=[page_spec(i) for i in range(pg)] + [page_spec(i) for i in range(pg)]
                 + [pl.BlockSpec((NSA_BLOCK, kvw), lambda bi, s, pt: (0, 0)),
                    pl.BlockSpec((NSA_BLOCK, kvw), lambda bi, s, pt: (0, 0))],
        out_specs=[pl.BlockSpec((1, pg * per, kvw), lambda bi, s, pt: (bi, s, 0)),
                   pl.BlockSpec((1, pg * per, kvw), lambda bi, s, pt: (bi, s, 0))])
    n_blk = n_pages * per
    return pl.pallas_call(
        functools.partial(_pool_pages_body, pg=pg), grid_spec=gs,
        out_shape=[jax.ShapeDtypeStruct((b, n_blk, kvw), F32), jax.ShapeDtypeStruct((b, n_blk, kvw), F32)],
        compiler_params=_cp("arbitrary", "arbitrary"), name="nsa_pool_pages")(
            page_table, *([pool_k] * pg), *([pool_v] * pg), pk, pv)


def _cmp_body(q_ref, kc_ref, vc_ref, o_ref, sel_ref, *, q_start, n_cb, n_blk, nb_pad):
    tq = q_ref.shape[1]
    ncp = kc_ref.shape[1]
    qt = pl.program_id(2)
    kc = kc_ref[0]
    vc = vc_ref[0]
    scale = NSA_DH ** -0.5
    colc = lax.broadcasted_iota(jnp.int32, (tq, ncp), 1)
    qposc = q_start + qt * tq + lax.broadcasted_iota(jnp.int32, (tq, ncp), 0)
    valid = ((colc + 1) * NSA_BLOCK - 1 <= qposc) & (colc < n_cb)
    imp = jnp.zeros((tq, ncp), F32)
    for r in range(NSA_GROUP):
        qr = q_ref[0, :, r * LANES:(r + 1) * LANES]
        s = jnp.where(valid, _mm_nt(qr, kc) * scale, NEG)
        m = jnp.max(s, axis=-1, keepdims=True)
        e = jnp.where(valid, jnp.exp(s - m), 0.0)
        p = e / jnp.maximum(jnp.sum(e, axis=-1, keepdims=True), 1e-30)
        o_ref[0, :, r * LANES:(r + 1) * LANES] = _mm(p, vc)
        imp = imp + p
    if nb_pad > ncp:
        imp = jnp.concatenate([imp, jnp.zeros((tq, nb_pad - ncp), F32)], axis=1)
    blk = lax.broadcasted_iota(jnp.int32, (tq, nb_pad), 1)
    qpos = q_start + qt * tq + lax.broadcasted_iota(jnp.int32, (tq, nb_pad), 0)
    cur = qpos // NSA_BLOCK
    forced = (blk == cur) | (blk == 0)
    score = jnp.where(blk > cur, -1.0, jnp.where(forced, NSA_GROUP + 1.0, imp))
    score = jnp.where(blk < n_blk, score, -2.0)
    blkf = blk.astype(F32)

    def pick(_, carry):
        sc, sel = carry
        mx = jnp.max(sc, axis=-1, keepdims=True)
        first = jnp.min(jnp.where(sc == mx, blkf, 1e9), axis=-1, keepdims=True)
        hit = blkf == first
        return jnp.where(hit, -3.0, sc), jnp.where(hit, 1.0, sel)

    _, sel = lax.fori_loop(0, min(NSA_TOP_N, n_blk), pick, (score, jnp.zeros((tq, nb_pad), F32)))
    sel_ref[0, 0] = sel


def nsa_cmp(q3, kcmp, vcmp, tq, q_start, n_cb, n_blk, nb_pad, kvw):
    b, t = q3.shape[0], q3.shape[1]
    ncp = kcmp.shape[1]
    kv = kvw // LANES
    gw = NSA_GROUP * LANES
    body = functools.partial(_cmp_body, q_start=q_start, n_cb=n_cb, n_blk=n_blk, nb_pad=nb_pad)
    return pl.pallas_call(
        body, grid=(b, kv, t // tq),
        in_specs=[pl.BlockSpec((1, tq, gw), lambda i, g, s: (i, s, g)),
                  pl.BlockSpec((1, ncp, LANES), lambda i, g, s: (i, 0, g)),
                  pl.BlockSpec((1, ncp, LANES), lambda i, g, s: (i, 0, g))],
        out_specs=[pl.BlockSpec((1, tq, gw), lambda i, g, s: (i, s, g)),
                   pl.BlockSpec((1, 1, tq, nb_pad), lambda i, g, s: (i, g, s, 0))],
        out_shape=[jax.ShapeDtypeStruct((b, t, kv * gw), F32),
                   jax.ShapeDtypeStruct((b, kv, t, nb_pad), F32)],
        compiler_params=_cp("arbitrary", "arbitrary", "arbitrary"), name="nsa_cmp")(q3, kcmp, vcmp)


def _softmax_step(carry, s, valid, vv):
    m, l, acc = carry
    s = jnp.where(valid, s, NEG)
    m_new = jnp.maximum(m, jnp.max(s, axis=-1, keepdims=True))
    p = jnp.where(valid, jnp.exp(s - m_new), 0.0)
    alpha = jnp.exp(m - m_new)
    l = alpha * l + jnp.sum(p, axis=-1, keepdims=True)
    acc = alpha * acc + _mm(p, vv)
    return m_new, l, acc


def _softmax_init(rows):
    return (jnp.full((rows, 1), NEG, F32), jnp.zeros((rows, 1), F32), jnp.zeros((rows, LANES), F32))


def _attn_body(*refs, tq, tk, n_kt, q_start, k_start, do_sel, do_win):
    refs = list(refs)
    q_ref = refs.pop(0)
    if do_sel:
        ks_ref, vs_ref, sel_ref = refs[:3]
        refs = refs[3:]
    if do_win:
        kw_ref, vw_ref = refs[:2]
        refs = refs[2:]
    outs = refs
    qt = pl.program_id(2)
    scale = NSA_DH ** -0.5
    rows = NSA_GROUP * tq
    q4 = jnp.concatenate([q_ref[0, :, r * LANES:(r + 1) * LANES] for r in range(NSA_GROUP)], axis=0).astype(BF16)
    q0 = q_start + qt * tq
    qpos = q0 + lax.broadcasted_iota(jnp.int32, (rows, tk), 0) % tq
    kcol = lax.broadcasted_iota(jnp.int32, (rows, tk), 1)

    def finish(carry, o_ref):
        m, l, acc = carry
        o = acc / jnp.maximum(l, 1e-30)
        for r in range(NSA_GROUP):
            o_ref[0, :, r * LANES:(r + 1) * LANES] = o[r * tq:(r + 1) * tq, :]

    if do_sel:
        selb = sel_ref[0, 0].astype(BF16)
        nbp = selb.shape[1]
        en = lax.broadcasted_iota(jnp.int32, (nbp, tk), 0)
        es = lax.broadcasted_iota(jnp.int32, (nbp, tk), 1)

        def sel_step(kt, carry):
            off = pl.multiple_of(kt * tk, tk)
            kk = ks_ref[0, pl.ds(off, tk), :]
            vv = vs_ref[0, pl.ds(off, tk), :]
            s = _mm_nt(q4, kk) * scale
            kp0 = k_start + kt * tk
            expand = ((kp0 + es) // NSA_BLOCK == en).astype(BF16)
            chosen = jnp.dot(selb, expand, preferred_element_type=F32)
            chosen = jnp.concatenate([chosen] * NSA_GROUP, axis=0)
            valid = (chosen > 0.5) & (kp0 + kcol <= qpos)
            return _softmax_step(carry, s, valid, vv)

        hi = jnp.minimum(n_kt, (q0 + tq - 1 - k_start) // tk + 1)
        finish(lax.fori_loop(0, hi, sel_step, _softmax_init(rows)), outs.pop(0))

    if do_win:
        def win_step(kt, carry):
            off = pl.multiple_of(kt * tk, tk)
            kk = kw_ref[0, pl.ds(off, tk), :]
            vv = vw_ref[0, pl.ds(off, tk), :]
            s = _mm_nt(q4, kk) * scale
            kpos = k_start + kt * tk + kcol
            dist = qpos - kpos
            valid = (dist >= 0) & (dist < NSA_WINDOW) & (kpos >= 0)
            return _softmax_step(carry, s, valid, vv)

        lo = jnp.maximum(0, (q0 - (NSA_WINDOW - 1) - k_start) // tk)
        hi = jnp.minimum(n_kt, (q0 + tq - 1 - k_start) // tk + 1)
        finish(lax.fori_loop(lo, hi, win_step, _softmax_init(rows)), outs.pop(0))


def nsa_attend(q_rot3, tq, tk, q_start, k_start, sel_args=None, win_args=None):
    b, t, qw = q_rot3.shape
    kv = qw // (NSA_GROUP * LANES)
    gw = NSA_GROUP * LANES
    in_specs = [pl.BlockSpec((1, tq, gw), lambda i, g, s: (i, s, g))]
    args = [q_rot3]
    n_out = 0
    t_k = None
    if sel_args is not None:
        k, v, sel = sel_args
        t_k = k.shape[1]
        nbp = sel.shape[-1]
        in_specs += [pl.BlockSpec((1, t_k, LANES), lambda i, g, s: (i, 0, g)),
                     pl.BlockSpec((1, t_k, LANES), lambda i, g, s: (i, 0, g)),
                     pl.BlockSpec((1, 1, tq, nbp), lambda i, g, s: (i, g, s, 0))]
        args += [k, v, sel]
        n_out += 1
    if win_args is not None:
        k, v = win_args
        t_k = k.shape[1]
        in_specs += [pl.BlockSpec((1, t_k, LANES), lambda i, g, s: (i, 0, g)),
                     pl.BlockSpec((1, t_k, LANES), lambda i, g, s: (i, 0, g))]
        args += [k, v]
        n_out += 1
    body = functools.partial(_attn_body, tq=tq, tk=tk, n_kt=t_k // tk, q_start=q_start, k_start=k_start,
                             do_sel=sel_args is not None, do_win=win_args is not None)
    return pl.pallas_call(
        body, grid=(b, kv, t // tq), in_specs=in_specs,
        out_specs=[pl.BlockSpec((1, tq, gw), lambda i, g, s: (i, s, g))] * n_out,
        out_shape=[jax.ShapeDtypeStruct((b, t, qw), F32)] * n_out,
        compiler_params=_cp("arbitrary", "arbitrary", "arbitrary"), name="nsa_attend")(*args)


def _attn_t_body(q_ref, ks_ref, vs_ref, kw_ref, vw_ref, sel_ref, osel_ref, owin_ref, vst_ref, vwt_ref, *, tq, tk, n_kt):
    qt = pl.program_id(2)
    scale = NSA_DH ** -0.5
    t_k = ks_ref.shape[1]

    @pl.when(qt == 0)
    def _():
        for j in range(t_k // LANES):
            sl = slice(j * LANES, (j + 1) * LANES)
            vst_ref[:, sl] = vs_ref[0, sl, :].T.astype(BF16)
            vwt_ref[:, sl] = vw_ref[0, sl, :].T.astype(BF16)

    q0 = qt * tq
    cols = NSA_GROUP * tq
    sel_t = sel_ref[0, 0].T.astype(BF16)
    nbp = sel_t.shape[0]
    q4 = jnp.concatenate([q_ref[0, :, r * LANES:(r + 1) * LANES] for r in range(NSA_GROUP)], axis=0).astype(BF16)

    def scores(k_ref, off, n, allowed):
        allowed = jnp.concatenate([allowed] * NSA_GROUP, axis=1) > 0.5
        s = _mm_nt(k_ref[0, pl.ds(off, n), :], q4) * scale
        return jnp.where(allowed, s, NEG), allowed

    def write(o_ref, acc, l):
        o = acc / jnp.maximum(l, 1e-30)
        for r in range(NSA_GROUP):
            o_ref[0, :, r * LANES:(r + 1) * LANES] = o[:, r * tq:(r + 1) * tq].T

    kpos_l = lax.broadcasted_iota(jnp.int32, (tk, tq), 0)
    qpos = q0 + lax.broadcasted_iota(jnp.int32, (tk, tq), 1)
    en = lax.broadcasted_iota(jnp.int32, (tk, nbp), 1)
    es = lax.broadcasted_iota(jnp.int32, (tk, nbp), 0)

    def sel_step(kt, carry):
        m, l, acc = carry
        off = pl.multiple_of(kt * tk, tk)
        expand = ((kt * tk + es) // NSA_BLOCK == en).astype(BF16)
        chosen = jnp.dot(expand, sel_t, preferred_element_type=F32)
        allowed = jnp.where(kt * tk + kpos_l <= qpos, chosen, 0.0)
        s, valid = scores(ks_ref, off, tk, allowed)
        m_new = jnp.maximum(m, jnp.max(s, axis=0, keepdims=True))
        p = jnp.where(valid, jnp.exp(s - m_new), 0.0)
        alpha = jnp.exp(m - m_new)
        l = alpha * l + jnp.sum(p, axis=0, keepdims=True)
        acc = alpha * acc + jnp.dot(vst_ref[:, pl.ds(off, tk)], p.astype(BF16), preferred_element_type=F32)
        return m_new, l, acc

    init = (jnp.full((1, cols), NEG, F32), jnp.zeros((1, cols), F32), jnp.zeros((NSA_DH, cols), F32))
    m, l, acc = lax.fori_loop(0, jnp.minimum(n_kt, (q0 + tq - 1) // tk + 1), sel_step, init)
    write(osel_ref, acc, l)

    wk = min(t_k, NSA_WINDOW + tq)
    ws = pl.multiple_of(jnp.clip(q0 - NSA_WINDOW, 0, t_k - wk), LANES)
    dist = (q0 + lax.broadcasted_iota(jnp.int32, (wk, tq), 1)) - (ws + lax.broadcasted_iota(jnp.int32, (wk, tq), 0))
    s, valid = scores(kw_ref, ws, wk, jnp.where((dist >= 0) & (dist < NSA_WINDOW), 1.0, 0.0))
    p = jnp.where(valid, jnp.exp(s - jnp.max(s, axis=0, keepdims=True)), 0.0)
    acc = jnp.dot(vwt_ref[:, pl.ds(ws, wk)], p.astype(BF16), preferred_element_type=F32)
    write(owin_ref, acc, jnp.sum(p, axis=0, keepdims=True))


def nsa_attend_prompt(q_rot3, ks, vs, kw, vw, sel, tq, tk):
    b, t, qw = q_rot3.shape
    kv = qw // (NSA_GROUP * LANES)
    gw = NSA_GROUP * LANES
    kspec = pl.BlockSpec((1, t, LANES), lambda i, g, s: (i, 0, g))
    ospec = pl.BlockSpec((1, tq, gw), lambda i, g, s: (i, s, g))
    body = functools.partial(_attn_t_body, tq=tq, tk=tk, n_kt=t // tk)
    return pl.pallas_call(
        body, grid=(b, kv, t // tq),
        in_specs=[ospec, kspec, kspec, kspec, kspec,
                  pl.BlockSpec((1, 1, tq, sel.shape[-1]), lambda i, g, s: (i, g, s, 0))],
        out_specs=[ospec, ospec],
        out_shape=[jax.ShapeDtypeStruct((b, t, qw), F32)] * 2,
        scratch_shapes=[pltpu.VMEM((NSA_DH, t), BF16), pltpu.VMEM((NSA_DH, t), BF16)],
        compiler_params=_cp("arbitrary", "arbitrary", "arbitrary"), name="nsa_attend_prompt")(
            q_rot3, ks, vs, kw, vw, sel)


def _paged_sel_body(pt_ref, *refs, pg, past_len, t_new):
    k_refs = refs[:pg]
    v_refs = refs[pg:2 * pg]
    q_ref, sel_ref, kn_ref, vn_ref, o_ref, m_ref, l_ref, acc_ref = refs[2 * pg:]
    step = pl.program_id(1)
    page = k_refs[0].shape[1]
    rows = q_ref.shape[1]
    kv = kn_ref.shape[2] // LANES
    scale = NSA_DH ** -0.5
    nbp = sel_ref.shape[-1]
    qb = q_ref[0].astype(BF16)
    selb = sel_ref[0].astype(BF16)

    @pl.when(step == 0)
    def _():
        m_ref[...] = jnp.full(m_ref.shape, NEG, F32)
        l_ref[...] = jnp.zeros(l_ref.shape, F32)
        acc_ref[...] = jnp.zeros(acc_ref.shape, F32)

    def update(ks, vs, kp0):
        n = len(ks) * page
        tloc = lax.broadcasted_iota(jnp.int32, (rows, n), 0) % t_new
        kcol = lax.broadcasted_iota(jnp.int32, (rows, n), 1)
        en = lax.broadcasted_iota(jnp.int32, (nbp, n), 0)
        es = lax.broadcasted_iota(jnp.int32, (nbp, n), 1)
        s = jnp.concatenate([_mm_nt(qb, k) for k in ks], axis=1) * scale
        expand = ((kp0 + es) // NSA_BLOCK == en).astype(BF16)
        chosen = jnp.dot(selb, expand, preferred_element_type=F32)
        valid = (chosen > 0.5) & (kp0 + kcol <= past_len + tloc)
        s = jnp.where(valid, s, NEG)
        m = m_ref[:, 0:1]
        m_new = jnp.maximum(m, jnp.max(s, axis=-1, keepdims=True))
        p = jnp.where(valid, jnp.exp(s - m_new), 0.0)
        alpha = jnp.exp(m - m_new)
        pv = _mm(p[:, 0:page], vs[0])
        for i in range(1, len(ks)):
            pv = pv + _mm(p[:, i * page:(i + 1) * page], vs[i])
        m_ref[...] = jnp.broadcast_to(m_new, m_ref.shape)
        l_ref[...] = jnp.broadcast_to(alpha * l_ref[:, 0:1] + jnp.sum(p, axis=-1, keepdims=True), l_ref.shape)
        acc_ref[...] = alpha * acc_ref[...] + pv

    update([r[0] for r in k_refs], [r[0] for r in v_refs], step * (pg * page))

    @pl.when(step == pl.num_programs(1) - 1)
    def _():
        update([kn_ref[0]], [vn_ref[0]], past_len)
        o = acc_ref[...] / jnp.maximum(l_ref[:, 0:1], 1e-30)
        per = rows // kv
        for g in range(kv):
            o_ref[0, g * per:(g + 1) * per, :] = o[g * per:(g + 1) * per, g * LANES:(g + 1) * LANES]


def nsa_paged_sel(q_rot3, sel, pool_k, pool_v, page_table, k_new, v_new, pg):
    b, t_new, qw = q_rot3.shape
    n_pages = page_table.shape[1]
    _, page, kvw = pool_k.shape
    kv = kvw // LANES
    nbp = sel.shape[-1]
    per = NSA_GROUP * t_new
    rows = kv * per
    qg = q_rot3.reshape(b, t_new, kv, NSA_GROUP, NSA_DH).transpose(0, 2, 3, 1, 4).reshape(b, kv, per, 1, NSA_DH)
    q_rows = (qg * jnp.eye(kv, dtype=F32)[None, :, None, :, None]).reshape(b, rows, kvw)
    sel_rows = jnp.broadcast_to(sel[:, :, None], (b, kv, NSA_GROUP, t_new, nbp)).reshape(b, rows, nbp)
    page_spec = lambda i: pl.BlockSpec((1, page, kvw), lambda bi, s, pt, i=i: (pt[bi, s * pg + i], 0, 0))
    gs = pltpu.PrefetchScalarGridSpec(
        num_scalar_prefetch=1, grid=(b, n_pages // pg),
        in_specs=[page_spec(i) for i in range(pg)] + [page_spec(i) for i in range(pg)]
                 + [pl.BlockSpec((1, rows, kvw), lambda bi, s, pt: (bi, 0, 0)),
                    pl.BlockSpec((1, rows, nbp), lambda bi, s, pt: (bi, 0, 0)),
                    pl.BlockSpec((1, page, kvw), lambda bi, s, pt: (bi, 0, 0)),
                    pl.BlockSpec((1, page, kvw), lambda bi, s, pt: (bi, 0, 0))],
        out_specs=pl.BlockSpec((1, rows, NSA_DH), lambda bi, s, pt: (bi, 0, 0)),
        scratch_shapes=[pltpu.VMEM((rows, LANES), F32), pltpu.VMEM((rows, LANES), F32),
                        pltpu.VMEM((rows, kvw), F32)])
    body = functools.partial(_paged_sel_body, pg=pg, past_len=n_pages * page, t_new=t_new)
    o_rows = pl.pallas_call(
        body, grid_spec=gs, out_shape=jax.ShapeDtypeStruct((b, rows, NSA_DH), F32),
        compiler_params=_cp("arbitrary", "arbitrary"), name="nsa_paged_sel")(
            page_table, *([pool_k] * pg), *([pool_v] * pg), q_rows, sel_rows, k_new, v_new)
    return o_rows.reshape(b, kv, NSA_GROUP, t_new, NSA_DH).transpose(0, 3, 1, 2, 4).reshape(b, t_new, qw)


def _combine_body(oc_ref, os_ref, ow_ref, gt_ref, a_ref):
    gs = _sigmoid(gt_ref[...])
    for hh in range(oc_ref.shape[1] // LANES):
        sl = slice(hh * LANES, (hh + 1) * LANES)
        a = (gs[:, 3 * hh:3 * hh + 1] * oc_ref[:, sl] + gs[:, 3 * hh + 1:3 * hh + 2] * os_ref[:, sl]
             + gs[:, 3 * hh + 2:3 * hh + 3] * ow_ref[:, sl])
        a_ref[:, sl] = a.astype(BF16)


def nsa_combine(o_cmp, o_sel, o_win, gates, tm):
    m, qw = o_cmp.shape
    spec = pl.BlockSpec((tm, qw), lambda i: (i, 0))
    return pl.pallas_call(
        _combine_body, grid=(m // tm,),
        in_specs=[spec, spec, spec, pl.BlockSpec((tm, LANES), lambda i: (i, 0))],
        out_specs=spec, out_shape=jax.ShapeDtypeStruct((m, qw), BF16),
        compiler_params=_cp("arbitrary"), name="nsa_combine")(o_cmp, o_sel, o_win, gates)


def _s5_body(*refs, seg, seq_len, has_state):
    if has_state:
        (u_ref, a1_ref, a2_ref, dt_ref, b1_ref, b2_ref, cm_ref, d_ref, s0_ref,
         z_ref, st_ref, x_ref, y_ref, up_ref) = refs
    else:
        (u_ref, a1_ref, a2_ref, dt_ref, b1_ref, b2_ref, cm_ref, d_ref,
         z_ref, st_ref, x_ref, y_ref, up_ref) = refs
    gq = pl.program_id(1)
    m = u_ref.shape[0]
    nseg = m // seg
    nb = m // seq_len
    half = LANES // 2
    lane = lax.broadcasted_iota(jnp.int32, (1, LANES), 1)
    sgn = jnp.where(lane < half, -1.0, 1.0)

    are, aim, dt = a1_ref[0], a2_ref[0], jnp.exp(dt_ref[0])
    er = jnp.exp(are * dt)
    abr, abi = er * jnp.cos(aim * dt), er * jnp.sin(aim * dt)
    nr, ni, den = abr - 1.0, abi, are * are + aim * aim
    cr, ci = (nr * are + ni * aim) / den, (ni * are - nr * aim) / den
    bcat = cr * b1_ref[0] + ci * b2_ref[0]

    def cmul(x, pr, pi):
        return x * pr + pltpu.roll(x, half, 1) * (pi * sgn)

    pb = SUBLANES * seg
    nblk = m // pb
    ri = lax.broadcasted_iota(jnp.int32, (pb, pb), 0)
    ci = lax.broadcasted_iota(jnp.int32, (pb, pb), 1)

    @pl.when(gq == 0)
    def _():
        perm = (ci == (ri % SUBLANES) * seg + ri // SUBLANES).astype(BF16)
        for k in range(nblk):
            uk = u_ref[k * pb:(k + 1) * pb, :].astype(BF16)
            up_ref[k] = jnp.dot(perm, uk, preferred_element_type=F32).astype(BF16)

    x_ref[...] = jnp.dot(up_ref[...].reshape(m, LANES), bcat.astype(BF16),
                         preferred_element_type=F32).reshape(nblk, pb, LANES)
    x = jnp.zeros((nseg, LANES), F32)
    for s in range(seg):
        sl = slice(s * SUBLANES, (s + 1) * SUBLANES)
        x = cmul(x, abr, abi) + x_ref[:, sl, :].reshape(nseg, LANES)
        x_ref[:, sl, :] = x.reshape(nblk, SUBLANES, LANES)
    if has_state:
        carry = s0_ref[0]
    else:
        spb = seq_len // seg
        pr, pi = abr, abi
        for _ in range(int(math.log2(seg))):
            pr, pi = pr * pr - pi * pi, 2.0 * pr * pi
        rown = lax.broadcasted_iota(jnp.int32, (nseg, LANES), 0) % spb
        inc = x
        sh = 1
        while sh < spb:
            inc = inc + jnp.where(rown >= sh, cmul(pltpu.roll(inc, sh, 0), pr, pi), 0.0)
            pr, pi = pr * pr - pi * pi, 2.0 * pr * pi
            sh *= 2
        carry = jnp.where(rown >= 1, pltpu.roll(inc, 1, 0), 0.0)
    pr, pi = abr, abi
    for s in range(seg):
        sl = slice(s * SUBLANES, (s + 1) * SUBLANES)
        x_ref[:, sl, :] = x_ref[:, sl, :] + cmul(carry, pr, pi).reshape(nblk, SUBLANES, LANES)
        pr, pi = pr * abr - pi * abi, pr * abi + pi * abr
    finals = []
    for bi in range(nb):
        last_seg = (bi + 1) * (seq_len // seg) - 1
        row = (seg - 1) * SUBLANES + last_seg % SUBLANES
        finals.append(x_ref[last_seg // SUBLANES, row:row + 1, :])
    st_ref[0] = jnp.concatenate(finals, axis=0)

    yg = _mm(x_ref[...].reshape(m, LANES), cm_ref[0])

    @pl.when(gq == 0)
    def _():
        y_ref[...] = yg

    @pl.when(gq > 0)
    def _():
        y_ref[...] = y_ref[...] + yg

    @pl.when(gq == pl.num_programs(1) - 1)
    def _():
        unperm = (ri == (ci % SUBLANES) * seg + ci // SUBLANES).astype(BF16)
        for k in range(nblk):
            rows = slice(k * pb, (k + 1) * pb)
            yk = y_ref[rows, :]
            hi = yk.astype(BF16)
            lo = (yk - hi.astype(F32)).astype(BF16)
            y = (jnp.dot(unperm, hi, preferred_element_type=F32) + jnp.dot(unperm, lo, preferred_element_type=F32)
                 + d_ref[...] * u_ref[rows, :])
            z = 0.5 * y * (1.0 + jnp.tanh(math.sqrt(2.0 / math.pi) * (y + 0.044715 * (y * y * y))))
            z_ref[rows, :] = z.astype(BF16)


def s5_scan(u, prm, seq_len, seg, s0=None):
    m, d = u.shape
    a1, a2, dtb, b1, b2, cm, dsk = prm
    groups = a1.shape[0]
    per_tile = LANES // S5_CH
    nb = m // seq_len
    has_state = s0 is not None
    gidx = lambda j, q: (j * per_tile + q, 0, 0)
    in_specs = [pl.BlockSpec((m, LANES), lambda j, q: (0, j)),
                pl.BlockSpec((1, 1, LANES), gidx), pl.BlockSpec((1, 1, LANES), gidx), pl.BlockSpec((1, 1, LANES), gidx),
                pl.BlockSpec((1, LANES, LANES), gidx), pl.BlockSpec((1, LANES, LANES), gidx),
                pl.BlockSpec((1, LANES, LANES), gidx),
                pl.BlockSpec((1, LANES), lambda j, q: (0, j))]
    args = [u, a1, a2, dtb, b1, b2, cm, dsk]
    if has_state:
        in_specs.append(pl.BlockSpec((1, nb, LANES), gidx))
        args.append(s0)
    body = functools.partial(_s5_body, seg=seg, seq_len=seq_len, has_state=has_state)
    return pl.pallas_call(
        body, grid=(d // LANES, per_tile), in_specs=in_specs,
        out_specs=[pl.BlockSpec((m, LANES), lambda j, q: (0, j)),
                   pl.BlockSpec((1, nb, LANES), gidx)],
        out_shape=[jax.ShapeDtypeStruct((m, d), BF16), jax.ShapeDtypeStruct((groups, nb, LANES), F32)],
        scratch_shapes=[pltpu.VMEM((m // (SUBLANES * seg), SUBLANES * seg, LANES), F32),
                        pltpu.VMEM((m, LANES), F32),
                        pltpu.VMEM((m // (SUBLANES * seg), SUBLANES * seg, LANES), BF16)],
        compiler_params=_cp("arbitrary", "arbitrary"), name="s5_scan")(*args)


def _s5_params(a_re, a_im, log_dt, b_re, b_im, c_re, c_im, d_skip):
    groups, p = a_re.shape
    per_tile = LANES // S5_CH
    dup = lambda a: jnp.concatenate([a, a], axis=-1)[:, None, :]
    a1, a2 = dup(a_re), dup(a_im)
    dtb = jnp.broadcast_to(log_dt[:, None, None], (groups, 1, LANES))
    slot = jax.nn.one_hot(jnp.arange(groups) % per_tile, per_tile, dtype=F32)

    def rows_in_tile(w):
        return (slot[:, :, None, None] * w[:, None]).reshape(groups, LANES, w.shape[-1])

    bre_t, bim_t = b_re.transpose(0, 2, 1), b_im.transpose(0, 2, 1)
    b1 = rows_in_tile(jnp.concatenate([bre_t, bim_t], axis=-1))
    b2 = rows_in_tile(jnp.concatenate([-bim_t, bre_t], axis=-1))
    cmat = jnp.concatenate([c_re, -c_im], axis=-1)
    cm = rows_in_tile(cmat).transpose(0, 2, 1)
    return a1, a2, dtb, b1, b2, cm, d_skip.reshape(1, -1)


def _tiles(m):
    if m >= 1024:
        return 1024, 512
    return m, m


def _gla_layer(h, b, t, gain, w_in, w_tail, w_alpha_pad, b_alpha, head_norm, w_out, s0, heads, dk, dv):
    tm, tmo = _tiles(h.shape[0])
    n_main = 2 * heads * dk + 2 * heads * dv
    proj = norm_proj(h, gain, w_in, n_main, tm, 512)
    lr = norm_proj(h, gain, w_tail, LANES, tm, LANES)
    c = math.gcd(t, CHUNK)
    tb = math.gcd(t, 256)
    og, st = recurrence("gla", proj.reshape(b, t, n_main), heads, dk, dv, head_norm, tb, c,
                        (lr.reshape(b, t, LANES), w_alpha_pad, b_alpha.reshape(1, -1)), s0)
    return out_proj(og.reshape(b * t, heads * dv), w_out, h, tmo, 512), st


def _hgrn_layer(h, b, t, gain, w_in, lower_bound, layer, head_norm, w_out, s0, heads, dk):
    tm, tmo = _tiles(h.shape[0])
    n = 4 * heads * dk
    proj = norm_proj(h, gain, w_in, n, tm, 512)
    c = math.gcd(t, CHUNK)
    tb = math.gcd(t, 256)
    og, st = recurrence("hgrn", proj.reshape(b, t, n), heads, dk, dk, head_norm, tb, c,
                        (lower_bound,), s0, layer=layer)
    return out_proj(og.reshape(b * t, heads * dk), w_out, h, tmo, 512), st


def _rope_tables(start, t):
    half = NSA_DH // 2
    inv = ROPE_THETA ** (-jnp.arange(half, dtype=F32) / half)
    ang = (start + jnp.arange(t, dtype=jnp.int32)).astype(F32)[:, None] * inv[None, :]
    cos, sin = jnp.cos(ang), jnp.sin(ang)
    return jnp.concatenate([cos, cos], axis=-1), jnp.concatenate([-sin, sin], axis=-1)


def _nsa_layer(h, b, t, start, gain, w_in, w_gates, pool_k, pool_v, w_out, past, heads, kv):
    tm, tmo = _tiles(h.shape[0])
    qw, kvw = heads * NSA_DH, kv * NSA_DH
    n_main = qw + 6 * kvw
    proj = norm_proj(h, gain, w_in, n_main, tm, 512)
    gates = norm_proj(h, gain, w_gates, LANES, tm, LANES)
    cos, sin = _rope_tables(start, t)
    pk = jnp.broadcast_to(pool_k[:, None], (NSA_BLOCK, kvw))
    pv = jnp.broadcast_to(pool_v[:, None], (NSA_BLOCK, kvw))
    proj3 = proj.reshape(b, t, n_main)
    col = lambda i: proj3[:, :, qw + i * kvw:qw + (i + 1) * kvw]
    kc, vc, vs, vw = col(0), col(1), col(3), col(5)

    if past is None:
        trope = min(256, t)
        q_rot, ks, kw = nsa_rope(proj, cos, sin, trope, t // trope, qw, kvw)
        q_rot3, ks3, kw3 = q_rot.reshape(b, t, qw), ks.reshape(b, t, kvw), kw.reshape(b, t, kvw)
        n_cb = t // NSA_BLOCK
        n_blk = -(-t // NSA_BLOCK)
        kcmp, vcmp = nsa_pool_prompt(proj3, pk, pv, qw, kvw, LANES)
        o_cmp, sel = nsa_cmp(proj3, kcmp, vcmp, min(t, 512), 0, n_cb, n_blk, LANES, kvw)
        tq = min(t, 128)
        o_sel, o_win = nsa_attend_prompt(q_rot3, ks3, vs, kw3, vw, sel, tq, min(t, 512))
        keep = min(NSA_WINDOW, t)
        win_k, win_v = kw3[:, t - keep:], vw[:, t - keep:]
    else:
        pool_ck, pool_cv, pool_sk, pool_sv, page_table, prev_kw, prev_vw = past
        n_pages = page_table.shape[1]
        page = pool_ck.shape[1]
        past_len = n_pages * page
        cos_r, sin_r = jnp.tile(cos, (b, 1)), jnp.tile(sin, (b, 1))
        q_rot, ks, kw = nsa_rope(proj, cos_r, sin_r, b * t, 1, qw, kvw)
        q_rot3, ks3, kw3 = q_rot.reshape(b, t, qw), ks.reshape(b, t, kvw), kw.reshape(b, t, kvw)
        flat = lambda p_: p_.reshape(p_.shape[0], page, kvw)
        kcmp, vcmp = nsa_pool_pages(flat(pool_ck), flat(pool_cv), page_table, pk, pv, 4)
        total = past_len + t
        n_cb = total // NSA_BLOCK
        n_blk = -(-total // NSA_BLOCK)
        nb_pad = -(-n_blk // LANES) * LANES
        o_cmp, sel = nsa_cmp(proj3, kcmp, vcmp, t, past_len, n_cb, n_blk, nb_pad, kvw)
        padp = lambda a: jnp.concatenate([a, jnp.zeros((b, page - t, kvw), F32)], axis=1)
        o_sel = nsa_paged_sel(q_rot3, sel, flat(pool_sk), flat(pool_sv), page_table, padp(ks3), padp(vs), 8)
        keep = prev_kw.shape[1]
        kw_ext = jnp.concatenate([prev_kw.reshape(b, keep, kvw), kw3], axis=1)
        vw_ext = jnp.concatenate([prev_vw.reshape(b, keep, kvw), vw], axis=1)
        t_ext = keep + t
        t_pad = -(-t_ext // LANES) * LANES
        pade = lambda a: jnp.concatenate([a, jnp.zeros((b, t_pad - t_ext, kvw), F32)], axis=1)
        (o_win,) = nsa_attend(q_rot3, t, LANES, past_len, past_len - keep, win_args=(pade(kw_ext), pade(vw_ext)))
        win_k, win_v = kw_ext[:, t_ext - keep:], vw_ext[:, t_ext - keep:]

    a = nsa_combine(o_cmp.reshape(b * t, qw), o_sel.reshape(b * t, qw), o_win.reshape(b * t, qw), gates,
                    min(b * t, 512))
    y = out_proj(a, w_out, h, tmo, 512)
    shp = lambda x, n: x.reshape(b, n, kv, NSA_DH)
    return y, (shp(kc, t), shp(vc, t), shp(ks3, t), shp(vs, t), shp(win_k, keep), shp(win_v, keep))


def _s5_layer(h, b, t, gain, prm, w_glu, s_re, s_im):
    tm, tmo = _tiles(h.shape[0])
    u = rmsnorm_rows(h, gain, min(h.shape[0], 512))
    groups = prm[0].shape[0]
    if s_re is None:
        z, st = s5_scan(u, prm, t, math.gcd(t, 32))
    else:
        s0 = jnp.concatenate([s_re, s_im], axis=-1).transpose(1, 0, 2)
        z, st = s5_scan(u, prm, t, t, s0)
    y = out_glu(z, w_glu, h, tmo, 512)
    st = st.transpose(1, 0, 2)
    return y, (st[..., :S5_STATE], st[..., S5_STATE:])


def _ffn_layer(h, b, t, gain, w_in, conv_w, conv_b, w_out, buf):
    tm, tmo = _tiles(h.shape[0])
    ff2 = w_in.shape[1]
    if buf is None:
        tm = min(512, t)
        act, tg, tv = ffn_in(h, gain, w_in, conv_w, conv_b, tm, 512, t)
        per = t // tm
        last = lambda a: a[per - 1::per, SUBLANES - (CONV_W - 1):, :]
        state = jnp.concatenate([last(tg), last(tv)], axis=-1)
    else:
        zrow = jnp.zeros((b, t - 1, ff2), F32)
        p1 = jnp.concatenate([buf[:, 1:2], zrow], axis=1).reshape(b * t, ff2)
        p2 = jnp.concatenate([buf, zrow[:, 1:]], axis=1).reshape(b * t, ff2)
        act, tg, tv = ffn_in(h, gain, w_in, conv_w, conv_b, b * t, 512, t, hist=(p1, p2))
        up = jnp.concatenate([tg[0], tv[0]], axis=-1).reshape(b, t, ff2)
        state = jnp.concatenate([buf, up], axis=1)[:, t:]
    return out_proj(act, w_out, h, tmo, 512), state


def kernel(x_prompt, x_sample, state_gla, state_hgrn, cache_nsa_cmp_k, cache_nsa_cmp_v, cache_nsa_sel_k, cache_nsa_sel_v, cache_nsa_win_k, cache_nsa_win_v, state_s5_re, state_s5_im, state_ffn_conv, page_table, norm_mix, norm_ffn, final_norm, gla_w_in, gla_w_alpha, gla_b_alpha, gla_head_norm, gla_w_out, hgrn_w_in, hgrn_lower_bound, hgrn_head_norm, hgrn_w_out, nsa_w_in, nsa_pool_k, nsa_pool_v, nsa_w_out, s5_a_re, s5_a_im, s5_log_dt, s5_b_re, s5_b_im, s5_c_re, s5_c_im, s5_d, s5_w_glu, ffn_w_in, ffn_conv_w, ffn_conv_b, ffn_w_out):
    bp, tp, d = x_prompt.shape
    bs, ts, _ = x_sample.shape
    depth = norm_mix.shape[0]
    n_mixers = 4
    gla_heads, gla_dk, gla_dv = state_gla.shape[2], state_gla.shape[3], state_gla.shape[4]
    hgrn_heads, hgrn_dk = state_hgrn.shape[2], state_hgrn.shape[3]
    nsa_kv = cache_nsa_cmp_k.shape[3]
    nsa_heads = d // NSA_DH
    hp = x_prompt.reshape(bp * tp, d)
    hs = x_sample.reshape(bs * ts, d)
    bf = lambda w: w.astype(BF16)

    def pad_cols(w, n):
        return jnp.concatenate([w, jnp.zeros((w.shape[0], n - w.shape[1]), w.dtype)], axis=1)

    outs = {k: [] for k in ("gla_p", "gla_s", "hgrn_p", "hgrn_s", "nsa_p", "nsa_s", "s5_p", "s5_s", "conv_p", "conv_s")}
    for i in range(depth):
        kind, j = i % n_mixers, i // n_mixers
        if kind == 0:
            n_main = 2 * gla_heads * gla_dk + 2 * gla_heads * gla_dv
            w_in = cast_bf16(gla_w_in, j)
            w_tail = bf(pad_cols(gla_w_in[j, :, n_main:], LANES))
            rank = gla_w_alpha.shape[1]
            wa = bf(jnp.concatenate([gla_w_alpha[j], jnp.zeros((LANES - rank, gla_w_alpha.shape[2]), F32)], axis=0))
            common = (norm_mix[i], w_in, w_tail, wa, gla_b_alpha[j], gla_head_norm[j], cast_bf16(gla_w_out, j))
            hp, st_p = _gla_layer(hp, bp, tp, *common, None, gla_heads, gla_dk, gla_dv)
            hs, st_s = _gla_layer(hs, bs, ts, *common, state_gla[j], gla_heads, gla_dk, gla_dv)
            outs["gla_p"].append(st_p)
            outs["gla_s"].append(st_s)
        elif kind == 1:
            common = (norm_mix[i], cast_bf16(hgrn_w_in, j), hgrn_lower_bound, i, hgrn_head_norm[j],
                      cast_bf16(hgrn_w_out, j))
            hp, st_p = _hgrn_layer(hp, bp, tp, *common, None, hgrn_heads, hgrn_dk)
            hs, st_s = _hgrn_layer(hs, bs, ts, *common, state_hgrn[j], hgrn_heads, hgrn_dk)
            outs["hgrn_p"].append(st_p)
            outs["hgrn_s"].append(st_s)
        elif kind == 2:
            n_main = nsa_heads * NSA_DH + 6 * nsa_kv * NSA_DH
            w_in = cast_bf16(nsa_w_in, j)
            w_gates = bf(pad_cols(nsa_w_in[j, :, n_main:], LANES))
            common = (norm_mix[i], w_in, w_gates, nsa_pool_k[j], nsa_pool_v[j], cast_bf16(nsa_w_out, j))
            hp, st_p = _nsa_layer(hp, bp, tp, 0, *common, None, nsa_heads, nsa_kv)
            past = (cache_nsa_cmp_k[j], cache_nsa_cmp_v[j], cache_nsa_sel_k[j], cache_nsa_sel_v[j],
                    page_table, cache_nsa_win_k[j], cache_nsa_win_v[j])
            hs, st_s = _nsa_layer(hs, bs, ts, page_table.shape[1] * cache_nsa_cmp_k.shape[2], *common, past,
                                  nsa_heads, nsa_kv)
            outs["nsa_p"].append(st_p)
            outs["nsa_s"].append(st_s)
        else:
            prm = _s5_params(s5_a_re[j], s5_a_im[j], s5_log_dt[j], s5_b_re[j], s5_b_im[j], s5_c_re[j],
                             s5_c_im[j], s5_d[j])
            w_glu = cast_bf16(s5_w_glu, j)
            hp, st_p = _s5_layer(hp, bp, tp, norm_mix[i], prm, w_glu, None, None)
            hs, st_s = _s5_layer(hs, bs, ts, norm_mix[i], prm, w_glu, state_s5_re[j], state_s5_im[j])
            outs["s5_p"].append(st_p)
            outs["s5_s"].append(st_s)
        fw = (norm_ffn[i], cast_bf16(ffn_w_in, i), ffn_conv_w[i], ffn_conv_b[i], cast_bf16(ffn_w_out, i))
        hp, cb_p = _ffn_layer(hp, bp, tp, *fw, None)
        hs, cb_s = _ffn_layer(hs, bs, ts, *fw, state_ffn_conv[i])
        outs["conv_p"].append(cb_p)
        outs["conv_s"].append(cb_s)

    y_prompt = rmsnorm_rows(hp, final_norm, min(hp.shape[0], 512)).reshape(bp, tp, d)
    y_sample = rmsnorm_rows(hs, final_norm, min(hs.shape[0], 512)).reshape(bs, ts, d)
    stack = lambda xs: jnp.stack(xs)
    pick = lambda key, r: stack([e[r] for e in outs[key]])
    res = [y_prompt, y_sample, stack(outs["gla_p"]), stack(outs["gla_s"]), stack(outs["hgrn_p"]), stack(outs["hgrn_s"])]
    for r in range(6):
        res += [pick("nsa_p", r), pick("nsa_s", r)]
    for r in range(2):
        res += [pick("s5_p", r), pick("s5_s", r)]
    res += [stack(outs["conv_p"]), stack(outs["conv_s"])]
    return tuple(res)
```

```python
import functools
import math

import jax
import jax.numpy as jnp
from jax import lax
from jax.experimental import pallas as pl
from jax.experimental.pallas import tpu as pltpu

F32 = jnp.float32
BF16 = jnp.bfloat16
HIGHEST = lax.Precision.HIGHEST

RMS_EPS = 1e-6
ROPE_THETA = 10000.0
NEG = -1e30
CHUNK = 64
SUBCHUNK = 16
GLA_TEMP = 16.0
NSA_BLOCK = 64
NSA_TOP_N = 16
NSA_WINDOW = 512
NSA_GROUP = 4
NSA_DH = 128
S5_CH = 16
S5_STATE = 64
S5_GROUPS_PER_STEP = 2
CONV_W = 3
LANES = 128
SUBLANES = 8
VMEM_LIMIT = 48 * 1024 * 1024


def _cp(*sem):
    return pltpu.CompilerParams(dimension_semantics=sem, vmem_limit_bytes=VMEM_LIMIT)


def _mm(a, b):
    return jnp.dot(a.astype(BF16), b.astype(BF16), preferred_element_type=F32)


def _mm_nt(a, b):
    return lax.dot_general(a.astype(BF16), b.astype(BF16), (((1,), (1,)), ((), ())),
                           preferred_element_type=F32)


def _sigmoid(x):
    return 1.0 / (1.0 + jnp.exp(-x))


def _rms(x, g):
    return x * lax.rsqrt(jnp.mean(x * x, axis=-1, keepdims=True) + RMS_EPS) * g


CAST_BLOCK_BYTES = 4 * 1024 * 1024


def _cast_body(w_ref, o_ref):
    o_ref[...] = w_ref[0].astype(BF16)


def cast_bf16(w_stack, layer):
    _, k, n = w_stack.shape
    tk = 16
    while k % (2 * tk) == 0 and 2 * tk * n * 4 <= CAST_BLOCK_BYTES:
        tk *= 2
    return pl.pallas_call(
        _cast_body, grid=(k // tk,),
        in_specs=[pl.BlockSpec((1, tk, n), lambda i: (layer, i, 0))],
        out_specs=pl.BlockSpec((tk, n), lambda i: (i, 0)),
        out_shape=jax.ShapeDtypeStruct((k, n), BF16),
        compiler_params=_cp("arbitrary"), name="cast_bf16")(w_stack)


def _norm_body(x_ref, g_ref, o_ref):
    o_ref[...] = _rms(x_ref[...], g_ref[...])


def rmsnorm_rows(x, gain, tm):
    m, d = x.shape
    return pl.pallas_call(
        _norm_body, grid=(m // tm,),
        in_specs=[pl.BlockSpec((tm, d), lambda i: (i, 0)), pl.BlockSpec((1, d), lambda i: (0, 0))],
        out_specs=pl.BlockSpec((tm, d), lambda i: (i, 0)),
        out_shape=jax.ShapeDtypeStruct((m, d), F32),
        compiler_params=_cp("arbitrary"), name="rmsnorm")(x, gain.reshape(1, d))


def _proj_body(x_ref, g_ref, w_ref, o_ref, xn_ref):
    @pl.when(pl.program_id(1) == 0)
    def _():
        xn_ref[...] = _rms(x_ref[...], g_ref[...]).astype(BF16)
    o_ref[...] = jnp.dot(xn_ref[...], w_ref[...], preferred_element_type=F32)


def norm_proj(x, gain, w, n_out, tm, tn):
    m, d = x.shape
    return pl.pallas_call(
        _proj_body, grid=(m // tm, n_out // tn),
        in_specs=[pl.BlockSpec((tm, d), lambda i, j: (i, 0)),
                  pl.BlockSpec((1, d), lambda i, j: (0, 0)),
                  pl.BlockSpec((d, tn), lambda i, j: (0, j))],
        out_specs=pl.BlockSpec((tm, tn), lambda i, j: (i, j)),
        out_shape=jax.ShapeDtypeStruct((m, n_out), F32),
        scratch_shapes=[pltpu.VMEM((tm, d), BF16)],
        compiler_params=_cp("arbitrary", "arbitrary"), name="norm_proj")(x, gain.reshape(1, d), w)


def _out_body(a_ref, w_ref, r_ref, o_ref):
    o_ref[...] = r_ref[...] + jnp.dot(a_ref[...].astype(BF16), w_ref[...], preferred_element_type=F32)


def out_proj(a, w, res, tm, tn):
    m, k = a.shape
    n = w.shape[1]
    return pl.pallas_call(
        _out_body, grid=(m // tm, n // tn),
        in_specs=[pl.BlockSpec((tm, k), lambda i, j: (i, 0)),
                  pl.BlockSpec((k, tn), lambda i, j: (0, j)),
                  pl.BlockSpec((tm, tn), lambda i, j: (i, j))],
        out_specs=pl.BlockSpec((tm, tn), lambda i, j: (i, j)),
        out_shape=jax.ShapeDtypeStruct((m, n), F32),
        compiler_params=_cp("arbitrary", "arbitrary"), name="out_proj")(a, w, res)


def _out_glu_body(a_ref, w1_ref, w2_ref, r_ref, o_ref):
    a = a_ref[...]
    g1 = jnp.dot(a, w1_ref[...], preferred_element_type=F32)
    g2 = jnp.dot(a, w2_ref[...], preferred_element_type=F32)
    o_ref[...] = r_ref[...] + g1 * _sigmoid(g2)


def out_glu(a, w, res, tm, tn):
    m, k = a.shape
    n = w.shape[1] // 2
    nj = n // tn
    return pl.pallas_call(
        _out_glu_body, grid=(m // tm, nj),
        in_specs=[pl.BlockSpec((tm, k), lambda i, j: (i, 0)),
                  pl.BlockSpec((k, tn), lambda i, j: (0, j)),
                  pl.BlockSpec((k, tn), lambda i, j: (0, nj + j)),
                  pl.BlockSpec((tm, tn), lambda i, j: (i, j))],
        out_specs=pl.BlockSpec((tm, tn), lambda i, j: (i, j)),
        out_shape=jax.ShapeDtypeStruct((m, n), F32),
        compiler_params=_cp("arbitrary", "arbitrary"), name="out_glu")(a, w, w, res)


def _ffn_in_body(*refs, seg, tiles_per_seq, tail_rows, has_state):
    if has_state:
        (x_ref, g_ref, wg_ref, wv_ref, cwg_ref, cwv_ref, cbg_ref, cbv_ref,
         p1g_ref, p2g_ref, p1v_ref, p2v_ref,
         act_ref, tg_ref, tv_ref, xn_ref, carry_ref) = refs
    else:
        (x_ref, g_ref, wg_ref, wv_ref, cwg_ref, cwv_ref, cbg_ref, cbv_ref,
         act_ref, tg_ref, tv_ref, xn_ref, carry_ref) = refs
    i = pl.program_id(0)
    f = pl.program_id(1)
    tm = x_ref.shape[0]

    @pl.when(f == 0)
    def _():
        xn_ref[...] = _rms(x_ref[...], g_ref[...]).astype(BF16)

    xn = xn_ref[...]
    tf = wg_ref.shape[1]
    sb = min(tf, 2 * LANES)
    row = lax.broadcasted_iota(jnp.int32, (tm, sb), 0)
    rowm = row % seg
    fresh = (i % tiles_per_seq) == 0

    def conv(w_ref, cw_ref, cb_ref, kind, p1_ref, p2_ref, cs, t_ref):
        u = jnp.dot(xn, w_ref[:, cs], preferred_element_type=F32)
        if has_state:
            p1 = p1_ref[:, cs]
            p2 = p2_ref[:, cs]
        else:
            prev = carry_ref[kind, f, :, cs]
            prev = jnp.where(fresh, 0.0, prev)
            prev0 = prev[SUBLANES - 2:SUBLANES - 1, :]
            prev1 = prev[SUBLANES - 1:SUBLANES, :]
            p1 = jnp.broadcast_to(prev1, u.shape)
            p2 = jnp.where(row == 0, prev0, prev1)
            carry_ref[kind, f, :, cs] = u[tm - SUBLANES:, :]
        u1 = jnp.where(rowm < 1, p1, pltpu.roll(u, 1, 0))
        u2 = jnp.where(rowm < 2, p2, pltpu.roll(u, 2, 0))
        cw = cw_ref[:, cs]
        t_ref[0, :, cs] = u[tm - tail_rows:, :]
        return cw[0:1, :] * u2 + cw[1:2, :] * u1 + cw[2:3, :] * u + cb_ref[:, cs]

    for jb in range(tf // sb):
        cs = slice(jb * sb, (jb + 1) * sb)
        mg = conv(wg_ref, cwg_ref, cbg_ref, 0, p1g_ref if has_state else None, p2g_ref if has_state else None,
                  cs, tg_ref)
        mv = conv(wv_ref, cwv_ref, cbv_ref, 1, p1v_ref if has_state else None, p2v_ref if has_state else None,
                  cs, tv_ref)
        act_ref[:, cs] = (mg * _sigmoid(mg) * mv).astype(BF16)


def ffn_in(x, gain, w_in, conv_w, conv_b, tm, tf, seq_len, hist=None):
    m, d = x.shape
    ff = w_in.shape[1] // 2
    nf = ff // tf
    nb = m // tm
    has_state = hist is not None
    if has_state:
        seg, tiles_per_seq, tail_rows = seq_len, 1, tm
    else:
        seg, tiles_per_seq, tail_rows = tm, seq_len // tm, SUBLANES
    wspec_g = pl.BlockSpec((d, tf), lambda i, f: (0, f))
    wspec_v = pl.BlockSpec((d, tf), lambda i, f: (0, nf + f))
    cspec_g = lambda r: pl.BlockSpec((r, tf), lambda i, f: (0, f))
    cspec_v = lambda r: pl.BlockSpec((r, tf), lambda i, f: (0, nf + f))
    in_specs = [pl.BlockSpec((tm, d), lambda i, f: (i, 0)), pl.BlockSpec((1, d), lambda i, f: (0, 0)),
                wspec_g, wspec_v, cspec_g(CONV_W), cspec_v(CONV_W), cspec_g(1), cspec_v(1)]
    args = [x, gain.reshape(1, d), w_in, w_in, conv_w, conv_w, conv_b.reshape(1, -1), conv_b.reshape(1, -1)]
    if has_state:
        p1, p2 = hist
        in_specs += [pl.BlockSpec((tm, tf), lambda i, f: (i, f)), pl.BlockSpec((tm, tf), lambda i, f: (i, f)),
                     pl.BlockSpec((tm, tf), lambda i, f: (i, nf + f)), pl.BlockSpec((tm, tf), lambda i, f: (i, nf + f))]
        args += [p1, p2, p1, p2]
    body = functools.partial(_ffn_in_body, seg=seg, tiles_per_seq=tiles_per_seq,
                             tail_rows=tail_rows, has_state=has_state)
    return pl.pallas_call(
        body, grid=(nb, nf), in_specs=in_specs,
        out_specs=[pl.BlockSpec((tm, tf), lambda i, f: (i, f)),
                   pl.BlockSpec((1, tail_rows, tf), lambda i, f: (i, 0, f)),
                   pl.BlockSpec((1, tail_rows, tf), lambda i, f: (i, 0, f))],
        out_shape=[jax.ShapeDtypeStruct((m, ff), BF16),
                   jax.ShapeDtypeStruct((nb, tail_rows, ff), F32),
                   jax.ShapeDtypeStruct((nb, tail_rows, ff), F32)],
        scratch_shapes=[pltpu.VMEM((tm, d), BF16), pltpu.VMEM((2, nf, SUBLANES, tf), F32)],
        compiler_params=_cp("arbitrary", "arbitrary"), name="ffn_in")(*args)


def _pad_rows(a, rows):
    if a.shape[0] == rows:
        return a
    return jnp.concatenate([a, jnp.zeros((rows - a.shape[0], a.shape[1]), a.dtype)], axis=0)


def _glr_chunk(q, k, v, g, st, c, sub):
    dk = q.shape[1]
    row = lax.broadcasted_iota(jnp.int32, (c, LANES), 0)
    col = lax.broadcasted_iota(jnp.int32, (c, LANES), 1)
    trow = lax.broadcasted_iota(jnp.int32, (c, c), 0)
    tcol = lax.broadcasted_iota(jnp.int32, (c, c), 1)
    tri = (trow >= tcol).astype(F32)
    cum = jnp.dot(tri, g, preferred_element_type=F32, precision=HIGHEST)
    last = cum[c - 1:c, :]
    inter = _mm_nt(q * jnp.exp(cum), st)

    rowk = lax.broadcasted_iota(jnp.int32, (c, dk), 0)
    rm = rowk % sub
    ones = jnp.ones((dk, LANES), BF16)
    att = jnp.zeros((c, LANES), F32)
    for d in range(sub):
        if d == 0:
            p = q * k
        else:
            ks = pltpu.roll(k, d, 0)
            cs = pltpu.roll(cum, d, 0)
            p = q * ks * jnp.exp(jnp.where(rm >= d, cum - cs, NEG))
        a = jnp.dot(p.astype(BF16), ones, preferred_element_type=F32)
        att = att + jnp.where(col == row - d, a, 0.0)
    if c > sub:
        blocks = [jnp.zeros((sub, LANES), F32)]
        for i in range(1, c // sub):
            cs = cum[i * sub - 1:i * sub, :]
            qi = q[i * sub:(i + 1) * sub, :] * jnp.exp(cum[i * sub:(i + 1) * sub, :] - cs)
            kj = k * jnp.exp(jnp.where(rowk < i * sub, cs - cum, NEG))
            blocks.append(_mm_nt(qi, _pad_rows(kj, LANES)))
        att = att + jnp.concatenate(blocks, axis=0)
    vpad = _pad_rows(v, LANES)
    intra = _mm(att, vpad)
    kd = _pad_rows(k * jnp.exp(last - cum), LANES)
    st_new = st * jnp.exp(last) + _mm(vpad.T, kd)
    return inter + intra, st_new


def _rec_body(*refs, mode, c, sub, n_chunks, dk, dv, hb, layer, has_state):
    refs = list(refs)
    if mode == "gla":
        q_ref, k_ref, v_ref, r_ref, lr_ref, wa_ref, ba_ref, hn_ref = refs[:8]
        rest = refs[8:]
    else:
        q_ref, k_ref, v_ref, r_ref, lb_ref, hn_ref = refs[:6]
        rest = refs[6:]
    if has_state:
        s0_ref, og_ref, sout_ref, st_ref = rest
    else:
        og_ref, sout_ref, st_ref = rest
    tstep = pl.program_id(2)

    @pl.when(tstep == 0)
    def _():
        for hh in range(hb):
            if has_state:
                st_ref[hh] = s0_ref[0, hh].T
            else:
                st_ref[hh] = jnp.zeros(st_ref.shape[1:], F32)

    if mode == "hgrn":
        lbx = lb_ref[...]
        e = jnp.exp(lbx - jnp.max(lbx, axis=0, keepdims=True))
        sm = e / jnp.sum(e, axis=0, keepdims=True)
        lb_all = jnp.zeros((1, hb * dk), F32)
        for li in range(1, layer + 1):
            lb_all = lb_all + sm[li:li + 1, :]

    for ci in range(n_chunks):
        sl = slice(ci * c, (ci + 1) * c)
        for hh in range(hb):
            hk = slice(hh * dk, (hh + 1) * dk)
            hv = slice(hh * dv, (hh + 1) * dv)
            if mode == "gla":
                q = q_ref[0, sl, hk] * (dk ** -0.5)
                k = k_ref[0, sl, hk]
                z = _mm(lr_ref[0, sl, :], wa_ref[:, hk]) + ba_ref[:, hk]
                g = -(jnp.maximum(-z, 0.0) + jnp.log1p(jnp.exp(-jnp.abs(z)))) / GLA_TEMP
            else:
                qz = q_ref[0, sl, hk]
                q = qz * _sigmoid(qz)
                lbv = lb_all[:, hk]
                fg = lbv + (1.0 - lbv) * _sigmoid(k_ref[0, sl, hk])
                k = 1.0 - fg
                g = jnp.log(fg)
            v = v_ref[0, sl, hv]
            o, st_new = _glr_chunk(q, k, v, g, st_ref[hh], c, sub)
            st_ref[hh] = st_new
            of = o * lax.rsqrt(jnp.mean(o * o, axis=-1, keepdims=True) + RMS_EPS) * hn_ref[...]
            gate = r_ref[0, sl, hv]
            og_ref[0, sl, hv] = (of * (gate * _sigmoid(gate))).astype(BF16)

    @pl.when(tstep == pl.num_programs(2) - 1)
    def _():
        for hh in range(hb):
            sout_ref[0, hh] = st_ref[hh].T


def recurrence(mode, proj, heads, dk, dv, hn, tb, c, extra, s0=None, layer=0, hb=1):
    b, t, _ = proj.shape
    sub = min(SUBCHUNK, c)
    has_state = s0 is not None
    wk, wv = hb * dk, hb * dv
    nh = heads // hb
    if mode == "gla":
        lr, wa, ba = extra
        koff, voff = nh, (2 * heads * dk) // wv
        roff = voff + nh
        in_specs = [pl.BlockSpec((1, tb, wk), lambda i, h, s: (i, s, h)),
                    pl.BlockSpec((1, tb, wk), lambda i, h, s: (i, s, koff + h)),
                    pl.BlockSpec((1, tb, wv), lambda i, h, s: (i, s, voff + h)),
                    pl.BlockSpec((1, tb, wv), lambda i, h, s: (i, s, roff + h)),
                    pl.BlockSpec((1, tb, LANES), lambda i, h, s: (i, s, 0)),
                    pl.BlockSpec((LANES, wk), lambda i, h, s: (0, h)),
                    pl.BlockSpec((1, wk), lambda i, h, s: (0, h)),
                    pl.BlockSpec((1, dv), lambda i, h, s: (0, 0))]
        args = [proj, proj, proj, proj, lr, wa, ba, hn.reshape(1, dv)]
    else:
        (lb,) = extra
        in_specs = [pl.BlockSpec((1, tb, wk), lambda i, h, s: (i, s, h)),
                    pl.BlockSpec((1, tb, wk), lambda i, h, s: (i, s, nh + h)),
                    pl.BlockSpec((1, tb, wv), lambda i, h, s: (i, s, 2 * nh + h)),
                    pl.BlockSpec((1, tb, wv), lambda i, h, s: (i, s, 3 * nh + h)),
                    pl.BlockSpec((lb.shape[0], wk), lambda i, h, s: (0, h)),
                    pl.BlockSpec((1, dv), lambda i, h, s: (0, 0))]
        args = [proj, proj, proj, proj, lb, hn.reshape(1, dv)]
    if has_state:
        in_specs.append(pl.BlockSpec((1, hb, dk, dv), lambda i, h, s: (i, h, 0, 0)))
        args.append(s0)
    body = functools.partial(_rec_body, mode=mode, c=c, sub=sub, n_chunks=tb // c, dk=dk, dv=dv, hb=hb,
                             layer=layer, has_state=has_state)
    return pl.pallas_call(
        body, grid=(b, nh, t // tb), in_specs=in_specs,
        out_specs=[pl.BlockSpec((1, tb, wv), lambda i, h, s: (i, s, h)),
                   pl.BlockSpec((1, hb, dk, dv), lambda i, h, s: (i, h, 0, 0))],
        out_shape=[jax.ShapeDtypeStruct((b, t, heads * dv), BF16),
                   jax.ShapeDtypeStruct((b, heads, dk, dv), F32)],
        scratch_shapes=[pltpu.VMEM((hb, dv, dk), F32)],
        compiler_params=_cp("arbitrary", "arbitrary", "arbitrary"), name="recurrence_" + mode)(*args)


def _rope_body(q_ref, ks_ref, kw_ref, cos_ref, sin_ref, qo_ref, kso_ref, kwo_ref):
    cos = cos_ref[...]
    sin = sin_ref[...]

    def rot(src, dst):
        for h in range(src.shape[1] // LANES):
            x = src[:, h * LANES:(h + 1) * LANES]
            dst[:, h * LANES:(h + 1) * LANES] = x * cos + pltpu.roll(x, LANES // 2, 1) * sin

    rot(q_ref, qo_ref)
    rot(ks_ref, kso_ref)
    rot(kw_ref, kwo_ref)


def nsa_rope(proj, cos, sin, tm, tiles_per_seq, qw, kvw):
    m = proj.shape[0]
    ks_blk = (qw + 2 * kvw) // kvw
    kw_blk = (qw + 4 * kvw) // kvw
    return pl.pallas_call(
        _rope_body, grid=(m // tm,),
        in_specs=[pl.BlockSpec((tm, qw), lambda i: (i, 0)),
                  pl.BlockSpec((tm, kvw), lambda i: (i, ks_blk)),
                  pl.BlockSpec((tm, kvw), lambda i: (i, kw_blk)),
                  pl.BlockSpec((tm, LANES), lambda i: (i % tiles_per_seq, 0)),
                  pl.BlockSpec((tm, LANES), lambda i: (i % tiles_per_seq, 0))],
        out_specs=[pl.BlockSpec((tm, qw), lambda i: (i, 0)),
                   pl.BlockSpec((tm, kvw), lambda i: (i, 0)),
                   pl.BlockSpec((tm, kvw), lambda i: (i, 0))],
        out_shape=[jax.ShapeDtypeStruct((m, qw), F32), jax.ShapeDtypeStruct((m, kvw), F32),
                   jax.ShapeDtypeStruct((m, kvw), F32)],
        compiler_params=_cp("arbitrary"), name="nsa_rope")(proj, proj, proj, cos, sin)


def _pool_rows(x, pw):
    n = x.shape[0] // NSA_BLOCK
    return jnp.sum(x.reshape(n, NSA_BLOCK, x.shape[1]) * pw[None], axis=1)


def _pool_body(kc_ref, vc_ref, pk_ref, pv_ref, ko_ref, vo_ref, *, n_cb):
    ko_ref[...] = jnp.zeros(ko_ref.shape, F32)
    vo_ref[...] = jnp.zeros(vo_ref.shape, F32)
    kp = _pool_rows(kc_ref[0, 0:n_cb * NSA_BLOCK, :], pk_ref[...])
    vp = _pool_rows(vc_ref[0, 0:n_cb * NSA_BLOCK, :], pv_ref[...])
    for g in range(ko_ref.shape[1]):
        ko_ref[0, g, 0:n_cb, :] = kp[:, g * LANES:(g + 1) * LANES]
        vo_ref[0, g, 0:n_cb, :] = vp[:, g * LANES:(g + 1) * LANES]


def nsa_pool_prompt(proj3, pk, pv, qw, kvw, n_pad):
    b, t, _ = proj3.shape
    n_cb = t // NSA_BLOCK
    kc_blk = qw // kvw
    return pl.pallas_call(
        functools.partial(_pool_body, n_cb=n_cb), grid=(b,),
        in_specs=[pl.BlockSpec((1, t, kvw), lambda i: (i, 0, kc_blk)),
                  pl.BlockSpec((1, t, kvw), lambda i: (i, 0, kc_blk + 1)),
                  pl.BlockSpec((NSA_BLOCK, kvw), lambda i: (0, 0)),
                  pl.BlockSpec((NSA_BLOCK, kvw), lambda i: (0, 0))],
        out_specs=[pl.BlockSpec((1, kvw // LANES, n_pad, LANES), lambda i: (i, 0, 0, 0)),
                   pl.BlockSpec((1, kvw // LANES, n_pad, LANES), lambda i: (i, 0, 0, 0))],
        out_shape=[jax.ShapeDtypeStruct((b, kvw // LANES, n_pad, LANES), F32)] * 2,
        compiler_params=_cp("arbitrary"), name="nsa_pool")(proj3, proj3, pk, pv)


def _pool_pages_body(pt_ref, *refs, pg):
    k_refs = refs[:pg]
    v_refs = refs[pg:2 * pg]
    pk_ref, pv_ref, ko_ref, vo_ref = refs[2 * pg:]
    _, page, kv, dh = k_refs[0].shape
    per = page // NSA_BLOCK

    def pooled(x_ref, pw_ref):
        return jnp.sum(x_ref[0].reshape(per, NSA_BLOCK, kv, dh) * pw_ref[...][None], axis=1)

    for i in range(pg):
        kp, vp = pooled(k_refs[i], pk_ref), pooled(v_refs[i], pv_ref)
        for g in range(kv):
            ko_ref[0, g, i * per:(i + 1) * per, :] = kp[:, g, :]
            vo_ref[0, g, i * per:(i + 1) * per, :] = vp[:, g, :]


def nsa_pool_pages(pool_k, pool_v, page_table, pk, pv, pg):
    b, n_pages = page_table.shape
    _, page, kv, dh = pool_k.shape
    per = page // NSA_BLOCK
    page_spec = lambda i: pl.BlockSpec((1, page, kv, dh), lambda bi, s, pt, i=i: (pt[bi, s * pg + i], 0, 0, 0))
    wspec = pl.BlockSpec((NSA_BLOCK, kv, dh), lambda bi, s, pt: (0, 0, 0))
    ospec = pl.BlockSpec((1, kv, pg * per, dh), lambda bi, s, pt: (bi, 0, s, 0))
    gs = pltpu.PrefetchScalarGridSpec(
        num_scalar_prefetch=1, grid=(b, n_pages // pg),
        in_specs=[page_spec(i) for i in range(pg)] + [page_spec(i) for i in range(pg)] + [wspec, wspec],
        out_specs=[ospec, ospec])
    n_blk = n_pages * per
    return pl.pallas_call(
        functools.partial(_pool_pages_body, pg=pg), grid_spec=gs,
        out_shape=[jax.ShapeDtypeStruct((b, kv, n_blk, dh), F32)] * 2,
        compiler_params=_cp("arbitrary", "arbitrary"), name="nsa_pool_pages")(
            page_table, *([pool_k] * pg), *([pool_v] * pg), pk, pv)


def _cmp_body(q_ref, kc_ref, vc_ref, o_ref, sel_ref, *, q_start, n_cb, n_blk, nb_pad):
    tq = q_ref.shape[1]
    ncp = kc_ref.shape[2]
    qt = pl.program_id(2)
    kc = kc_ref[0, 0]
    vc = vc_ref[0, 0]
    scale = NSA_DH ** -0.5
    colc = lax.broadcasted_iota(jnp.int32, (tq, ncp), 1)
    qposc = q_start + qt * tq + lax.broadcasted_iota(jnp.int32, (tq, ncp), 0)
    valid = ((colc + 1) * NSA_BLOCK - 1 <= qposc) & (colc < n_cb)
    imp = jnp.zeros((tq, ncp), F32)
    for r in range(NSA_GROUP):
        qr = q_ref[0, :, r * LANES:(r + 1) * LANES]
        s = jnp.where(valid, _mm_nt(qr, kc) * scale, NEG)
        m = jnp.max(s, axis=-1, keepdims=True)
        e = jnp.where(valid, jnp.exp(s - m), 0.0)
        p = e / jnp.maximum(jnp.sum(e, axis=-1, keepdims=True), 1e-30)
        o_ref[0, :, r * LANES:(r + 1) * LANES] = _mm(p, vc)
        imp = imp + p
    if nb_pad > ncp:
        imp = jnp.concatenate([imp, jnp.zeros((tq, nb_pad - ncp), F32)], axis=1)
    blk = lax.broadcasted_iota(jnp.int32, (tq, nb_pad), 1)
    qpos = q_start + qt * tq + lax.broadcasted_iota(jnp.int32, (tq, nb_pad), 0)
    cur = qpos // NSA_BLOCK
    forced = (blk == cur) | (blk == 0)
    score = jnp.where(blk > cur, -1.0, jnp.where(forced, NSA_GROUP + 1.0, imp))
    score = jnp.where(blk < n_blk, score, -2.0)
    blkf = blk.astype(F32)

    def pick(_, carry):
        sc, sel = carry
        mx = jnp.max(sc, axis=-1, keepdims=True)
        first = jnp.min(jnp.where(sc == mx, blkf, 1e9), axis=-1, keepdims=True)
        hit = blkf == first
        return jnp.where(hit, -3.0, sc), jnp.where(hit, 1.0, sel)

    _, sel = lax.fori_loop(0, min(NSA_TOP_N, n_blk), pick, (score, jnp.zeros((tq, nb_pad), F32)))
    sel_ref[0, 0] = sel


def nsa_cmp(q3, kcmp, vcmp, tq, q_start, n_cb, n_blk, nb_pad, kvw):
    b, t = q3.shape[0], q3.shape[1]
    ncp = kcmp.shape[2]
    kv = kvw // LANES
    gw = NSA_GROUP * LANES
    body = functools.partial(_cmp_body, q_start=q_start, n_cb=n_cb, n_blk=n_blk, nb_pad=nb_pad)
    return pl.pallas_call(
        body, grid=(b, kv, t // tq),
        in_specs=[pl.BlockSpec((1, tq, gw), lambda i, g, s: (i, s, g)),
                  pl.BlockSpec((1, 1, ncp, LANES), lambda i, g, s: (i, g, 0, 0)),
                  pl.BlockSpec((1, 1, ncp, LANES), lambda i, g, s: (i, g, 0, 0))],
        out_specs=[pl.BlockSpec((1, tq, gw), lambda i, g, s: (i, s, g)),
                   pl.BlockSpec((1, 1, tq, nb_pad), lambda i, g, s: (i, g, s, 0))],
        out_shape=[jax.ShapeDtypeStruct((b, t, kv * gw), F32),
                   jax.ShapeDtypeStruct((b, kv, t, nb_pad), F32)],
        compiler_params=_cp("arbitrary", "arbitrary", "arbitrary"), name="nsa_cmp")(q3, kcmp, vcmp)


def _softmax_step(carry, s, valid, vv):
    m, l, acc = carry
    s = jnp.where(valid, s, NEG)
    m_new = jnp.maximum(m, jnp.max(s, axis=-1, keepdims=True))
    p = jnp.where(valid, jnp.exp(s - m_new), 0.0)
    alpha = jnp.exp(m - m_new)
    l = alpha * l + jnp.sum(p, axis=-1, keepdims=True)
    acc = alpha * acc + _mm(p, vv)
    return m_new, l, acc


def _softmax_init(rows):
    return (jnp.full((rows, 1), NEG, F32), jnp.zeros((rows, 1), F32), jnp.zeros((rows, LANES), F32))


def _attn_body(*refs, tq, tk, n_kt, q_start, k_start, do_sel, do_win):
    refs = list(refs)
    q_ref = refs.pop(0)
    if do_sel:
        ks_ref, vs_ref, sel_ref = refs[:3]
        refs = refs[3:]
    if do_win:
        kw_ref, vw_ref = refs[:2]
        refs = refs[2:]
    outs = refs
    qt = pl.program_id(2)
    scale = NSA_DH ** -0.5
    rows = NSA_GROUP * tq
    q4 = jnp.concatenate([q_ref[0, :, r * LANES:(r + 1) * LANES] for r in range(NSA_GROUP)], axis=0).astype(BF16)
    q0 = q_start + qt * tq
    qpos = q0 + lax.broadcasted_iota(jnp.int32, (rows, tk), 0) % tq
    kcol = lax.broadcasted_iota(jnp.int32, (rows, tk), 1)

    def finish(carry, o_ref):
        m, l, acc = carry
        o = acc / jnp.maximum(l, 1e-30)
        for r in range(NSA_GROUP):
            o_ref[0, :, r * LANES:(r + 1) * LANES] = o[r * tq:(r + 1) * tq, :]

    if do_sel:
        selb = sel_ref[0, 0].astype(BF16)
        nbp = selb.shape[1]
        en = lax.broadcasted_iota(jnp.int32, (nbp, tk), 0)
        es = lax.broadcasted_iota(jnp.int32, (nbp, tk), 1)

        def sel_step(kt, carry):
            off = pl.multiple_of(kt * tk, tk)
            kk = ks_ref[0, pl.ds(off, tk), :]
            vv = vs_ref[0, pl.ds(off, tk), :]
            s = _mm_nt(q4, kk) * scale
            kp0 = k_start + kt * tk
            expand = ((kp0 + es) // NSA_BLOCK == en).astype(BF16)
            chosen = jnp.dot(selb, expand, preferred_element_type=F32)
            chosen = jnp.concatenate([chosen] * NSA_GROUP, axis=0)
            valid = (chosen > 0.5) & (kp0 + kcol <= qpos)
            return _softmax_step(carry, s, valid, vv)

        hi = jnp.minimum(n_kt, (q0 + tq - 1 - k_start) // tk + 1)
        finish(lax.fori_loop(0, hi, sel_step, _softmax_init(rows)), outs.pop(0))

    if do_win:
        def win_step(kt, carry):
            off = pl.multiple_of(kt * tk, tk)
            kk = kw_ref[0, pl.ds(off, tk), :]
            vv = vw_ref[0, pl.ds(off, tk), :]
            s = _mm_nt(q4, kk) * scale
            kpos = k_start + kt * tk + kcol
            dist = qpos - kpos
            valid = (dist >= 0) & (dist < NSA_WINDOW) & (kpos >= 0)
            return _softmax_step(carry, s, valid, vv)

        lo = jnp.maximum(0, (q0 - (NSA_WINDOW - 1) - k_start) // tk)
        hi = jnp.minimum(n_kt, (q0 + tq - 1 - k_start) // tk + 1)
        finish(lax.fori_loop(lo, hi, win_step, _softmax_init(rows)), outs.pop(0))


def nsa_attend(q_rot3, tq, tk, q_start, k_start, sel_args=None, win_args=None):
    b, t, qw = q_rot3.shape
    kv = qw // (NSA_GROUP * LANES)
    gw = NSA_GROUP * LANES
    in_specs = [pl.BlockSpec((1, tq, gw), lambda i, g, s: (i, s, g))]
    args = [q_rot3]
    n_out = 0
    t_k = None
    if sel_args is not None:
        k, v, sel = sel_args
        t_k = k.shape[1]
        nbp = sel.shape[-1]
        in_specs += [pl.BlockSpec((1, t_k, LANES), lambda i, g, s: (i, 0, g)),
                     pl.BlockSpec((1, t_k, LANES), lambda i, g, s: (i, 0, g)),
                     pl.BlockSpec((1, 1, tq, nbp), lambda i, g, s: (i, g, s, 0))]
        args += [k, v, sel]
        n_out += 1
    if win_args is not None:
        k, v = win_args
        t_k = k.shape[1]
        in_specs += [pl.BlockSpec((1, t_k, LANES), lambda i, g, s: (i, 0, g)),
                     pl.BlockSpec((1, t_k, LANES), lambda i, g, s: (i, 0, g))]
        args += [k, v]
        n_out += 1
    body = functools.partial(_attn_body, tq=tq, tk=tk, n_kt=t_k // tk, q_start=q_start, k_start=k_start,
                             do_sel=sel_args is not None, do_win=win_args is not None)
    return pl.pallas_call(
        body, grid=(b, kv, t // tq), in_specs=in_specs,
        out_specs=[pl.BlockSpec((1, tq, gw), lambda i, g, s: (i, s, g))] * n_out,
        out_shape=[jax.ShapeDtypeStruct((b, t, qw), F32)] * n_out,
        compiler_params=_cp("arbitrary", "arbitrary", "arbitrary"), name="nsa_attend")(*args)


def _attn_t_body(q_ref, ks_ref, vs_ref, kw_ref, vw_ref, sel_ref, osel_ref, owin_ref, vst_ref, vwt_ref, *, tq, tk, n_kt):
    qt = pl.program_id(2)
    scale = NSA_DH ** -0.5
    t_k = ks_ref.shape[1]

    @pl.when(qt == 0)
    def _():
        for j in range(t_k // LANES):
            sl = slice(j * LANES, (j + 1) * LANES)
            vst_ref[:, sl] = vs_ref[0, sl, :].T.astype(BF16)
            vwt_ref[:, sl] = vw_ref[0, sl, :].T.astype(BF16)

    q0 = qt * tq
    cols = NSA_GROUP * tq
    sel_t = sel_ref[0, 0].T.astype(BF16)
    nbp = sel_t.shape[0]
    q4 = jnp.concatenate([q_ref[0, :, r * LANES:(r + 1) * LANES] for r in range(NSA_GROUP)], axis=0).astype(BF16)

    def scores(k_ref, off, n, allowed):
        allowed = jnp.concatenate([allowed] * NSA_GROUP, axis=1) > 0.5
        s = _mm_nt(k_ref[0, pl.ds(off, n), :], q4) * scale
        return jnp.where(allowed, s, NEG), allowed

    def write(o_ref, acc, l):
        o = acc / jnp.maximum(l, 1e-30)
        for r in range(NSA_GROUP):
            o_ref[0, :, r * LANES:(r + 1) * LANES] = o[:, r * tq:(r + 1) * tq].T

    kpos_l = lax.broadcasted_iota(jnp.int32, (tk, tq), 0)
    qpos = q0 + lax.broadcasted_iota(jnp.int32, (tk, tq), 1)
    en = lax.broadcasted_iota(jnp.int32, (tk, nbp), 1)
    es = lax.broadcasted_iota(jnp.int32, (tk, nbp), 0)

    def sel_step(kt, carry):
        m, l, acc = carry
        off = pl.multiple_of(kt * tk, tk)
        expand = ((kt * tk + es) // NSA_BLOCK == en).astype(BF16)
        chosen = jnp.dot(expand, sel_t, preferred_element_type=F32)
        allowed = jnp.where(kt * tk + kpos_l <= qpos, chosen, 0.0)
        s, valid = scores(ks_ref, off, tk, allowed)
        m_new = jnp.maximum(m, jnp.max(s, axis=0, keepdims=True))
        p = jnp.where(valid, jnp.exp(s - m_new), 0.0)
        alpha = jnp.exp(m - m_new)
        l = alpha * l + jnp.sum(p, axis=0, keepdims=True)
        acc = alpha * acc + jnp.dot(vst_ref[:, pl.ds(off, tk)], p.astype(BF16), preferred_element_type=F32)
        return m_new, l, acc

    init = (jnp.full((1, cols), NEG, F32), jnp.zeros((1, cols), F32), jnp.zeros((NSA_DH, cols), F32))
    m, l, acc = lax.fori_loop(0, jnp.minimum(n_kt, (q0 + tq - 1) // tk + 1), sel_step, init)
    write(osel_ref, acc, l)

    wk = min(t_k, NSA_WINDOW + tq)
    ws = pl.multiple_of(jnp.clip(q0 - NSA_WINDOW, 0, t_k - wk), LANES)
    dist = (q0 + lax.broadcasted_iota(jnp.int32, (wk, tq), 1)) - (ws + lax.broadcasted_iota(jnp.int32, (wk, tq), 0))
    s, valid = scores(kw_ref, ws, wk, jnp.where((dist >= 0) & (dist < NSA_WINDOW), 1.0, 0.0))
    p = jnp.where(valid, jnp.exp(s - jnp.max(s, axis=0, keepdims=True)), 0.0)
    acc = jnp.dot(vwt_ref[:, pl.ds(ws, wk)], p.astype(BF16), preferred_element_type=F32)
    write(owin_ref, acc, jnp.sum(p, axis=0, keepdims=True))


def nsa_attend_prompt(q_rot3, ks, vs, kw, vw, sel, tq, tk):
    b, t, qw = q_rot3.shape
    kv = qw // (NSA_GROUP * LANES)
    gw = NSA_GROUP * LANES
    kspec = pl.BlockSpec((1, t, LANES), lambda i, g, s: (i, 0, g))
    ospec = pl.BlockSpec((1, tq, gw), lambda i, g, s: (i, s, g))
    body = functools.partial(_attn_t_body, tq=tq, tk=tk, n_kt=t // tk)
    return pl.pallas_call(
        body, grid=(b, kv, t // tq),
        in_specs=[ospec, kspec, kspec, kspec, kspec,
                  pl.BlockSpec((1, 1, tq, sel.shape[-1]), lambda i, g, s: (i, g, s, 0))],
        out_specs=[ospec, ospec],
        out_shape=[jax.ShapeDtypeStruct((b, t, qw), F32)] * 2,
        scratch_shapes=[pltpu.VMEM((NSA_DH, t), BF16), pltpu.VMEM((NSA_DH, t), BF16)],
        compiler_params=_cp("arbitrary", "arbitrary", "arbitrary"), name="nsa_attend_prompt")(
            q_rot3, ks, vs, kw, vw, sel)


def _paged_sel_body(pt_ref, *refs, pg, past_len, t_new):
    k_refs = refs[:pg]
    v_refs = refs[pg:2 * pg]
    q_ref, sel_ref, kn_ref, vn_ref, o_ref, m_ref, l_ref, acc_ref = refs[2 * pg:]
    step = pl.program_id(1)
    page = k_refs[0].shape[1]
    rows = q_ref.shape[1]
    kv = kn_ref.shape[2] // LANES
    scale = NSA_DH ** -0.5
    nbp = sel_ref.shape[-1]
    qb = q_ref[0].astype(BF16)
    selb = sel_ref[0].astype(BF16)

    @pl.when(step == 0)
    def _():
        m_ref[...] = jnp.full(m_ref.shape, NEG, F32)
        l_ref[...] = jnp.zeros(l_ref.shape, F32)
        acc_ref[...] = jnp.zeros(acc_ref.shape, F32)

    def update(ks, vs, kp0):
        n = len(ks) * page
        tloc = lax.broadcasted_iota(jnp.int32, (rows, n), 0) % t_new
        kcol = lax.broadcasted_iota(jnp.int32, (rows, n), 1)
        en = lax.broadcasted_iota(jnp.int32, (nbp, n), 0)
        es = lax.broadcasted_iota(jnp.int32, (nbp, n), 1)
        s = jnp.concatenate([_mm_nt(qb, k) for k in ks], axis=1) * scale
        expand = ((kp0 + es) // NSA_BLOCK == en).astype(BF16)
        chosen = jnp.dot(selb, expand, preferred_element_type=F32)
        valid = (chosen > 0.5) & (kp0 + kcol <= past_len + tloc)
        s = jnp.where(valid, s, NEG)
        m = m_ref[:, 0:1]
        m_new = jnp.maximum(m, jnp.max(s, axis=-1, keepdims=True))
        p = jnp.where(valid, jnp.exp(s - m_new), 0.0)
        alpha = jnp.exp(m - m_new)
        pv = _mm(p[:, 0:page], vs[0])
        for i in range(1, len(ks)):
            pv = pv + _mm(p[:, i * page:(i + 1) * page], vs[i])
        m_ref[...] = jnp.broadcast_to(m_new, m_ref.shape)
        l_ref[...] = jnp.broadcast_to(alpha * l_ref[:, 0:1] + jnp.sum(p, axis=-1, keepdims=True), l_ref.shape)
        acc_ref[...] = alpha * acc_ref[...] + pv

    update([r[0] for r in k_refs], [r[0] for r in v_refs], step * (pg * page))

    @pl.when(step == pl.num_programs(1) - 1)
    def _():
        update([kn_ref[0]], [vn_ref[0]], past_len)
        o = acc_ref[...] / jnp.maximum(l_ref[:, 0:1], 1e-30)
        per = rows // kv
        for g in range(kv):
            o_ref[0, g * per:(g + 1) * per, :] = o[g * per:(g + 1) * per, g * LANES:(g + 1) * LANES]


def nsa_paged_sel(q_rot3, sel, pool_k, pool_v, page_table, k_new, v_new, pg):
    b, t_new, qw = q_rot3.shape
    n_pages = page_table.shape[1]
    _, page, kvw = pool_k.shape
    kv = kvw // LANES
    nbp = sel.shape[-1]
    per = NSA_GROUP * t_new
    rows = kv * per
    qg = q_rot3.reshape(b, t_new, kv, NSA_GROUP, NSA_DH).transpose(0, 2, 3, 1, 4).reshape(b, kv, per, 1, NSA_DH)
    q_rows = (qg * jnp.eye(kv, dtype=F32)[None, :, None, :, None]).reshape(b, rows, kvw)
    sel_rows = jnp.broadcast_to(sel[:, :, None], (b, kv, NSA_GROUP, t_new, nbp)).reshape(b, rows, nbp)
    page_spec = lambda i: pl.BlockSpec((1, page, kvw), lambda bi, s, pt, i=i: (pt[bi, s * pg + i], 0, 0))
    gs = pltpu.PrefetchScalarGridSpec(
        num_scalar_prefetch=1, grid=(b, n_pages // pg),
        in_specs=[page_spec(i) for i in range(pg)] + [page_spec(i) for i in range(pg)]
                 + [pl.BlockSpec((1, rows, kvw), lambda bi, s, pt: (bi, 0, 0)),
                    pl.BlockSpec((1, rows, nbp), lambda bi, s, pt: (bi, 0, 0)),
                    pl.BlockSpec((1, page, kvw), lambda bi, s, pt: (bi, 0, 0)),
                    pl.BlockSpec((1, page, kvw), lambda bi, s, pt: (bi, 0, 0))],
        out_specs=pl.BlockSpec((1, rows, NSA_DH), lambda bi, s, pt: (bi, 0, 0)),
        scratch_shapes=[pltpu.VMEM((rows, LANES), F32), pltpu.VMEM((rows, LANES), F32),
                        pltpu.VMEM((rows, kvw), F32)])
    body = functools.partial(_paged_sel_body, pg=pg, past_len=n_pages * page, t_new=t_new)
    o_rows = pl.pallas_call(
        body, grid_spec=gs, out_shape=jax.ShapeDtypeStruct((b, rows, NSA_DH), F32),
        compiler_params=_cp("arbitrary", "arbitrary"), name="nsa_paged_sel")(
            page_table, *([pool_k] * pg), *([pool_v] * pg), q_rows, sel_rows, k_new, v_new)
    return o_rows.reshape(b, kv, NSA_GROUP, t_new, NSA_DH).transpose(0, 3, 1, 2, 4).reshape(b, t_new, qw)


def _combine_body(oc_ref, os_ref, ow_ref, gt_ref, a_ref):
    gs = _sigmoid(gt_ref[...])
    for hh in range(oc_ref.shape[1] // LANES):
        sl = slice(hh * LANES, (hh + 1) * LANES)
        a = (gs[:, 3 * hh:3 * hh + 1] * oc_ref[:, sl] + gs[:, 3 * hh + 1:3 * hh + 2] * os_ref[:, sl]
             + gs[:, 3 * hh + 2:3 * hh + 3] * ow_ref[:, sl])
        a_ref[:, sl] = a.astype(BF16)


def nsa_combine(o_cmp, o_sel, o_win, gates, tm):
    m, qw = o_cmp.shape
    spec = pl.BlockSpec((tm, qw), lambda i: (i, 0))
    return pl.pallas_call(
        _combine_body, grid=(m // tm,),
        in_specs=[spec, spec, spec, pl.BlockSpec((tm, LANES), lambda i: (i, 0))],
        out_specs=spec, out_shape=jax.ShapeDtypeStruct((m, qw), BF16),
        compiler_params=_cp("arbitrary"), name="nsa_combine")(o_cmp, o_sel, o_win, gates)


def _s5_body(*refs, seg, seq_len, has_state):
    if has_state:
        (u_ref, a1_ref, a2_ref, dt_ref, b1_ref, b2_ref, cm_ref, d_ref, s0_ref,
         z_ref, st_ref, x_ref, y_ref, up_ref) = refs
    else:
        (u_ref, a1_ref, a2_ref, dt_ref, b1_ref, b2_ref, cm_ref, d_ref,
         z_ref, st_ref, x_ref, y_ref, up_ref) = refs
    gq = pl.program_id(1)
    m = u_ref.shape[0]
    nseg = m // seg
    nb = m // seq_len
    half = LANES // 2
    lane = lax.broadcasted_iota(jnp.int32, (1, LANES), 1)
    sgn = jnp.where(lane < half, -1.0, 1.0)

    gs = a1_ref.shape[0]
    abar, bcats = [], []
    for gi in range(gs):
        are, aim, dt = a1_ref[gi], a2_ref[gi], jnp.exp(dt_ref[gi])
        er = jnp.exp(are * dt)
        abr, abi = er * jnp.cos(aim * dt), er * jnp.sin(aim * dt)
        nr, ni, den = abr - 1.0, abi, are * are + aim * aim
        cr, cim = (nr * are + ni * aim) / den, (ni * are - nr * aim) / den
        abar.append((abr, abi))
        bcats.append((cr * b1_ref[gi] + cim * b2_ref[gi]).astype(BF16))

    def cmul(x, pr, pi):
        return x * pr + pltpu.roll(x, half, 1) * (pi * sgn)

    pb = SUBLANES * seg
    nblk = m // pb
    ri = lax.broadcasted_iota(jnp.int32, (pb, pb), 0)
    ci = lax.broadcasted_iota(jnp.int32, (pb, pb), 1)

    @pl.when(gq == 0)
    def _():
        perm = (ci == (ri % SUBLANES) * seg + ri // SUBLANES).astype(BF16)
        for k in range(nblk):
            uk = u_ref[k * pb:(k + 1) * pb, :].astype(BF16)
            up_ref[k] = jnp.dot(perm, uk, preferred_element_type=F32).astype(BF16)

    x_ref[...] = jnp.dot(up_ref[...].reshape(m, LANES), jnp.concatenate(bcats, axis=1),
                         preferred_element_type=F32).reshape(nblk, pb, gs * LANES)

    def scan_group(gi):
        abr, abi = abar[gi]
        gl = slice(gi * LANES, (gi + 1) * LANES)
        x = jnp.zeros((nseg, LANES), F32)
        for s in range(seg):
            sl = slice(s * SUBLANES, (s + 1) * SUBLANES)
            x = cmul(x, abr, abi) + x_ref[:, sl, gl].reshape(nseg, LANES)
            x_ref[:, sl, gl] = x.reshape(nblk, SUBLANES, LANES)
        if has_state:
            carry = s0_ref[gi]
        else:
            spb = seq_len // seg
            pr, pi = abr, abi
            for _ in range(int(math.log2(seg))):
                pr, pi = pr * pr - pi * pi, 2.0 * pr * pi
            rown = lax.broadcasted_iota(jnp.int32, (nseg, LANES), 0) % spb
            inc = x
            sh = 1
            while sh < spb:
                inc = inc + jnp.where(rown >= sh, cmul(pltpu.roll(inc, sh, 0), pr, pi), 0.0)
                pr, pi = pr * pr - pi * pi, 2.0 * pr * pi
                sh *= 2
            carry = jnp.where(rown >= 1, pltpu.roll(inc, 1, 0), 0.0)
        pr, pi = abr, abi
        for s in range(seg):
            sl = slice(s * SUBLANES, (s + 1) * SUBLANES)
            x_ref[:, sl, gl] = x_ref[:, sl, gl] + cmul(carry, pr, pi).reshape(nblk, SUBLANES, LANES)
            pr, pi = pr * abr - pi * abi, pr * abi + pi * abr
        finals = []
        for bi in range(nb):
            last_seg = (bi + 1) * (seq_len // seg) - 1
            row = (seg - 1) * SUBLANES + last_seg % SUBLANES
            finals.append(x_ref[last_seg // SUBLANES, row:row + 1, gl])
        st_ref[gi] = jnp.concatenate(finals, axis=0)

    for gi in range(gs):
        scan_group(gi)

    yg = _mm(x_ref[...].reshape(m, gs * LANES), cm_ref[...].reshape(gs * LANES, LANES))

    @pl.when(gq == 0)
    def _():
        y_ref[...] = yg

    @pl.when(gq > 0)
    def _():
        y_ref[...] = y_ref[...] + yg

    @pl.when(gq == pl.num_programs(1) - 1)
    def _():
        unperm = (ri == (ci % SUBLANES) * seg + ci // SUBLANES).astype(BF16)
        for k in range(nblk):
            rows = slice(k * pb, (k + 1) * pb)
            yk = y_ref[rows, :]
            hi = yk.astype(BF16)
            lo = (yk - hi.astype(F32)).astype(BF16)
            y = (jnp.dot(unperm, hi, preferred_element_type=F32) + jnp.dot(unperm, lo, preferred_element_type=F32)
                 + d_ref[...] * u_ref[rows, :])
            z = 0.5 * y * (1.0 + jnp.tanh(math.sqrt(2.0 / math.pi) * (y + 0.044715 * (y * y * y))))
            z_ref[rows, :] = z.astype(BF16)


def s5_scan(u, prm, seq_len, seg, s0=None):
    m, d = u.shape
    a1, a2, dtb, b1, b2, cm, dsk = prm
    groups = a1.shape[0]
    per_tile = LANES // S5_CH
    nb = m // seq_len
    has_state = s0 is not None
    gs = S5_GROUPS_PER_STEP
    steps = per_tile // gs
    gidx = lambda j, q: (j * steps + q, 0, 0)
    vspec = pl.BlockSpec((gs, 1, LANES), gidx)
    mspec = pl.BlockSpec((gs, LANES, LANES), gidx)
    in_specs = [pl.BlockSpec((m, LANES), lambda j, q: (0, j)), vspec, vspec, vspec, mspec, mspec, mspec,
                pl.BlockSpec((1, LANES), lambda j, q: (0, j))]
    args = [u, a1, a2, dtb, b1, b2, cm, dsk]
    if has_state:
        in_specs.append(pl.BlockSpec((gs, nb, LANES), gidx))
        args.append(s0)
    body = functools.partial(_s5_body, seg=seg, seq_len=seq_len, has_state=has_state)
    return pl.pallas_call(
        body, grid=(d // LANES, steps), in_specs=in_specs,
        out_specs=[pl.BlockSpec((m, LANES), lambda j, q: (0, j)),
                   pl.BlockSpec((gs, nb, LANES), gidx)],
        out_shape=[jax.ShapeDtypeStruct((m, d), BF16), jax.ShapeDtypeStruct((groups, nb, LANES), F32)],
        scratch_shapes=[pltpu.VMEM((m // (SUBLANES * seg), SUBLANES * seg, gs * LANES), F32),
                        pltpu.VMEM((m, LANES), F32),
                        pltpu.VMEM((m // (SUBLANES * seg), SUBLANES * seg, LANES), BF16)],
        compiler_params=_cp("arbitrary", "arbitrary"), name="s5_scan")(*args)


def _s5_params(a_re, a_im, log_dt, b_re, b_im, c_re, c_im, d_skip):
    groups, p = a_re.shape
    per_tile = LANES // S5_CH
    dup = lambda a: jnp.concatenate([a, a], axis=-1)[:, None, :]
    a1, a2 = dup(a_re), dup(a_im)
    dtb = jnp.broadcast_to(log_dt[:, None, None], (groups, 1, LANES))
    slot = jax.nn.one_hot(jnp.arange(groups) % per_tile, per_tile, dtype=F32)

    def rows_in_tile(w):
        return (slot[:, :, None, None] * w[:, None]).reshape(groups, LANES, w.shape[-1])

    bre_t, bim_t = b_re.transpose(0, 2, 1), b_im.transpose(0, 2, 1)
    b1 = rows_in_tile(jnp.concatenate([bre_t, bim_t], axis=-1))
    b2 = rows_in_tile(jnp.concatenate([-bim_t, bre_t], axis=-1))
    cmat = jnp.concatenate([c_re, -c_im], axis=-1)
    cm = rows_in_tile(cmat).transpose(0, 2, 1)
    return a1, a2, dtb, b1, b2, cm, d_skip.reshape(1, -1)


def _tiles(m):
    if m >= 1024:
        return 1024, 512
    return m, m


def _gla_layer(h, b, t, gain, w_in, w_tail, w_alpha_pad, b_alpha, head_norm, w_out, s0, heads, dk, dv):
    tm, tmo = _tiles(h.shape[0])
    n_main = 2 * heads * dk + 2 * heads * dv
    proj = norm_proj(h, gain, w_in, n_main, tm, 512)
    lr = norm_proj(h, gain, w_tail, LANES, tm, LANES)
    c = math.gcd(t, CHUNK)
    tb = math.gcd(t, 128)
    og, st = recurrence("gla", proj.reshape(b, t, n_main), heads, dk, dv, head_norm, tb, c,
                        (lr.reshape(b, t, LANES), w_alpha_pad, b_alpha.reshape(1, -1)), s0, hb=2)
    return out_proj(og.reshape(b * t, heads * dv), w_out, h, tmo, 512), st


def _hgrn_layer(h, b, t, gain, w_in, lower_bound, layer, head_norm, w_out, s0, heads, dk):
    tm, tmo = _tiles(h.shape[0])
    n = 4 * heads * dk
    proj = norm_proj(h, gain, w_in, n, tm, 512)
    c = math.gcd(t, CHUNK)
    tb = math.gcd(t, 128)
    og, st = recurrence("hgrn", proj.reshape(b, t, n), heads, dk, dk, head_norm, tb, c,
                        (lower_bound,), s0, layer=layer, hb=4)
    return out_proj(og.reshape(b * t, heads * dk), w_out, h, tmo, 512), st


def _rope_tables(start, t):
    half = NSA_DH // 2
    inv = ROPE_THETA ** (-jnp.arange(half, dtype=F32) / half)
    ang = (start + jnp.arange(t, dtype=jnp.int32)).astype(F32)[:, None] * inv[None, :]
    cos, sin = jnp.cos(ang), jnp.sin(ang)
    return jnp.concatenate([cos, cos], axis=-1), jnp.concatenate([-sin, sin], axis=-1)


def _nsa_layer(h, b, t, start, gain, w_in, w_gates, pool_k, pool_v, w_out, past, heads, kv):
    tm, tmo = _tiles(h.shape[0])
    qw, kvw = heads * NSA_DH, kv * NSA_DH
    n_main = qw + 6 * kvw
    proj = norm_proj(h, gain, w_in, n_main, tm, 512)
    gates = norm_proj(h, gain, w_gates, LANES, tm, LANES)
    cos, sin = _rope_tables(start, t)
    pk = jnp.broadcast_to(pool_k[:, None], (NSA_BLOCK, kvw))
    pv = jnp.broadcast_to(pool_v[:, None], (NSA_BLOCK, kvw))
    proj3 = proj.reshape(b, t, n_main)
    col = lambda i: proj3[:, :, qw + i * kvw:qw + (i + 1) * kvw]
    kc, vc, vs, vw = col(0), col(1), col(3), col(5)

    if past is None:
        trope = min(256, t)
        q_rot, ks, kw = nsa_rope(proj, cos, sin, trope, t // trope, qw, kvw)
        q_rot3, ks3, kw3 = q_rot.reshape(b, t, qw), ks.reshape(b, t, kvw), kw.reshape(b, t, kvw)
        n_cb = t // NSA_BLOCK
        n_blk = -(-t // NSA_BLOCK)
        kcmp, vcmp = nsa_pool_prompt(proj3, pk, pv, qw, kvw, LANES)
        o_cmp, sel = nsa_cmp(proj3, kcmp, vcmp, min(t, 512), 0, n_cb, n_blk, LANES, kvw)
        tq = min(t, 128)
        o_sel, o_win = nsa_attend_prompt(q_rot3, ks3, vs, kw3, vw, sel, tq, min(t, 512))
        keep = min(NSA_WINDOW, t)
        win_k, win_v = kw3[:, t - keep:], vw[:, t - keep:]
    else:
        pool_ck, pool_cv, pool_sk, pool_sv, page_table, prev_kw, prev_vw = past
        n_pages = page_table.shape[1]
        page = pool_ck.shape[1]
        past_len = n_pages * page
        cos_r, sin_r = jnp.tile(cos, (b, 1)), jnp.tile(sin, (b, 1))
        q_rot, ks, kw = nsa_rope(proj, cos_r, sin_r, b * t, 1, qw, kvw)
        q_rot3, ks3, kw3 = q_rot.reshape(b, t, qw), ks.reshape(b, t, kvw), kw.reshape(b, t, kvw)
        pk4 = jnp.broadcast_to(pool_k[:, None, None], (NSA_BLOCK, kv, NSA_DH))
        pv4 = jnp.broadcast_to(pool_v[:, None, None], (NSA_BLOCK, kv, NSA_DH))
        kcmp, vcmp = nsa_pool_pages(pool_ck, pool_cv, page_table, pk4, pv4, 4)
        total = past_len + t
        n_cb = total // NSA_BLOCK
        n_blk = -(-total // NSA_BLOCK)
        nb_pad = -(-n_blk // LANES) * LANES
        o_cmp, sel = nsa_cmp(proj3, kcmp, vcmp, t, past_len, n_cb, n_blk, nb_pad, kvw)
        padp = lambda a: jnp.concatenate([a, jnp.zeros((b, page - t, kvw), F32)], axis=1)
        flat = lambda p_: p_.reshape(p_.shape[0], page, kvw)
        o_sel = nsa_paged_sel(q_rot3, sel, flat(pool_sk), flat(pool_sv), page_table, padp(ks3), padp(vs), 8)
        keep = prev_kw.shape[1]
        kw_ext = jnp.concatenate([prev_kw.reshape(b, keep, kvw), kw3], axis=1)
        vw_ext = jnp.concatenate([prev_vw.reshape(b, keep, kvw), vw], axis=1)
        t_ext = keep + t
        t_pad = -(-t_ext // LANES) * LANES
        pade = lambda a: jnp.concatenate([a, jnp.zeros((b, t_pad - t_ext, kvw), F32)], axis=1)
        (o_win,) = nsa_attend(q_rot3, t, LANES, past_len, past_len - keep, win_args=(pade(kw_ext), pade(vw_ext)))
        win_k, win_v = kw_ext[:, t_ext - keep:], vw_ext[:, t_ext - keep:]

    a = nsa_combine(o_cmp.reshape(b * t, qw), o_sel.reshape(b * t, qw), o_win.reshape(b * t, qw), gates,
                    min(b * t, 512))
    y = out_proj(a, w_out, h, tmo, 512)
    shp = lambda x, n: x.reshape(b, n, kv, NSA_DH)
    return y, (shp(kc, t), shp(vc, t), shp(ks3, t), shp(vs, t), shp(win_k, keep), shp(win_v, keep))


def _s5_layer(h, b, t, gain, prm, w_glu, s_re, s_im):
    tm, tmo = _tiles(h.shape[0])
    u = rmsnorm_rows(h, gain, min(h.shape[0], 512))
    groups = prm[0].shape[0]
    if s_re is None:
        z, st = s5_scan(u, prm, t, math.gcd(t, 32))
    else:
        s0 = jnp.concatenate([s_re, s_im], axis=-1).transpose(1, 0, 2)
        z, st = s5_scan(u, prm, t, t, s0)
    y = out_glu(z, w_glu, h, tmo, 512)
    st = st.transpose(1, 0, 2)
    return y, (st[..., :S5_STATE], st[..., S5_STATE:])


def _ffn_layer(h, b, t, gain, w_in, conv_w, conv_b, w_out, buf):
    tm, tmo = _tiles(h.shape[0])
    ff2 = w_in.shape[1]
    if buf is None:
        tm = min(512, t)
        act, tg, tv = ffn_in(h, gain, w_in, conv_w, conv_b, tm, 512, t)
        per = t // tm
        last = lambda a: a[per - 1::per, SUBLANES - (CONV_W - 1):, :]
        state = jnp.concatenate([last(tg), last(tv)], axis=-1)
    else:
        zrow = jnp.zeros((b, t - 1, ff2), F32)
        p1 = jnp.concatenate([buf[:, 1:2], zrow], axis=1).reshape(b * t, ff2)
        p2 = jnp.concatenate([buf, zrow[:, 1:]], axis=1).reshape(b * t, ff2)
        act, tg, tv = ffn_in(h, gain, w_in, conv_w, conv_b, b * t, 512, t, hist=(p1, p2))
        up = jnp.concatenate([tg[0], tv[0]], axis=-1).reshape(b, t, ff2)
        state = jnp.concatenate([buf, up], axis=1)[:, t:]
    return out_proj(act, w_out, h, min(h.shape[0], 1024), 256), state


def kernel(x_prompt, x_sample, state_gla, state_hgrn, cache_nsa_cmp_k, cache_nsa_cmp_v, cache_nsa_sel_k, cache_nsa_sel_v, cache_nsa_win_k, cache_nsa_win_v, state_s5_re, state_s5_im, state_ffn_conv, page_table, norm_mix, norm_ffn, final_norm, gla_w_in, gla_w_alpha, gla_b_alpha, gla_head_norm, gla_w_out, hgrn_w_in, hgrn_lower_bound, hgrn_head_norm, hgrn_w_out, nsa_w_in, nsa_pool_k, nsa_pool_v, nsa_w_out, s5_a_re, s5_a_im, s5_log_dt, s5_b_re, s5_b_im, s5_c_re, s5_c_im, s5_d, s5_w_glu, ffn_w_in, ffn_conv_w, ffn_conv_b, ffn_w_out):
    bp, tp, d = x_prompt.shape
    bs, ts, _ = x_sample.shape
    depth = norm_mix.shape[0]
    n_mixers = 4
    gla_heads, gla_dk, gla_dv = state_gla.shape[2], state_gla.shape[3], state_gla.shape[4]
    hgrn_heads, hgrn_dk = state_hgrn.shape[2], state_hgrn.shape[3]
    nsa_kv = cache_nsa_cmp_k.shape[3]
    nsa_heads = d // NSA_DH
    hp = x_prompt.reshape(bp * tp, d)
    hs = x_sample.reshape(bs * ts, d)
    bf = lambda w: w.astype(BF16)

    def pad_cols(w, n):
        return jnp.concatenate([w, jnp.zeros((w.shape[0], n - w.shape[1]), w.dtype)], axis=1)

    outs = {k: [] for k in ("gla_p", "gla_s", "hgrn_p", "hgrn_s", "nsa_p", "nsa_s", "s5_p", "s5_s", "conv_p", "conv_s")}
    for i in range(depth):
        kind, j = i % n_mixers, i // n_mixers
        if kind == 0:
            n_main = 2 * gla_heads * gla_dk + 2 * gla_heads * gla_dv
            w_in = cast_bf16(gla_w_in, j)
            w_tail = bf(pad_cols(gla_w_in[j, :, n_main:], LANES))
            rank = gla_w_alpha.shape[1]
            wa = bf(jnp.concatenate([gla_w_alpha[j], jnp.zeros((LANES - rank, gla_w_alpha.shape[2]), F32)], axis=0))
            common = (norm_mix[i], w_in, w_tail, wa, gla_b_alpha[j], gla_head_norm[j], cast_bf16(gla_w_out, j))
            hp, st_p = _gla_layer(hp, bp, tp, *common, None, gla_heads, gla_dk, gla_dv)
            hs, st_s = _gla_layer(hs, bs, ts, *common, state_gla[j], gla_heads, gla_dk, gla_dv)
            outs["gla_p"].append(st_p)
            outs["gla_s"].append(st_s)
        elif kind == 1:
            common = (norm_mix[i], cast_bf16(hgrn_w_in, j), hgrn_lower_bound, i, hgrn_head_norm[j],
                      cast_bf16(hgrn_w_out, j))
            hp, st_p = _hgrn_layer(hp, bp, tp, *common, None, hgrn_heads, hgrn_dk)
            hs, st_s = _hgrn_layer(hs, bs, ts, *common, state_hgrn[j], hgrn_heads, hgrn_dk)
            outs["hgrn_p"].append(st_p)
            outs["hgrn_s"].append(st_s)
        elif kind == 2:
            n_main = nsa_heads * NSA_DH + 6 * nsa_kv * NSA_DH
            w_in = cast_bf16(nsa_w_in, j)
            w_gates = bf(pad_cols(nsa_w_in[j, :, n_main:], LANES))
            common = (norm_mix[i], w_in, w_gates, nsa_pool_k[j], nsa_pool_v[j], cast_bf16(nsa_w_out, j))
            hp, st_p = _nsa_layer(hp, bp, tp, 0, *common, None, nsa_heads, nsa_kv)
            past = (cache_nsa_cmp_k[j], cache_nsa_cmp_v[j], cache_nsa_sel_k[j], cache_nsa_sel_v[j],
                    page_table, cache_nsa_win_k[j], cache_nsa_win_v[j])
            hs, st_s = _nsa_layer(hs, bs, ts, page_table.shape[1] * cache_nsa_cmp_k.shape[2], *common, past,
                                  nsa_heads, nsa_kv)
            outs["nsa_p"].append(st_p)
            outs["nsa_s"].append(st_s)
        else:
            prm = _s5_params(s5_a_re[j], s5_a_im[j], s5_log_dt[j], s5_b_re[j], s5_b_im[j], s5_c_re[j],
                             s5_c_im[j], s5_d[j])
            w_glu = cast_bf16(s5_w_glu, j)
            hp, st_p = _s5_layer(hp, bp, tp, norm_mix[i], prm, w_glu, None, None)
            hs, st_s = _s5_layer(hs, bs, ts, norm_mix[i], prm, w_glu, state_s5_re[j], state_s5_im[j])
            outs["s5_p"].append(st_p)
            outs["s5_s"].append(st_s)
        fw = (norm_ffn[i], cast_bf16(ffn_w_in, i), ffn_conv_w[i], ffn_conv_b[i], cast_bf16(ffn_w_out, i))
        hp, cb_p = _ffn_layer(hp, bp, tp, *fw, None)
        hs, cb_s = _ffn_layer(hs, bs, ts, *fw, state_ffn_conv[i])
        outs["conv_p"].append(cb_p)
        outs["conv_s"].append(cb_s)

    y_prompt = rmsnorm_rows(hp, final_norm, min(hp.shape[0], 512)).reshape(bp, tp, d)
    y_sample = rmsnorm_rows(hs, final_norm, min(hs.shape[0], 512)).reshape(bs, ts, d)
    stack = lambda xs: jnp.stack(xs)
    pick = lambda key, r: stack([e[r] for e in outs[key]])
    res = [y_prompt, y_sample, stack(outs["gla_p"]), stack(outs["gla_s"]), stack(outs["hgrn_p"]), stack(outs["hgrn_s"])]
    for r in range(6):
        res += [pick("nsa_p", r), pick("nsa_s", r)]
    for r in range(2):
        res += [pick("s5_p", r), pick("s5_s", r)]
    res += [stack(outs["conv_p"]), stack(outs["conv_s"])]
    return tuple(res)
```

```python
import functools
import math

import jax
import jax.numpy as jnp
from jax import lax
from jax.experimental import pallas as pl
from jax.experimental.pallas import tpu as pltpu

F32 = jnp.float32
BF16 = jnp.bfloat16
HIGHEST = lax.Precision.HIGHEST

RMS_EPS = 1e-6
ROPE_THETA = 10000.0
NEG = -1e30
CHUNK = 64
SUBCHUNK = 16
GLA_TEMP = 16.0
NSA_BLOCK = 64
NSA_TOP_N = 16
NSA_WINDOW = 512
NSA_GROUP = 4
NSA_DH = 128
S5_CH = 16
S5_STATE = 64
S5_GROUPS_PER_STEP = 2
CONV_W = 3
LANES = 128
SUBLANES = 8
VMEM_LIMIT = 48 * 1024 * 1024


def _cp(*sem):
    return pltpu.CompilerParams(dimension_semantics=sem, vmem_limit_bytes=VMEM_LIMIT)


def _mm(a, b):
    return jnp.dot(a.astype(BF16), b.astype(BF16), preferred_element_type=F32)


def _mm_nt(a, b):
    return lax.dot_general(a.astype(BF16), b.astype(BF16), (((1,), (1,)), ((), ())),
                           preferred_element_type=F32)


def _sigmoid(x):
    return 1.0 / (1.0 + jnp.exp(-x))


def _rms(x, g):
    return x * lax.rsqrt(jnp.mean(x * x, axis=-1, keepdims=True) + RMS_EPS) * g


CAST_BLOCK_BYTES = 4 * 1024 * 1024


def _cast_body(w_ref, o_ref):
    o_ref[...] = w_ref[0].astype(BF16)


def cast_bf16(w_stack, layer):
    _, k, n = w_stack.shape
    tk = 16
    while k % (2 * tk) == 0 and 2 * tk * n * 4 <= CAST_BLOCK_BYTES:
        tk *= 2
    return pl.pallas_call(
        _cast_body, grid=(k // tk,),
        in_specs=[pl.BlockSpec((1, tk, n), lambda i: (layer, i, 0))],
        out_specs=pl.BlockSpec((tk, n), lambda i: (i, 0)),
        out_shape=jax.ShapeDtypeStruct((k, n), BF16),
        compiler_params=_cp("arbitrary"), name="cast_bf16")(w_stack)


def _norm_body(x_ref, g_ref, o_ref):
    o_ref[...] = _rms(x_ref[...], g_ref[...])


def rmsnorm_rows(x, gain, tm):
    m, d = x.shape
    return pl.pallas_call(
        _norm_body, grid=(m // tm,),
        in_specs=[pl.BlockSpec((tm, d), lambda i: (i, 0)), pl.BlockSpec((1, d), lambda i: (0, 0))],
        out_specs=pl.BlockSpec((tm, d), lambda i: (i, 0)),
        out_shape=jax.ShapeDtypeStruct((m, d), F32),
        compiler_params=_cp("arbitrary"), name="rmsnorm")(x, gain.reshape(1, d))


def _proj_body(x_ref, g_ref, w_ref, o_ref, xn_ref):
    @pl.when(pl.program_id(1) == 0)
    def _():
        xn_ref[...] = _rms(x_ref[...], g_ref[...]).astype(BF16)
    o_ref[...] = jnp.dot(xn_ref[...], w_ref[...], preferred_element_type=F32)


def norm_proj(x, gain, w, n_out, tm, tn):
    m, d = x.shape
    return pl.pallas_call(
        _proj_body, grid=(m // tm, n_out // tn),
        in_specs=[pl.BlockSpec((tm, d), lambda i, j: (i, 0)),
                  pl.BlockSpec((1, d), lambda i, j: (0, 0)),
                  pl.BlockSpec((d, tn), lambda i, j: (0, j))],
        out_specs=pl.BlockSpec((tm, tn), lambda i, j: (i, j)),
        out_shape=jax.ShapeDtypeStruct((m, n_out), F32),
        scratch_shapes=[pltpu.VMEM((tm, d), BF16)],
        compiler_params=_cp("arbitrary", "arbitrary"), name="norm_proj")(x, gain.reshape(1, d), w)


def _out_body(a_ref, w_ref, r_ref, o_ref):
    o_ref[...] = r_ref[...] + jnp.dot(a_ref[...].astype(BF16), w_ref[...], preferred_element_type=F32)


def out_proj(a, w, res, tm, tn):
    m, k = a.shape
    n = w.shape[1]
    return pl.pallas_call(
        _out_body, grid=(m // tm, n // tn),
        in_specs=[pl.BlockSpec((tm, k), lambda i, j: (i, 0)),
                  pl.BlockSpec((k, tn), lambda i, j: (0, j)),
                  pl.BlockSpec((tm, tn), lambda i, j: (i, j))],
        out_specs=pl.BlockSpec((tm, tn), lambda i, j: (i, j)),
        out_shape=jax.ShapeDtypeStruct((m, n), F32),
        compiler_params=_cp("arbitrary", "arbitrary"), name="out_proj")(a, w, res)


def _out_glu_body(a_ref, w1_ref, w2_ref, r_ref, o_ref):
    a = a_ref[...]
    g1 = jnp.dot(a, w1_ref[...], preferred_element_type=F32)
    g2 = jnp.dot(a, w2_ref[...], preferred_element_type=F32)
    o_ref[...] = r_ref[...] + g1 * _sigmoid(g2)


def out_glu(a, w, res, tm, tn):
    m, k = a.shape
    n = w.shape[1] // 2
    nj = n // tn
    return pl.pallas_call(
        _out_glu_body, grid=(m // tm, nj),
        in_specs=[pl.BlockSpec((tm, k), lambda i, j: (i, 0)),
                  pl.BlockSpec((k, tn), lambda i, j: (0, j)),
                  pl.BlockSpec((k, tn), lambda i, j: (0, nj + j)),
                  pl.BlockSpec((tm, tn), lambda i, j: (i, j))],
        out_specs=pl.BlockSpec((tm, tn), lambda i, j: (i, j)),
        out_shape=jax.ShapeDtypeStruct((m, n), F32),
        compiler_params=_cp("arbitrary", "arbitrary"), name="out_glu")(a, w, w, res)


def _ffn_in_body(*refs, seg, tiles_per_seq, tail_rows, has_state):
    if has_state:
        (x_ref, g_ref, wg_ref, wv_ref, cwg_ref, cwv_ref, cbg_ref, cbv_ref,
         p1g_ref, p2g_ref, p1v_ref, p2v_ref,
         act_ref, tg_ref, tv_ref, xn_ref, carry_ref) = refs
    else:
        (x_ref, g_ref, wg_ref, wv_ref, cwg_ref, cwv_ref, cbg_ref, cbv_ref,
         act_ref, tg_ref, tv_ref, xn_ref, carry_ref) = refs
    i = pl.program_id(0)
    f = pl.program_id(1)
    tm = x_ref.shape[0]

    @pl.when(f == 0)
    def _():
        xn_ref[...] = _rms(x_ref[...], g_ref[...]).astype(BF16)

    xn = xn_ref[...]
    tf = wg_ref.shape[1]
    sb = tf
    row = lax.broadcasted_iota(jnp.int32, (tm, sb), 0)
    rowm = row % seg
    fresh = (i % tiles_per_seq) == 0

    def conv(w_ref, cw_ref, cb_ref, kind, p1_ref, p2_ref, cs, t_ref):
        u = jnp.dot(xn, w_ref[:, cs], preferred_element_type=F32)
        if has_state:
            p1 = p1_ref[:, cs]
            p2 = p2_ref[:, cs]
        else:
            prev = carry_ref[kind, f, :, cs]
            prev = jnp.where(fresh, 0.0, prev)
            prev0 = prev[SUBLANES - 2:SUBLANES - 1, :]
            prev1 = prev[SUBLANES - 1:SUBLANES, :]
            p1 = jnp.broadcast_to(prev1, u.shape)
            p2 = jnp.where(row == 0, prev0, prev1)
            carry_ref[kind, f, :, cs] = u[tm - SUBLANES:, :]
        u1 = jnp.where(rowm < 1, p1, pltpu.roll(u, 1, 0))
        u2 = jnp.where(rowm < 2, p2, pltpu.roll(u, 2, 0))
        cw = cw_ref[:, cs]
        t_ref[0, :, cs] = u[tm - tail_rows:, :]
        return cw[0:1, :] * u2 + cw[1:2, :] * u1 + cw[2:3, :] * u + cb_ref[:, cs]

    for jb in range(tf // sb):
        cs = slice(jb * sb, (jb + 1) * sb)
        mg = conv(wg_ref, cwg_ref, cbg_ref, 0, p1g_ref if has_state else None, p2g_ref if has_state else None,
                  cs, tg_ref)
        mv = conv(wv_ref, cwv_ref, cbv_ref, 1, p1v_ref if has_state else None, p2v_ref if has_state else None,
                  cs, tv_ref)
        act_ref[:, cs] = (mg * _sigmoid(mg) * mv).astype(BF16)


def ffn_in(x, gain, w_in, conv_w, conv_b, tm, tf, seq_len, hist=None):
    m, d = x.shape
    ff = w_in.shape[1] // 2
    nf = ff // tf
    nb = m // tm
    has_state = hist is not None
    if has_state:
        seg, tiles_per_seq, tail_rows = seq_len, 1, tm
    else:
        seg, tiles_per_seq, tail_rows = tm, seq_len // tm, SUBLANES
    wspec_g = pl.BlockSpec((d, tf), lambda i, f: (0, f))
    wspec_v = pl.BlockSpec((d, tf), lambda i, f: (0, nf + f))
    cspec_g = lambda r: pl.BlockSpec((r, tf), lambda i, f: (0, f))
    cspec_v = lambda r: pl.BlockSpec((r, tf), lambda i, f: (0, nf + f))
    in_specs = [pl.BlockSpec((tm, d), lambda i, f: (i, 0)), pl.BlockSpec((1, d), lambda i, f: (0, 0)),
                wspec_g, wspec_v, cspec_g(CONV_W), cspec_v(CONV_W), cspec_g(1), cspec_v(1)]
    args = [x, gain.reshape(1, d), w_in, w_in, conv_w, conv_w, conv_b.reshape(1, -1), conv_b.reshape(1, -1)]
    if has_state:
        p1, p2 = hist
        in_specs += [pl.BlockSpec((tm, tf), lambda i, f: (i, f)), pl.BlockSpec((tm, tf), lambda i, f: (i, f)),
                     pl.BlockSpec((tm, tf), lambda i, f: (i, nf + f)), pl.BlockSpec((tm, tf), lambda i, f: (i, nf + f))]
        args += [p1, p2, p1, p2]
    body = functools.partial(_ffn_in_body, seg=seg, tiles_per_seq=tiles_per_seq,
                             tail_rows=tail_rows, has_state=has_state)
    return pl.pallas_call(
        body, grid=(nb, nf), in_specs=in_specs,
        out_specs=[pl.BlockSpec((tm, tf), lambda i, f: (i, f)),
                   pl.BlockSpec((1, tail_rows, tf), lambda i, f: (i, 0, f)),
                   pl.BlockSpec((1, tail_rows, tf), lambda i, f: (i, 0, f))],
        out_shape=[jax.ShapeDtypeStruct((m, ff), BF16),
                   jax.ShapeDtypeStruct((nb, tail_rows, ff), F32),
                   jax.ShapeDtypeStruct((nb, tail_rows, ff), F32)],
        scratch_shapes=[pltpu.VMEM((tm, d), BF16), pltpu.VMEM((2, nf, SUBLANES, tf), F32)],
        compiler_params=_cp("arbitrary", "arbitrary"), name="ffn_in")(*args)


def _pad_rows(a, rows):
    if a.shape[0] == rows:
        return a
    return jnp.concatenate([a, jnp.zeros((rows - a.shape[0], a.shape[1]), a.dtype)], axis=0)


def _glr_chunk(q, k, v, g, st, c, sub):
    dk = q.shape[1]
    row = lax.broadcasted_iota(jnp.int32, (c, LANES), 0)
    col = lax.broadcasted_iota(jnp.int32, (c, LANES), 1)
    trow = lax.broadcasted_iota(jnp.int32, (c, c), 0)
    tcol = lax.broadcasted_iota(jnp.int32, (c, c), 1)
    tri = (trow >= tcol).astype(F32)
    cum = jnp.dot(tri, g, preferred_element_type=F32, precision=HIGHEST)
    last = cum[c - 1:c, :]
    inter = _mm_nt(q * jnp.exp(cum), st)

    rowk = lax.broadcasted_iota(jnp.int32, (c, dk), 0)
    rm = rowk % sub
    ones = jnp.ones((dk, LANES), BF16)
    att = jnp.zeros((c, LANES), F32)
    for d in range(sub):
        if d == 0:
            p = q * k
        else:
            ks = pltpu.roll(k, d, 0)
            cs = pltpu.roll(cum, d, 0)
            p = q * ks * jnp.exp(jnp.where(rm >= d, cum - cs, NEG))
        a = jnp.dot(p.astype(BF16), ones, preferred_element_type=F32)
        att = att + jnp.where(col == row - d, a, 0.0)
    if c > sub:
        blocks = [jnp.zeros((sub, LANES), F32)]
        for i in range(1, c // sub):
            cs = cum[i * sub - 1:i * sub, :]
            qi = q[i * sub:(i + 1) * sub, :] * jnp.exp(cum[i * sub:(i + 1) * sub, :] - cs)
            kj = k * jnp.exp(jnp.where(rowk < i * sub, cs - cum, NEG))
            blocks.append(_mm_nt(qi, _pad_rows(kj, LANES)))
        att = att + jnp.concatenate(blocks, axis=0)
    vpad = _pad_rows(v, LANES)
    intra = _mm(att, vpad)
    kd = _pad_rows(k * jnp.exp(last - cum), LANES)
    st_new = st * jnp.exp(last) + _mm(vpad.T, kd)
    return inter + intra, st_new


def _rec_body(*refs, mode, c, sub, n_chunks, dk, dv, hb, layer, has_state):
    refs = list(refs)
    if mode == "gla":
        q_ref, k_ref, v_ref, r_ref, lr_ref, wa_ref, ba_ref, hn_ref = refs[:8]
        rest = refs[8:]
    else:
        q_ref, k_ref, v_ref, r_ref, lb_ref, hn_ref = refs[:6]
        rest = refs[6:]
    if has_state:
        s0_ref, og_ref, sout_ref, st_ref = rest
    else:
        og_ref, sout_ref, st_ref = rest
    tstep = pl.program_id(2)

    @pl.when(tstep == 0)
    def _():
        for hh in range(hb):
            if has_state:
                st_ref[hh] = s0_ref[0, hh].T
            else:
                st_ref[hh] = jnp.zeros(st_ref.shape[1:], F32)

    if mode == "hgrn":
        lbx = lb_ref[...]
        e = jnp.exp(lbx - jnp.max(lbx, axis=0, keepdims=True))
        sm = e / jnp.sum(e, axis=0, keepdims=True)
        lb_all = jnp.zeros((1, hb * dk), F32)
        for li in range(1, layer + 1):
            lb_all = lb_all + sm[li:li + 1, :]

    for ci in range(n_chunks):
        sl = slice(ci * c, (ci + 1) * c)
        for hh in range(hb):
            hk = slice(hh * dk, (hh + 1) * dk)
            hv = slice(hh * dv, (hh + 1) * dv)
            if mode == "gla":
                q = q_ref[0, sl, hk] * (dk ** -0.5)
                k = k_ref[0, sl, hk]
                z = _mm(lr_ref[0, sl, :], wa_ref[:, hk]) + ba_ref[:, hk]
                g = -(jnp.maximum(-z, 0.0) + jnp.log1p(jnp.exp(-jnp.abs(z)))) / GLA_TEMP
            else:
                qz = q_ref[0, sl, hk]
                q = qz * _sigmoid(qz)
                lbv = lb_all[:, hk]
                fg = lbv + (1.0 - lbv) * _sigmoid(k_ref[0, sl, hk])
                k = 1.0 - fg
                g = jnp.log(fg)
            v = v_ref[0, sl, hv]
            o, st_new = _glr_chunk(q, k, v, g, st_ref[hh], c, sub)
            st_ref[hh] = st_new
            of = o * lax.rsqrt(jnp.mean(o * o, axis=-1, keepdims=True) + RMS_EPS) * hn_ref[...]
            gate = r_ref[0, sl, hv]
            og_ref[0, sl, hv] = (of * (gate * _sigmoid(gate))).astype(BF16)

    @pl.when(tstep == pl.num_programs(2) - 1)
    def _():
        for hh in range(hb):
            sout_ref[0, hh] = st_ref[hh].T


def recurrence(mode, proj, heads, dk, dv, hn, tb, c, extra, s0=None, layer=0, hb=1):
    b, t, _ = proj.shape
    sub = min(SUBCHUNK, c)
    has_state = s0 is not None
    wk, wv = hb * dk, hb * dv
    nh = heads // hb
    if mode == "gla":
        lr, wa, ba = extra
        koff, voff = nh, (2 * heads * dk) // wv
        roff = voff + nh
        in_specs = [pl.BlockSpec((1, tb, wk), lambda i, h, s: (i, s, h)),
                    pl.BlockSpec((1, tb, wk), lambda i, h, s: (i, s, koff + h)),
                    pl.BlockSpec((1, tb, wv), lambda i, h, s: (i, s, voff + h)),
                    pl.BlockSpec((1, tb, wv), lambda i, h, s: (i, s, roff + h)),
                    pl.BlockSpec((1, tb, LANES), lambda i, h, s: (i, s, 0)),
                    pl.BlockSpec((LANES, wk), lambda i, h, s: (0, h)),
                    pl.BlockSpec((1, wk), lambda i, h, s: (0, h)),
                    pl.BlockSpec((1, dv), lambda i, h, s: (0, 0))]
        args = [proj, proj, proj, proj, lr, wa, ba, hn.reshape(1, dv)]
    else:
        (lb,) = extra
        in_specs = [pl.BlockSpec((1, tb, wk), lambda i, h, s: (i, s, h)),
                    pl.BlockSpec((1, tb, wk), lambda i, h, s: (i, s, nh + h)),
                    pl.BlockSpec((1, tb, wv), lambda i, h, s: (i, s, 2 * nh + h)),
                    pl.BlockSpec((1, tb, wv), lambda i, h, s: (i, s, 3 * nh + h)),
                    pl.BlockSpec((lb.shape[0], wk), lambda i, h, s: (0, h)),
                    pl.BlockSpec((1, dv), lambda i, h, s: (0, 0))]
        args = [proj, proj, proj, proj, lb, hn.reshape(1, dv)]
    if has_state:
        in_specs.append(pl.BlockSpec((1, hb, dk, dv), lambda i, h, s: (i, h, 0, 0)))
        args.append(s0)
    body = functools.partial(_rec_body, mode=mode, c=c, sub=sub, n_chunks=tb // c, dk=dk, dv=dv, hb=hb,
                             layer=layer, has_state=has_state)
    return pl.pallas_call(
        body, grid=(b, nh, t // tb), in_specs=in_specs,
        out_specs=[pl.BlockSpec((1, tb, wv), lambda i, h, s: (i, s, h)),
                   pl.BlockSpec((1, hb, dk, dv), lambda i, h, s: (i, h, 0, 0))],
        out_shape=[jax.ShapeDtypeStruct((b, t, heads * dv), BF16),
                   jax.ShapeDtypeStruct((b, heads, dk, dv), F32)],
        scratch_shapes=[pltpu.VMEM((hb, dv, dk), F32)],
        compiler_params=_cp("arbitrary", "arbitrary", "arbitrary"), name="recurrence_" + mode)(*args)


def _rope_body(q_ref, ks_ref, kw_ref, cos_ref, sin_ref, qo_ref, kso_ref, kwo_ref):
    cos = cos_ref[...]
    sin = sin_ref[...]

    def rot(src, dst):
        for h in range(src.shape[1] // LANES):
            x = src[:, h * LANES:(h + 1) * LANES]
            dst[:, h * LANES:(h + 1) * LANES] = x * cos + pltpu.roll(x, LANES // 2, 1) * sin

    rot(q_ref, qo_ref)
    rot(ks_ref, kso_ref)
    rot(kw_ref, kwo_ref)


def nsa_rope(proj, cos, sin, tm, tiles_per_seq, qw, kvw):
    m = proj.shape[0]
    ks_blk = (qw + 2 * kvw) // kvw
    kw_blk = (qw + 4 * kvw) // kvw
    return pl.pallas_call(
        _rope_body, grid=(m // tm,),
        in_specs=[pl.BlockSpec((tm, qw), lambda i: (i, 0)),
                  pl.BlockSpec((tm, kvw), lambda i: (i, ks_blk)),
                  pl.BlockSpec((tm, kvw), lambda i: (i, kw_blk)),
                  pl.BlockSpec((tm, LANES), lambda i: (i % tiles_per_seq, 0)),
                  pl.BlockSpec((tm, LANES), lambda i: (i % tiles_per_seq, 0))],
        out_specs=[pl.BlockSpec((tm, qw), lambda i: (i, 0)),
                   pl.BlockSpec((tm, kvw), lambda i: (i, 0)),
                   pl.BlockSpec((tm, kvw), lambda i: (i, 0))],
        out_shape=[jax.ShapeDtypeStruct((m, qw), F32), jax.ShapeDtypeStruct((m, kvw), F32),
                   jax.ShapeDtypeStruct((m, kvw), F32)],
        compiler_params=_cp("arbitrary"), name="nsa_rope")(proj, proj, proj, cos, sin)


def _pool_rows(x, pw):
    n = x.shape[0] // NSA_BLOCK
    return jnp.sum(x.reshape(n, NSA_BLOCK, x.shape[1]) * pw[None], axis=1)


def _pool_body(kc_ref, vc_ref, pk_ref, pv_ref, ko_ref, vo_ref, *, n_cb):
    ko_ref[...] = jnp.zeros(ko_ref.shape, F32)
    vo_ref[...] = jnp.zeros(vo_ref.shape, F32)
    kp = _pool_rows(kc_ref[0, 0:n_cb * NSA_BLOCK, :], pk_ref[...])
    vp = _pool_rows(vc_ref[0, 0:n_cb * NSA_BLOCK, :], pv_ref[...])
    for g in range(ko_ref.shape[1]):
        ko_ref[0, g, 0:n_cb, :] = kp[:, g * LANES:(g + 1) * LANES]
        vo_ref[0, g, 0:n_cb, :] = vp[:, g * LANES:(g + 1) * LANES]


def nsa_pool_prompt(proj3, pk, pv, qw, kvw, n_pad):
    b, t, _ = proj3.shape
    n_cb = t // NSA_BLOCK
    kc_blk = qw // kvw
    return pl.pallas_call(
        functools.partial(_pool_body, n_cb=n_cb), grid=(b,),
        in_specs=[pl.BlockSpec((1, t, kvw), lambda i: (i, 0, kc_blk)),
                  pl.BlockSpec((1, t, kvw), lambda i: (i, 0, kc_blk + 1)),
                  pl.BlockSpec((NSA_BLOCK, kvw), lambda i: (0, 0)),
                  pl.BlockSpec((NSA_BLOCK, kvw), lambda i: (0, 0))],
        out_specs=[pl.BlockSpec((1, kvw // LANES, n_pad, LANES), lambda i: (i, 0, 0, 0)),
                   pl.BlockSpec((1, kvw // LANES, n_pad, LANES), lambda i: (i, 0, 0, 0))],
        out_shape=[jax.ShapeDtypeStruct((b, kvw // LANES, n_pad, LANES), F32)] * 2,
        compiler_params=_cp("arbitrary"), name="nsa_pool")(proj3, proj3, pk, pv)


def _pool_pages_body(pt_ref, *refs, pg):
    k_refs = refs[:pg]
    v_refs = refs[pg:2 * pg]
    pk_ref, pv_ref, ko_ref, vo_ref = refs[2 * pg:]
    _, page, kv, dh = k_refs[0].shape
    per = page // NSA_BLOCK

    def pooled(x_ref, pw_ref):
        return jnp.sum(x_ref[0].reshape(per, NSA_BLOCK, kv, dh) * pw_ref[...][None], axis=1)

    for i in range(pg):
        kp, vp = pooled(k_refs[i], pk_ref), pooled(v_refs[i], pv_ref)
        for g in range(kv):
            ko_ref[0, g, i * per:(i + 1) * per, :] = kp[:, g, :]
            vo_ref[0, g, i * per:(i + 1) * per, :] = vp[:, g, :]


def nsa_pool_pages(pool_k, pool_v, page_table, pk, pv, pg):
    b, n_pages = page_table.shape
    _, page, kv, dh = pool_k.shape
    per = page // NSA_BLOCK
    page_spec = lambda i: pl.BlockSpec((1, page, kv, dh), lambda bi, s, pt, i=i: (pt[bi, s * pg + i], 0, 0, 0))
    wspec = pl.BlockSpec((NSA_BLOCK, kv, dh), lambda bi, s, pt: (0, 0, 0))
    ospec = pl.BlockSpec((1, kv, pg * per, dh), lambda bi, s, pt: (bi, 0, s, 0))
    gs = pltpu.PrefetchScalarGridSpec(
        num_scalar_prefetch=1, grid=(b, n_pages // pg),
        in_specs=[page_spec(i) for i in range(pg)] + [page_spec(i) for i in range(pg)] + [wspec, wspec],
        out_specs=[ospec, ospec])
    n_blk = n_pages * per
    return pl.pallas_call(
        functools.partial(_pool_pages_body, pg=pg), grid_spec=gs,
        out_shape=[jax.ShapeDtypeStruct((b, kv, n_blk, dh), F32)] * 2,
        compiler_params=_cp("arbitrary", "arbitrary"), name="nsa_pool_pages")(
            page_table, *([pool_k] * pg), *([pool_v] * pg), pk, pv)


def _cmp_body(q_ref, kc_ref, vc_ref, o_ref, sel_ref, *, q_start, n_cb, n_blk, nb_pad):
    tq = q_ref.shape[1]
    ncp = kc_ref.shape[2]
    qt = pl.program_id(2)
    kc = kc_ref[0, 0]
    vc = vc_ref[0, 0]
    scale = NSA_DH ** -0.5
    colc = lax.broadcasted_iota(jnp.int32, (tq, ncp), 1)
    qposc = q_start + qt * tq + lax.broadcasted_iota(jnp.int32, (tq, ncp), 0)
    valid = ((colc + 1) * NSA_BLOCK - 1 <= qposc) & (colc < n_cb)
    imp = jnp.zeros((tq, ncp), F32)
    for r in range(NSA_GROUP):
        qr = q_ref[0, :, r * LANES:(r + 1) * LANES]
        s = jnp.where(valid, _mm_nt(qr, kc) * scale, NEG)
        m = jnp.max(s, axis=-1, keepdims=True)
        e = jnp.where(valid, jnp.exp(s - m), 0.0)
        p = e / jnp.maximum(jnp.sum(e, axis=-1, keepdims=True), 1e-30)
        o_ref[0, :, r * LANES:(r + 1) * LANES] = _mm(p, vc)
        imp = imp + p
    if nb_pad > ncp:
        imp = jnp.concatenate([imp, jnp.zeros((tq, nb_pad - ncp), F32)], axis=1)
    blk = lax.broadcasted_iota(jnp.int32, (tq, nb_pad), 1)
    qpos = q_start + qt * tq + lax.broadcasted_iota(jnp.int32, (tq, nb_pad), 0)
    cur = qpos // NSA_BLOCK
    forced = (blk == cur) | (blk == 0)
    score = jnp.where(blk > cur, -1.0, jnp.where(forced, NSA_GROUP + 1.0, imp))
    score = jnp.where(blk < n_blk, score, -2.0)
    blkf = blk.astype(F32)

    def pick(_, carry):
        sc, sel = carry
        mx = jnp.max(sc, axis=-1, keepdims=True)
        first = jnp.min(jnp.where(sc == mx, blkf, 1e9), axis=-1, keepdims=True)
        hit = blkf == first
        return jnp.where(hit, -3.0, sc), jnp.where(hit, 1.0, sel)

    _, sel = lax.fori_loop(0, min(NSA_TOP_N, n_blk), pick, (score, jnp.zeros((tq, nb_pad), F32)))
    sel_ref[0, 0] = sel


def nsa_cmp(q3, kcmp, vcmp, tq, q_start, n_cb, n_blk, nb_pad, kvw):
    b, t = q3.shape[0], q3.shape[1]
    ncp = kcmp.shape[2]
    kv = kvw // LANES
    gw = NSA_GROUP * LANES
    body = functools.partial(_cmp_body, q_start=q_start, n_cb=n_cb, n_blk=n_blk, nb_pad=nb_pad)
    return pl.pallas_call(
        body, grid=(b, kv, t // tq),
        in_specs=[pl.BlockSpec((1, tq, gw), lambda i, g, s: (i, s, g)),
                  pl.BlockSpec((1, 1, ncp, LANES), lambda i, g, s: (i, g, 0, 0)),
                  pl.BlockSpec((1, 1, ncp, LANES), lambda i, g, s: (i, g, 0, 0))],
        out_specs=[pl.BlockSpec((1, tq, gw), lambda i, g, s: (i, s, g)),
                   pl.BlockSpec((1, 1, tq, nb_pad), lambda i, g, s: (i, g, s, 0))],
        out_shape=[jax.ShapeDtypeStruct((b, t, kv * gw), F32),
                   jax.ShapeDtypeStruct((b, kv, t, nb_pad), F32)],
        compiler_params=_cp("arbitrary", "arbitrary", "arbitrary"), name="nsa_cmp")(q3, kcmp, vcmp)


def _softmax_step(carry, s, valid, vv):
    m, l, acc = carry
    s = jnp.where(valid, s, NEG)
    m_new = jnp.maximum(m, jnp.max(s, axis=-1, keepdims=True))
    p = jnp.where(valid, jnp.exp(s - m_new), 0.0)
    alpha = jnp.exp(m - m_new)
    l = alpha * l + jnp.sum(p, axis=-1, keepdims=True)
    acc = alpha * acc + _mm(p, vv)
    return m_new, l, acc


def _softmax_init(rows):
    return (jnp.full((rows, 1), NEG, F32), jnp.zeros((rows, 1), F32), jnp.zeros((rows, LANES), F32))


def _attn_body(*refs, tq, tk, n_kt, q_start, k_start, do_sel, do_win):
    refs = list(refs)
    q_ref = refs.pop(0)
    if do_sel:
        ks_ref, vs_ref, sel_ref = refs[:3]
        refs = refs[3:]
    if do_win:
        kw_ref, vw_ref = refs[:2]
        refs = refs[2:]
    outs = refs
    qt = pl.program_id(2)
    scale = NSA_DH ** -0.5
    rows = NSA_GROUP * tq
    q4 = jnp.concatenate([q_ref[0, :, r * LANES:(r + 1) * LANES] for r in range(NSA_GROUP)], axis=0).astype(BF16)
    q0 = q_start + qt * tq
    qpos = q0 + lax.broadcasted_iota(jnp.int32, (rows, tk), 0) % tq
    kcol = lax.broadcasted_iota(jnp.int32, (rows, tk), 1)

    def finish(carry, o_ref):
        m, l, acc = carry
        o = acc / jnp.maximum(l, 1e-30)
        for r in range(NSA_GROUP):
            o_ref[0, :, r * LANES:(r + 1) * LANES] = o[r * tq:(r + 1) * tq, :]

    if do_sel:
        selb = sel_ref[0, 0].astype(BF16)
        nbp = selb.shape[1]
        en = lax.broadcasted_iota(jnp.int32, (nbp, tk), 0)
        es = lax.broadcasted_iota(jnp.int32, (nbp, tk), 1)

        def sel_step(kt, carry):
            off = pl.multiple_of(kt * tk, tk)
            kk = ks_ref[0, pl.ds(off, tk), :]
            vv = vs_ref[0, pl.ds(off, tk), :]
            s = _mm_nt(q4, kk) * scale
            kp0 = k_start + kt * tk
            expand = ((kp0 + es) // NSA_BLOCK == en).astype(BF16)
            chosen = jnp.dot(selb, expand, preferred_element_type=F32)
            chosen = jnp.concatenate([chosen] * NSA_GROUP, axis=0)
            valid = (chosen > 0.5) & (kp0 + kcol <= qpos)
            return _softmax_step(carry, s, valid, vv)

        hi = jnp.minimum(n_kt, (q0 + tq - 1 - k_start) // tk + 1)
        finish(lax.fori_loop(0, hi, sel_step, _softmax_init(rows)), outs.pop(0))

    if do_win:
        def win_step(kt, carry):
            off = pl.multiple_of(kt * tk, tk)
            kk = kw_ref[0, pl.ds(off, tk), :]
            vv = vw_ref[0, pl.ds(off, tk), :]
            s = _mm_nt(q4, kk) * scale
            kpos = k_start + kt * tk + kcol
            dist = qpos - kpos
            valid = (dist >= 0) & (dist < NSA_WINDOW) & (kpos >= 0)
            return _softmax_step(carry, s, valid, vv)

        lo = jnp.maximum(0, (q0 - (NSA_WINDOW - 1) - k_start) // tk)
        hi = jnp.minimum(n_kt, (q0 + tq - 1 - k_start) // tk + 1)
        finish(lax.fori_loop(lo, hi, win_step, _softmax_init(rows)), outs.pop(0))


def nsa_attend(q_rot3, tq, tk, q_start, k_start, sel_args=None, win_args=None):
    b, t, qw = q_rot3.shape
    kv = qw // (NSA_GROUP * LANES)
    gw = NSA_GROUP * LANES
    in_specs = [pl.BlockSpec((1, tq, gw), lambda i, g, s: (i, s, g))]
    args = [q_rot3]
    n_out = 0
    t_k = None
    if sel_args is not None:
        k, v, sel = sel_args
        t_k = k.shape[1]
        nbp = sel.shape[-1]
        in_specs += [pl.BlockSpec((1, t_k, LANES), lambda i, g, s: (i, 0, g)),
                     pl.BlockSpec((1, t_k, LANES), lambda i, g, s: (i, 0, g)),
                     pl.BlockSpec((1, 1, tq, nbp), lambda i, g, s: (i, g, s, 0))]
        args += [k, v, sel]
        n_out += 1
    if win_args is not None:
        k, v = win_args
        t_k = k.shape[1]
        in_specs += [pl.BlockSpec((1, t_k, LANES), lambda i, g, s: (i, 0, g)),
                     pl.BlockSpec((1, t_k, LANES), lambda i, g, s: (i, 0, g))]
        args += [k, v]
        n_out += 1
    body = functools.partial(_attn_body, tq=tq, tk=tk, n_kt=t_k // tk, q_start=q_start, k_start=k_start,
                             do_sel=sel_args is not None, do_win=win_args is not None)
    return pl.pallas_call(
        body, grid=(b, kv, t // tq), in_specs=in_specs,
        out_specs=[pl.BlockSpec((1, tq, gw), lambda i, g, s: (i, s, g))] * n_out,
        out_shape=[jax.ShapeDtypeStruct((b, t, qw), F32)] * n_out,
        compiler_params=_cp("arbitrary", "arbitrary", "arbitrary"), name="nsa_attend")(*args)


def _attn_t_body(q_ref, ks_ref, vs_ref, kw_ref, vw_ref, sel_ref, osel_ref, owin_ref, vst_ref, vwt_ref, *, tq, tk, n_kt):
    qt = pl.program_id(2)
    scale = NSA_DH ** -0.5
    t_k = ks_ref.shape[1]

    @pl.when(qt == 0)
    def _():
        for j in range(t_k // LANES):
            sl = slice(j * LANES, (j + 1) * LANES)
            vst_ref[:, sl] = vs_ref[0, sl, :].T.astype(BF16)
            vwt_ref[:, sl] = vw_ref[0, sl, :].T.astype(BF16)

    q0 = qt * tq
    cols = NSA_GROUP * tq
    sel_t = sel_ref[0, 0].T.astype(BF16)
    nbp = sel_t.shape[0]
    q4 = jnp.concatenate([q_ref[0, :, r * LANES:(r + 1) * LANES] for r in range(NSA_GROUP)], axis=0).astype(BF16)

    def scores(k_ref, off, n, allowed):
        allowed = jnp.concatenate([allowed] * NSA_GROUP, axis=1) > 0.5
        s = _mm_nt(k_ref[0, pl.ds(off, n), :], q4) * scale
        return jnp.where(allowed, s, NEG), allowed

    def write(o_ref, acc, l):
        o = acc / jnp.maximum(l, 1e-30)
        for r in range(NSA_GROUP):
            o_ref[0, :, r * LANES:(r + 1) * LANES] = o[:, r * tq:(r + 1) * tq].T

    kpos_l = lax.broadcasted_iota(jnp.int32, (tk, tq), 0)
    qpos = q0 + lax.broadcasted_iota(jnp.int32, (tk, tq), 1)
    en = lax.broadcasted_iota(jnp.int32, (tk, nbp), 1)
    es = lax.broadcasted_iota(jnp.int32, (tk, nbp), 0)

    def sel_step(kt, carry):
        m, l, acc = carry
        off = pl.multiple_of(kt * tk, tk)
        expand = ((kt * tk + es) // NSA_BLOCK == en).astype(BF16)
        chosen = jnp.dot(expand, sel_t, preferred_element_type=F32)
        allowed = jnp.where(kt * tk + kpos_l <= qpos, chosen, 0.0)
        s, valid = scores(ks_ref, off, tk, allowed)
        m_new = jnp.maximum(m, jnp.max(s, axis=0, keepdims=True))
        p = jnp.where(valid, jnp.exp(s - m_new), 0.0)
        alpha = jnp.exp(m - m_new)
        l = alpha * l + jnp.sum(p, axis=0, keepdims=True)
        acc = alpha * acc + jnp.dot(vst_ref[:, pl.ds(off, tk)], p.astype(BF16), preferred_element_type=F32)
        return m_new, l, acc

    init = (jnp.full((1, cols), NEG, F32), jnp.zeros((1, cols), F32), jnp.zeros((NSA_DH, cols), F32))
    m, l, acc = lax.fori_loop(0, jnp.minimum(n_kt, (q0 + tq - 1) // tk + 1), sel_step, init)
    write(osel_ref, acc, l)

    wk = min(t_k, NSA_WINDOW + tq)
    ws = pl.multiple_of(jnp.clip(q0 - NSA_WINDOW, 0, t_k - wk), LANES)
    dist = (q0 + lax.broadcasted_iota(jnp.int32, (wk, tq), 1)) - (ws + lax.broadcasted_iota(jnp.int32, (wk, tq), 0))
    s, valid = scores(kw_ref, ws, wk, jnp.where((dist >= 0) & (dist < NSA_WINDOW), 1.0, 0.0))
    p = jnp.where(valid, jnp.exp(s - jnp.max(s, axis=0, keepdims=True)), 0.0)
    acc = jnp.dot(vwt_ref[:, pl.ds(ws, wk)], p.astype(BF16), preferred_element_type=F32)
    write(owin_ref, acc, jnp.sum(p, axis=0, keepdims=True))


def nsa_attend_prompt(q_rot3, ks, vs, kw, vw, sel, tq, tk):
    b, t, qw = q_rot3.shape
    kv = qw // (NSA_GROUP * LANES)
    gw = NSA_GROUP * LANES
    kspec = pl.BlockSpec((1, t, LANES), lambda i, g, s: (i, 0, g))
    ospec = pl.BlockSpec((1, tq, gw), lambda i, g, s: (i, s, g))
    body = functools.partial(_attn_t_body, tq=tq, tk=tk, n_kt=t // tk)
    return pl.pallas_call(
        body, grid=(b, kv, t // tq),
        in_specs=[ospec, kspec, kspec, kspec, kspec,
                  pl.BlockSpec((1, 1, tq, sel.shape[-1]), lambda i, g, s: (i, g, s, 0))],
        out_specs=[ospec, ospec],
        out_shape=[jax.ShapeDtypeStruct((b, t, qw), F32)] * 2,
        scratch_shapes=[pltpu.VMEM((NSA_DH, t), BF16), pltpu.VMEM((NSA_DH, t), BF16)],
        compiler_params=_cp("arbitrary", "arbitrary", "arbitrary"), name="nsa_attend_prompt")(
            q_rot3, ks, vs, kw, vw, sel)


def _paged_sel_body(pt_ref, q_ref, sel_ref, kn_ref, vn_ref, k_hbm, v_hbm, o_ref,
                    kbuf, vbuf, sem, m_ref, l_ref, acc_ref, *, pg, past_len, t_new):
    bi, step = pl.program_id(0), pl.program_id(1)
    nsteps = pl.num_programs(1)
    total = pl.num_programs(0) * nsteps
    lin = bi * nsteps + step
    _, page, kv, _ = k_hbm.shape
    rows = q_ref.shape[1]
    per = rows // kv
    scale = NSA_DH ** -0.5
    nbp = sel_ref.shape[-1]
    qb = q_ref[0].astype(BF16)
    selb = sel_ref[0].astype(BF16)

    def page_copies(b_, s_, slot):
        out = []
        for i in range(pg):
            phys = pt_ref[b_, s_ * pg + i]
            for g in range(kv):
                dst = pl.ds(i * page, page)
                out.append(pltpu.make_async_copy(k_hbm.at[phys, :, g, :], kbuf.at[slot, g, dst, :], sem.at[slot, 0]))
                out.append(pltpu.make_async_copy(v_hbm.at[phys, :, g, :], vbuf.at[slot, g, dst, :], sem.at[slot, 1]))
        return out

    @pl.when(lin == 0)
    def _():
        for c in page_copies(0, 0, 0):
            c.start()

    @pl.when(lin + 1 < total)
    def _():
        nxt = lin + 1
        for c in page_copies(nxt // nsteps, nxt % nsteps, nxt % 2):
            c.start()

    @pl.when(step == 0)
    def _():
        m_ref[...] = jnp.full(m_ref.shape, NEG, F32)
        l_ref[...] = jnp.zeros(l_ref.shape, F32)
        acc_ref[...] = jnp.zeros(acc_ref.shape, F32)

    def update(key_of, val_of, n, kp0):
        tloc = lax.broadcasted_iota(jnp.int32, (rows, n), 0) % t_new
        kcol = lax.broadcasted_iota(jnp.int32, (rows, n), 1)
        en = lax.broadcasted_iota(jnp.int32, (nbp, n), 0)
        es = lax.broadcasted_iota(jnp.int32, (nbp, n), 1)
        s = jnp.concatenate([_mm_nt(qb[g * per:(g + 1) * per], key_of(g)) for g in range(kv)], axis=0) * scale
        expand = ((kp0 + es) // NSA_BLOCK == en).astype(BF16)
        chosen = jnp.dot(selb, expand, preferred_element_type=F32)
        valid = (chosen > 0.5) & (kp0 + kcol <= past_len + tloc)
        s = jnp.where(valid, s, NEG)
        m = m_ref[:, 0:1]
        m_new = jnp.maximum(m, jnp.max(s, axis=-1, keepdims=True))
        p = jnp.where(valid, jnp.exp(s - m_new), 0.0)
        alpha = jnp.exp(m - m_new)
        pv = jnp.concatenate([_mm(p[g * per:(g + 1) * per], val_of(g)) for g in range(kv)], axis=0)
        m_ref[...] = jnp.broadcast_to(m_new, m_ref.shape)
        l_ref[...] = jnp.broadcast_to(alpha * l_ref[:, 0:1] + jnp.sum(p, axis=-1, keepdims=True), l_ref.shape)
        acc_ref[...] = alpha * acc_ref[...] + pv

    slot = lin % 2
    for c in page_copies(bi, step, slot):
        c.wait()
    update(lambda g: kbuf[slot, g], lambda g: vbuf[slot, g], pg * page, step * (pg * page))

    @pl.when(step == nsteps - 1)
    def _():
        update(lambda g: kn_ref[0, :, g * LANES:(g + 1) * LANES],
               lambda g: vn_ref[0, :, g * LANES:(g + 1) * LANES], kn_ref.shape[1], past_len)
        o_ref[0] = acc_ref[...] / jnp.maximum(l_ref[:, 0:1], 1e-30)


def nsa_paged_sel(q_rot3, sel, pool_k, pool_v, page_table, k_new, v_new, pg):
    b, t_new, qw = q_rot3.shape
    n_pages = page_table.shape[1]
    _, page, kv, dh = pool_k.shape
    kvw = kv * dh
    nbp = sel.shape[-1]
    per = NSA_GROUP * t_new
    rows = kv * per
    q_rows = q_rot3.reshape(b, t_new, kv, NSA_GROUP, dh).transpose(0, 2, 3, 1, 4).reshape(b, rows, dh)
    sel_rows = jnp.broadcast_to(sel[:, :, None], (b, kv, NSA_GROUP, t_new, nbp)).reshape(b, rows, nbp)
    gs = pltpu.PrefetchScalarGridSpec(
        num_scalar_prefetch=1, grid=(b, n_pages // pg),
        in_specs=[pl.BlockSpec((1, rows, dh), lambda bi, s, pt: (bi, 0, 0)),
                  pl.BlockSpec((1, rows, nbp), lambda bi, s, pt: (bi, 0, 0)),
                  pl.BlockSpec((1, page, kvw), lambda bi, s, pt: (bi, 0, 0)),
                  pl.BlockSpec((1, page, kvw), lambda bi, s, pt: (bi, 0, 0)),
                  pl.BlockSpec(memory_space=pl.ANY), pl.BlockSpec(memory_space=pl.ANY)],
        out_specs=pl.BlockSpec((1, rows, dh), lambda bi, s, pt: (bi, 0, 0)),
        scratch_shapes=[pltpu.VMEM((2, kv, pg * page, dh), F32), pltpu.VMEM((2, kv, pg * page, dh), F32),
                        pltpu.SemaphoreType.DMA((2, 2)),
                        pltpu.VMEM((rows, LANES), F32), pltpu.VMEM((rows, LANES), F32),
                        pltpu.VMEM((rows, dh), F32)])
    body = functools.partial(_paged_sel_body, pg=pg, past_len=n_pages * page, t_new=t_new)
    o_rows = pl.pallas_call(
        body, grid_spec=gs, out_shape=jax.ShapeDtypeStruct((b, rows, dh), F32),
        compiler_params=_cp("arbitrary", "arbitrary"), name="nsa_paged_sel")(
            page_table, q_rows, sel_rows, k_new, v_new, pool_k, pool_v)
    return o_rows.reshape(b, kv, NSA_GROUP, t_new, NSA_DH).transpose(0, 3, 1, 2, 4).reshape(b, t_new, qw)


def _combine_body(oc_ref, os_ref, ow_ref, gt_ref, a_ref):
    gs = _sigmoid(gt_ref[...])
    for hh in range(oc_ref.shape[1] // LANES):
        sl = slice(hh * LANES, (hh + 1) * LANES)
        a = (gs[:, 3 * hh:3 * hh + 1] * oc_ref[:, sl] + gs[:, 3 * hh + 1:3 * hh + 2] * os_ref[:, sl]
             + gs[:, 3 * hh + 2:3 * hh + 3] * ow_ref[:, sl])
        a_ref[:, sl] = a.astype(BF16)


def nsa_combine(o_cmp, o_sel, o_win, gates, tm):
    m, qw = o_cmp.shape
    spec = pl.BlockSpec((tm, qw), lambda i: (i, 0))
    return pl.pallas_call(
        _combine_body, grid=(m // tm,),
        in_specs=[spec, spec, spec, pl.BlockSpec((tm, LANES), lambda i: (i, 0))],
        out_specs=spec, out_shape=jax.ShapeDtypeStruct((m, qw), BF16),
        compiler_params=_cp("arbitrary"), name="nsa_combine")(o_cmp, o_sel, o_win, gates)


def _s5_body(*refs, seg, seq_len, has_state):
    if has_state:
        (u_ref, a1_ref, a2_ref, dt_ref, b1_ref, b2_ref, cm_ref, d_ref, s0_ref,
         z_ref, st_ref, x_ref, y_ref, up_ref) = refs
    else:
        (u_ref, a1_ref, a2_ref, dt_ref, b1_ref, b2_ref, cm_ref, d_ref,
         z_ref, st_ref, x_ref, y_ref, up_ref) = refs
    gq = pl.program_id(1)
    m = u_ref.shape[0]
    nseg = m // seg
    nb = m // seq_len
    half = LANES // 2
    lane = lax.broadcasted_iota(jnp.int32, (1, LANES), 1)
    sgn = jnp.where(lane < half, -1.0, 1.0)

    gs = a1_ref.shape[0]
    abar, bcats = [], []
    for gi in range(gs):
        are, aim, dt = a1_ref[gi], a2_ref[gi], jnp.exp(dt_ref[gi])
        er = jnp.exp(are * dt)
        abr, abi = er * jnp.cos(aim * dt), er * jnp.sin(aim * dt)
        nr, ni, den = abr - 1.0, abi, are * are + aim * aim
        cr, cim = (nr * are + ni * aim) / den, (ni * are - nr * aim) / den
        abar.append((abr, abi))
        bcats.append((cr * b1_ref[gi] + cim * b2_ref[gi]).astype(BF16))

    def cmul(x, pr, pi):
        return x * pr + pltpu.roll(x, half, 1) * (pi * sgn)

    pb = SUBLANES * seg
    nblk = m // pb
    ri = lax.broadcasted_iota(jnp.int32, (pb, pb), 0)
    ci = lax.broadcasted_iota(jnp.int32, (pb, pb), 1)

    @pl.when(gq == 0)
    def _():
        perm = (ci == (ri % SUBLANES) * seg + ri // SUBLANES).astype(BF16)
        for k in range(nblk):
            uk = u_ref[k * pb:(k + 1) * pb, :].astype(BF16)
            up_ref[k] = jnp.dot(perm, uk, preferred_element_type=F32).astype(BF16)

    x_ref[...] = jnp.dot(up_ref[...].reshape(m, LANES), jnp.concatenate(bcats, axis=1),
                         preferred_element_type=F32).reshape(nblk, pb, gs * LANES)

    def scan_group(gi):
        abr, abi = abar[gi]
        gl = slice(gi * LANES, (gi + 1) * LANES)
        x = jnp.zeros((nseg, LANES), F32)
        for s in range(seg):
            sl = slice(s * SUBLANES, (s + 1) * SUBLANES)
            x = cmul(x, abr, abi) + x_ref[:, sl, gl].reshape(nseg, LANES)
            x_ref[:, sl, gl] = x.reshape(nblk, SUBLANES, LANES)
        if has_state:
            carry = s0_ref[gi]
        else:
            spb = seq_len // seg
            pr, pi = abr, abi
            for _ in range(int(math.log2(seg))):
                pr, pi = pr * pr - pi * pi, 2.0 * pr * pi
            rown = lax.broadcasted_iota(jnp.int32, (nseg, LANES), 0) % spb
            inc = x
            sh = 1
            while sh < spb:
                inc = inc + jnp.where(rown >= sh, cmul(pltpu.roll(inc, sh, 0), pr, pi), 0.0)
                pr, pi = pr * pr - pi * pi, 2.0 * pr * pi
                sh *= 2
            carry = jnp.where(rown >= 1, pltpu.roll(inc, 1, 0), 0.0)
        pr, pi = abr, abi
        for s in range(seg):
            sl = slice(s * SUBLANES, (s + 1) * SUBLANES)
            x_ref[:, sl, gl] = x_ref[:, sl, gl] + cmul(carry, pr, pi).reshape(nblk, SUBLANES, LANES)
            pr, pi = pr * abr - pi * abi, pr * abi + pi * abr
        finals = []
        for bi in range(nb):
            last_seg = (bi + 1) * (seq_len // seg) - 1
            row = (seg - 1) * SUBLANES + last_seg % SUBLANES
            finals.append(x_ref[last_seg // SUBLANES, row:row + 1, gl])
        st_ref[gi] = jnp.concatenate(finals, axis=0)

    for gi in range(gs):
        scan_group(gi)

    yg = _mm(x_ref[...].reshape(m, gs * LANES), cm_ref[...].reshape(gs * LANES, LANES))

    @pl.when(gq == 0)
    def _():
        y_ref[...] = yg

    @pl.when(gq > 0)
    def _():
        y_ref[...] = y_ref[...] + yg

    @pl.when(gq == pl.num_programs(1) - 1)
    def _():
        unperm = (ri == (ci % SUBLANES) * seg + ci // SUBLANES).astype(BF16)
        for k in range(nblk):
            rows = slice(k * pb, (k + 1) * pb)
            yk = y_ref[rows, :]
            hi = yk.astype(BF16)
            lo = (yk - hi.astype(F32)).astype(BF16)
            y = (jnp.dot(unperm, hi, preferred_element_type=F32) + jnp.dot(unperm, lo, preferred_element_type=F32)
                 + d_ref[...] * u_ref[rows, :])
            z = 0.5 * y * (1.0 + jnp.tanh(math.sqrt(2.0 / math.pi) * (y + 0.044715 * (y * y * y))))
            z_ref[rows, :] = z.astype(BF16)


def s5_scan(u, prm, seq_len, seg, s0=None):
    m, d = u.shape
    a1, a2, dtb, b1, b2, cm, dsk = prm
    groups = a1.shape[0]
    per_tile = LANES // S5_CH
    nb = m // seq_len
    has_state = s0 is not None
    gs = S5_GROUPS_PER_STEP
    steps = per_tile // gs
    gidx = lambda j, q: (j * steps + q, 0, 0)
    vspec = pl.BlockSpec((gs, 1, LANES), gidx)
    mspec = pl.BlockSpec((gs, LANES, LANES), gidx)
    in_specs = [pl.BlockSpec((m, LANES), lambda j, q: (0, j)), vspec, vspec, vspec, mspec, mspec, mspec,
                pl.BlockSpec((1, LANES), lambda j, q: (0, j))]
    args = [u, a1, a2, dtb, b1, b2, cm, dsk]
    if has_state:
        in_specs.append(pl.BlockSpec((gs, nb, LANES), gidx))
        args.append(s0)
    body = functools.partial(_s5_body, seg=seg, seq_len=seq_len, has_state=has_state)
    return pl.pallas_call(
        body, grid=(d // LANES, steps), in_specs=in_specs,
        out_specs=[pl.BlockSpec((m, LANES), lambda j, q: (0, j)),
                   pl.BlockSpec((gs, nb, LANES), gidx)],
        out_shape=[jax.ShapeDtypeStruct((m, d), BF16), jax.ShapeDtypeStruct((groups, nb, LANES), F32)],
        scratch_shapes=[pltpu.VMEM((m // (SUBLANES * seg), SUBLANES * seg, gs * LANES), F32),
                        pltpu.VMEM((m, LANES), F32),
                        pltpu.VMEM((m // (SUBLANES * seg), SUBLANES * seg, LANES), BF16)],
        compiler_params=_cp("arbitrary", "arbitrary"), name="s5_scan")(*args)


def _s5_params(a_re, a_im, log_dt, b_re, b_im, c_re, c_im, d_skip):
    groups, p = a_re.shape
    per_tile = LANES // S5_CH
    dup = lambda a: jnp.concatenate([a, a], axis=-1)[:, None, :]
    a1, a2 = dup(a_re), dup(a_im)
    dtb = jnp.broadcast_to(log_dt[:, None, None], (groups, 1, LANES))
    slot = jax.nn.one_hot(jnp.arange(groups) % per_tile, per_tile, dtype=F32)

    def rows_in_tile(w):
        return (slot[:, :, None, None] * w[:, None]).reshape(groups, LANES, w.shape[-1])

    bre_t, bim_t = b_re.transpose(0, 2, 1), b_im.transpose(0, 2, 1)
    b1 = rows_in_tile(jnp.concatenate([bre_t, bim_t], axis=-1))
    b2 = rows_in_tile(jnp.concatenate([-bim_t, bre_t], axis=-1))
    cmat = jnp.concatenate([c_re, -c_im], axis=-1)
    cm = rows_in_tile(cmat).transpose(0, 2, 1)
    return a1, a2, dtb, b1, b2, cm, d_skip.reshape(1, -1)


def _tiles(m):
    if m >= 1024:
        return 1024, 512
    return m, m


def _gla_layer(h, b, t, gain, w_in, w_tail, w_alpha_pad, b_alpha, head_norm, w_out, s0, heads, dk, dv):
    tm, tmo = _tiles(h.shape[0])
    n_main = 2 * heads * dk + 2 * heads * dv
    proj = norm_proj(h, gain, w_in, n_main, tm, 512)
    lr = norm_proj(h, gain, w_tail, LANES, tm, LANES)
    c = math.gcd(t, CHUNK)
    tb = math.gcd(t, 128)
    og, st = recurrence("gla", proj.reshape(b, t, n_main), heads, dk, dv, head_norm, tb, c,
                        (lr.reshape(b, t, LANES), w_alpha_pad, b_alpha.reshape(1, -1)), s0, hb=2)
    return out_proj(og.reshape(b * t, heads * dv), w_out, h, tmo, 512), st


def _hgrn_layer(h, b, t, gain, w_in, lower_bound, layer, head_norm, w_out, s0, heads, dk):
    tm, tmo = _tiles(h.shape[0])
    n = 4 * heads * dk
    proj = norm_proj(h, gain, w_in, n, tm, 512)
    c = math.gcd(t, CHUNK)
    tb = math.gcd(t, 128)
    og, st = recurrence("hgrn", proj.reshape(b, t, n), heads, dk, dk, head_norm, tb, c,
                        (lower_bound,), s0, layer=layer, hb=4)
    return out_proj(og.reshape(b * t, heads * dk), w_out, h, tmo, 512), st


def _rope_tables(start, t):
    half = NSA_DH // 2
    inv = ROPE_THETA ** (-jnp.arange(half, dtype=F32) / half)
    ang = (start + jnp.arange(t, dtype=jnp.int32)).astype(F32)[:, None] * inv[None, :]
    cos, sin = jnp.cos(ang), jnp.sin(ang)
    return jnp.concatenate([cos, cos], axis=-1), jnp.concatenate([-sin, sin], axis=-1)


def _nsa_layer(h, b, t, start, gain, w_in, w_gates, pool_k, pool_v, w_out, past, heads, kv):
    tm, tmo = _tiles(h.shape[0])
    qw, kvw = heads * NSA_DH, kv * NSA_DH
    n_main = qw + 6 * kvw
    proj = norm_proj(h, gain, w_in, n_main, tm, 512)
    gates = norm_proj(h, gain, w_gates, LANES, tm, LANES)
    cos, sin = _rope_tables(start, t)
    pk = jnp.broadcast_to(pool_k[:, None], (NSA_BLOCK, kvw))
    pv = jnp.broadcast_to(pool_v[:, None], (NSA_BLOCK, kvw))
    proj3 = proj.reshape(b, t, n_main)
    col = lambda i: proj3[:, :, qw + i * kvw:qw + (i + 1) * kvw]
    kc, vc, vs, vw = col(0), col(1), col(3), col(5)

    if past is None:
        trope = min(256, t)
        q_rot, ks, kw = nsa_rope(proj, cos, sin, trope, t // trope, qw, kvw)
        q_rot3, ks3, kw3 = q_rot.reshape(b, t, qw), ks.reshape(b, t, kvw), kw.reshape(b, t, kvw)
        n_cb = t // NSA_BLOCK
        n_blk = -(-t // NSA_BLOCK)
        kcmp, vcmp = nsa_pool_prompt(proj3, pk, pv, qw, kvw, LANES)
        o_cmp, sel = nsa_cmp(proj3, kcmp, vcmp, min(t, 512), 0, n_cb, n_blk, LANES, kvw)
        tq = min(t, 128)
        o_sel, o_win = nsa_attend_prompt(q_rot3, ks3, vs, kw3, vw, sel, tq, min(t, 512))
        keep = min(NSA_WINDOW, t)
        win_k, win_v = kw3[:, t - keep:], vw[:, t - keep:]
    else:
        pool_ck, pool_cv, pool_sk, pool_sv, page_table, prev_kw, prev_vw = past
        n_pages = page_table.shape[1]
        page = pool_ck.shape[1]
        past_len = n_pages * page
        cos_r, sin_r = jnp.tile(cos, (b, 1)), jnp.tile(sin, (b, 1))
        q_rot, ks, kw = nsa_rope(proj, cos_r, sin_r, b * t, 1, qw, kvw)
        q_rot3, ks3, kw3 = q_rot.reshape(b, t, qw), ks.reshape(b, t, kvw), kw.reshape(b, t, kvw)
        pk4 = jnp.broadcast_to(pool_k[:, None, None], (NSA_BLOCK, kv, NSA_DH))
        pv4 = jnp.broadcast_to(pool_v[:, None, None], (NSA_BLOCK, kv, NSA_DH))
        kcmp, vcmp = nsa_pool_pages(pool_ck, pool_cv, page_table, pk4, pv4, 4)
        total = past_len + t
        n_cb = total // NSA_BLOCK
        n_blk = -(-total // NSA_BLOCK)
        nb_pad = -(-n_blk // LANES) * LANES
        o_cmp, sel = nsa_cmp(proj3, kcmp, vcmp, t, past_len, n_cb, n_blk, nb_pad, kvw)
        padp = lambda a: jnp.concatenate([a, jnp.zeros((b, page - t, kvw), F32)], axis=1)
        o_sel = nsa_paged_sel(q_rot3, sel, pool_sk, pool_sv, page_table, padp(ks3), padp(vs), 8)
        keep = prev_kw.shape[1]
        kw_ext = jnp.concatenate([prev_kw.reshape(b, keep, kvw), kw3], axis=1)
        vw_ext = jnp.concatenate([prev_vw.reshape(b, keep, kvw), vw], axis=1)
        t_ext = keep + t
        t_pad = -(-t_ext // LANES) * LANES
        pade = lambda a: jnp.concatenate([a, jnp.zeros((b, t_pad - t_ext, kvw), F32)], axis=1)
        (o_win,) = nsa_attend(q_rot3, t, LANES, past_len, past_len - keep, win_args=(pade(kw_ext), pade(vw_ext)))
        win_k, win_v = kw_ext[:, t_ext - keep:], vw_ext[:, t_ext - keep:]

    a = nsa_combine(o_cmp.reshape(b * t, qw), o_sel.reshape(b * t, qw), o_win.reshape(b * t, qw), gates,
                    min(b * t, 512))
    y = out_proj(a, w_out, h, tmo, 512)
    shp = lambda x, n: x.reshape(b, n, kv, NSA_DH)
    return y, (shp(kc, t), shp(vc, t), shp(ks3, t), shp(vs, t), shp(win_k, keep), shp(win_v, keep))


def _s5_layer(h, b, t, gain, prm, w_glu, s_re, s_im):
    tm, tmo = _tiles(h.shape[0])
    u = rmsnorm_rows(h, gain, min(h.shape[0], 512))
    groups = prm[0].shape[0]
    if s_re is None:
        z, st = s5_scan(u, prm, t, math.gcd(t, 32))
    else:
        s0 = jnp.concatenate([s_re, s_im], axis=-1).transpose(1, 0, 2)
        z, st = s5_scan(u, prm, t, t, s0)
    y = out_glu(z, w_glu, h, tmo, 512)
    st = st.transpose(1, 0, 2)
    return y, (st[..., :S5_STATE], st[..., S5_STATE:])


def _ffn_layer(h, b, t, gain, w_in, conv_w, conv_b, w_out, buf):
    tm, tmo = _tiles(h.shape[0])
    ff2 = w_in.shape[1]
    if buf is None:
        tm = min(512, t)
        act, tg, tv = ffn_in(h, gain, w_in, conv_w, conv_b, tm, 512, t)
        per = t // tm
        last = lambda a: a[per - 1::per, SUBLANES - (CONV_W - 1):, :]
        state = jnp.concatenate([last(tg), last(tv)], axis=-1)
    else:
        zrow = jnp.zeros((b, t - 1, ff2), F32)
        p1 = jnp.concatenate([buf[:, 1:2], zrow], axis=1).reshape(b * t, ff2)
        p2 = jnp.concatenate([buf, zrow[:, 1:]], axis=1).reshape(b * t, ff2)
        act, tg, tv = ffn_in(h, gain, w_in, conv_w, conv_b, b * t, 512, t, hist=(p1, p2))
        up = jnp.concatenate([tg[0], tv[0]], axis=-1).reshape(b, t, ff2)
        state = jnp.concatenate([buf, up], axis=1)[:, t:]
    return out_proj(act, w_out, h, min(h.shape[0], 1024), 256), state


def kernel(x_prompt, x_sample, state_gla, state_hgrn, cache_nsa_cmp_k, cache_nsa_cmp_v, cache_nsa_sel_k, cache_nsa_sel_v, cache_nsa_win_k, cache_nsa_win_v, state_s5_re, state_s5_im, state_ffn_conv, page_table, norm_mix, norm_ffn, final_norm, gla_w_in, gla_w_alpha, gla_b_alpha, gla_head_norm, gla_w_out, hgrn_w_in, hgrn_lower_bound, hgrn_head_norm, hgrn_w_out, nsa_w_in, nsa_pool_k, nsa_pool_v, nsa_w_out, s5_a_re, s5_a_im, s5_log_dt, s5_b_re, s5_b_im, s5_c_re, s5_c_im, s5_d, s5_w_glu, ffn_w_in, ffn_conv_w, ffn_conv_b, ffn_w_out):
    bp, tp, d = x_prompt.shape
    bs, ts, _ = x_sample.shape
    depth = norm_mix.shape[0]
    n_mixers = 4
    gla_heads, gla_dk, gla_dv = state_gla.shape[2], state_gla.shape[3], state_gla.shape[4]
    hgrn_heads, hgrn_dk = state_hgrn.shape[2], state_hgrn.shape[3]
    nsa_kv = cache_nsa_cmp_k.shape[3]
    nsa_heads = d // NSA_DH
    hp = x_prompt.reshape(bp * tp, d)
    hs = x_sample.reshape(bs * ts, d)
    bf = lambda w: w.astype(BF16)

    def pad_cols(w, n):
        return jnp.concatenate([w, jnp.zeros((w.shape[0], n - w.shape[1]), w.dtype)], axis=1)

    outs = {k: [] for k in ("gla_p", "gla_s", "hgrn_p", "hgrn_s", "nsa_p", "nsa_s", "s5_p", "s5_s", "conv_p", "conv_s")}
    for i in range(depth):
        kind, j = i % n_mixers, i // n_mixers
        if kind == 0:
            n_main = 2 * gla_heads * gla_dk + 2 * gla_heads * gla_dv
            w_in = cast_bf16(gla_w_in, j)
            w_tail = bf(pad_cols(gla_w_in[j, :, n_main:], LANES))
            rank = gla_w_alpha.shape[1]
            wa = bf(jnp.concatenate([gla_w_alpha[j], jnp.zeros((LANES - rank, gla_w_alpha.shape[2]), F32)], axis=0))
            common = (norm_mix[i], w_in, w_tail, wa, gla_b_alpha[j], gla_head_norm[j], cast_bf16(gla_w_out, j))
            hp, st_p = _gla_layer(hp, bp, tp, *common, None, gla_heads, gla_dk, gla_dv)
            hs, st_s = _gla_layer(hs, bs, ts, *common, state_gla[j], gla_heads, gla_dk, gla_dv)
            outs["gla_p"].append(st_p)
            outs["gla_s"].append(st_s)
        elif kind == 1:
            common = (norm_mix[i], cast_bf16(hgrn_w_in, j), hgrn_lower_bound, i, hgrn_head_norm[j],
                      cast_bf16(hgrn_w_out, j))
            hp, st_p = _hgrn_layer(hp, bp, tp, *common, None, hgrn_heads, hgrn_dk)
            hs, st_s = _hgrn_layer(hs, bs, ts, *common, state_hgrn[j], hgrn_heads, hgrn_dk)
            outs["hgrn_p"].append(st_p)
            outs["hgrn_s"].append(st_s)
        elif kind == 2:
            n_main = nsa_heads * NSA_DH + 6 * nsa_kv * NSA_DH
            w_in = cast_bf16(nsa_w_in, j)
            w_gates = bf(pad_cols(nsa_w_in[j, :, n_main:], LANES))
            common = (norm_mix[i], w_in, w_gates, nsa_pool_k[j], nsa_pool_v[j], cast_bf16(nsa_w_out, j))
            hp, st_p = _nsa_layer(hp, bp, tp, 0, *common, None, nsa_heads, nsa_kv)
            past = (cache_nsa_cmp_k[j], cache_nsa_cmp_v[j], cache_nsa_sel_k[j], cache_nsa_sel_v[j],
                    page_table, cache_nsa_win_k[j], cache_nsa_win_v[j])
            hs, st_s = _nsa_layer(hs, bs, ts, page_table.shape[1] * cache_nsa_cmp_k.shape[2], *common, past,
                                  nsa_heads, nsa_kv)
            outs["nsa_p"].append(st_p)
            outs["nsa_s"].append(st_s)
        else:
            prm = _s5_params(s5_a_re[j], s5_a_im[j], s5_log_dt[j], s5_b_re[j], s5_b_im[j], s5_c_re[j],
                             s5_c_im[j], s5_d[j])
            w_glu = cast_bf16(s5_w_glu, j)
            hp, st_p = _s5_layer(hp, bp, tp, norm_mix[i], prm, w_glu, None, None)
            hs, st_s = _s5_layer(hs, bs, ts, norm_mix[i], prm, w_glu, state_s5_re[j], state_s5_im[j])
            outs["s5_p"].append(st_p)
            outs["s5_s"].append(st_s)
        fw = (norm_ffn[i], cast_bf16(ffn_w_in, i), ffn_conv_w[i], ffn_conv_b[i], cast_bf16(ffn_w_out, i))
        hp, cb_p = _ffn_layer(hp, bp, tp, *fw, None)
        hs, cb_s = _ffn_layer(hs, bs, ts, *fw, state_ffn_conv[i])
        outs["conv_p"].append(cb_p)
        outs["conv_s"].append(cb_s)

    y_prompt = rmsnorm_rows(hp, final_norm, min(hp.shape[0], 512)).reshape(bp, tp, d)
    y_sample = rmsnorm_rows(hs, final_norm, min(hs.shape[0], 512)).reshape(bs, ts, d)
    stack = lambda xs: jnp.stack(xs)
    pick = lambda key, r: stack([e[r] for e in outs[key]])
    res = [y_prompt, y_sample, stack(outs["gla_p"]), stack(outs["gla_s"]), stack(outs["hgrn_p"]), stack(outs["hgrn_s"])]
    for r in range(6):
        res += [pick("nsa_p", r), pick("nsa_s", r)]
    for r in range(2):
        res += [pick("s5_p", r), pick("s5_s", r)]
    res += [stack(outs["conv_p"]), stack(outs["conv_s"])]
    return tuple(res)
```

```python
import functools
import math

import jax
import jax.numpy as jnp
from jax import lax
from jax.experimental import pallas as pl
from jax.experimental.pallas import tpu as pltpu

F32 = jnp.float32
BF16 = jnp.bfloat16
HIGHEST = lax.Precision.HIGHEST

RMS_EPS = 1e-6
ROPE_THETA = 10000.0
NEG = -1e30
CHUNK = 64
SUBCHUNK = 8
GLA_TEMP = 16.0
NSA_BLOCK = 64
NSA_TOP_N = 16
NSA_WINDOW = 512
NSA_GROUP = 4
NSA_DH = 128
S5_CH = 16
S5_STATE = 64
S5_GROUPS_PER_STEP = 2
CONV_W = 3
LANES = 128
SUBLANES = 8
VMEM_LIMIT = 48 * 1024 * 1024


def _cp(*sem):
    return pltpu.CompilerParams(dimension_semantics=sem, vmem_limit_bytes=VMEM_LIMIT)


def _mm(a, b):
    return jnp.dot(a.astype(BF16), b.astype(BF16), preferred_element_type=F32)


def _mm_nt(a, b):
    return lax.dot_general(a.astype(BF16), b.astype(BF16), (((1,), (1,)), ((), ())),
                           preferred_element_type=F32)


def _sigmoid(x):
    return 1.0 / (1.0 + jnp.exp(-x))


def _rms(x, g):
    return x * lax.rsqrt(jnp.mean(x * x, axis=-1, keepdims=True) + RMS_EPS) * g


CAST_BLOCK_BYTES = 4 * 1024 * 1024


def _cast_body(w_ref, o_ref):
    o_ref[...] = w_ref[0].astype(BF16)


def cast_bf16(w_stack, layer):
    _, k, n = w_stack.shape
    tk = 16
    while k % (2 * tk) == 0 and 2 * tk * n * 4 <= CAST_BLOCK_BYTES:
        tk *= 2
    return pl.pallas_call(
        _cast_body, grid=(k // tk,),
        in_specs=[pl.BlockSpec((1, tk, n), lambda i: (layer, i, 0))],
        out_specs=pl.BlockSpec((tk, n), lambda i: (i, 0)),
        out_shape=jax.ShapeDtypeStruct((k, n), BF16),
        compiler_params=_cp("arbitrary"), name="cast_bf16")(w_stack)


def _norm_body(x_ref, g_ref, o_ref):
    o_ref[...] = _rms(x_ref[...], g_ref[...])


def rmsnorm_rows(x, gain, tm):
    m, d = x.shape
    return pl.pallas_call(
        _norm_body, grid=(m // tm,),
        in_specs=[pl.BlockSpec((tm, d), lambda i: (i, 0)), pl.BlockSpec((1, d), lambda i: (0, 0))],
        out_specs=pl.BlockSpec((tm, d), lambda i: (i, 0)),
        out_shape=jax.ShapeDtypeStruct((m, d), F32),
        compiler_params=_cp("arbitrary"), name="rmsnorm")(x, gain.reshape(1, d))


def _proj_body(x_ref, g_ref, w_ref, o_ref, xn_ref):
    @pl.when(pl.program_id(1) == 0)
    def _():
        xn_ref[...] = _rms(x_ref[...], g_ref[...]).astype(BF16)
    o_ref[...] = jnp.dot(xn_ref[...], w_ref[...], preferred_element_type=F32)


def norm_proj(x, gain, w, n_out, tm, tn):
    m, d = x.shape
    return pl.pallas_call(
        _proj_body, grid=(m // tm, n_out // tn),
        in_specs=[pl.BlockSpec((tm, d), lambda i, j: (i, 0)),
                  pl.BlockSpec((1, d), lambda i, j: (0, 0)),
                  pl.BlockSpec((d, tn), lambda i, j: (0, j))],
        out_specs=pl.BlockSpec((tm, tn), lambda i, j: (i, j)),
        out_shape=jax.ShapeDtypeStruct((m, n_out), F32),
        scratch_shapes=[pltpu.VMEM((tm, d), BF16)],
        compiler_params=_cp("arbitrary", "arbitrary"), name="norm_proj")(x, gain.reshape(1, d), w)


def _out_body(a_ref, w_ref, r_ref, o_ref):
    o_ref[...] = r_ref[...] + jnp.dot(a_ref[...].astype(BF16), w_ref[...], preferred_element_type=F32)


def out_proj(a, w, res, tm, tn):
    m, k = a.shape
    n = w.shape[1]
    return pl.pallas_call(
        _out_body, grid=(m // tm, n // tn),
        in_specs=[pl.BlockSpec((tm, k), lambda i, j: (i, 0)),
                  pl.BlockSpec((k, tn), lambda i, j: (0, j)),
                  pl.BlockSpec((tm, tn), lambda i, j: (i, j))],
        out_specs=pl.BlockSpec((tm, tn), lambda i, j: (i, j)),
        out_shape=jax.ShapeDtypeStruct((m, n), F32),
        compiler_params=_cp("arbitrary", "arbitrary"), name="out_proj")(a, w, res)


def _out_glu_body(a_ref, w1_ref, w2_ref, r_ref, o_ref):
    a = a_ref[...]
    g1 = jnp.dot(a, w1_ref[...], preferred_element_type=F32)
    g2 = jnp.dot(a, w2_ref[...], preferred_element_type=F32)
    o_ref[...] = r_ref[...] + g1 * _sigmoid(g2)


def out_glu(a, w, res, tm, tn):
    m, k = a.shape
    n = w.shape[1] // 2
    nj = n // tn
    return pl.pallas_call(
        _out_glu_body, grid=(m // tm, nj),
        in_specs=[pl.BlockSpec((tm, k), lambda i, j: (i, 0)),
                  pl.BlockSpec((k, tn), lambda i, j: (0, j)),
                  pl.BlockSpec((k, tn), lambda i, j: (0, nj + j)),
                  pl.BlockSpec((tm, tn), lambda i, j: (i, j))],
        out_specs=pl.BlockSpec((tm, tn), lambda i, j: (i, j)),
        out_shape=jax.ShapeDtypeStruct((m, n), F32),
        compiler_params=_cp("arbitrary", "arbitrary"), name="out_glu")(a, w, w, res)


def _ffn_in_body(*refs, seg, tiles_per_seq, tail_rows, has_state):
    if has_state:
        (x_ref, g_ref, wg_ref, wv_ref, cwg_ref, cwv_ref, cbg_ref, cbv_ref,
         p1g_ref, p2g_ref, p1v_ref, p2v_ref,
         act_ref, tg_ref, tv_ref, xn_ref, carry_ref) = refs
    else:
        (x_ref, g_ref, wg_ref, wv_ref, cwg_ref, cwv_ref, cbg_ref, cbv_ref,
         act_ref, tg_ref, tv_ref, xn_ref, carry_ref) = refs
    i = pl.program_id(0)
    f = pl.program_id(1)
    tm = x_ref.shape[0]

    @pl.when(f == 0)
    def _():
        xn_ref[...] = _rms(x_ref[...], g_ref[...]).astype(BF16)

    xn = xn_ref[...]
    tf = wg_ref.shape[1]
    sb = tf
    row = lax.broadcasted_iota(jnp.int32, (tm, sb), 0)
    rowm = row % seg
    fresh = (i % tiles_per_seq) == 0

    def conv(w_ref, cw_ref, cb_ref, kind, p1_ref, p2_ref, cs, t_ref):
        u = jnp.dot(xn, w_ref[:, cs], preferred_element_type=F32)
        if has_state:
            p1 = p1_ref[:, cs]
            p2 = p2_ref[:, cs]
        else:
            prev = carry_ref[kind, f, :, cs]
            prev = jnp.where(fresh, 0.0, prev)
            prev0 = prev[SUBLANES - 2:SUBLANES - 1, :]
            prev1 = prev[SUBLANES - 1:SUBLANES, :]
            p1 = jnp.broadcast_to(prev1, u.shape)
            p2 = jnp.where(row == 0, prev0, prev1)
            carry_ref[kind, f, :, cs] = u[tm - SUBLANES:, :]
        u1 = jnp.where(rowm < 1, p1, pltpu.roll(u, 1, 0))
        u2 = jnp.where(rowm < 2, p2, pltpu.roll(u, 2, 0))
        cw = cw_ref[:, cs]
        t_ref[0, :, cs] = u[tm - tail_rows:, :]
        return cw[0:1, :] * u2 + cw[1:2, :] * u1 + cw[2:3, :] * u + cb_ref[:, cs]

    for jb in range(tf // sb):
        cs = slice(jb * sb, (jb + 1) * sb)
        mg = conv(wg_ref, cwg_ref, cbg_ref, 0, p1g_ref if has_state else None, p2g_ref if has_state else None,
                  cs, tg_ref)
        mv = conv(wv_ref, cwv_ref, cbv_ref, 1, p1v_ref if has_state else None, p2v_ref if has_state else None,
                  cs, tv_ref)
        act_ref[:, cs] = (mg * _sigmoid(mg) * mv).astype(BF16)


def ffn_in(x, gain, w_in, conv_w, conv_b, tm, tf, seq_len, hist=None):
    m, d = x.shape
    ff = w_in.shape[1] // 2
    nf = ff // tf
    nb = m // tm
    has_state = hist is not None
    if has_state:
        seg, tiles_per_seq, tail_rows = seq_len, 1, tm
    else:
        seg, tiles_per_seq, tail_rows = tm, seq_len // tm, SUBLANES
    wspec_g = pl.BlockSpec((d, tf), lambda i, f: (0, f))
    wspec_v = pl.BlockSpec((d, tf), lambda i, f: (0, nf + f))
    cspec_g = lambda r: pl.BlockSpec((r, tf), lambda i, f: (0, f))
    cspec_v = lambda r: pl.BlockSpec((r, tf), lambda i, f: (0, nf + f))
    in_specs = [pl.BlockSpec((tm, d), lambda i, f: (i, 0)), pl.BlockSpec((1, d), lambda i, f: (0, 0)),
                wspec_g, wspec_v, cspec_g(CONV_W), cspec_v(CONV_W), cspec_g(1), cspec_v(1)]
    args = [x, gain.reshape(1, d), w_in, w_in, conv_w, conv_w, conv_b.reshape(1, -1), conv_b.reshape(1, -1)]
    if has_state:
        p1, p2 = hist
        in_specs += [pl.BlockSpec((tm, tf), lambda i, f: (i, f)), pl.BlockSpec((tm, tf), lambda i, f: (i, f)),
                     pl.BlockSpec((tm, tf), lambda i, f: (i, nf + f)), pl.BlockSpec((tm, tf), lambda i, f: (i, nf + f))]
        args += [p1, p2, p1, p2]
    body = functools.partial(_ffn_in_body, seg=seg, tiles_per_seq=tiles_per_seq,
                             tail_rows=tail_rows, has_state=has_state)
    return pl.pallas_call(
        body, grid=(nb, nf), in_specs=in_specs,
        out_specs=[pl.BlockSpec((tm, tf), lambda i, f: (i, f)),
                   pl.BlockSpec((1, tail_rows, tf), lambda i, f: (i, 0, f)),
                   pl.BlockSpec((1, tail_rows, tf), lambda i, f: (i, 0, f))],
        out_shape=[jax.ShapeDtypeStruct((m, ff), BF16),
                   jax.ShapeDtypeStruct((nb, tail_rows, ff), F32),
                   jax.ShapeDtypeStruct((nb, tail_rows, ff), F32)],
        scratch_shapes=[pltpu.VMEM((tm, d), BF16), pltpu.VMEM((2, nf, SUBLANES, tf), F32)],
        compiler_params=_cp("arbitrary", "arbitrary"), name="ffn_in")(*args)


def _pad_rows(a, rows):
    if a.shape[0] == rows:
        return a
    return jnp.concatenate([a, jnp.zeros((rows - a.shape[0], a.shape[1]), a.dtype)], axis=0)


def _roll_in_tiles(x, d):
    c, n = x.shape
    return pltpu.roll(x.reshape(c // SUBLANES, SUBLANES, n), d, 1).reshape(c, n)


def _glr_chunk(q, k, v, g, st, c, sub):
    dk = q.shape[1]
    row = lax.broadcasted_iota(jnp.int32, (c, LANES), 0)
    col = lax.broadcasted_iota(jnp.int32, (c, LANES), 1)
    trow = lax.broadcasted_iota(jnp.int32, (c, c), 0)
    tcol = lax.broadcasted_iota(jnp.int32, (c, c), 1)
    tri = (trow >= tcol).astype(F32)
    cum = jnp.dot(tri, g, preferred_element_type=F32, precision=HIGHEST)
    last = cum[c - 1:c, :]
    inter = _mm_nt(q * jnp.exp(cum), st)

    rowk = lax.broadcasted_iota(jnp.int32, (c, dk), 0)
    rm = rowk % sub
    ones = jnp.ones((dk, LANES), BF16)
    att = jnp.zeros((c, LANES), F32)
    for d in range(sub):
        if d == 0:
            p = q * k
        else:
            ks = _roll_in_tiles(k, d)
            cs = _roll_in_tiles(cum, d)
            p = q * ks * jnp.exp(jnp.where(rm >= d, cum - cs, NEG))
        a = jnp.dot(p.astype(BF16), ones, preferred_element_type=F32)
        att = att + jnp.where(col == row - d, a, 0.0)
    if c > sub:
        blocks = [jnp.zeros((sub, LANES), F32)]
        for i in range(1, c // sub):
            cs = cum[i * sub - 1:i * sub, :]
            qi = q[i * sub:(i + 1) * sub, :] * jnp.exp(cum[i * sub:(i + 1) * sub, :] - cs)
            kj = k * jnp.exp(jnp.where(rowk < i * sub, cs - cum, NEG))
            blocks.append(_mm_nt(qi, _pad_rows(kj, LANES)))
        att = att + jnp.concatenate(blocks, axis=0)
    vpad = _pad_rows(v, LANES)
    intra = _mm(att, vpad)
    kd = _pad_rows(k * jnp.exp(last - cum), LANES)
    st_new = st * jnp.exp(last) + _mm(vpad.T, kd)
    return inter + intra, st_new


def _rec_body(*refs, mode, c, sub, n_chunks, dk, dv, hb, layer, has_state):
    refs = list(refs)
    if mode == "gla":
        q_ref, k_ref, v_ref, r_ref, lr_ref, wa_ref, ba_ref, hn_ref = refs[:8]
        rest = refs[8:]
    else:
        q_ref, k_ref, v_ref, r_ref, lb_ref, hn_ref = refs[:6]
        rest = refs[6:]
    if has_state:
        s0_ref, og_ref, sout_ref, st_ref = rest
    else:
        og_ref, sout_ref, st_ref = rest
    tstep = pl.program_id(2)

    @pl.when(tstep == 0)
    def _():
        for hh in range(hb):
            if has_state:
                st_ref[hh] = s0_ref[0, hh].T
            else:
                st_ref[hh] = jnp.zeros(st_ref.shape[1:], F32)

    if mode == "hgrn":
        lbx = lb_ref[...]
        e = jnp.exp(lbx - jnp.max(lbx, axis=0, keepdims=True))
        sm = e / jnp.sum(e, axis=0, keepdims=True)
        lb_all = jnp.zeros((1, hb * dk), F32)
        for li in range(1, layer + 1):
            lb_all = lb_all + sm[li:li + 1, :]

    for ci in range(n_chunks):
        sl = slice(ci * c, (ci + 1) * c)
        for hh in range(hb):
            hk = slice(hh * dk, (hh + 1) * dk)
            hv = slice(hh * dv, (hh + 1) * dv)
            if mode == "gla":
                q = q_ref[0, sl, hk] * (dk ** -0.5)
                k = k_ref[0, sl, hk]
                z = _mm(lr_ref[0, sl, :], wa_ref[:, hk]) + ba_ref[:, hk]
                g = -(jnp.maximum(-z, 0.0) + jnp.log1p(jnp.exp(-jnp.abs(z)))) / GLA_TEMP
            else:
                qz = q_ref[0, sl, hk]
                q = qz * _sigmoid(qz)
                lbv = lb_all[:, hk]
                fg = lbv + (1.0 - lbv) * _sigmoid(k_ref[0, sl, hk])
                k = 1.0 - fg
                g = jnp.log(fg)
            v = v_ref[0, sl, hv]
            o, st_new = _glr_chunk(q, k, v, g, st_ref[hh], c, sub)
            st_ref[hh] = st_new
            of = o * lax.rsqrt(jnp.mean(o * o, axis=-1, keepdims=True) + RMS_EPS) * hn_ref[...]
            gate = r_ref[0, sl, hv]
            og_ref[0, sl, hv] = (of * (gate * _sigmoid(gate))).astype(BF16)

    @pl.when(tstep == pl.num_programs(2) - 1)
    def _():
        for hh in range(hb):
            sout_ref[0, hh] = st_ref[hh].T


def recurrence(mode, proj, heads, dk, dv, hn, tb, c, extra, s0=None, layer=0, hb=1):
    b, t, _ = proj.shape
    sub = min(SUBCHUNK, c)
    has_state = s0 is not None
    wk, wv = hb * dk, hb * dv
    nh = heads // hb
    if mode == "gla":
        lr, wa, ba = extra
        koff, voff = nh, (2 * heads * dk) // wv
        roff = voff + nh
        in_specs = [pl.BlockSpec((1, tb, wk), lambda i, h, s: (i, s, h)),
                    pl.BlockSpec((1, tb, wk), lambda i, h, s: (i, s, koff + h)),
                    pl.BlockSpec((1, tb, wv), lambda i, h, s: (i, s, voff + h)),
                    pl.BlockSpec((1, tb, wv), lambda i, h, s: (i, s, roff + h)),
                    pl.BlockSpec((1, tb, LANES), lambda i, h, s: (i, s, 0)),
                    pl.BlockSpec((LANES, wk), lambda i, h, s: (0, h)),
                    pl.BlockSpec((1, wk), lambda i, h, s: (0, h)),
                    pl.BlockSpec((1, dv), lambda i, h, s: (0, 0))]
        args = [proj, proj, proj, proj, lr, wa, ba, hn.reshape(1, dv)]
    else:
        (lb,) = extra
        in_specs = [pl.BlockSpec((1, tb, wk), lambda i, h, s: (i, s, h)),
                    pl.BlockSpec((1, tb, wk), lambda i, h, s: (i, s, nh + h)),
                    pl.BlockSpec((1, tb, wv), lambda i, h, s: (i, s, 2 * nh + h)),
                    pl.BlockSpec((1, tb, wv), lambda i, h, s: (i, s, 3 * nh + h)),
                    pl.BlockSpec((lb.shape[0], wk), lambda i, h, s: (0, h)),
                    pl.BlockSpec((1, dv), lambda i, h, s: (0, 0))]
        args = [proj, proj, proj, proj, lb, hn.reshape(1, dv)]
    if has_state:
        in_specs.append(pl.BlockSpec((1, hb, dk, dv), lambda i, h, s: (i, h, 0, 0)))
        args.append(s0)
    body = functools.partial(_rec_body, mode=mode, c=c, sub=sub, n_chunks=tb // c, dk=dk, dv=dv, hb=hb,
                             layer=layer, has_state=has_state)
    return pl.pallas_call(
        body, grid=(b, nh, t // tb), in_specs=in_specs,
        out_specs=[pl.BlockSpec((1, tb, wv), lambda i, h, s: (i, s, h)),
                   pl.BlockSpec((1, hb, dk, dv), lambda i, h, s: (i, h, 0, 0))],
        out_shape=[jax.ShapeDtypeStruct((b, t, heads * dv), BF16),
                   jax.ShapeDtypeStruct((b, heads, dk, dv), F32)],
        scratch_shapes=[pltpu.VMEM((hb, dv, dk), F32)],
        compiler_params=_cp("arbitrary", "arbitrary", "arbitrary"), name="recurrence_" + mode)(*args)


def _rope_body(q_ref, ks_ref, kw_ref, cos_ref, sin_ref, qo_ref, kso_ref, kwo_ref):
    cos = cos_ref[...]
    sin = sin_ref[...]

    def rot(src, dst):
        for h in range(src.shape[1] // LANES):
            x = src[:, h * LANES:(h + 1) * LANES]
            dst[:, h * LANES:(h + 1) * LANES] = x * cos + pltpu.roll(x, LANES // 2, 1) * sin

    rot(q_ref, qo_ref)
    rot(ks_ref, kso_ref)
    rot(kw_ref, kwo_ref)


def nsa_rope(proj, cos, sin, tm, tiles_per_seq, qw, kvw):
    m = proj.shape[0]
    ks_blk = (qw + 2 * kvw) // kvw
    kw_blk = (qw + 4 * kvw) // kvw
    return pl.pallas_call(
        _rope_body, grid=(m // tm,),
        in_specs=[pl.BlockSpec((tm, qw), lambda i: (i, 0)),
                  pl.BlockSpec((tm, kvw), lambda i: (i, ks_blk)),
                  pl.BlockSpec((tm, kvw), lambda i: (i, kw_blk)),
                  pl.BlockSpec((tm, LANES), lambda i: (i % tiles_per_seq, 0)),
                  pl.BlockSpec((tm, LANES), lambda i: (i % tiles_per_seq, 0))],
        out_specs=[pl.BlockSpec((tm, qw), lambda i: (i, 0)),
                   pl.BlockSpec((tm, kvw), lambda i: (i, 0)),
                   pl.BlockSpec((tm, kvw), lambda i: (i, 0))],
        out_shape=[jax.ShapeDtypeStruct((m, qw), F32), jax.ShapeDtypeStruct((m, kvw), F32),
                   jax.ShapeDtypeStruct((m, kvw), F32)],
        compiler_params=_cp("arbitrary"), name="nsa_rope")(proj, proj, proj, cos, sin)


def _pool_rows(x, pw):
    n = x.shape[0] // NSA_BLOCK
    return jnp.sum(x.reshape(n, NSA_BLOCK, x.shape[1]) * pw[None], axis=1)


def _pool_body(kc_ref, vc_ref, pk_ref, pv_ref, ko_ref, vo_ref, *, n_cb):
    ko_ref[...] = jnp.zeros(ko_ref.shape, F32)
    vo_ref[...] = jnp.zeros(vo_ref.shape, F32)
    kp = _pool_rows(kc_ref[0, 0:n_cb * NSA_BLOCK, :], pk_ref[...])
    vp = _pool_rows(vc_ref[0, 0:n_cb * NSA_BLOCK, :], pv_ref[...])
    for g in range(ko_ref.shape[1]):
        ko_ref[0, g, 0:n_cb, :] = kp[:, g * LANES:(g + 1) * LANES]
        vo_ref[0, g, 0:n_cb, :] = vp[:, g * LANES:(g + 1) * LANES]


def nsa_pool_prompt(proj3, pk, pv, qw, kvw, n_pad):
    b, t, _ = proj3.shape
    n_cb = t // NSA_BLOCK
    kc_blk = qw // kvw
    return pl.pallas_call(
        functools.partial(_pool_body, n_cb=n_cb), grid=(b,),
        in_specs=[pl.BlockSpec((1, t, kvw), lambda i: (i, 0, kc_blk)),
                  pl.BlockSpec((1, t, kvw), lambda i: (i, 0, kc_blk + 1)),
                  pl.BlockSpec((NSA_BLOCK, kvw), lambda i: (0, 0)),
                  pl.BlockSpec((NSA_BLOCK, kvw), lambda i: (0, 0))],
        out_specs=[pl.BlockSpec((1, kvw // LANES, n_pad, LANES), lambda i: (i, 0, 0, 0)),
                   pl.BlockSpec((1, kvw // LANES, n_pad, LANES), lambda i: (i, 0, 0, 0))],
        out_shape=[jax.ShapeDtypeStruct((b, kvw // LANES, n_pad, LANES), F32)] * 2,
        compiler_params=_cp("arbitrary"), name="nsa_pool")(proj3, proj3, pk, pv)


def _pool_pages_body(pt_ref, *refs, pg):
    k_refs = refs[:pg]
    v_refs = refs[pg:2 * pg]
    pk_ref, pv_ref, ko_ref, vo_ref = refs[2 * pg:]
    _, page, kv, dh = k_refs[0].shape
    per = page // NSA_BLOCK

    def pooled(x_ref, pw_ref):
        return jnp.sum(x_ref[0].reshape(per, NSA_BLOCK, kv, dh) * pw_ref[...][None], axis=1)

    for i in range(pg):
        kp, vp = pooled(k_refs[i], pk_ref), pooled(v_refs[i], pv_ref)
        for g in range(kv):
            ko_ref[0, g, i * per:(i + 1) * per, :] = kp[:, g, :]
            vo_ref[0, g, i * per:(i + 1) * per, :] = vp[:, g, :]


def nsa_pool_pages(pool_k, pool_v, page_table, pk, pv, pg):
    b, n_pages = page_table.shape
    _, page, kv, dh = pool_k.shape
    per = page // NSA_BLOCK
    page_spec = lambda i: pl.BlockSpec((1, page, kv, dh), lambda bi, s, pt, i=i: (pt[bi, s * pg + i], 0, 0, 0))
    wspec = pl.BlockSpec((NSA_BLOCK, kv, dh), lambda bi, s, pt: (0, 0, 0))
    ospec = pl.BlockSpec((1, kv, pg * per, dh), lambda bi, s, pt: (bi, 0, s, 0))
    gs = pltpu.PrefetchScalarGridSpec(
        num_scalar_prefetch=1, grid=(b, n_pages // pg),
        in_specs=[page_spec(i) for i in range(pg)] + [page_spec(i) for i in range(pg)] + [wspec, wspec],
        out_specs=[ospec, ospec])
    n_blk = n_pages * per
    return pl.pallas_call(
        functools.partial(_pool_pages_body, pg=pg), grid_spec=gs,
        out_shape=[jax.ShapeDtypeStruct((b, kv, n_blk, dh), F32)] * 2,
        compiler_params=_cp("arbitrary", "arbitrary"), name="nsa_pool_pages")(
            page_table, *([pool_k] * pg), *([pool_v] * pg), pk, pv)


def _cmp_body(q_ref, kc_ref, vc_ref, o_ref, sel_ref, *, q_start, n_cb, n_blk, nb_pad):
    tq = q_ref.shape[1]
    ncp = kc_ref.shape[2]
    qt = pl.program_id(2)
    kc = kc_ref[0, 0]
    vc = vc_ref[0, 0]
    scale = NSA_DH ** -0.5
    colc = lax.broadcasted_iota(jnp.int32, (tq, ncp), 1)
    qposc = q_start + qt * tq + lax.broadcasted_iota(jnp.int32, (tq, ncp), 0)
    valid = ((colc + 1) * NSA_BLOCK - 1 <= qposc) & (colc < n_cb)
    imp = jnp.zeros((tq, ncp), F32)
    for r in range(NSA_GROUP):
        qr = q_ref[0, :, r * LANES:(r + 1) * LANES]
        s = jnp.where(valid, _mm_nt(qr, kc) * scale, NEG)
        m = jnp.max(s, axis=-1, keepdims=True)
        e = jnp.where(valid, jnp.exp(s - m), 0.0)
        p = e / jnp.maximum(jnp.sum(e, axis=-1, keepdims=True), 1e-30)
        o_ref[0, :, r * LANES:(r + 1) * LANES] = _mm(p, vc)
        imp = imp + p
    if nb_pad > ncp:
        imp = jnp.concatenate([imp, jnp.zeros((tq, nb_pad - ncp), F32)], axis=1)
    blk = lax.broadcasted_iota(jnp.int32, (tq, nb_pad), 1)
    qpos = q_start + qt * tq + lax.broadcasted_iota(jnp.int32, (tq, nb_pad), 0)
    cur = qpos // NSA_BLOCK
    forced = (blk == cur) | (blk == 0)
    score = jnp.where(blk > cur, -1.0, jnp.where(forced, NSA_GROUP + 1.0, imp))
    score = jnp.where(blk < n_blk, score, -2.0)
    blkf = blk.astype(F32)

    def pick(_, carry):
        sc, sel = carry
        mx = jnp.max(sc, axis=-1, keepdims=True)
        first = jnp.min(jnp.where(sc == mx, blkf, 1e9), axis=-1, keepdims=True)
        hit = blkf == first
        return jnp.where(hit, -3.0, sc), jnp.where(hit, 1.0, sel)

    _, sel = lax.fori_loop(0, min(NSA_TOP_N, n_blk), pick, (score, jnp.zeros((tq, nb_pad), F32)))
    sel_ref[0, 0] = sel


def _cmp_t_body(q_ref, kc_ref, vc_ref, o_ref, sel_ref, *, q_start, n_cb, n_blk, nb_pad):
    tq = q_ref.shape[1]
    ncp = kc_ref.shape[2]
    qt = pl.program_id(2)
    scale = NSA_DH ** -0.5
    nr = min(ncp, -(-n_blk // SUBLANES) * SUBLANES)
    kc = kc_ref[0, 0, 0:nr, :]
    vc = vc_ref[0, 0]
    blk = lax.broadcasted_iota(jnp.int32, (nr, tq), 0)
    qpos = q_start + qt * tq + lax.broadcasted_iota(jnp.int32, (nr, tq), 1)
    valid = ((blk + 1) * NSA_BLOCK - 1 <= qpos) & (blk < n_cb)
    pad = jnp.zeros((ncp - nr, tq), F32)
    imp = jnp.zeros((nr, tq), F32)
    for r in range(NSA_GROUP):
        qr = q_ref[0, :, r * LANES:(r + 1) * LANES]
        s = jnp.where(valid, _mm_nt(kc, qr) * scale, NEG)
        m = jnp.max(s, axis=0, keepdims=True)
        e = jnp.where(valid, jnp.exp(s - m), 0.0)
        p = e / jnp.maximum(jnp.sum(e, axis=0, keepdims=True), 1e-30)
        p_rows = jnp.concatenate([p, pad], axis=0).T if ncp > nr else p.T
        o_ref[0, :, r * LANES:(r + 1) * LANES] = _mm(p_rows, vc)
        imp = imp + p
    cur = qpos // NSA_BLOCK
    forced = (blk == cur) | (blk == 0)
    score = jnp.where(blk > cur, -1.0, jnp.where(forced, NSA_GROUP + 1.0, imp))
    score = jnp.where(blk < n_blk, score, -2.0)
    blkf = blk.astype(F32)

    def pick(_, carry):
        sc, sel = carry
        mx = jnp.max(sc, axis=0, keepdims=True)
        first = jnp.min(jnp.where(sc == mx, blkf, 1e9), axis=0, keepdims=True)
        hit = blkf == first
        return jnp.where(hit, -3.0, sc), jnp.where(hit, 1.0, sel)

    _, sel = lax.fori_loop(0, min(NSA_TOP_N, n_blk), pick, (score, jnp.zeros((nr, tq), F32)))
    if nb_pad > nr:
        sel = jnp.concatenate([sel, jnp.zeros((nb_pad - nr, tq), F32)], axis=0)
    sel_ref[0, 0] = sel.T


def nsa_cmp(q3, kcmp, vcmp, tq, q_start, n_cb, n_blk, nb_pad, kvw):
    b, t = q3.shape[0], q3.shape[1]
    ncp = kcmp.shape[2]
    kv = kvw // LANES
    gw = NSA_GROUP * LANES
    transposed = tq % LANES == 0 and nb_pad % LANES == 0 and ncp % LANES == 0
    body = functools.partial(_cmp_t_body if transposed else _cmp_body,
                             q_start=q_start, n_cb=n_cb, n_blk=n_blk, nb_pad=nb_pad)
    return pl.pallas_call(
        body, grid=(b, kv, t // tq),
        in_specs=[pl.BlockSpec((1, tq, gw), lambda i, g, s: (i, s, g)),
                  pl.BlockSpec((1, 1, ncp, LANES), lambda i, g, s: (i, g, 0, 0)),
                  pl.BlockSpec((1, 1, ncp, LANES), lambda i, g, s: (i, g, 0, 0))],
        out_specs=[pl.BlockSpec((1, tq, gw), lambda i, g, s: (i, s, g)),
                   pl.BlockSpec((1, 1, tq, nb_pad), lambda i, g, s: (i, g, s, 0))],
        out_shape=[jax.ShapeDtypeStruct((b, t, kv * gw), F32),
                   jax.ShapeDtypeStruct((b, kv, t, nb_pad), F32)],
        compiler_params=_cp("arbitrary", "arbitrary", "arbitrary"), name="nsa_cmp")(q3, kcmp, vcmp)


def _softmax_step(carry, s, valid, vv):
    m, l, acc = carry
    s = jnp.where(valid, s, NEG)
    m_new = jnp.maximum(m, jnp.max(s, axis=-1, keepdims=True))
    p = jnp.where(valid, jnp.exp(s - m_new), 0.0)
    alpha = jnp.exp(m - m_new)
    l = alpha * l + jnp.sum(p, axis=-1, keepdims=True)
    acc = alpha * acc + _mm(p, vv)
    return m_new, l, acc


def _softmax_init(rows):
    return (jnp.full((rows, 1), NEG, F32), jnp.zeros((rows, 1), F32), jnp.zeros((rows, LANES), F32))


def _attn_body(*refs, tq, tk, n_kt, q_start, k_start, do_sel, do_win):
    refs = list(refs)
    q_ref = refs.pop(0)
    if do_sel:
        ks_ref, vs_ref, sel_ref = refs[:3]
        refs = refs[3:]
    if do_win:
        kw_ref, vw_ref = refs[:2]
        refs = refs[2:]
    outs = refs
    qt = pl.program_id(2)
    scale = NSA_DH ** -0.5
    rows = NSA_GROUP * tq
    q4 = jnp.concatenate([q_ref[0, :, r * LANES:(r + 1) * LANES] for r in range(NSA_GROUP)], axis=0).astype(BF16)
    q0 = q_start + qt * tq
    qpos = q0 + lax.broadcasted_iota(jnp.int32, (rows, tk), 0) % tq
    kcol = lax.broadcasted_iota(jnp.int32, (rows, tk), 1)

    def finish(carry, o_ref):
        m, l, acc = carry
        o = acc / jnp.maximum(l, 1e-30)
        for r in range(NSA_GROUP):
            o_ref[0, :, r * LANES:(r + 1) * LANES] = o[r * tq:(r + 1) * tq, :]

    if do_sel:
        selb = sel_ref[0, 0].astype(BF16)
        nbp = selb.shape[1]
        en = lax.broadcasted_iota(jnp.int32, (nbp, tk), 0)
        es = lax.broadcasted_iota(jnp.int32, (nbp, tk), 1)

        def sel_step(kt, carry):
            off = pl.multiple_of(kt * tk, tk)
            kk = ks_ref[0, pl.ds(off, tk), :]
            vv = vs_ref[0, pl.ds(off, tk), :]
            s = _mm_nt(q4, kk) * scale
            kp0 = k_start + kt * tk
            expand = ((kp0 + es) // NSA_BLOCK == en).astype(BF16)
            chosen = jnp.dot(selb, expand, preferred_element_type=F32)
            chosen = jnp.concatenate([chosen] * NSA_GROUP, axis=0)
            valid = (chosen > 0.5) & (kp0 + kcol <= qpos)
            return _softmax_step(carry, s, valid, vv)

        hi = jnp.minimum(n_kt, (q0 + tq - 1 - k_start) // tk + 1)
        finish(lax.fori_loop(0, hi, sel_step, _softmax_init(rows)), outs.pop(0))

    if do_win:
        def win_step(kt, carry):
            off = pl.multiple_of(kt * tk, tk)
            kk = kw_ref[0, pl.ds(off, tk), :]
            vv = vw_ref[0, pl.ds(off, tk), :]
            s = _mm_nt(q4, kk) * scale
            kpos = k_start + kt * tk + kcol
            dist = qpos - kpos
            valid = (dist >= 0) & (dist < NSA_WINDOW) & (kpos >= 0)
            return _softmax_step(carry, s, valid, vv)

        lo = jnp.maximum(0, (q0 - (NSA_WINDOW - 1) - k_start) // tk)
        hi = jnp.minimum(n_kt, (q0 + tq - 1 - k_start) // tk + 1)
        finish(lax.fori_loop(lo, hi, win_step, _softmax_init(rows)), outs.pop(0))


def nsa_attend(q_rot3, tq, tk, q_start, k_start, sel_args=None, win_args=None):
    b, t, qw = q_rot3.shape
    kv = qw // (NSA_GROUP * LANES)
    gw = NSA_GROUP * LANES
    in_specs = [pl.BlockSpec((1, tq, gw), lambda i, g, s: (i, s, g))]
    args = [q_rot3]
    n_out = 0
    t_k = None
    if sel_args is not None:
        k, v, sel = sel_args
        t_k = k.shape[1]
        nbp = sel.shape[-1]
        in_specs += [pl.BlockSpec((1, t_k, LANES), lambda i, g, s: (i, 0, g)),
                     pl.BlockSpec((1, t_k, LANES), lambda i, g, s: (i, 0, g)),
                     pl.BlockSpec((1, 1, tq, nbp), lambda i, g, s: (i, g, s, 0))]
        args += [k, v, sel]
        n_out += 1
    if win_args is not None:
        k, v = win_args
        t_k = k.shape[1]
        in_specs += [pl.BlockSpec((1, t_k, LANES), lambda i, g, s: (i, 0, g)),
                     pl.BlockSpec((1, t_k, LANES), lambda i, g, s: (i, 0, g))]
        args += [k, v]
        n_out += 1
    body = functools.partial(_attn_body, tq=tq, tk=tk, n_kt=t_k // tk, q_start=q_start, k_start=k_start,
                             do_sel=sel_args is not None, do_win=win_args is not None)
    return pl.pallas_call(
        body, grid=(b, kv, t // tq), in_specs=in_specs,
        out_specs=[pl.BlockSpec((1, tq, gw), lambda i, g, s: (i, s, g))] * n_out,
        out_shape=[jax.ShapeDtypeStruct((b, t, qw), F32)] * n_out,
        compiler_params=_cp("arbitrary", "arbitrary", "arbitrary"), name="nsa_attend")(*args)


def _attn_t_body(q_ref, ks_ref, vs_ref, kw_ref, vw_ref, sel_ref, osel_ref, owin_ref, vst_ref, vwt_ref, *, tq, tk, n_kt):
    qt = pl.program_id(2)
    scale = NSA_DH ** -0.5
    t_k = ks_ref.shape[1]

    @pl.when(qt == 0)
    def _():
        for j in range(t_k // LANES):
            sl = slice(j * LANES, (j + 1) * LANES)
            vst_ref[:, sl] = vs_ref[0, sl, :].T.astype(BF16)
            vwt_ref[:, sl] = vw_ref[0, sl, :].T.astype(BF16)

    q0 = qt * tq
    cols = NSA_GROUP * tq
    sel_t = sel_ref[0, 0].T.astype(BF16)
    nbp = sel_t.shape[0]
    q4 = jnp.concatenate([q_ref[0, :, r * LANES:(r + 1) * LANES] for r in range(NSA_GROUP)], axis=0).astype(BF16)

    def scores(k_ref, off, n, allowed):
        bias = jnp.where(allowed, 0.0, NEG)
        return _mm_nt(k_ref[0, pl.ds(off, n), :], q4) * scale + jnp.concatenate([bias] * NSA_GROUP, axis=1)

    def write(o_ref, acc, l):
        o = acc / jnp.maximum(l, 1e-30)
        for r in range(NSA_GROUP):
            o_ref[0, :, r * LANES:(r + 1) * LANES] = o[:, r * tq:(r + 1) * tq].T

    kpos_l = lax.broadcasted_iota(jnp.int32, (tk, tq), 0)
    qpos = q0 + lax.broadcasted_iota(jnp.int32, (tk, tq), 1)
    en = lax.broadcasted_iota(jnp.int32, (tk, nbp), 1)
    es = lax.broadcasted_iota(jnp.int32, (tk, nbp), 0)

    def sel_step(kt, carry):
        m, l, acc = carry
        off = pl.multiple_of(kt * tk, tk)
        expand = ((kt * tk + es) // NSA_BLOCK == en).astype(BF16)
        chosen = jnp.dot(expand, sel_t, preferred_element_type=F32)
        s = scores(ks_ref, off, tk, (chosen > 0.5) & (kt * tk + kpos_l <= qpos))
        m_new = jnp.maximum(m, jnp.max(s, axis=0, keepdims=True))
        p = jnp.exp(s - m_new)
        alpha = jnp.exp(m - m_new)
        l = alpha * l + jnp.sum(p, axis=0, keepdims=True)
        acc = alpha * acc + jnp.dot(vst_ref[:, pl.ds(off, tk)], p.astype(BF16), preferred_element_type=F32)
        return m_new, l, acc

    init = (jnp.full((1, cols), NEG, F32), jnp.zeros((1, cols), F32), jnp.zeros((NSA_DH, cols), F32))
    m, l, acc = lax.fori_loop(0, jnp.minimum(n_kt, (q0 + tq - 1) // tk + 1), sel_step, init)
    write(osel_ref, acc, l)

    wk = min(t_k, NSA_WINDOW + tq)
    ws = pl.multiple_of(jnp.clip(q0 - NSA_WINDOW, 0, t_k - wk), LANES)
    dist = (q0 + lax.broadcasted_iota(jnp.int32, (wk, tq), 1)) - (ws + lax.broadcasted_iota(jnp.int32, (wk, tq), 0))
    s = scores(kw_ref, ws, wk, (dist >= 0) & (dist < NSA_WINDOW))
    p = jnp.exp(s - jnp.max(s, axis=0, keepdims=True))
    acc = jnp.dot(vwt_ref[:, pl.ds(ws, wk)], p.astype(BF16), preferred_element_type=F32)
    write(owin_ref, acc, jnp.sum(p, axis=0, keepdims=True))


def nsa_attend_prompt(q_rot3, ks, vs, kw, vw, sel, tq, tk):
    b, t, qw = q_rot3.shape
    kv = qw // (NSA_GROUP * LANES)
    gw = NSA_GROUP * LANES
    kspec = pl.BlockSpec((1, t, LANES), lambda i, g, s: (i, 0, g))
    ospec = pl.BlockSpec((1, tq, gw), lambda i, g, s: (i, s, g))
    body = functools.partial(_attn_t_body, tq=tq, tk=tk, n_kt=t // tk)
    return pl.pallas_call(
        body, grid=(b, kv, t // tq),
        in_specs=[ospec, kspec, kspec, kspec, kspec,
                  pl.BlockSpec((1, 1, tq, sel.shape[-1]), lambda i, g, s: (i, g, s, 0))],
        out_specs=[ospec, ospec],
        out_shape=[jax.ShapeDtypeStruct((b, t, qw), F32)] * 2,
        scratch_shapes=[pltpu.VMEM((NSA_DH, t), BF16), pltpu.VMEM((NSA_DH, t), BF16)],
        compiler_params=_cp("arbitrary", "arbitrary", "arbitrary"), name="nsa_attend_prompt")(
            q_rot3, ks, vs, kw, vw, sel)


def _paged_sel_body(pt_ref, q_ref, sel_ref, kn_ref, vn_ref, k_hbm, v_hbm, o_ref,
                    kbuf, vbuf, sem, m_ref, l_ref, acc_ref, *, pg, past_len, t_new):
    bi, step = pl.program_id(0), pl.program_id(1)
    nsteps = pl.num_programs(1)
    total = pl.num_programs(0) * nsteps
    lin = bi * nsteps + step
    _, page, kv, _ = k_hbm.shape
    rows = q_ref.shape[1]
    per = rows // kv
    scale = NSA_DH ** -0.5
    nbp = sel_ref.shape[-1]
    qb = q_ref[0].astype(BF16)
    selb = sel_ref[0].astype(BF16)

    def page_copies(b_, s_, slot):
        out = []
        for i in range(pg):
            phys = pt_ref[b_, s_ * pg + i]
            for g in range(kv):
                dst = pl.ds(i * page, page)
                out.append(pltpu.make_async_copy(k_hbm.at[phys, :, g, :], kbuf.at[slot, g, dst, :], sem.at[slot, 0]))
                out.append(pltpu.make_async_copy(v_hbm.at[phys, :, g, :], vbuf.at[slot, g, dst, :], sem.at[slot, 1]))
        return out

    @pl.when(lin == 0)
    def _():
        for c in page_copies(0, 0, 0):
            c.start()

    @pl.when(lin + 1 < total)
    def _():
        nxt = lin + 1
        for c in page_copies(nxt // nsteps, nxt % nsteps, nxt % 2):
            c.start()

    @pl.when(step == 0)
    def _():
        m_ref[...] = jnp.full(m_ref.shape, NEG, F32)
        l_ref[...] = jnp.zeros(l_ref.shape, F32)
        acc_ref[...] = jnp.zeros(acc_ref.shape, F32)

    def update(key_of, val_of, n, kp0):
        tloc = lax.broadcasted_iota(jnp.int32, (rows, n), 0) % t_new
        kcol = lax.broadcasted_iota(jnp.int32, (rows, n), 1)
        en = lax.broadcasted_iota(jnp.int32, (nbp, n), 0)
        es = lax.broadcasted_iota(jnp.int32, (nbp, n), 1)
        s = jnp.concatenate([_mm_nt(qb[g * per:(g + 1) * per], key_of(g)) for g in range(kv)], axis=0) * scale
        expand = ((kp0 + es) // NSA_BLOCK == en).astype(BF16)
        chosen = jnp.dot(selb, expand, preferred_element_type=F32)
        valid = (chosen > 0.5) & (kp0 + kcol <= past_len + tloc)
        s = jnp.where(valid, s, NEG)
        m = m_ref[:, 0:1]
        m_new = jnp.maximum(m, jnp.max(s, axis=-1, keepdims=True))
        p = jnp.where(valid, jnp.exp(s - m_new), 0.0)
        alpha = jnp.exp(m - m_new)
        pv = jnp.concatenate([_mm(p[g * per:(g + 1) * per], val_of(g)) for g in range(kv)], axis=0)
        m_ref[...] = jnp.broadcast_to(m_new, m_ref.shape)
        l_ref[...] = jnp.broadcast_to(alpha * l_ref[:, 0:1] + jnp.sum(p, axis=-1, keepdims=True), l_ref.shape)
        acc_ref[...] = alpha * acc_ref[...] + pv

    slot = lin % 2
    for c in page_copies(bi, step, slot):
        c.wait()
    update(lambda g: kbuf[slot, g], lambda g: vbuf[slot, g], pg * page, step * (pg * page))

    @pl.when(step == nsteps - 1)
    def _():
        update(lambda g: kn_ref[0, :, g * LANES:(g + 1) * LANES],
               lambda g: vn_ref[0, :, g * LANES:(g + 1) * LANES], kn_ref.shape[1], past_len)
        o_ref[0] = acc_ref[...] / jnp.maximum(l_ref[:, 0:1], 1e-30)


def nsa_paged_sel(q_rot3, sel, pool_k, pool_v, page_table, k_new, v_new, pg):
    b, t_new, qw = q_rot3.shape
    n_pages = page_table.shape[1]
    _, page, kv, dh = pool_k.shape
    kvw = kv * dh
    nbp = sel.shape[-1]
    per = NSA_GROUP * t_new
    rows = kv * per
    q_rows = q_rot3.reshape(b, t_new, kv, NSA_GROUP, dh).transpose(0, 2, 3, 1, 4).reshape(b, rows, dh)
    sel_rows = jnp.broadcast_to(sel[:, :, None], (b, kv, NSA_GROUP, t_new, nbp)).reshape(b, rows, nbp)
    gs = pltpu.PrefetchScalarGridSpec(
        num_scalar_prefetch=1, grid=(b, n_pages // pg),
        in_specs=[pl.BlockSpec((1, rows, dh), lambda bi, s, pt: (bi, 0, 0)),
                  pl.BlockSpec((1, rows, nbp), lambda bi, s, pt: (bi, 0, 0)),
                  pl.BlockSpec((1, page, kvw), lambda bi, s, pt: (bi, 0, 0)),
                  pl.BlockSpec((1, page, kvw), lambda bi, s, pt: (bi, 0, 0)),
                  pl.BlockSpec(memory_space=pl.ANY), pl.BlockSpec(memory_space=pl.ANY)],
        out_specs=pl.BlockSpec((1, rows, dh), lambda bi, s, pt: (bi, 0, 0)),
        scratch_shapes=[pltpu.VMEM((2, kv, pg * page, dh), F32), pltpu.VMEM((2, kv, pg * page, dh), F32),
                        pltpu.SemaphoreType.DMA((2, 2)),
                        pltpu.VMEM((rows, LANES), F32), pltpu.VMEM((rows, LANES), F32),
                        pltpu.VMEM((rows, dh), F32)])
    body = functools.partial(_paged_sel_body, pg=pg, past_len=n_pages * page, t_new=t_new)
    o_rows = pl.pallas_call(
        body, grid_spec=gs, out_shape=jax.ShapeDtypeStruct((b, rows, dh), F32),
        compiler_params=_cp("arbitrary", "arbitrary"), name="nsa_paged_sel")(
            page_table, q_rows, sel_rows, k_new, v_new, pool_k, pool_v)
    return o_rows.reshape(b, kv, NSA_GROUP, t_new, NSA_DH).transpose(0, 3, 1, 2, 4).reshape(b, t_new, qw)


def _combine_body(oc_ref, os_ref, ow_ref, gt_ref, a_ref):
    gs = _sigmoid(gt_ref[...])
    for hh in range(oc_ref.shape[1] // LANES):
        sl = slice(hh * LANES, (hh + 1) * LANES)
        a = (gs[:, 3 * hh:3 * hh + 1] * oc_ref[:, sl] + gs[:, 3 * hh + 1:3 * hh + 2] * os_ref[:, sl]
             + gs[:, 3 * hh + 2:3 * hh + 3] * ow_ref[:, sl])
        a_ref[:, sl] = a.astype(BF16)


def nsa_combine(o_cmp, o_sel, o_win, gates, tm):
    m, qw = o_cmp.shape
    spec = pl.BlockSpec((tm, qw), lambda i: (i, 0))
    return pl.pallas_call(
        _combine_body, grid=(m // tm,),
        in_specs=[spec, spec, spec, pl.BlockSpec((tm, LANES), lambda i: (i, 0))],
        out_specs=spec, out_shape=jax.ShapeDtypeStruct((m, qw), BF16),
        compiler_params=_cp("arbitrary"), name="nsa_combine")(o_cmp, o_sel, o_win, gates)


def _s5_body(*refs, seg, seq_len, has_state):
    if has_state:
        (u_ref, a1_ref, a2_ref, dt_ref, b1_ref, b2_ref, cm_ref, d_ref, s0_ref,
         z_ref, st_ref, x_ref, y_ref, up_ref) = refs
    else:
        (u_ref, a1_ref, a2_ref, dt_ref, b1_ref, b2_ref, cm_ref, d_ref,
         z_ref, st_ref, x_ref, y_ref, up_ref) = refs
    gq = pl.program_id(1)
    m = u_ref.shape[0]
    nseg = m // seg
    nb = m // seq_len
    half = LANES // 2
    lane = lax.broadcasted_iota(jnp.int32, (1, LANES), 1)
    sgn = jnp.where(lane < half, -1.0, 1.0)

    gs = a1_ref.shape[0]
    abar, bcats = [], []
    for gi in range(gs):
        are, aim, dt = a1_ref[gi], a2_ref[gi], jnp.exp(dt_ref[gi])
        er = jnp.exp(are * dt)
        abr, abi = er * jnp.cos(aim * dt), er * jnp.sin(aim * dt)
        nr, ni, den = abr - 1.0, abi, are * are + aim * aim
        cr, cim = (nr * are + ni * aim) / den, (ni * are - nr * aim) / den
        abar.append((abr, abi))
        bcats.append((cr * b1_ref[gi] + cim * b2_ref[gi]).astype(BF16))

    def cmul(x, pr, pi):
        return x * pr + pltpu.roll(x, half, 1) * (pi * sgn)

    pb = SUBLANES * seg
    nblk = m // pb
    ri = lax.broadcasted_iota(jnp.int32, (pb, pb), 0)
    ci = lax.broadcasted_iota(jnp.int32, (pb, pb), 1)

    @pl.when(gq == 0)
    def _():
        perm = (ci == (ri % SUBLANES) * seg + ri // SUBLANES).astype(BF16)
        for k in range(nblk):
            uk = u_ref[k * pb:(k + 1) * pb, :].astype(BF16)
            up_ref[k] = jnp.dot(perm, uk, preferred_element_type=F32).astype(BF16)

    x_ref[...] = jnp.dot(up_ref[...].reshape(m, LANES), jnp.concatenate(bcats, axis=1),
                         preferred_element_type=F32).reshape(nblk, pb, gs * LANES)

    def scan_group(gi):
        abr, abi = abar[gi]
        gl = slice(gi * LANES, (gi + 1) * LANES)
        x = jnp.zeros((nseg, LANES), F32)
        for s in range(seg):
            sl = slice(s * SUBLANES, (s + 1) * SUBLANES)
            x = cmul(x, abr, abi) + x_ref[:, sl, gl].reshape(nseg, LANES)
            x_ref[:, sl, gl] = x.reshape(nblk, SUBLANES, LANES)
        if has_state:
            carry = s0_ref[gi]
        else:
            spb = seq_len // seg
            pr, pi = abr, abi
            for _ in range(int(math.log2(seg))):
                pr, pi = pr * pr - pi * pi, 2.0 * pr * pi
            rown = lax.broadcasted_iota(jnp.int32, (nseg, LANES), 0) % spb
            inc = x
            sh = 1
            while sh < spb:
                inc = inc + jnp.where(rown >= sh, cmul(pltpu.roll(inc, sh, 0), pr, pi), 0.0)
                pr, pi = pr * pr - pi * pi, 2.0 * pr * pi
                sh *= 2
            carry = jnp.where(rown >= 1, pltpu.roll(inc, 1, 0), 0.0)
        pr, pi = abr, abi
        for s in range(seg):
            sl = slice(s * SUBLANES, (s + 1) * SUBLANES)
            x_ref[:, sl, gl] = x_ref[:, sl, gl] + cmul(carry, pr, pi).reshape(nblk, SUBLANES, LANES)
            pr, pi = pr * abr - pi * abi, pr * abi + pi * abr
        finals = []
        for bi in range(nb):
            last_seg = (bi + 1) * (seq_len // seg) - 1
            row = (seg - 1) * SUBLANES + last_seg % SUBLANES
            finals.append(x_ref[last_seg // SUBLANES, row:row + 1, gl])
        st_ref[gi] = jnp.concatenate(finals, axis=0)

    for gi in range(gs):
        scan_group(gi)

    yg = _mm(x_ref[...].reshape(m, gs * LANES), cm_ref[...].reshape(gs * LANES, LANES))

    @pl.when(gq == 0)
    def _():
        y_ref[...] = yg

    @pl.when(gq > 0)
    def _():
        y_ref[...] = y_ref[...] + yg

    @pl.when(gq == pl.num_programs(1) - 1)
    def _():
        unperm = (ri == (ci % SUBLANES) * seg + ci // SUBLANES).astype(BF16)
        for k in range(nblk):
            rows = slice(k * pb, (k + 1) * pb)
            yk = y_ref[rows, :]
            hi = yk.astype(BF16)
            lo = (yk - hi.astype(F32)).astype(BF16)
            y = (jnp.dot(unperm, hi, preferred_element_type=F32) + jnp.dot(unperm, lo, preferred_element_type=F32)
                 + d_ref[...] * u_ref[rows, :])
            z = 0.5 * y * (1.0 + jnp.tanh(math.sqrt(2.0 / math.pi) * (y + 0.044715 * (y * y * y))))
            z_ref[rows, :] = z.astype(BF16)


def s5_scan(u, prm, seq_len, seg, s0=None):
    m, d = u.shape
    a1, a2, dtb, b1, b2, cm, dsk = prm
    groups = a1.shape[0]
    per_tile = LANES // S5_CH
    nb = m // seq_len
    has_state = s0 is not None
    gs = S5_GROUPS_PER_STEP
    steps = per_tile // gs
    gidx = lambda j, q: (j * steps + q, 0, 0)
    vspec = pl.BlockSpec((gs, 1, LANES), gidx)
    mspec = pl.BlockSpec((gs, LANES, LANES), gidx)
    in_specs = [pl.BlockSpec((m, LANES), lambda j, q: (0, j)), vspec, vspec, vspec, mspec, mspec, mspec,
                pl.BlockSpec((1, LANES), lambda j, q: (0, j))]
    args = [u, a1, a2, dtb, b1, b2, cm, dsk]
    if has_state:
        in_specs.append(pl.BlockSpec((gs, nb, LANES), gidx))
        args.append(s0)
    body = functools.partial(_s5_body, seg=seg, seq_len=seq_len, has_state=has_state)
    return pl.pallas_call(
        body, grid=(d // LANES, steps), in_specs=in_specs,
        out_specs=[pl.BlockSpec((m, LANES), lambda j, q: (0, j)),
                   pl.BlockSpec((gs, nb, LANES), gidx)],
        out_shape=[jax.ShapeDtypeStruct((m, d), BF16), jax.ShapeDtypeStruct((groups, nb, LANES), F32)],
        scratch_shapes=[pltpu.VMEM((m // (SUBLANES * seg), SUBLANES * seg, gs * LANES), F32),
                        pltpu.VMEM((m, LANES), F32),
                        pltpu.VMEM((m // (SUBLANES * seg), SUBLANES * seg, LANES), BF16)],
        compiler_params=_cp("arbitrary", "arbitrary"), name="s5_scan")(*args)


def _s5_params(a_re, a_im, log_dt, b_re, b_im, c_re, c_im, d_skip):
    groups, p = a_re.shape
    per_tile = LANES // S5_CH
    dup = lambda a: jnp.concatenate([a, a], axis=-1)[:, None, :]
    a1, a2 = dup(a_re), dup(a_im)
    dtb = jnp.broadcast_to(log_dt[:, None, None], (groups, 1, LANES))
    slot = jax.nn.one_hot(jnp.arange(groups) % per_tile, per_tile, dtype=F32)

    def rows_in_tile(w):
        return (slot[:, :, None, None] * w[:, None]).reshape(groups, LANES, w.shape[-1])

    bre_t, bim_t = b_re.transpose(0, 2, 1), b_im.transpose(0, 2, 1)
    b1 = rows_in_tile(jnp.concatenate([bre_t, bim_t], axis=-1))
    b2 = rows_in_tile(jnp.concatenate([-bim_t, bre_t], axis=-1))
    cmat = jnp.concatenate([c_re, -c_im], axis=-1)
    cm = rows_in_tile(cmat).transpose(0, 2, 1)
    return a1, a2, dtb, b1, b2, cm, d_skip.reshape(1, -1)


def _tiles(m):
    if m >= 1024:
        return 1024, 512
    return m, m


def _gla_layer(h, b, t, gain, w_in, w_tail, w_alpha_pad, b_alpha, head_norm, w_out, s0, heads, dk, dv):
    tm, tmo = _tiles(h.shape[0])
    n_main = 2 * heads * dk + 2 * heads * dv
    proj = norm_proj(h, gain, w_in, n_main, tm, 512)
    lr = norm_proj(h, gain, w_tail, LANES, tm, LANES)
    c = math.gcd(t, CHUNK)
    tb = math.gcd(t, 128)
    og, st = recurrence("gla", proj.reshape(b, t, n_main), heads, dk, dv, head_norm, tb, c,
                        (lr.reshape(b, t, LANES), w_alpha_pad, b_alpha.reshape(1, -1)), s0, hb=2)
    return out_proj(og.reshape(b * t, heads * dv), w_out, h, tmo, 512), st


def _hgrn_layer(h, b, t, gain, w_in, lower_bound, layer, head_norm, w_out, s0, heads, dk):
    tm, tmo = _tiles(h.shape[0])
    n = 4 * heads * dk
    proj = norm_proj(h, gain, w_in, n, tm, 512)
    c = math.gcd(t, CHUNK)
    tb = math.gcd(t, 128)
    og, st = recurrence("hgrn", proj.reshape(b, t, n), heads, dk, dk, head_norm, tb, c,
                        (lower_bound,), s0, layer=layer, hb=4)
    return out_proj(og.reshape(b * t, heads * dk), w_out, h, tmo, 512), st


def _rope_tables(start, t):
    half = NSA_DH // 2
    inv = ROPE_THETA ** (-jnp.arange(half, dtype=F32) / half)
    ang = (start + jnp.arange(t, dtype=jnp.int32)).astype(F32)[:, None] * inv[None, :]
    cos, sin = jnp.cos(ang), jnp.sin(ang)
    return jnp.concatenate([cos, cos], axis=-1), jnp.concatenate([-sin, sin], axis=-1)


def _nsa_layer(h, b, t, start, gain, w_in, w_gates, pool_k, pool_v, w_out, past, heads, kv):
    tm, tmo = _tiles(h.shape[0])
    qw, kvw = heads * NSA_DH, kv * NSA_DH
    n_main = qw + 6 * kvw
    proj = norm_proj(h, gain, w_in, n_main, tm, 512)
    gates = norm_proj(h, gain, w_gates, LANES, tm, LANES)
    cos, sin = _rope_tables(start, t)
    pk = jnp.broadcast_to(pool_k[:, None], (NSA_BLOCK, kvw))
    pv = jnp.broadcast_to(pool_v[:, None], (NSA_BLOCK, kvw))
    proj3 = proj.reshape(b, t, n_main)
    col = lambda i: proj3[:, :, qw + i * kvw:qw + (i + 1) * kvw]
    kc, vc, vs, vw = col(0), col(1), col(3), col(5)

    if past is None:
        trope = min(256, t)
        q_rot, ks, kw = nsa_rope(proj, cos, sin, trope, t // trope, qw, kvw)
        q_rot3, ks3, kw3 = q_rot.reshape(b, t, qw), ks.reshape(b, t, kvw), kw.reshape(b, t, kvw)
        n_cb = t // NSA_BLOCK
        n_blk = -(-t // NSA_BLOCK)
        kcmp, vcmp = nsa_pool_prompt(proj3, pk, pv, qw, kvw, LANES)
        o_cmp, sel = nsa_cmp(proj3, kcmp, vcmp, min(t, 512), 0, n_cb, n_blk, LANES, kvw)
        tq = min(t, 128)
        o_sel, o_win = nsa_attend_prompt(q_rot3, ks3, vs, kw3, vw, sel, tq, min(t, 512))
        keep = min(NSA_WINDOW, t)
        win_k, win_v = kw3[:, t - keep:], vw[:, t - keep:]
    else:
        pool_ck, pool_cv, pool_sk, pool_sv, page_table, prev_kw, prev_vw = past
        n_pages = page_table.shape[1]
        page = pool_ck.shape[1]
        past_len = n_pages * page
        cos_r, sin_r = jnp.tile(cos, (b, 1)), jnp.tile(sin, (b, 1))
        q_rot, ks, kw = nsa_rope(proj, cos_r, sin_r, b * t, 1, qw, kvw)
        q_rot3, ks3, kw3 = q_rot.reshape(b, t, qw), ks.reshape(b, t, kvw), kw.reshape(b, t, kvw)
        pk4 = jnp.broadcast_to(pool_k[:, None, None], (NSA_BLOCK, kv, NSA_DH))
        pv4 = jnp.broadcast_to(pool_v[:, None, None], (NSA_BLOCK, kv, NSA_DH))
        kcmp, vcmp = nsa_pool_pages(pool_ck, pool_cv, page_table, pk4, pv4, 8)
        total = past_len + t
        n_cb = total // NSA_BLOCK
        n_blk = -(-total // NSA_BLOCK)
        nb_pad = -(-n_blk // LANES) * LANES
        o_cmp, sel = nsa_cmp(proj3, kcmp, vcmp, t, past_len, n_cb, n_blk, nb_pad, kvw)
        padp = lambda a: jnp.concatenate([a, jnp.zeros((b, page - t, kvw), F32)], axis=1)
        o_sel = nsa_paged_sel(q_rot3, sel, pool_sk, pool_sv, page_table, padp(ks3), padp(vs), 8)
        keep = prev_kw.shape[1]
        kw_ext = jnp.concatenate([prev_kw.reshape(b, keep, kvw), kw3], axis=1)
        vw_ext = jnp.concatenate([prev_vw.reshape(b, keep, kvw), vw], axis=1)
        t_ext = keep + t
        t_pad = -(-t_ext // LANES) * LANES
        pade = lambda a: jnp.concatenate([a, jnp.zeros((b, t_pad - t_ext, kvw), F32)], axis=1)
        (o_win,) = nsa_attend(q_rot3, t, LANES, past_len, past_len - keep, win_args=(pade(kw_ext), pade(vw_ext)))
        win_k, win_v = kw_ext[:, t_ext - keep:], vw_ext[:, t_ext - keep:]

    a = nsa_combine(o_cmp.reshape(b * t, qw), o_sel.reshape(b * t, qw), o_win.reshape(b * t, qw), gates,
                    min(b * t, 512))
    y = out_proj(a, w_out, h, tmo, 512)
    shp = lambda x, n: x.reshape(b, n, kv, NSA_DH)
    return y, (shp(kc, t), shp(vc, t), shp(ks3, t), shp(vs, t), shp(win_k, keep), shp(win_v, keep))


def _s5_layer(h, b, t, gain, prm, w_glu, s_re, s_im):
    tm, tmo = _tiles(h.shape[0])
    u = rmsnorm_rows(h, gain, min(h.shape[0], 512))
    groups = prm[0].shape[0]
    if s_re is None:
        z, st = s5_scan(u, prm, t, math.gcd(t, 32))
    else:
        s0 = jnp.concatenate([s_re, s_im], axis=-1).transpose(1, 0, 2)
        z, st = s5_scan(u, prm, t, t, s0)
    y = out_glu(z, w_glu, h, tmo, 512)
    st = st.transpose(1, 0, 2)
    return y, (st[..., :S5_STATE], st[..., S5_STATE:])


def _ffn_layer(h, b, t, gain, w_in, conv_w, conv_b, w_out, buf):
    tm, tmo = _tiles(h.shape[0])
    ff2 = w_in.shape[1]
    if buf is None:
        tm = min(512, t)
        act, tg, tv = ffn_in(h, gain, w_in, conv_w, conv_b, tm, 512, t)
        per = t // tm
        last = lambda a: a[per - 1::per, SUBLANES - (CONV_W - 1):, :]
        state = jnp.concatenate([last(tg), last(tv)], axis=-1)
    else:
        zrow = jnp.zeros((b, t - 1, ff2), F32)
        p1 = jnp.concatenate([buf[:, 1:2], zrow], axis=1).reshape(b * t, ff2)
        p2 = jnp.concatenate([buf, zrow[:, 1:]], axis=1).reshape(b * t, ff2)
        act, tg, tv = ffn_in(h, gain, w_in, conv_w, conv_b, b * t, 512, t, hist=(p1, p2))
        up = jnp.concatenate([tg[0], tv[0]], axis=-1).reshape(b, t, ff2)
        state = jnp.concatenate([buf, up], axis=1)[:, t:]
    return out_proj(act, w_out, h, min(h.shape[0], 1024), 256), state


def kernel(x_prompt, x_sample, state_gla, state_hgrn, cache_nsa_cmp_k, cache_nsa_cmp_v, cache_nsa_sel_k, cache_nsa_sel_v, cache_nsa_win_k, cache_nsa_win_v, state_s5_re, state_s5_im, state_ffn_conv, page_table, norm_mix, norm_ffn, final_norm, gla_w_in, gla_w_alpha, gla_b_alpha, gla_head_norm, gla_w_out, hgrn_w_in, hgrn_lower_bound, hgrn_head_norm, hgrn_w_out, nsa_w_in, nsa_pool_k, nsa_pool_v, nsa_w_out, s5_a_re, s5_a_im, s5_log_dt, s5_b_re, s5_b_im, s5_c_re, s5_c_im, s5_d, s5_w_glu, ffn_w_in, ffn_conv_w, ffn_conv_b, ffn_w_out):
    bp, tp, d = x_prompt.shape
    bs, ts, _ = x_sample.shape
    depth = norm_mix.shape[0]
    n_mixers = 4
    gla_heads, gla_dk, gla_dv = state_gla.shape[2], state_gla.shape[3], state_gla.shape[4]
    hgrn_heads, hgrn_dk = state_hgrn.shape[2], state_hgrn.shape[3]
    nsa_kv = cache_nsa_cmp_k.shape[3]
    nsa_heads = d // NSA_DH
    hp = x_prompt.reshape(bp * tp, d)
    hs = x_sample.reshape(bs * ts, d)
    bf = lambda w: w.astype(BF16)

    def pad_cols(w, n):
        return jnp.concatenate([w, jnp.zeros((w.shape[0], n - w.shape[1]), w.dtype)], axis=1)

    outs = {k: [] for k in ("gla_p", "gla_s", "hgrn_p", "hgrn_s", "nsa_p", "nsa_s", "s5_p", "s5_s", "conv_p", "conv_s")}
    for i in range(depth):
        kind, j = i % n_mixers, i // n_mixers
        if kind == 0:
            n_main = 2 * gla_heads * gla_dk + 2 * gla_heads * gla_dv
            w_in = cast_bf16(gla_w_in, j)
            w_tail = bf(pad_cols(gla_w_in[j, :, n_main:], LANES))
            rank = gla_w_alpha.shape[1]
            wa = bf(jnp.concatenate([gla_w_alpha[j], jnp.zeros((LANES - rank, gla_w_alpha.shape[2]), F32)], axis=0))
            common = (norm_mix[i], w_in, w_tail, wa, gla_b_alpha[j], gla_head_norm[j], cast_bf16(gla_w_out, j))
            hp, st_p = _gla_layer(hp, bp, tp, *common, None, gla_heads, gla_dk, gla_dv)
            hs, st_s = _gla_layer(hs, bs, ts, *common, state_gla[j], gla_heads, gla_dk, gla_dv)
            outs["gla_p"].append(st_p)
            outs["gla_s"].append(st_s)
        elif kind == 1:
            common = (norm_mix[i], cast_bf16(hgrn_w_in, j), hgrn_lower_bound, i, hgrn_head_norm[j],
                      cast_bf16(hgrn_w_out, j))
            hp, st_p = _hgrn_layer(hp, bp, tp, *common, None, hgrn_heads, hgrn_dk)
            hs, st_s = _hgrn_layer(hs, bs, ts, *common, state_hgrn[j], hgrn_heads, hgrn_dk)
            outs["hgrn_p"].append(st_p)
            outs["hgrn_s"].append(st_s)
        elif kind == 2:
            n_main = nsa_heads * NSA_DH + 6 * nsa_kv * NSA_DH
            w_in = cast_bf16(nsa_w_in, j)
            w_gates = bf(pad_cols(nsa_w_in[j, :, n_main:], LANES))
            common = (norm_mix[i], w_in, w_gates, nsa_pool_k[j], nsa_pool_v[j], cast_bf16(nsa_w_out, j))
            hp, st_p = _nsa_layer(hp, bp, tp, 0, *common, None, nsa_heads, nsa_kv)
            past = (cache_nsa_cmp_k[j], cache_nsa_cmp_v[j], cache_nsa_sel_k[j], cache_nsa_sel_v[j],
                    page_table, cache_nsa_win_k[j], cache_nsa_win_v[j])
            hs, st_s = _nsa_layer(hs, bs, ts, page_table.shape[1] * cache_nsa_cmp_k.shape[2], *common, past,
                                  nsa_heads, nsa_kv)
            outs["nsa_p"].append(st_p)
            outs["nsa_s"].append(st_s)
        else:
            prm = _s5_params(s5_a_re[j], s5_a_im[j], s5_log_dt[j], s5_b_re[j], s5_b_im[j], s5_c_re[j],
                             s5_c_im[j], s5_d[j])
            w_glu = cast_bf16(s5_w_glu, j)
            hp, st_p = _s5_layer(hp, bp, tp, norm_mix[i], prm, w_glu, None, None)
            hs, st_s = _s5_layer(hs, bs, ts, norm_mix[i], prm, w_glu, state_s5_re[j], state_s5_im[j])
            outs["s5_p"].append(st_p)
            outs["s5_s"].append(st_s)
        fw = (norm_ffn[i], cast_bf16(ffn_w_in, i), ffn_conv_w[i], ffn_conv_b[i], cast_bf16(ffn_w_out, i))
        hp, cb_p = _ffn_layer(hp, bp, tp, *fw, None)
        hs, cb_s = _ffn_layer(hs, bs, ts, *fw, state_ffn_conv[i])
        outs["conv_p"].append(cb_p)
        outs["conv_s"].append(cb_s)

    y_prompt = rmsnorm_rows(hp, final_norm, min(hp.shape[0], 512)).reshape(bp, tp, d)
    y_sample = rmsnorm_rows(hs, final_norm, min(hs.shape[0], 512)).reshape(bs, ts, d)
    stack = lambda xs: jnp.stack(xs)
    pick = lambda key, r: stack([e[r] for e in outs[key]])
    res = [y_prompt, y_sample, stack(outs["gla_p"]), stack(outs["gla_s"]), stack(outs["hgrn_p"]), stack(outs["hgrn_s"])]
    for r in range(6):
        res += [pick("nsa_p", r), pick("nsa_s", r)]
    for r in range(2):
        res += [pick("s5_p", r), pick("s5_s", r)]
    res += [stack(outs["conv_p"]), stack(outs["conv_s"])]
    return tuple(res)
```

```python
import functools
import math

import jax
import jax.numpy as jnp
from jax import lax
from jax.experimental import pallas as pl
from jax.experimental.pallas import tpu as pltpu

F32 = jnp.float32
BF16 = jnp.bfloat16
HIGHEST = lax.Precision.HIGHEST

RMS_EPS = 1e-6
ROPE_THETA = 10000.0
NEG = -1e30
CHUNK = 64
SUBCHUNK = 8
GLA_TEMP = 16.0
NSA_BLOCK = 64
NSA_TOP_N = 16
NSA_WINDOW = 512
NSA_GROUP = 4
NSA_DH = 128
S5_CH = 16
S5_STATE = 64
S5_GROUPS_PER_STEP = 2
CONV_W = 3
LANES = 128
SUBLANES = 8
VMEM_LIMIT = 48 * 1024 * 1024


def _cp(*sem):
    return pltpu.CompilerParams(dimension_semantics=sem, vmem_limit_bytes=VMEM_LIMIT)


def _mm(a, b):
    return jnp.dot(a.astype(BF16), b.astype(BF16), preferred_element_type=F32)


def _mm_nt(a, b):
    return lax.dot_general(a.astype(BF16), b.astype(BF16), (((1,), (1,)), ((), ())),
                           preferred_element_type=F32)


def _sigmoid(x):
    return 1.0 / (1.0 + jnp.exp(-x))


def _rms(x, g):
    return x * lax.rsqrt(jnp.mean(x * x, axis=-1, keepdims=True) + RMS_EPS) * g


CAST_BLOCK_BYTES = 4 * 1024 * 1024


def _cast_body(w_ref, o_ref):
    o_ref[...] = w_ref[0].astype(BF16)


def cast_bf16(w_stack, layer):
    _, k, n = w_stack.shape
    tk = 16
    while k % (2 * tk) == 0 and 2 * tk * n * 4 <= CAST_BLOCK_BYTES:
        tk *= 2
    return pl.pallas_call(
        _cast_body, grid=(k // tk,),
        in_specs=[pl.BlockSpec((1, tk, n), lambda i: (layer, i, 0))],
        out_specs=pl.BlockSpec((tk, n), lambda i: (i, 0)),
        out_shape=jax.ShapeDtypeStruct((k, n), BF16),
        compiler_params=_cp("arbitrary"), name="cast_bf16")(w_stack)


def _norm_body(x_ref, g_ref, o_ref):
    o_ref[...] = _rms(x_ref[...], g_ref[...])


def rmsnorm_rows(x, gain, tm):
    m, d = x.shape
    return pl.pallas_call(
        _norm_body, grid=(m // tm,),
        in_specs=[pl.BlockSpec((tm, d), lambda i: (i, 0)), pl.BlockSpec((1, d), lambda i: (0, 0))],
        out_specs=pl.BlockSpec((tm, d), lambda i: (i, 0)),
        out_shape=jax.ShapeDtypeStruct((m, d), F32),
        compiler_params=_cp("arbitrary"), name="rmsnorm")(x, gain.reshape(1, d))


def _proj_body(x_ref, g_ref, w_ref, o_ref, xn_ref):
    @pl.when(pl.program_id(1) == 0)
    def _():
        xn_ref[...] = _rms(x_ref[...], g_ref[...]).astype(BF16)
    o_ref[...] = jnp.dot(xn_ref[...], w_ref[...], preferred_element_type=F32)


def norm_proj(x, gain, w, n_out, tm, tn):
    m, d = x.shape
    return pl.pallas_call(
        _proj_body, grid=(m // tm, n_out // tn),
        in_specs=[pl.BlockSpec((tm, d), lambda i, j: (i, 0)),
                  pl.BlockSpec((1, d), lambda i, j: (0, 0)),
                  pl.BlockSpec((d, tn), lambda i, j: (0, j))],
        out_specs=pl.BlockSpec((tm, tn), lambda i, j: (i, j)),
        out_shape=jax.ShapeDtypeStruct((m, n_out), F32),
        scratch_shapes=[pltpu.VMEM((tm, d), BF16)],
        compiler_params=_cp("arbitrary", "arbitrary"), name="norm_proj")(x, gain.reshape(1, d), w)


def _out_body(a_ref, w_ref, r_ref, o_ref):
    o_ref[...] = r_ref[...] + jnp.dot(a_ref[...].astype(BF16), w_ref[...], preferred_element_type=F32)


def out_proj(a, w, res, tm, tn):
    m, k = a.shape
    n = w.shape[1]
    return pl.pallas_call(
        _out_body, grid=(m // tm, n // tn),
        in_specs=[pl.BlockSpec((tm, k), lambda i, j: (i, 0)),
                  pl.BlockSpec((k, tn), lambda i, j: (0, j)),
                  pl.BlockSpec((tm, tn), lambda i, j: (i, j))],
        out_specs=pl.BlockSpec((tm, tn), lambda i, j: (i, j)),
        out_shape=jax.ShapeDtypeStruct((m, n), F32),
        compiler_params=_cp("arbitrary", "arbitrary"), name="out_proj")(a, w, res)


def _out_glu_body(a_ref, w1_ref, w2_ref, r_ref, o_ref):
    a = a_ref[...]
    g1 = jnp.dot(a, w1_ref[...], preferred_element_type=F32)
    g2 = jnp.dot(a, w2_ref[...], preferred_element_type=F32)
    o_ref[...] = r_ref[...] + g1 * _sigmoid(g2)


def out_glu(a, w, res, tm, tn):
    m, k = a.shape
    n = w.shape[1] // 2
    nj = n // tn
    return pl.pallas_call(
        _out_glu_body, grid=(m // tm, nj),
        in_specs=[pl.BlockSpec((tm, k), lambda i, j: (i, 0)),
                  pl.BlockSpec((k, tn), lambda i, j: (0, j)),
                  pl.BlockSpec((k, tn), lambda i, j: (0, nj + j)),
                  pl.BlockSpec((tm, tn), lambda i, j: (i, j))],
        out_specs=pl.BlockSpec((tm, tn), lambda i, j: (i, j)),
        out_shape=jax.ShapeDtypeStruct((m, n), F32),
        compiler_params=_cp("arbitrary", "arbitrary"), name="out_glu")(a, w, w, res)


def _ffn_in_body(*refs, seg, tiles_per_seq, tail_rows, has_state):
    if has_state:
        (x_ref, g_ref, wg_ref, wv_ref, cwg_ref, cwv_ref, cbg_ref, cbv_ref,
         p1g_ref, p2g_ref, p1v_ref, p2v_ref,
         act_ref, tg_ref, tv_ref, xn_ref, carry_ref) = refs
    else:
        (x_ref, g_ref, wg_ref, wv_ref, cwg_ref, cwv_ref, cbg_ref, cbv_ref,
         act_ref, tg_ref, tv_ref, xn_ref, carry_ref) = refs
    i = pl.program_id(0)
    f = pl.program_id(1)
    tm = x_ref.shape[0]

    @pl.when(f == 0)
    def _():
        xn_ref[...] = _rms(x_ref[...], g_ref[...]).astype(BF16)

    xn = xn_ref[...]
    tf = wg_ref.shape[1]
    sb = tf
    row = lax.broadcasted_iota(jnp.int32, (tm, sb), 0)
    rowm = row % seg
    fresh = (i % tiles_per_seq) == 0

    def conv(w_ref, cw_ref, cb_ref, kind, p1_ref, p2_ref, cs, t_ref):
        u = jnp.dot(xn, w_ref[:, cs], preferred_element_type=F32)
        if has_state:
            p1 = p1_ref[:, cs]
            p2 = p2_ref[:, cs]
        else:
            prev = carry_ref[kind, f, :, cs]
            prev = jnp.where(fresh, 0.0, prev)
            prev0 = prev[SUBLANES - 2:SUBLANES - 1, :]
            prev1 = prev[SUBLANES - 1:SUBLANES, :]
            p1 = jnp.broadcast_to(prev1, u.shape)
            p2 = jnp.where(row == 0, prev0, prev1)
            carry_ref[kind, f, :, cs] = u[tm - SUBLANES:, :]
        u1 = jnp.where(rowm < 1, p1, pltpu.roll(u, 1, 0))
        u2 = jnp.where(rowm < 2, p2, pltpu.roll(u, 2, 0))
        cw = cw_ref[:, cs]
        t_ref[0, :, cs] = u[tm - tail_rows:, :]
        return cw[0:1, :] * u2 + cw[1:2, :] * u1 + cw[2:3, :] * u + cb_ref[:, cs]

    for jb in range(tf // sb):
        cs = slice(jb * sb, (jb + 1) * sb)
        mg = conv(wg_ref, cwg_ref, cbg_ref, 0, p1g_ref if has_state else None, p2g_ref if has_state else None,
                  cs, tg_ref)
        mv = conv(wv_ref, cwv_ref, cbv_ref, 1, p1v_ref if has_state else None, p2v_ref if has_state else None,
                  cs, tv_ref)
        act_ref[:, cs] = (mg * _sigmoid(mg) * mv).astype(BF16)


def ffn_in(x, gain, w_in, conv_w, conv_b, tm, tf, seq_len, hist=None):
    m, d = x.shape
    ff = w_in.shape[1] // 2
    nf = ff // tf
    nb = m // tm
    has_state = hist is not None
    if has_state:
        seg, tiles_per_seq, tail_rows = seq_len, 1, tm
    else:
        seg, tiles_per_seq, tail_rows = tm, seq_len // tm, SUBLANES
    wspec_g = pl.BlockSpec((d, tf), lambda i, f: (0, f))
    wspec_v = pl.BlockSpec((d, tf), lambda i, f: (0, nf + f))
    cspec_g = lambda r: pl.BlockSpec((r, tf), lambda i, f: (0, f))
    cspec_v = lambda r: pl.BlockSpec((r, tf), lambda i, f: (0, nf + f))
    in_specs = [pl.BlockSpec((tm, d), lambda i, f: (i, 0)), pl.BlockSpec((1, d), lambda i, f: (0, 0)),
                wspec_g, wspec_v, cspec_g(CONV_W), cspec_v(CONV_W), cspec_g(1), cspec_v(1)]
    args = [x, gain.reshape(1, d), w_in, w_in, conv_w, conv_w, conv_b.reshape(1, -1), conv_b.reshape(1, -1)]
    if has_state:
        p1, p2 = hist
        in_specs += [pl.BlockSpec((tm, tf), lambda i, f: (i, f)), pl.BlockSpec((tm, tf), lambda i, f: (i, f)),
                     pl.BlockSpec((tm, tf), lambda i, f: (i, nf + f)), pl.BlockSpec((tm, tf), lambda i, f: (i, nf + f))]
        args += [p1, p2, p1, p2]
    body = functools.partial(_ffn_in_body, seg=seg, tiles_per_seq=tiles_per_seq,
                             tail_rows=tail_rows, has_state=has_state)
    return pl.pallas_call(
        body, grid=(nb, nf), in_specs=in_specs,
        out_specs=[pl.BlockSpec((tm, tf), lambda i, f: (i, f)),
                   pl.BlockSpec((1, tail_rows, tf), lambda i, f: (i, 0, f)),
                   pl.BlockSpec((1, tail_rows, tf), lambda i, f: (i, 0, f))],
        out_shape=[jax.ShapeDtypeStruct((m, ff), BF16),
                   jax.ShapeDtypeStruct((nb, tail_rows, ff), F32),
                   jax.ShapeDtypeStruct((nb, tail_rows, ff), F32)],
        scratch_shapes=[pltpu.VMEM((tm, d), BF16), pltpu.VMEM((2, nf, SUBLANES, tf), F32)],
        compiler_params=_cp("arbitrary", "arbitrary"), name="ffn_in")(*args)


def _pad_rows(a, rows):
    if a.shape[0] == rows:
        return a
    return jnp.concatenate([a, jnp.zeros((rows - a.shape[0], a.shape[1]), a.dtype)], axis=0)


def _roll_in_tiles(x, d):
    c, n = x.shape
    return pltpu.roll(x.reshape(c // SUBLANES, SUBLANES, n), d, 1).reshape(c, n)


def _glr_chunk(q, k, v, g, st, c, sub):
    dk = q.shape[1]
    row = lax.broadcasted_iota(jnp.int32, (c, LANES), 0)
    col = lax.broadcasted_iota(jnp.int32, (c, LANES), 1)
    trow = lax.broadcasted_iota(jnp.int32, (c, c), 0)
    tcol = lax.broadcasted_iota(jnp.int32, (c, c), 1)
    tri = (trow >= tcol).astype(F32)
    cum = jnp.dot(tri, g, preferred_element_type=F32, precision=HIGHEST)
    last = cum[c - 1:c, :]
    inter = _mm_nt(q * jnp.exp(cum), st)

    rowk = lax.broadcasted_iota(jnp.int32, (c, dk), 0)
    rm = rowk % sub
    ones = jnp.ones((dk, LANES), BF16)
    att = jnp.zeros((c, LANES), F32)
    for d in range(sub):
        if d == 0:
            p = q * k
        else:
            ks = _roll_in_tiles(k, d)
            cs = _roll_in_tiles(cum, d)
            p = q * ks * jnp.exp(jnp.where(rm >= d, cum - cs, NEG))
        a = jnp.dot(p.astype(BF16), ones, preferred_element_type=F32)
        att = att + jnp.where(col == row - d, a, 0.0)
    if c > sub:
        blocks = [jnp.zeros((sub, LANES), F32)]
        for i in range(1, c // sub):
            cs = cum[i * sub - 1:i * sub, :]
            qi = q[i * sub:(i + 1) * sub, :] * jnp.exp(cum[i * sub:(i + 1) * sub, :] - cs)
            kj = k * jnp.exp(jnp.where(rowk < i * sub, cs - cum, NEG))
            blocks.append(_mm_nt(qi, _pad_rows(kj, LANES)))
        att = att + jnp.concatenate(blocks, axis=0)
    vpad = _pad_rows(v, LANES)
    intra = _mm(att, vpad)
    kd = _pad_rows(k * jnp.exp(last - cum), LANES)
    st_new = st * jnp.exp(last) + _mm(vpad.T, kd)
    return inter + intra, st_new


def _rec_body(*refs, mode, c, sub, n_chunks, dk, dv, hb, layer, has_state):
    refs = list(refs)
    if mode == "gla":
        q_ref, k_ref, v_ref, r_ref, lr_ref, wa_ref, ba_ref, hn_ref = refs[:8]
        rest = refs[8:]
    else:
        q_ref, k_ref, v_ref, r_ref, lb_ref, hn_ref = refs[:6]
        rest = refs[6:]
    if has_state:
        s0_ref, og_ref, sout_ref, st_ref = rest
    else:
        og_ref, sout_ref, st_ref = rest
    tstep = pl.program_id(2)

    @pl.when(tstep == 0)
    def _():
        for hh in range(hb):
            if has_state:
                st_ref[hh] = s0_ref[0, hh].T
            else:
                st_ref[hh] = jnp.zeros(st_ref.shape[1:], F32)

    if mode == "hgrn":
        lbx = lb_ref[...]
        e = jnp.exp(lbx - jnp.max(lbx, axis=0, keepdims=True))
        sm = e / jnp.sum(e, axis=0, keepdims=True)
        lb_all = jnp.zeros((1, hb * dk), F32)
        for li in range(1, layer + 1):
            lb_all = lb_all + sm[li:li + 1, :]

    for ci in range(n_chunks):
        sl = slice(ci * c, (ci + 1) * c)
        for hh in range(hb):
            hk = slice(hh * dk, (hh + 1) * dk)
            hv = slice(hh * dv, (hh + 1) * dv)
            if mode == "gla":
                q = q_ref[0, sl, hk] * (dk ** -0.5)
                k = k_ref[0, sl, hk]
                z = _mm(lr_ref[0, sl, :], wa_ref[:, hk]) + ba_ref[:, hk]
                g = -(jnp.maximum(-z, 0.0) + jnp.log1p(jnp.exp(-jnp.abs(z)))) / GLA_TEMP
            else:
                qz = q_ref[0, sl, hk]
                q = qz * _sigmoid(qz)
                lbv = lb_all[:, hk]
                fg = lbv + (1.0 - lbv) * _sigmoid(k_ref[0, sl, hk])
                k = 1.0 - fg
                g = jnp.log(fg)
            v = v_ref[0, sl, hv]
            o, st_new = _glr_chunk(q, k, v, g, st_ref[hh], c, sub)
            st_ref[hh] = st_new
            of = o * lax.rsqrt(jnp.mean(o * o, axis=-1, keepdims=True) + RMS_EPS) * hn_ref[...]
            gate = r_ref[0, sl, hv]
            og_ref[0, sl, hv] = (of * (gate * _sigmoid(gate))).astype(BF16)

    @pl.when(tstep == pl.num_programs(2) - 1)
    def _():
        for hh in range(hb):
            sout_ref[0, hh] = st_ref[hh].T


def recurrence(mode, proj, heads, dk, dv, hn, tb, c, extra, s0=None, layer=0, hb=1):
    b, t, _ = proj.shape
    sub = min(SUBCHUNK, c)
    has_state = s0 is not None
    wk, wv = hb * dk, hb * dv
    nh = heads // hb
    if mode == "gla":
        lr, wa, ba = extra
        koff, voff = nh, (2 * heads * dk) // wv
        roff = voff + nh
        in_specs = [pl.BlockSpec((1, tb, wk), lambda i, h, s: (i, s, h)),
                    pl.BlockSpec((1, tb, wk), lambda i, h, s: (i, s, koff + h)),
                    pl.BlockSpec((1, tb, wv), lambda i, h, s: (i, s, voff + h)),
                    pl.BlockSpec((1, tb, wv), lambda i, h, s: (i, s, roff + h)),
                    pl.BlockSpec((1, tb, LANES), lambda i, h, s: (i, s, 0)),
                    pl.BlockSpec((LANES, wk), lambda i, h, s: (0, h)),
                    pl.BlockSpec((1, wk), lambda i, h, s: (0, h)),
                    pl.BlockSpec((1, dv), lambda i, h, s: (0, 0))]
        args = [proj, proj, proj, proj, lr, wa, ba, hn.reshape(1, dv)]
    else:
        (lb,) = extra
        in_specs = [pl.BlockSpec((1, tb, wk), lambda i, h, s: (i, s, h)),
                    pl.BlockSpec((1, tb, wk), lambda i, h, s: (i, s, nh + h)),
                    pl.BlockSpec((1, tb, wv), lambda i, h, s: (i, s, 2 * nh + h)),
                    pl.BlockSpec((1, tb, wv), lambda i, h, s: (i, s, 3 * nh + h)),
                    pl.BlockSpec((lb.shape[0], wk), lambda i, h, s: (0, h)),
                    pl.BlockSpec((1, dv), lambda i, h, s: (0, 0))]
        args = [proj, proj, proj, proj, lb, hn.reshape(1, dv)]
    if has_state:
        in_specs.append(pl.BlockSpec((1, hb, dk, dv), lambda i, h, s: (i, h, 0, 0)))
        args.append(s0)
    body = functools.partial(_rec_body, mode=mode, c=c, sub=sub, n_chunks=tb // c, dk=dk, dv=dv, hb=hb,
                             layer=layer, has_state=has_state)
    return pl.pallas_call(
        body, grid=(b, nh, t // tb), in_specs=in_specs,
        out_specs=[pl.BlockSpec((1, tb, wv), lambda i, h, s: (i, s, h)),
                   pl.BlockSpec((1, hb, dk, dv), lambda i, h, s: (i, h, 0, 0))],
        out_shape=[jax.ShapeDtypeStruct((b, t, heads * dv), BF16),
                   jax.ShapeDtypeStruct((b, heads, dk, dv), F32)],
        scratch_shapes=[pltpu.VMEM((hb, dv, dk), F32)],
        compiler_params=_cp("arbitrary", "arbitrary", "arbitrary"), name="recurrence_" + mode)(*args)


def _rope_body(q_ref, ks_ref, kw_ref, cos_ref, sin_ref, qo_ref, kso_ref, kwo_ref):
    cos = cos_ref[...]
    sin = sin_ref[...]

    def rot(src, dst):
        for h in range(src.shape[1] // LANES):
            x = src[:, h * LANES:(h + 1) * LANES]
            dst[:, h * LANES:(h + 1) * LANES] = x * cos + pltpu.roll(x, LANES // 2, 1) * sin

    rot(q_ref, qo_ref)
    rot(ks_ref, kso_ref)
    rot(kw_ref, kwo_ref)


def nsa_rope(proj, cos, sin, tm, tiles_per_seq, qw, kvw):
    m = proj.shape[0]
    ks_blk = (qw + 2 * kvw) // kvw
    kw_blk = (qw + 4 * kvw) // kvw
    return pl.pallas_call(
        _rope_body, grid=(m // tm,),
        in_specs=[pl.BlockSpec((tm, qw), lambda i: (i, 0)),
                  pl.BlockSpec((tm, kvw), lambda i: (i, ks_blk)),
                  pl.BlockSpec((tm, kvw), lambda i: (i, kw_blk)),
                  pl.BlockSpec((tm, LANES), lambda i: (i % tiles_per_seq, 0)),
                  pl.BlockSpec((tm, LANES), lambda i: (i % tiles_per_seq, 0))],
        out_specs=[pl.BlockSpec((tm, qw), lambda i: (i, 0)),
                   pl.BlockSpec((tm, kvw), lambda i: (i, 0)),
                   pl.BlockSpec((tm, kvw), lambda i: (i, 0))],
        out_shape=[jax.ShapeDtypeStruct((m, qw), F32), jax.ShapeDtypeStruct((m, kvw), F32),
                   jax.ShapeDtypeStruct((m, kvw), F32)],
        compiler_params=_cp("arbitrary"), name="nsa_rope")(proj, proj, proj, cos, sin)


def _pool_rows(x, pw):
    n = x.shape[0] // NSA_BLOCK
    return jnp.sum(x.reshape(n, NSA_BLOCK, x.shape[1]) * pw[None], axis=1)


def _pool_body(kc_ref, vc_ref, pk_ref, pv_ref, ko_ref, vo_ref, *, n_cb):
    ko_ref[...] = jnp.zeros(ko_ref.shape, F32)
    vo_ref[...] = jnp.zeros(vo_ref.shape, F32)
    kp = _pool_rows(kc_ref[0, 0:n_cb * NSA_BLOCK, :], pk_ref[...])
    vp = _pool_rows(vc_ref[0, 0:n_cb * NSA_BLOCK, :], pv_ref[...])
    for g in range(ko_ref.shape[1]):
        ko_ref[0, g, 0:n_cb, :] = kp[:, g * LANES:(g + 1) * LANES]
        vo_ref[0, g, 0:n_cb, :] = vp[:, g * LANES:(g + 1) * LANES]


def nsa_pool_prompt(proj3, pk, pv, qw, kvw, n_pad):
    b, t, _ = proj3.shape
    n_cb = t // NSA_BLOCK
    kc_blk = qw // kvw
    return pl.pallas_call(
        functools.partial(_pool_body, n_cb=n_cb), grid=(b,),
        in_specs=[pl.BlockSpec((1, t, kvw), lambda i: (i, 0, kc_blk)),
                  pl.BlockSpec((1, t, kvw), lambda i: (i, 0, kc_blk + 1)),
                  pl.BlockSpec((NSA_BLOCK, kvw), lambda i: (0, 0)),
                  pl.BlockSpec((NSA_BLOCK, kvw), lambda i: (0, 0))],
        out_specs=[pl.BlockSpec((1, kvw // LANES, n_pad, LANES), lambda i: (i, 0, 0, 0)),
                   pl.BlockSpec((1, kvw // LANES, n_pad, LANES), lambda i: (i, 0, 0, 0))],
        out_shape=[jax.ShapeDtypeStruct((b, kvw // LANES, n_pad, LANES), F32)] * 2,
        compiler_params=_cp("arbitrary"), name="nsa_pool")(proj3, proj3, pk, pv)


def _pool_pages_body(pt_ref, *refs, pg):
    k_refs = refs[:pg]
    v_refs = refs[pg:2 * pg]
    pk_ref, pv_ref, ko_ref, vo_ref = refs[2 * pg:]
    _, page, kv, dh = k_refs[0].shape
    per = page // NSA_BLOCK

    def pooled(x_ref, pw_ref):
        return jnp.sum(x_ref[0].reshape(per, NSA_BLOCK, kv, dh) * pw_ref[...][None], axis=1)

    for i in range(pg):
        kp, vp = pooled(k_refs[i], pk_ref), pooled(v_refs[i], pv_ref)
        for g in range(kv):
            ko_ref[0, g, i * per:(i + 1) * per, :] = kp[:, g, :]
            vo_ref[0, g, i * per:(i + 1) * per, :] = vp[:, g, :]


def nsa_pool_pages(pool_k, pool_v, page_table, pk, pv, pg):
    b, n_pages = page_table.shape
    _, page, kv, dh = pool_k.shape
    per = page // NSA_BLOCK
    page_spec = lambda i: pl.BlockSpec((1, page, kv, dh), lambda bi, s, pt, i=i: (pt[bi, s * pg + i], 0, 0, 0))
    wspec = pl.BlockSpec((NSA_BLOCK, kv, dh), lambda bi, s, pt: (0, 0, 0))
    ospec = pl.BlockSpec((1, kv, pg * per, dh), lambda bi, s, pt: (bi, 0, s, 0))
    gs = pltpu.PrefetchScalarGridSpec(
        num_scalar_prefetch=1, grid=(b, n_pages // pg),
        in_specs=[page_spec(i) for i in range(pg)] + [page_spec(i) for i in range(pg)] + [wspec, wspec],
        out_specs=[ospec, ospec])
    n_blk = n_pages * per
    return pl.pallas_call(
        functools.partial(_pool_pages_body, pg=pg), grid_spec=gs,
        out_shape=[jax.ShapeDtypeStruct((b, kv, n_blk, dh), F32)] * 2,
        compiler_params=_cp("arbitrary", "arbitrary"), name="nsa_pool_pages")(
            page_table, *([pool_k] * pg), *([pool_v] * pg), pk, pv)


def _cmp_body(q_ref, kc_ref, vc_ref, o_ref, sel_ref, *, q_start, n_cb, n_blk, nb_pad):
    tq = q_ref.shape[1]
    kvb, ncp = kc_ref.shape[1], kc_ref.shape[2]
    qt = pl.program_id(2)
    scale = NSA_DH ** -0.5
    colc = lax.broadcasted_iota(jnp.int32, (tq, ncp), 1)
    qposc = q_start + qt * tq + lax.broadcasted_iota(jnp.int32, (tq, ncp), 0)
    valid = ((colc + 1) * NSA_BLOCK - 1 <= qposc) & (colc < n_cb)
    imps = []
    for g in range(kvb):
        kc, vc = kc_ref[0, g], vc_ref[0, g]
        imp = jnp.zeros((tq, ncp), F32)
        for r in range(NSA_GROUP):
            hs = slice((g * NSA_GROUP + r) * LANES, (g * NSA_GROUP + r + 1) * LANES)
            s = jnp.where(valid, _mm_nt(q_ref[0, :, hs], kc) * scale, NEG)
            m = jnp.max(s, axis=-1, keepdims=True)
            e = jnp.where(valid, jnp.exp(s - m), 0.0)
            p = e / jnp.maximum(jnp.sum(e, axis=-1, keepdims=True), 1e-30)
            o_ref[0, :, hs] = _mm(p, vc)
            imp = imp + p
        imps.append(imp)
    imp = jnp.concatenate(imps, axis=0)
    rows = kvb * tq
    if nb_pad > ncp:
        imp = jnp.concatenate([imp, jnp.zeros((rows, nb_pad - ncp), F32)], axis=1)
    blk = lax.broadcasted_iota(jnp.int32, (rows, nb_pad), 1)
    qpos = q_start + qt * tq + lax.broadcasted_iota(jnp.int32, (rows, nb_pad), 0) % tq
    cur = qpos // NSA_BLOCK
    forced = (blk == cur) | (blk == 0)
    score = jnp.where(blk > cur, -1.0, jnp.where(forced, NSA_GROUP + 1.0, imp))
    score = jnp.where(blk < n_blk, score, -2.0)
    blkf = blk.astype(F32)

    def pick(_, carry):
        sc, sel = carry
        mx = jnp.max(sc, axis=-1, keepdims=True)
        first = jnp.min(jnp.where(sc == mx, blkf, 1e9), axis=-1, keepdims=True)
        hit = blkf == first
        return jnp.where(hit, -3.0, sc), jnp.where(hit, 1.0, sel)

    _, sel = lax.fori_loop(0, min(NSA_TOP_N, n_blk), pick, (score, jnp.zeros((rows, nb_pad), F32)))
    for g in range(kvb):
        sel_ref[0, g] = sel[g * tq:(g + 1) * tq, :]


def _cmp_t_body(q_ref, kc_ref, vc_ref, o_ref, sel_ref, *, q_start, n_cb, n_blk, nb_pad):
    tq = q_ref.shape[1]
    ncp = kc_ref.shape[2]
    qt = pl.program_id(2)
    scale = NSA_DH ** -0.5
    nr = min(ncp, -(-n_blk // SUBLANES) * SUBLANES)
    kc = kc_ref[0, 0, 0:nr, :]
    vc = vc_ref[0, 0]
    blk = lax.broadcasted_iota(jnp.int32, (nr, tq), 0)
    qpos = q_start + qt * tq + lax.broadcasted_iota(jnp.int32, (nr, tq), 1)
    valid = ((blk + 1) * NSA_BLOCK - 1 <= qpos) & (blk < n_cb)
    pad = jnp.zeros((ncp - nr, tq), F32)
    imp = jnp.zeros((nr, tq), F32)
    for r in range(NSA_GROUP):
        qr = q_ref[0, :, r * LANES:(r + 1) * LANES]
        s = jnp.where(valid, _mm_nt(kc, qr) * scale, NEG)
        m = jnp.max(s, axis=0, keepdims=True)
        e = jnp.where(valid, jnp.exp(s - m), 0.0)
        p = e / jnp.maximum(jnp.sum(e, axis=0, keepdims=True), 1e-30)
        p_rows = jnp.concatenate([p, pad], axis=0).T if ncp > nr else p.T
        o_ref[0, :, r * LANES:(r + 1) * LANES] = _mm(p_rows, vc)
        imp = imp + p
    cur = qpos // NSA_BLOCK
    forced = (blk == cur) | (blk == 0)
    score = jnp.where(blk > cur, -1.0, jnp.where(forced, NSA_GROUP + 1.0, imp))
    score = jnp.where(blk < n_blk, score, -2.0)
    blkf = blk.astype(F32)

    def pick(_, carry):
        sc, sel = carry
        mx = jnp.max(sc, axis=0, keepdims=True)
        first = jnp.min(jnp.where(sc == mx, blkf, 1e9), axis=0, keepdims=True)
        hit = blkf == first
        return jnp.where(hit, -3.0, sc), jnp.where(hit, 1.0, sel)

    _, sel = lax.fori_loop(0, min(NSA_TOP_N, n_blk), pick, (score, jnp.zeros((nr, tq), F32)))
    if nb_pad > nr:
        sel = jnp.concatenate([sel, jnp.zeros((nb_pad - nr, tq), F32)], axis=0)
    sel_ref[0, 0] = sel.T


def nsa_cmp(q3, kcmp, vcmp, tq, q_start, n_cb, n_blk, nb_pad, kvw):
    b, t = q3.shape[0], q3.shape[1]
    ncp = kcmp.shape[2]
    kv = kvw // LANES
    gw = NSA_GROUP * LANES
    transposed = tq % LANES == 0 and nb_pad % LANES == 0 and ncp % LANES == 0
    body = functools.partial(_cmp_t_body if transposed else _cmp_body,
                             q_start=q_start, n_cb=n_cb, n_blk=n_blk, nb_pad=nb_pad)
    kvb = 1 if transposed else kv
    return pl.pallas_call(
        body, grid=(b, kv // kvb, t // tq),
        in_specs=[pl.BlockSpec((1, tq, kvb * gw), lambda i, g, s: (i, s, g)),
                  pl.BlockSpec((1, kvb, ncp, LANES), lambda i, g, s: (i, g, 0, 0)),
                  pl.BlockSpec((1, kvb, ncp, LANES), lambda i, g, s: (i, g, 0, 0))],
        out_specs=[pl.BlockSpec((1, tq, kvb * gw), lambda i, g, s: (i, s, g)),
                   pl.BlockSpec((1, kvb, tq, nb_pad), lambda i, g, s: (i, g, s, 0))],
        out_shape=[jax.ShapeDtypeStruct((b, t, kv * gw), F32),
                   jax.ShapeDtypeStruct((b, kv, t, nb_pad), F32)],
        compiler_params=_cp("arbitrary", "arbitrary", "arbitrary"), name="nsa_cmp")(q3, kcmp, vcmp)


def _softmax_step(carry, s, valid, vv):
    m, l, acc = carry
    s = jnp.where(valid, s, NEG)
    m_new = jnp.maximum(m, jnp.max(s, axis=-1, keepdims=True))
    p = jnp.where(valid, jnp.exp(s - m_new), 0.0)
    alpha = jnp.exp(m - m_new)
    l = alpha * l + jnp.sum(p, axis=-1, keepdims=True)
    acc = alpha * acc + _mm(p, vv)
    return m_new, l, acc


def _softmax_init(rows):
    return (jnp.full((rows, 1), NEG, F32), jnp.zeros((rows, 1), F32), jnp.zeros((rows, LANES), F32))


def _attn_body(*refs, tq, tk, n_kt, q_start, k_start, do_sel, do_win):
    refs = list(refs)
    q_ref = refs.pop(0)
    if do_sel:
        ks_ref, vs_ref, sel_ref = refs[:3]
        refs = refs[3:]
    if do_win:
        kw_ref, vw_ref = refs[:2]
        refs = refs[2:]
    outs = refs
    qt = pl.program_id(2)
    scale = NSA_DH ** -0.5
    rows = NSA_GROUP * tq
    q4 = jnp.concatenate([q_ref[0, :, r * LANES:(r + 1) * LANES] for r in range(NSA_GROUP)], axis=0).astype(BF16)
    q0 = q_start + qt * tq
    qpos = q0 + lax.broadcasted_iota(jnp.int32, (rows, tk), 0) % tq
    kcol = lax.broadcasted_iota(jnp.int32, (rows, tk), 1)

    def finish(carry, o_ref):
        m, l, acc = carry
        o = acc / jnp.maximum(l, 1e-30)
        for r in range(NSA_GROUP):
            o_ref[0, :, r * LANES:(r + 1) * LANES] = o[r * tq:(r + 1) * tq, :]

    if do_sel:
        selb = sel_ref[0, 0].astype(BF16)
        nbp = selb.shape[1]
        en = lax.broadcasted_iota(jnp.int32, (nbp, tk), 0)
        es = lax.broadcasted_iota(jnp.int32, (nbp, tk), 1)

        def sel_step(kt, carry):
            off = pl.multiple_of(kt * tk, tk)
            kk = ks_ref[0, pl.ds(off, tk), :]
            vv = vs_ref[0, pl.ds(off, tk), :]
            s = _mm_nt(q4, kk) * scale
            kp0 = k_start + kt * tk
            expand = ((kp0 + es) // NSA_BLOCK == en).astype(BF16)
            chosen = jnp.dot(selb, expand, preferred_element_type=F32)
            chosen = jnp.concatenate([chosen] * NSA_GROUP, axis=0)
            valid = (chosen > 0.5) & (kp0 + kcol <= qpos)
            return _softmax_step(carry, s, valid, vv)

        hi = jnp.minimum(n_kt, (q0 + tq - 1 - k_start) // tk + 1)
        finish(lax.fori_loop(0, hi, sel_step, _softmax_init(rows)), outs.pop(0))

    if do_win:
        def win_step(kt, carry):
            off = pl.multiple_of(kt * tk, tk)
            kk = kw_ref[0, pl.ds(off, tk), :]
            vv = vw_ref[0, pl.ds(off, tk), :]
            s = _mm_nt(q4, kk) * scale
            kpos = k_start + kt * tk + kcol
            dist = qpos - kpos
            valid = (dist >= 0) & (dist < NSA_WINDOW) & (kpos >= 0)
            return _softmax_step(carry, s, valid, vv)

        lo = jnp.maximum(0, (q0 - (NSA_WINDOW - 1) - k_start) // tk)
        hi = jnp.minimum(n_kt, (q0 + tq - 1 - k_start) // tk + 1)
        finish(lax.fori_loop(lo, hi, win_step, _softmax_init(rows)), outs.pop(0))


def nsa_attend(q_rot3, tq, tk, q_start, k_start, sel_args=None, win_args=None):
    b, t, qw = q_rot3.shape
    kv = qw // (NSA_GROUP * LANES)
    gw = NSA_GROUP * LANES
    in_specs = [pl.BlockSpec((1, tq, gw), lambda i, g, s: (i, s, g))]
    args = [q_rot3]
    n_out = 0
    t_k = None
    if sel_args is not None:
        k, v, sel = sel_args
        t_k = k.shape[1]
        nbp = sel.shape[-1]
        in_specs += [pl.BlockSpec((1, t_k, LANES), lambda i, g, s: (i, 0, g)),
                     pl.BlockSpec((1, t_k, LANES), lambda i, g, s: (i, 0, g)),
                     pl.BlockSpec((1, 1, tq, nbp), lambda i, g, s: (i, g, s, 0))]
        args += [k, v, sel]
        n_out += 1
    if win_args is not None:
        k, v = win_args
        t_k = k.shape[1]
        in_specs += [pl.BlockSpec((1, t_k, LANES), lambda i, g, s: (i, 0, g)),
                     pl.BlockSpec((1, t_k, LANES), lambda i, g, s: (i, 0, g))]
        args += [k, v]
        n_out += 1
    body = functools.partial(_attn_body, tq=tq, tk=tk, n_kt=t_k // tk, q_start=q_start, k_start=k_start,
                             do_sel=sel_args is not None, do_win=win_args is not None)
    return pl.pallas_call(
        body, grid=(b, kv, t // tq), in_specs=in_specs,
        out_specs=[pl.BlockSpec((1, tq, gw), lambda i, g, s: (i, s, g))] * n_out,
        out_shape=[jax.ShapeDtypeStruct((b, t, qw), F32)] * n_out,
        compiler_params=_cp("arbitrary", "arbitrary", "arbitrary"), name="nsa_attend")(*args)


def _attn_t_body(q_ref, ks_ref, vs_ref, kw_ref, vw_ref, sel_ref, osel_ref, owin_ref, vst_ref, vwt_ref, *, tq, tk, n_kt):
    qt = pl.program_id(2)
    scale = NSA_DH ** -0.5
    t_k = ks_ref.shape[1]

    @pl.when(qt == 0)
    def _():
        for j in range(t_k // LANES):
            sl = slice(j * LANES, (j + 1) * LANES)
            vst_ref[:, sl] = vs_ref[0, sl, :].T.astype(BF16)
            vwt_ref[:, sl] = vw_ref[0, sl, :].T.astype(BF16)

    q0 = qt * tq
    cols = NSA_GROUP * tq
    sel_t = sel_ref[0, 0].T.astype(BF16)
    nbp = sel_t.shape[0]
    q4 = jnp.concatenate([q_ref[0, :, r * LANES:(r + 1) * LANES] for r in range(NSA_GROUP)], axis=0).astype(BF16)

    def scores(k_ref, off, n, allowed):
        bias = jnp.where(allowed, 0.0, NEG)
        return _mm_nt(k_ref[0, pl.ds(off, n), :], q4) * scale + jnp.concatenate([bias] * NSA_GROUP, axis=1)

    def write(o_ref, acc, l):
        o = acc / jnp.maximum(l, 1e-30)
        for r in range(NSA_GROUP):
            o_ref[0, :, r * LANES:(r + 1) * LANES] = o[:, r * tq:(r + 1) * tq].T

    kpos_l = lax.broadcasted_iota(jnp.int32, (tk, tq), 0)
    qpos = q0 + lax.broadcasted_iota(jnp.int32, (tk, tq), 1)
    en = lax.broadcasted_iota(jnp.int32, (tk, nbp), 1)
    es = lax.broadcasted_iota(jnp.int32, (tk, nbp), 0)

    def sel_step(kt, carry):
        m, l, acc = carry
        off = pl.multiple_of(kt * tk, tk)
        expand = ((kt * tk + es) // NSA_BLOCK == en).astype(BF16)
        chosen = jnp.dot(expand, sel_t, preferred_element_type=F32)
        s = scores(ks_ref, off, tk, (chosen > 0.5) & (kt * tk + kpos_l <= qpos))
        m_new = jnp.maximum(m, jnp.max(s, axis=0, keepdims=True))
        p = jnp.exp(s - m_new)
        alpha = jnp.exp(m - m_new)
        l = alpha * l + jnp.sum(p, axis=0, keepdims=True)
        acc = alpha * acc + jnp.dot(vst_ref[:, pl.ds(off, tk)], p.astype(BF16), preferred_element_type=F32)
        return m_new, l, acc

    init = (jnp.full((1, cols), NEG, F32), jnp.zeros((1, cols), F32), jnp.zeros((NSA_DH, cols), F32))
    m, l, acc = lax.fori_loop(0, jnp.minimum(n_kt, (q0 + tq - 1) // tk + 1), sel_step, init)
    write(osel_ref, acc, l)

    wk = min(t_k, NSA_WINDOW + tq)
    ws = pl.multiple_of(jnp.clip(q0 - NSA_WINDOW, 0, t_k - wk), LANES)
    dist = (q0 + lax.broadcasted_iota(jnp.int32, (wk, tq), 1)) - (ws + lax.broadcasted_iota(jnp.int32, (wk, tq), 0))
    s = scores(kw_ref, ws, wk, (dist >= 0) & (dist < NSA_WINDOW))
    p = jnp.exp(s - jnp.max(s, axis=0, keepdims=True))
    acc = jnp.dot(vwt_ref[:, pl.ds(ws, wk)], p.astype(BF16), preferred_element_type=F32)
    write(owin_ref, acc, jnp.sum(p, axis=0, keepdims=True))


def nsa_attend_prompt(q_rot3, ks, vs, kw, vw, sel, tq, tk):
    b, t, qw = q_rot3.shape
    kv = qw // (NSA_GROUP * LANES)
    gw = NSA_GROUP * LANES
    kspec = pl.BlockSpec((1, t, LANES), lambda i, g, s: (i, 0, g))
    ospec = pl.BlockSpec((1, tq, gw), lambda i, g, s: (i, s, g))
    body = functools.partial(_attn_t_body, tq=tq, tk=tk, n_kt=t // tk)
    return pl.pallas_call(
        body, grid=(b, kv, t // tq),
        in_specs=[ospec, kspec, kspec, kspec, kspec,
                  pl.BlockSpec((1, 1, tq, sel.shape[-1]), lambda i, g, s: (i, g, s, 0))],
        out_specs=[ospec, ospec],
        out_shape=[jax.ShapeDtypeStruct((b, t, qw), F32)] * 2,
        scratch_shapes=[pltpu.VMEM((NSA_DH, t), BF16), pltpu.VMEM((NSA_DH, t), BF16)],
        compiler_params=_cp("arbitrary", "arbitrary", "arbitrary"), name="nsa_attend_prompt")(
            q_rot3, ks, vs, kw, vw, sel)


def _paged_sel_body(pt_ref, q_ref, sel_ref, kn_ref, vn_ref, k_hbm, v_hbm, o_ref,
                    kbuf, vbuf, sem, m_ref, l_ref, acc_ref, *, pg, past_len, t_new):
    bi, step = pl.program_id(0), pl.program_id(1)
    nsteps = pl.num_programs(1)
    total = pl.num_programs(0) * nsteps
    lin = bi * nsteps + step
    _, page, kv, _ = k_hbm.shape
    rows = q_ref.shape[1]
    per = rows // kv
    scale = NSA_DH ** -0.5
    nbp = sel_ref.shape[-1]
    qb = q_ref[0].astype(BF16)
    selb = sel_ref[0].astype(BF16)

    def page_copies(b_, s_, slot):
        out = []
        for i in range(pg):
            phys = pt_ref[b_, s_ * pg + i]
            for g in range(kv):
                dst = pl.ds(i * page, page)
                out.append(pltpu.make_async_copy(k_hbm.at[phys, :, g, :], kbuf.at[slot, g, dst, :], sem.at[slot, 0]))
                out.append(pltpu.make_async_copy(v_hbm.at[phys, :, g, :], vbuf.at[slot, g, dst, :], sem.at[slot, 1]))
        return out

    @pl.when(lin == 0)
    def _():
        for c in page_copies(0, 0, 0):
            c.start()

    @pl.when(lin + 1 < total)
    def _():
        nxt = lin + 1
        for c in page_copies(nxt // nsteps, nxt % nsteps, nxt % 2):
            c.start()

    @pl.when(step == 0)
    def _():
        m_ref[...] = jnp.full(m_ref.shape, NEG, F32)
        l_ref[...] = jnp.zeros(l_ref.shape, F32)
        acc_ref[...] = jnp.zeros(acc_ref.shape, F32)

    def update(key_of, val_of, n, kp0):
        tloc = lax.broadcasted_iota(jnp.int32, (rows, n), 0) % t_new
        kcol = lax.broadcasted_iota(jnp.int32, (rows, n), 1)
        en = lax.broadcasted_iota(jnp.int32, (nbp, n), 0)
        es = lax.broadcasted_iota(jnp.int32, (nbp, n), 1)
        s = jnp.concatenate([_mm_nt(qb[g * per:(g + 1) * per], key_of(g)) for g in range(kv)], axis=0) * scale
        expand = ((kp0 + es) // NSA_BLOCK == en).astype(BF16)
        chosen = jnp.dot(selb, expand, preferred_element_type=F32)
        valid = (chosen > 0.5) & (kp0 + kcol <= past_len + tloc)
        s = jnp.where(valid, s, NEG)
        m = m_ref[:, 0:1]
        m_new = jnp.maximum(m, jnp.max(s, axis=-1, keepdims=True))
        p = jnp.where(valid, jnp.exp(s - m_new), 0.0)
        alpha = jnp.exp(m - m_new)
        pv = jnp.concatenate([_mm(p[g * per:(g + 1) * per], val_of(g)) for g in range(kv)], axis=0)
        m_ref[...] = jnp.broadcast_to(m_new, m_ref.shape)
        l_ref[...] = jnp.broadcast_to(alpha * l_ref[:, 0:1] + jnp.sum(p, axis=-1, keepdims=True), l_ref.shape)
        acc_ref[...] = alpha * acc_ref[...] + pv

    slot = lin % 2
    for c in page_copies(bi, step, slot):
        c.wait()
    update(lambda g: kbuf[slot, g], lambda g: vbuf[slot, g], pg * page, step * (pg * page))

    @pl.when(step == nsteps - 1)
    def _():
        update(lambda g: kn_ref[0, :, g * LANES:(g + 1) * LANES],
               lambda g: vn_ref[0, :, g * LANES:(g + 1) * LANES], kn_ref.shape[1], past_len)
        o_ref[0] = acc_ref[...] / jnp.maximum(l_ref[:, 0:1], 1e-30)


def nsa_paged_sel(q_rot3, sel, pool_k, pool_v, page_table, k_new, v_new, pg):
    b, t_new, qw = q_rot3.shape
    n_pages = page_table.shape[1]
    _, page, kv, dh = pool_k.shape
    kvw = kv * dh
    nbp = sel.shape[-1]
    per = NSA_GROUP * t_new
    rows = kv * per
    q_rows = q_rot3.reshape(b, t_new, kv, NSA_GROUP, dh).transpose(0, 2, 3, 1, 4).reshape(b, rows, dh)
    sel_rows = jnp.broadcast_to(sel[:, :, None], (b, kv, NSA_GROUP, t_new, nbp)).reshape(b, rows, nbp)
    gs = pltpu.PrefetchScalarGridSpec(
        num_scalar_prefetch=1, grid=(b, n_pages // pg),
        in_specs=[pl.BlockSpec((1, rows, dh), lambda bi, s, pt: (bi, 0, 0)),
                  pl.BlockSpec((1, rows, nbp), lambda bi, s, pt: (bi, 0, 0)),
                  pl.BlockSpec((1, page, kvw), lambda bi, s, pt: (bi, 0, 0)),
                  pl.BlockSpec((1, page, kvw), lambda bi, s, pt: (bi, 0, 0)),
                  pl.BlockSpec(memory_space=pl.ANY), pl.BlockSpec(memory_space=pl.ANY)],
        out_specs=pl.BlockSpec((1, rows, dh), lambda bi, s, pt: (bi, 0, 0)),
        scratch_shapes=[pltpu.VMEM((2, kv, pg * page, dh), F32), pltpu.VMEM((2, kv, pg * page, dh), F32),
                        pltpu.SemaphoreType.DMA((2, 2)),
                        pltpu.VMEM((rows, LANES), F32), pltpu.VMEM((rows, LANES), F32),
                        pltpu.VMEM((rows, dh), F32)])
    body = functools.partial(_paged_sel_body, pg=pg, past_len=n_pages * page, t_new=t_new)
    o_rows = pl.pallas_call(
        body, grid_spec=gs, out_shape=jax.ShapeDtypeStruct((b, rows, dh), F32),
        compiler_params=_cp("arbitrary", "arbitrary"), name="nsa_paged_sel")(
            page_table, q_rows, sel_rows, k_new, v_new, pool_k, pool_v)
    return o_rows.reshape(b, kv, NSA_GROUP, t_new, NSA_DH).transpose(0, 3, 1, 2, 4).reshape(b, t_new, qw)


def _combine_body(oc_ref, os_ref, ow_ref, gt_ref, a_ref):
    gs = _sigmoid(gt_ref[...])
    for hh in range(oc_ref.shape[1] // LANES):
        sl = slice(hh * LANES, (hh + 1) * LANES)
        a = (gs[:, 3 * hh:3 * hh + 1] * oc_ref[:, sl] + gs[:, 3 * hh + 1:3 * hh + 2] * os_ref[:, sl]
             + gs[:, 3 * hh + 2:3 * hh + 3] * ow_ref[:, sl])
        a_ref[:, sl] = a.astype(BF16)


def nsa_combine(o_cmp, o_sel, o_win, gates, tm):
    m, qw = o_cmp.shape
    spec = pl.BlockSpec((tm, qw), lambda i: (i, 0))
    return pl.pallas_call(
        _combine_body, grid=(m // tm,),
        in_specs=[spec, spec, spec, pl.BlockSpec((tm, LANES), lambda i: (i, 0))],
        out_specs=spec, out_shape=jax.ShapeDtypeStruct((m, qw), BF16),
        compiler_params=_cp("arbitrary"), name="nsa_combine")(o_cmp, o_sel, o_win, gates)


def _s5_body(*refs, seg, seq_len, has_state):
    if has_state:
        (u_ref, a1_ref, a2_ref, dt_ref, b1_ref, b2_ref, cm_ref, d_ref, s0_ref,
         z_ref, st_ref, x_ref, y_ref, up_ref) = refs
    else:
        (u_ref, a1_ref, a2_ref, dt_ref, b1_ref, b2_ref, cm_ref, d_ref,
         z_ref, st_ref, x_ref, y_ref, up_ref) = refs
    gq = pl.program_id(1)
    m = u_ref.shape[0]
    nseg = m // seg
    nb = m // seq_len
    half = LANES // 2
    lane = lax.broadcasted_iota(jnp.int32, (1, LANES), 1)
    sgn = jnp.where(lane < half, -1.0, 1.0)

    gs = a1_ref.shape[0]
    abar, bcats = [], []
    for gi in range(gs):
        are, aim, dt = a1_ref[gi], a2_ref[gi], jnp.exp(dt_ref[gi])
        er = jnp.exp(are * dt)
        abr, abi = er * jnp.cos(aim * dt), er * jnp.sin(aim * dt)
        nr, ni, den = abr - 1.0, abi, are * are + aim * aim
        cr, cim = (nr * are + ni * aim) / den, (ni * are - nr * aim) / den
        abar.append((abr, abi))
        bcats.append((cr * b1_ref[gi] + cim * b2_ref[gi]).astype(BF16))

    def cmul(x, pr, pi):
        return x * pr + pltpu.roll(x, half, 1) * (pi * sgn)

    pb = SUBLANES * seg
    nblk = m // pb
    ri = lax.broadcasted_iota(jnp.int32, (pb, pb), 0)
    ci = lax.broadcasted_iota(jnp.int32, (pb, pb), 1)

    @pl.when(gq == 0)
    def _():
        perm = (ci == (ri % SUBLANES) * seg + ri // SUBLANES).astype(BF16)
        for k in range(nblk):
            uk = u_ref[k * pb:(k + 1) * pb, :].astype(BF16)
            up_ref[k] = jnp.dot(perm, uk, preferred_element_type=F32).astype(BF16)

    x_ref[...] = jnp.dot(up_ref[...].reshape(m, LANES), jnp.concatenate(bcats, axis=1),
                         preferred_element_type=F32).reshape(nblk, pb, gs * LANES)

    def scan_group(gi):
        abr, abi = abar[gi]
        gl = slice(gi * LANES, (gi + 1) * LANES)
        x = jnp.zeros((nseg, LANES), F32)
        for s in range(seg):
            sl = slice(s * SUBLANES, (s + 1) * SUBLANES)
            x = cmul(x, abr, abi) + x_ref[:, sl, gl].reshape(nseg, LANES)
            x_ref[:, sl, gl] = x.reshape(nblk, SUBLANES, LANES)
        if has_state:
            carry = s0_ref[gi]
        else:
            spb = seq_len // seg
            pr, pi = abr, abi
            for _ in range(int(math.log2(seg))):
                pr, pi = pr * pr - pi * pi, 2.0 * pr * pi
            rown = lax.broadcasted_iota(jnp.int32, (nseg, LANES), 0) % spb
            inc = x
            sh = 1
            while sh < spb:
                inc = inc + jnp.where(rown >= sh, cmul(pltpu.roll(inc, sh, 0), pr, pi), 0.0)
                pr, pi = pr * pr - pi * pi, 2.0 * pr * pi
                sh *= 2
            carry = jnp.where(rown >= 1, pltpu.roll(inc, 1, 0), 0.0)
        pr, pi = abr, abi
        for s in range(seg):
            sl = slice(s * SUBLANES, (s + 1) * SUBLANES)
            x_ref[:, sl, gl] = x_ref[:, sl, gl] + cmul(carry, pr, pi).reshape(nblk, SUBLANES, LANES)
            pr, pi = pr * abr - pi * abi, pr * abi + pi * abr
        finals = []
        for bi in range(nb):
            last_seg = (bi + 1) * (seq_len // seg) - 1
            row = (seg - 1) * SUBLANES + last_seg % SUBLANES
            finals.append(x_ref[last_seg // SUBLANES, row:row + 1, gl])
        st_ref[gi] = jnp.concatenate(finals, axis=0)

    for gi in range(gs):
        scan_group(gi)

    yg = _mm(x_ref[...].reshape(m, gs * LANES), cm_ref[...].reshape(gs * LANES, LANES))

    @pl.when(gq == 0)
    def _():
        y_ref[...] = yg

    @pl.when(gq > 0)
    def _():
        y_ref[...] = y_ref[...] + yg

    @pl.when(gq == pl.num_programs(1) - 1)
    def _():
        unperm = (ri == (ci % SUBLANES) * seg + ci // SUBLANES).astype(BF16)
        for k in range(nblk):
            rows = slice(k * pb, (k + 1) * pb)
            yk = y_ref[rows, :]
            hi = yk.astype(BF16)
            lo = (yk - hi.astype(F32)).astype(BF16)
            y = (jnp.dot(unperm, hi, preferred_element_type=F32) + jnp.dot(unperm, lo, preferred_element_type=F32)
                 + d_ref[...] * u_ref[rows, :])
            z = 0.5 * y * (1.0 + jnp.tanh(math.sqrt(2.0 / math.pi) * (y + 0.044715 * (y * y * y))))
            z_ref[rows, :] = z.astype(BF16)


def s5_scan(u, prm, seq_len, seg, s0=None):
    m, d = u.shape
    a1, a2, dtb, b1, b2, cm, dsk = prm
    groups = a1.shape[0]
    per_tile = LANES // S5_CH
    nb = m // seq_len
    has_state = s0 is not None
    gs = S5_GROUPS_PER_STEP
    steps = per_tile // gs
    gidx = lambda j, q: (j * steps + q, 0, 0)
    vspec = pl.BlockSpec((gs, 1, LANES), gidx)
    mspec = pl.BlockSpec((gs, LANES, LANES), gidx)
    in_specs = [pl.BlockSpec((m, LANES), lambda j, q: (0, j)), vspec, vspec, vspec, mspec, mspec, mspec,
                pl.BlockSpec((1, LANES), lambda j, q: (0, j))]
    args = [u, a1, a2, dtb, b1, b2, cm, dsk]
    if has_state:
        in_specs.append(pl.BlockSpec((gs, nb, LANES), gidx))
        args.append(s0)
    body = functools.partial(_s5_body, seg=seg, seq_len=seq_len, has_state=has_state)
    return pl.pallas_call(
        body, grid=(d // LANES, steps), in_specs=in_specs,
        out_specs=[pl.BlockSpec((m, LANES), lambda j, q: (0, j)),
                   pl.BlockSpec((gs, nb, LANES), gidx)],
        out_shape=[jax.ShapeDtypeStruct((m, d), BF16), jax.ShapeDtypeStruct((groups, nb, LANES), F32)],
        scratch_shapes=[pltpu.VMEM((m // (SUBLANES * seg), SUBLANES * seg, gs * LANES), F32),
                        pltpu.VMEM((m, LANES), F32),
                        pltpu.VMEM((m // (SUBLANES * seg), SUBLANES * seg, LANES), BF16)],
        compiler_params=_cp("arbitrary", "arbitrary"), name="s5_scan")(*args)


def _s5_params(a_re, a_im, log_dt, b_re, b_im, c_re, c_im, d_skip):
    groups, p = a_re.shape
    per_tile = LANES // S5_CH
    dup = lambda a: jnp.concatenate([a, a], axis=-1)[:, None, :]
    a1, a2 = dup(a_re), dup(a_im)
    dtb = jnp.broadcast_to(log_dt[:, None, None], (groups, 1, LANES))
    slot = jax.nn.one_hot(jnp.arange(groups) % per_tile, per_tile, dtype=F32)

    def rows_in_tile(w):
        return (slot[:, :, None, None] * w[:, None]).reshape(groups, LANES, w.shape[-1])

    bre_t, bim_t = b_re.transpose(0, 2, 1), b_im.transpose(0, 2, 1)
    b1 = rows_in_tile(jnp.concatenate([bre_t, bim_t], axis=-1))
    b2 = rows_in_tile(jnp.concatenate([-bim_t, bre_t], axis=-1))
    cmat = jnp.concatenate([c_re, -c_im], axis=-1)
    cm = rows_in_tile(cmat).transpose(0, 2, 1)
    return a1, a2, dtb, b1, b2, cm, d_skip.reshape(1, -1)


def _tiles(m):
    if m >= 1024:
        return 1024, 1024
    return m, m


def _gla_layer(h, b, t, gain, w_in, w_tail, w_alpha_pad, b_alpha, head_norm, w_out, s0, heads, dk, dv):
    tm, tmo = _tiles(h.shape[0])
    n_main = 2 * heads * dk + 2 * heads * dv
    proj = norm_proj(h, gain, w_in, n_main, tm, 512)
    lr = norm_proj(h, gain, w_tail, LANES, tm, LANES)
    c = math.gcd(t, CHUNK)
    tb = math.gcd(t, 128)
    og, st = recurrence("gla", proj.reshape(b, t, n_main), heads, dk, dv, head_norm, tb, c,
                        (lr.reshape(b, t, LANES), w_alpha_pad, b_alpha.reshape(1, -1)), s0, hb=4)
    return out_proj(og.reshape(b * t, heads * dv), w_out, h, tmo, 512), st


def _hgrn_layer(h, b, t, gain, w_in, lower_bound, layer, head_norm, w_out, s0, heads, dk):
    tm, tmo = _tiles(h.shape[0])
    n = 4 * heads * dk
    proj = norm_proj(h, gain, w_in, n, tm, 512)
    c = math.gcd(t, CHUNK)
    tb = math.gcd(t, 128)
    og, st = recurrence("hgrn", proj.reshape(b, t, n), heads, dk, dk, head_norm, tb, c,
                        (lower_bound,), s0, layer=layer, hb=8)
    return out_proj(og.reshape(b * t, heads * dk), w_out, h, tmo, 512), st


def _rope_tables(start, t):
    half = NSA_DH // 2
    inv = ROPE_THETA ** (-jnp.arange(half, dtype=F32) / half)
    ang = (start + jnp.arange(t, dtype=jnp.int32)).astype(F32)[:, None] * inv[None, :]
    cos, sin = jnp.cos(ang), jnp.sin(ang)
    return jnp.concatenate([cos, cos], axis=-1), jnp.concatenate([-sin, sin], axis=-1)


def _nsa_layer(h, b, t, start, gain, w_in, w_gates, pool_k, pool_v, w_out, past, heads, kv):
    tm, tmo = _tiles(h.shape[0])
    qw, kvw = heads * NSA_DH, kv * NSA_DH
    n_main = qw + 6 * kvw
    proj = norm_proj(h, gain, w_in, n_main, tm, 512)
    gates = norm_proj(h, gain, w_gates, LANES, tm, LANES)
    cos, sin = _rope_tables(start, t)
    pk = jnp.broadcast_to(pool_k[:, None], (NSA_BLOCK, kvw))
    pv = jnp.broadcast_to(pool_v[:, None], (NSA_BLOCK, kvw))
    proj3 = proj.reshape(b, t, n_main)
    col = lambda i: proj3[:, :, qw + i * kvw:qw + (i + 1) * kvw]
    kc, vc, vs, vw = col(0), col(1), col(3), col(5)

    if past is None:
        trope = min(256, t)
        q_rot, ks, kw = nsa_rope(proj, cos, sin, trope, t // trope, qw, kvw)
        q_rot3, ks3, kw3 = q_rot.reshape(b, t, qw), ks.reshape(b, t, kvw), kw.reshape(b, t, kvw)
        n_cb = t // NSA_BLOCK
        n_blk = -(-t // NSA_BLOCK)
        kcmp, vcmp = nsa_pool_prompt(proj3, pk, pv, qw, kvw, LANES)
        o_cmp, sel = nsa_cmp(proj3, kcmp, vcmp, min(t, 512), 0, n_cb, n_blk, LANES, kvw)
        tq = min(t, 128)
        o_sel, o_win = nsa_attend_prompt(q_rot3, ks3, vs, kw3, vw, sel, tq, min(t, 512))
        keep = min(NSA_WINDOW, t)
        win_k, win_v = kw3[:, t - keep:], vw[:, t - keep:]
    else:
        pool_ck, pool_cv, pool_sk, pool_sv, page_table, prev_kw, prev_vw = past
        n_pages = page_table.shape[1]
        page = pool_ck.shape[1]
        past_len = n_pages * page
        cos_r, sin_r = jnp.tile(cos, (b, 1)), jnp.tile(sin, (b, 1))
        q_rot, ks, kw = nsa_rope(proj, cos_r, sin_r, b * t, 1, qw, kvw)
        q_rot3, ks3, kw3 = q_rot.reshape(b, t, qw), ks.reshape(b, t, kvw), kw.reshape(b, t, kvw)
        pk4 = jnp.broadcast_to(pool_k[:, None, None], (NSA_BLOCK, kv, NSA_DH))
        pv4 = jnp.broadcast_to(pool_v[:, None, None], (NSA_BLOCK, kv, NSA_DH))
        kcmp, vcmp = nsa_pool_pages(pool_ck, pool_cv, page_table, pk4, pv4, 8)
        total = past_len + t
        n_cb = total // NSA_BLOCK
        n_blk = -(-total // NSA_BLOCK)
        nb_pad = -(-n_blk // LANES) * LANES
        o_cmp, sel = nsa_cmp(proj3, kcmp, vcmp, t, past_len, n_cb, n_blk, nb_pad, kvw)
        padp = lambda a: jnp.concatenate([a, jnp.zeros((b, page - t, kvw), F32)], axis=1)
        o_sel = nsa_paged_sel(q_rot3, sel, pool_sk, pool_sv, page_table, padp(ks3), padp(vs), 8)
        keep = prev_kw.shape[1]
        kw_ext = jnp.concatenate([prev_kw.reshape(b, keep, kvw), kw3], axis=1)
        vw_ext = jnp.concatenate([prev_vw.reshape(b, keep, kvw), vw], axis=1)
        t_ext = keep + t
        t_pad = -(-t_ext // LANES) * LANES
        pade = lambda a: jnp.concatenate([a, jnp.zeros((b, t_pad - t_ext, kvw), F32)], axis=1)
        (o_win,) = nsa_attend(q_rot3, t, LANES, past_len, past_len - keep, win_args=(pade(kw_ext), pade(vw_ext)))
        win_k, win_v = kw_ext[:, t_ext - keep:], vw_ext[:, t_ext - keep:]

    a = nsa_combine(o_cmp.reshape(b * t, qw), o_sel.reshape(b * t, qw), o_win.reshape(b * t, qw), gates,
                    min(b * t, 512))
    y = out_proj(a, w_out, h, tmo, 512)
    shp = lambda x, n: x.reshape(b, n, kv, NSA_DH)
    return y, (shp(kc, t), shp(vc, t), shp(ks3, t), shp(vs, t), shp(win_k, keep), shp(win_v, keep))


def _s5_layer(h, b, t, gain, prm, w_glu, s_re, s_im):
    tm, tmo = _tiles(h.shape[0])
    u = rmsnorm_rows(h, gain, min(h.shape[0], 512))
    groups = prm[0].shape[0]
    if s_re is None:
        z, st = s5_scan(u, prm, t, math.gcd(t, 32))
    else:
        s0 = jnp.concatenate([s_re, s_im], axis=-1).transpose(1, 0, 2)
        z, st = s5_scan(u, prm, t, t, s0)
    y = out_glu(z, w_glu, h, tmo, 512)
    st = st.transpose(1, 0, 2)
    return y, (st[..., :S5_STATE], st[..., S5_STATE:])


def _ffn_layer(h, b, t, gain, w_in, conv_w, conv_b, w_out, buf):
    tm, tmo = _tiles(h.shape[0])
    ff2 = w_in.shape[1]
    if buf is None:
        tm = min(1024, t)
        act, tg, tv = ffn_in(h, gain, w_in, conv_w, conv_b, tm, 512, t)
        per = t // tm
        last = lambda a: a[per - 1::per, SUBLANES - (CONV_W - 1):, :]
        state = jnp.concatenate([last(tg), last(tv)], axis=-1)
    else:
        zrow = jnp.zeros((b, t - 1, ff2), F32)
        p1 = jnp.concatenate([buf[:, 1:2], zrow], axis=1).reshape(b * t, ff2)
        p2 = jnp.concatenate([buf, zrow[:, 1:]], axis=1).reshape(b * t, ff2)
        act, tg, tv = ffn_in(h, gain, w_in, conv_w, conv_b, b * t, 512, t, hist=(p1, p2))
        up = jnp.concatenate([tg[0], tv[0]], axis=-1).reshape(b, t, ff2)
        state = jnp.concatenate([buf, up], axis=1)[:, t:]
    return out_proj(act, w_out, h, min(h.shape[0], 1024), 256), state


def kernel(x_prompt, x_sample, state_gla, state_hgrn, cache_nsa_cmp_k, cache_nsa_cmp_v, cache_nsa_sel_k, cache_nsa_sel_v, cache_nsa_win_k, cache_nsa_win_v, state_s5_re, state_s5_im, state_ffn_conv, page_table, norm_mix, norm_ffn, final_norm, gla_w_in, gla_w_alpha, gla_b_alpha, gla_head_norm, gla_w_out, hgrn_w_in, hgrn_lower_bound, hgrn_head_norm, hgrn_w_out, nsa_w_in, nsa_pool_k, nsa_pool_v, nsa_w_out, s5_a_re, s5_a_im, s5_log_dt, s5_b_re, s5_b_im, s5_c_re, s5_c_im, s5_d, s5_w_glu, ffn_w_in, ffn_conv_w, ffn_conv_b, ffn_w_out):
    bp, tp, d = x_prompt.shape
    bs, ts, _ = x_sample.shape
    depth = norm_mix.shape[0]
    n_mixers = 4
    gla_heads, gla_dk, gla_dv = state_gla.shape[2], state_gla.shape[3], state_gla.shape[4]
    hgrn_heads, hgrn_dk = state_hgrn.shape[2], state_hgrn.shape[3]
    nsa_kv = cache_nsa_cmp_k.shape[3]
    nsa_heads = d // NSA_DH
    hp = x_prompt.reshape(bp * tp, d)
    hs = x_sample.reshape(bs * ts, d)
    bf = lambda w: w.astype(BF16)

    def pad_cols(w, n):
        return jnp.concatenate([w, jnp.zeros((w.shape[0], n - w.shape[1]), w.dtype)], axis=1)

    outs = {k: [] for k in ("gla_p", "gla_s", "hgrn_p", "hgrn_s", "nsa_p", "nsa_s", "s5_p", "s5_s", "conv_p", "conv_s")}
    for i in range(depth):
        kind, j = i % n_mixers, i // n_mixers
        if kind == 0:
            n_main = 2 * gla_heads * gla_dk + 2 * gla_heads * gla_dv
            w_in = cast_bf16(gla_w_in, j)
            w_tail = bf(pad_cols(gla_w_in[j, :, n_main:], LANES))
            rank = gla_w_alpha.shape[1]
            wa = bf(jnp.concatenate([gla_w_alpha[j], jnp.zeros((LANES - rank, gla_w_alpha.shape[2]), F32)], axis=0))
            common = (norm_mix[i], w_in, w_tail, wa, gla_b_alpha[j], gla_head_norm[j], cast_bf16(gla_w_out, j))
            hp, st_p = _gla_layer(hp, bp, tp, *common, None, gla_heads, gla_dk, gla_dv)
            hs, st_s = _gla_layer(hs, bs, ts, *common, state_gla[j], gla_heads, gla_dk, gla_dv)
            outs["gla_p"].append(st_p)
            outs["gla_s"].append(st_s)
        elif kind == 1:
            common = (norm_mix[i], cast_bf16(hgrn_w_in, j), hgrn_lower_bound, i, hgrn_head_norm[j],
                      cast_bf16(hgrn_w_out, j))
            hp, st_p = _hgrn_layer(hp, bp, tp, *common, None, hgrn_heads, hgrn_dk)
            hs, st_s = _hgrn_layer(hs, bs, ts, *common, state_hgrn[j], hgrn_heads, hgrn_dk)
            outs["hgrn_p"].append(st_p)
            outs["hgrn_s"].append(st_s)
        elif kind == 2:
            n_main = nsa_heads * NSA_DH + 6 * nsa_kv * NSA_DH
            w_in = cast_bf16(nsa_w_in, j)
            w_gates = bf(pad_cols(nsa_w_in[j, :, n_main:], LANES))
            common = (norm_mix[i], w_in, w_gates, nsa_pool_k[j], nsa_pool_v[j], cast_bf16(nsa_w_out, j))
            hp, st_p = _nsa_layer(hp, bp, tp, 0, *common, None, nsa_heads, nsa_kv)
            past = (cache_nsa_cmp_k[j], cache_nsa_cmp_v[j], cache_nsa_sel_k[j], cache_nsa_sel_v[j],
                    page_table, cache_nsa_win_k[j], cache_nsa_win_v[j])
            hs, st_s = _nsa_layer(hs, bs, ts, page_table.shape[1] * cache_nsa_cmp_k.shape[2], *common, past,
                                  nsa_heads, nsa_kv)
            outs["nsa_p"].append(st_p)
            outs["nsa_s"].append(st_s)
        else:
            prm = _s5_params(s5_a_re[j], s5_a_im[j], s5_log_dt[j], s5_b_re[j], s5_b_im[j], s5_c_re[j],
                             s5_c_im[j], s5_d[j])
            w_glu = cast_bf16(s5_w_glu, j)
            hp, st_p = _s5_layer(hp, bp, tp, norm_mix[i], prm, w_glu, None, None)
            hs, st_s = _s5_layer(hs, bs, ts, norm_mix[i], prm, w_glu, state_s5_re[j], state_s5_im[j])
            outs["s5_p"].append(st_p)
            outs["s5_s"].append(st_s)
        fw = (norm_ffn[i], cast_bf16(ffn_w_in, i), ffn_conv_w[i], ffn_conv_b[i], cast_bf16(ffn_w_out, i))
        hp, cb_p = _ffn_layer(hp, bp, tp, *fw, None)
        hs, cb_s = _ffn_layer(hs, bs, ts, *fw, state_ffn_conv[i])
        outs["conv_p"].append(cb_p)
        outs["conv_s"].append(cb_s)

    y_prompt = rmsnorm_rows(hp, final_norm, min(hp.shape[0], 512)).reshape(bp, tp, d)
    y_sample = rmsnorm_rows(hs, final_norm, min(hs.shape[0], 512)).reshape(bs, ts, d)
    stack = lambda xs: jnp.stack(xs)
    pick = lambda key, r: stack([e[r] for e in outs[key]])
    res = [y_prompt, y_sample, stack(outs["gla_p"]), stack(outs["gla_s"]), stack(outs["hgrn_p"]), stack(outs["hgrn_s"])]
    for r in range(6):
        res += [pick("nsa_p", r), pick("nsa_s", r)]
    for r in range(2):
        res += [pick("s5_p", r), pick("s5_s", r)]
    res += [stack(outs["conv_p"]), stack(outs["conv_s"])]
    return tuple(res)
```

```python
import functools
import math

import jax
import jax.numpy as jnp
from jax import lax
from jax.experimental import pallas as pl
from jax.experimental.pallas import tpu as pltpu

F32 = jnp.float32
BF16 = jnp.bfloat16
HIGHEST = lax.Precision.HIGHEST

RMS_EPS = 1e-6
ROPE_THETA = 10000.0
NEG = -1e30
CHUNK = 64
SUBCHUNK = 8
GLA_TEMP = 16.0
NSA_BLOCK = 64
NSA_TOP_N = 16
NSA_WINDOW = 512
NSA_GROUP = 4
NSA_DH = 128
S5_CH = 16
S5_STATE = 64
S5_GROUPS_PER_STEP = 2
CONV_W = 3
LANES = 128
SUBLANES = 8
VMEM_LIMIT = 48 * 1024 * 1024


def _cp(*sem):
    return pltpu.CompilerParams(dimension_semantics=sem, vmem_limit_bytes=VMEM_LIMIT)


def _mm(a, b):
    return jnp.dot(a.astype(BF16), b.astype(BF16), preferred_element_type=F32)


def _mm_nt(a, b):
    return lax.dot_general(a.astype(BF16), b.astype(BF16), (((1,), (1,)), ((), ())),
                           preferred_element_type=F32)


def _sigmoid(x):
    return 1.0 / (1.0 + jnp.exp(-x))


def _rms(x, g):
    return x * lax.rsqrt(jnp.mean(x * x, axis=-1, keepdims=True) + RMS_EPS) * g


CAST_BLOCK_BYTES = 4 * 1024 * 1024


def _cast_body(w_ref, o_ref):
    o_ref[...] = w_ref[0].astype(BF16)


def cast_bf16(w_stack, layer):
    _, k, n = w_stack.shape
    tk = 16
    while k % (2 * tk) == 0 and 2 * tk * n * 4 <= CAST_BLOCK_BYTES:
        tk *= 2
    return pl.pallas_call(
        _cast_body, grid=(k // tk,),
        in_specs=[pl.BlockSpec((1, tk, n), lambda i: (layer, i, 0))],
        out_specs=pl.BlockSpec((tk, n), lambda i: (i, 0)),
        out_shape=jax.ShapeDtypeStruct((k, n), BF16),
        compiler_params=_cp("arbitrary"), name="cast_bf16")(w_stack)


def _norm_body(x_ref, g_ref, o_ref):
    o_ref[...] = _rms(x_ref[...], g_ref[...])


def rmsnorm_rows(x, gain, tm):
    m, d = x.shape
    return pl.pallas_call(
        _norm_body, grid=(m // tm,),
        in_specs=[pl.BlockSpec((tm, d), lambda i: (i, 0)), pl.BlockSpec((1, d), lambda i: (0, 0))],
        out_specs=pl.BlockSpec((tm, d), lambda i: (i, 0)),
        out_shape=jax.ShapeDtypeStruct((m, d), F32),
        compiler_params=_cp("arbitrary"), name="rmsnorm")(x, gain.reshape(1, d))


def _proj_body(x_ref, g_ref, w_ref, o_ref, xn_ref):
    @pl.when(pl.program_id(1) == 0)
    def _():
        xn_ref[...] = _rms(x_ref[...], g_ref[...]).astype(BF16)
    o_ref[...] = jnp.dot(xn_ref[...], w_ref[...], preferred_element_type=F32)


def norm_proj(x, gain, w, n_out, tm, tn):
    m, d = x.shape
    return pl.pallas_call(
        _proj_body, grid=(m // tm, n_out // tn),
        in_specs=[pl.BlockSpec((tm, d), lambda i, j: (i, 0)),
                  pl.BlockSpec((1, d), lambda i, j: (0, 0)),
                  pl.BlockSpec((d, tn), lambda i, j: (0, j))],
        out_specs=pl.BlockSpec((tm, tn), lambda i, j: (i, j)),
        out_shape=jax.ShapeDtypeStruct((m, n_out), F32),
        scratch_shapes=[pltpu.VMEM((tm, d), BF16)],
        compiler_params=_cp("arbitrary", "arbitrary"), name="norm_proj")(x, gain.reshape(1, d), w)


def _out_body(a_ref, w_ref, r_ref, o_ref):
    o_ref[...] = r_ref[...] + jnp.dot(a_ref[...].astype(BF16), w_ref[...], preferred_element_type=F32)


def out_proj(a, w, res, tm, tn):
    m, k = a.shape
    n = w.shape[1]
    return pl.pallas_call(
        _out_body, grid=(m // tm, n // tn),
        in_specs=[pl.BlockSpec((tm, k), lambda i, j: (i, 0)),
                  pl.BlockSpec((k, tn), lambda i, j: (0, j)),
                  pl.BlockSpec((tm, tn), lambda i, j: (i, j))],
        out_specs=pl.BlockSpec((tm, tn), lambda i, j: (i, j)),
        out_shape=jax.ShapeDtypeStruct((m, n), F32),
        compiler_params=_cp("arbitrary", "arbitrary"), name="out_proj")(a, w, res)


def _out_glu_body(a_ref, w1_ref, w2_ref, r_ref, o_ref):
    a = a_ref[...]
    g1 = jnp.dot(a, w1_ref[...], preferred_element_type=F32)
    g2 = jnp.dot(a, w2_ref[...], preferred_element_type=F32)
    o_ref[...] = r_ref[...] + g1 * _sigmoid(g2)


def out_glu(a, w, res, tm, tn):
    m, k = a.shape
    n = w.shape[1] // 2
    nj = n // tn
    return pl.pallas_call(
        _out_glu_body, grid=(m // tm, nj),
        in_specs=[pl.BlockSpec((tm, k), lambda i, j: (i, 0)),
                  pl.BlockSpec((k, tn), lambda i, j: (0, j)),
                  pl.BlockSpec((k, tn), lambda i, j: (0, nj + j)),
                  pl.BlockSpec((tm, tn), lambda i, j: (i, j))],
        out_specs=pl.BlockSpec((tm, tn), lambda i, j: (i, j)),
        out_shape=jax.ShapeDtypeStruct((m, n), F32),
        compiler_params=_cp("arbitrary", "arbitrary"), name="out_glu")(a, w, w, res)


def _ffn_in_body(*refs, seg, tiles_per_seq, tail_rows, has_state):
    if has_state:
        (x_ref, g_ref, wg_ref, wv_ref, cwg_ref, cwv_ref, cbg_ref, cbv_ref,
         p1g_ref, p2g_ref, p1v_ref, p2v_ref,
         act_ref, tg_ref, tv_ref, xn_ref, carry_ref) = refs
    else:
        (x_ref, g_ref, wg_ref, wv_ref, cwg_ref, cwv_ref, cbg_ref, cbv_ref,
         act_ref, tg_ref, tv_ref, xn_ref, carry_ref) = refs
    i = pl.program_id(0)
    f = pl.program_id(1)
    tm = x_ref.shape[0]

    @pl.when(f == 0)
    def _():
        xn_ref[...] = _rms(x_ref[...], g_ref[...]).astype(BF16)

    xn = xn_ref[...]
    tf = wg_ref.shape[1]
    sb = tf
    row = lax.broadcasted_iota(jnp.int32, (tm, sb), 0)
    rowm = row % seg
    fresh = (i % tiles_per_seq) == 0

    def conv(w_ref, cw_ref, cb_ref, kind, p1_ref, p2_ref, cs, t_ref):
        u = jnp.dot(xn, w_ref[:, cs], preferred_element_type=F32)
        if has_state:
            p1 = p1_ref[:, cs]
            p2 = p2_ref[:, cs]
        else:
            prev = carry_ref[kind, f, :, cs]
            prev = jnp.where(fresh, 0.0, prev)
            prev0 = prev[SUBLANES - 2:SUBLANES - 1, :]
            prev1 = prev[SUBLANES - 1:SUBLANES, :]
            p1 = jnp.broadcast_to(prev1, u.shape)
            p2 = jnp.where(row == 0, prev0, prev1)
            carry_ref[kind, f, :, cs] = u[tm - SUBLANES:, :]
        u1 = jnp.where(rowm < 1, p1, pltpu.roll(u, 1, 0))
        u2 = jnp.where(rowm < 2, p2, pltpu.roll(u, 2, 0))
        cw = cw_ref[:, cs]
        t_ref[0, :, cs] = u[tm - tail_rows:, :]
        return cw[0:1, :] * u2 + cw[1:2, :] * u1 + cw[2:3, :] * u + cb_ref[:, cs]

    for jb in range(tf // sb):
        cs = slice(jb * sb, (jb + 1) * sb)
        mg = conv(wg_ref, cwg_ref, cbg_ref, 0, p1g_ref if has_state else None, p2g_ref if has_state else None,
                  cs, tg_ref)
        mv = conv(wv_ref, cwv_ref, cbv_ref, 1, p1v_ref if has_state else None, p2v_ref if has_state else None,
                  cs, tv_ref)
        act_ref[:, cs] = (mg * _sigmoid(mg) * mv).astype(BF16)


def ffn_in(x, gain, w_in, conv_w, conv_b, tm, tf, seq_len, hist=None):
    m, d = x.shape
    ff = w_in.shape[1] // 2
    nf = ff // tf
    nb = m // tm
    has_state = hist is not None
    if has_state:
        seg, tiles_per_seq, tail_rows = seq_len, 1, tm
    else:
        seg, tiles_per_seq, tail_rows = tm, seq_len // tm, SUBLANES
    wspec_g = pl.BlockSpec((d, tf), lambda i, f: (0, f))
    wspec_v = pl.BlockSpec((d, tf), lambda i, f: (0, nf + f))
    cspec_g = lambda r: pl.BlockSpec((r, tf), lambda i, f: (0, f))
    cspec_v = lambda r: pl.BlockSpec((r, tf), lambda i, f: (0, nf + f))
    in_specs = [pl.BlockSpec((tm, d), lambda i, f: (i, 0)), pl.BlockSpec((1, d), lambda i, f: (0, 0)),
                wspec_g, wspec_v, cspec_g(CONV_W), cspec_v(CONV_W), cspec_g(1), cspec_v(1)]
    args = [x, gain.reshape(1, d), w_in, w_in, conv_w, conv_w, conv_b.reshape(1, -1), conv_b.reshape(1, -1)]
    if has_state:
        p1, p2 = hist
        in_specs += [pl.BlockSpec((tm, tf), lambda i, f: (i, f)), pl.BlockSpec((tm, tf), lambda i, f: (i, f)),
                     pl.BlockSpec((tm, tf), lambda i, f: (i, nf + f)), pl.BlockSpec((tm, tf), lambda i, f: (i, nf + f))]
        args += [p1, p2, p1, p2]
    body = functools.partial(_ffn_in_body, seg=seg, tiles_per_seq=tiles_per_seq,
                             tail_rows=tail_rows, has_state=has_state)
    return pl.pallas_call(
        body, grid=(nb, nf), in_specs=in_specs,
        out_specs=[pl.BlockSpec((tm, tf), lambda i, f: (i, f)),
                   pl.BlockSpec((1, tail_rows, tf), lambda i, f: (i, 0, f)),
                   pl.BlockSpec((1, tail_rows, tf), lambda i, f: (i, 0, f))],
        out_shape=[jax.ShapeDtypeStruct((m, ff), BF16),
                   jax.ShapeDtypeStruct((nb, tail_rows, ff), F32),
                   jax.ShapeDtypeStruct((nb, tail_rows, ff), F32)],
        scratch_shapes=[pltpu.VMEM((tm, d), BF16), pltpu.VMEM((2, nf, SUBLANES, tf), F32)],
        compiler_params=_cp("arbitrary", "arbitrary"), name="ffn_in")(*args)


def _pad_rows(a, rows):
    if a.shape[0] == rows:
        return a
    return jnp.concatenate([a, jnp.zeros((rows - a.shape[0], a.shape[1]), a.dtype)], axis=0)


def _roll_in_tiles(x, d):
    c, n = x.shape
    return pltpu.roll(x.reshape(c // SUBLANES, SUBLANES, n), d, 1).reshape(c, n)


def _glr_chunk(q, k, v, g, st, c, sub):
    dk = q.shape[1]
    row = lax.broadcasted_iota(jnp.int32, (c, LANES), 0)
    col = lax.broadcasted_iota(jnp.int32, (c, LANES), 1)
    trow = lax.broadcasted_iota(jnp.int32, (c, c), 0)
    tcol = lax.broadcasted_iota(jnp.int32, (c, c), 1)
    tri = (trow >= tcol).astype(F32)
    cum = jnp.dot(tri, g, preferred_element_type=F32, precision=HIGHEST)
    last = cum[c - 1:c, :]
    inter = _mm_nt(q * jnp.exp(cum), st)

    rowk = lax.broadcasted_iota(jnp.int32, (c, dk), 0)
    rm = rowk % sub
    ones = jnp.ones((dk, LANES), BF16)
    att = jnp.zeros((c, LANES), F32)
    for d in range(sub):
        if d == 0:
            p = q * k
        else:
            ks = _roll_in_tiles(k, d)
            cs = _roll_in_tiles(cum, d)
            p = q * ks * jnp.exp(jnp.where(rm >= d, cum - cs, NEG))
        a = jnp.dot(p.astype(BF16), ones, preferred_element_type=F32)
        att = att + jnp.where(col == row - d, a, 0.0)
    if c > sub:
        blocks = [jnp.zeros((sub, LANES), F32)]
        for i in range(1, c // sub):
            cs = cum[i * sub - 1:i * sub, :]
            qi = q[i * sub:(i + 1) * sub, :] * jnp.exp(cum[i * sub:(i + 1) * sub, :] - cs)
            kj = k * jnp.exp(jnp.where(rowk < i * sub, cs - cum, NEG))
            blocks.append(_mm_nt(qi, _pad_rows(kj, LANES)))
        att = att + jnp.concatenate(blocks, axis=0)
    vpad = _pad_rows(v, LANES)
    intra = _mm(att, vpad)
    kd = _pad_rows(k * jnp.exp(last - cum), LANES)
    st_new = st * jnp.exp(last) + _mm(vpad.T, kd)
    return inter + intra, st_new


def _rec_body(*refs, mode, c, sub, n_chunks, dk, dv, hb, layer, has_state):
    refs = list(refs)
    if mode == "gla":
        q_ref, k_ref, v_ref, r_ref, lr_ref, wa_ref, ba_ref, hn_ref = refs[:8]
        rest = refs[8:]
    else:
        q_ref, k_ref, v_ref, r_ref, lb_ref, hn_ref = refs[:6]
        rest = refs[6:]
    if has_state:
        s0_ref, og_ref, sout_ref, st_ref = rest
    else:
        og_ref, sout_ref, st_ref = rest
    tstep = pl.program_id(2)

    @pl.when(tstep == 0)
    def _():
        for hh in range(hb):
            if has_state:
                st_ref[hh] = s0_ref[0, hh].T
            else:
                st_ref[hh] = jnp.zeros(st_ref.shape[1:], F32)

    if mode == "hgrn":
        lbx = lb_ref[...]
        e = jnp.exp(lbx - jnp.max(lbx, axis=0, keepdims=True))
        sm = e / jnp.sum(e, axis=0, keepdims=True)
        lb_all = jnp.zeros((1, hb * dk), F32)
        for li in range(1, layer + 1):
            lb_all = lb_all + sm[li:li + 1, :]

    for ci in range(n_chunks):
        sl = slice(ci * c, (ci + 1) * c)
        for hh in range(hb):
            hk = slice(hh * dk, (hh + 1) * dk)
            hv = slice(hh * dv, (hh + 1) * dv)
            if mode == "gla":
                q = q_ref[0, sl, hk] * (dk ** -0.5)
                k = k_ref[0, sl, hk]
                z = _mm(lr_ref[0, sl, :], wa_ref[:, hk]) + ba_ref[:, hk]
                g = -(jnp.maximum(-z, 0.0) + jnp.log1p(jnp.exp(-jnp.abs(z)))) / GLA_TEMP
            else:
                qz = q_ref[0, sl, hk]
                q = qz * _sigmoid(qz)
                lbv = lb_all[:, hk]
                fg = lbv + (1.0 - lbv) * _sigmoid(k_ref[0, sl, hk])
                k = 1.0 - fg
                g = jnp.log(fg)
            v = v_ref[0, sl, hv]
            o, st_new = _glr_chunk(q, k, v, g, st_ref[hh], c, sub)
            st_ref[hh] = st_new
            of = o * lax.rsqrt(jnp.mean(o * o, axis=-1, keepdims=True) + RMS_EPS) * hn_ref[...]
            gate = r_ref[0, sl, hv]
            og_ref[0, sl, hv] = (of * (gate * _sigmoid(gate))).astype(BF16)

    @pl.when(tstep == pl.num_programs(2) - 1)
    def _():
        for hh in range(hb):
            sout_ref[0, hh] = st_ref[hh].T


def recurrence(mode, proj, heads, dk, dv, hn, tb, c, extra, s0=None, layer=0, hb=1):
    b, t, _ = proj.shape
    sub = min(SUBCHUNK, c)
    has_state = s0 is not None
    wk, wv = hb * dk, hb * dv
    nh = heads // hb
    if mode == "gla":
        lr, wa, ba = extra
        koff, voff = nh, (2 * heads * dk) // wv
        roff = voff + nh
        in_specs = [pl.BlockSpec((1, tb, wk), lambda i, h, s: (i, s, h)),
                    pl.BlockSpec((1, tb, wk), lambda i, h, s: (i, s, koff + h)),
                    pl.BlockSpec((1, tb, wv), lambda i, h, s: (i, s, voff + h)),
                    pl.BlockSpec((1, tb, wv), lambda i, h, s: (i, s, roff + h)),
                    pl.BlockSpec((1, tb, LANES), lambda i, h, s: (i, s, 0)),
                    pl.BlockSpec((LANES, wk), lambda i, h, s: (0, h)),
                    pl.BlockSpec((1, wk), lambda i, h, s: (0, h)),
                    pl.BlockSpec((1, dv), lambda i, h, s: (0, 0))]
        args = [proj, proj, proj, proj, lr, wa, ba, hn.reshape(1, dv)]
    else:
        (lb,) = extra
        in_specs = [pl.BlockSpec((1, tb, wk), lambda i, h, s: (i, s, h)),
                    pl.BlockSpec((1, tb, wk), lambda i, h, s: (i, s, nh + h)),
                    pl.BlockSpec((1, tb, wv), lambda i, h, s: (i, s, 2 * nh + h)),
                    pl.BlockSpec((1, tb, wv), lambda i, h, s: (i, s, 3 * nh + h)),
                    pl.BlockSpec((lb.shape[0], wk), lambda i, h, s: (0, h)),
                    pl.BlockSpec((1, dv), lambda i, h, s: (0, 0))]
        args = [proj, proj, proj, proj, lb, hn.reshape(1, dv)]
    if has_state:
        in_specs.append(pl.BlockSpec((1, hb, dk, dv), lambda i, h, s: (i, h, 0, 0)))
        args.append(s0)
    body = functools.partial(_rec_body, mode=mode, c=c, sub=sub, n_chunks=tb // c, dk=dk, dv=dv, hb=hb,
                             layer=layer, has_state=has_state)
    return pl.pallas_call(
        body, grid=(b, nh, t // tb), in_specs=in_specs,
        out_specs=[pl.BlockSpec((1, tb, wv), lambda i, h, s: (i, s, h)),
                   pl.BlockSpec((1, hb, dk, dv), lambda i, h, s: (i, h, 0, 0))],
        out_shape=[jax.ShapeDtypeStruct((b, t, heads * dv), BF16),
                   jax.ShapeDtypeStruct((b, heads, dk, dv), F32)],
        scratch_shapes=[pltpu.VMEM((hb, dv, dk), F32)],
        compiler_params=_cp("arbitrary", "arbitrary", "arbitrary"), name="recurrence_" + mode)(*args)


def _rope_body(q_ref, ks_ref, kw_ref, cos_ref, sin_ref, qo_ref, kso_ref, kwo_ref):
    cos = cos_ref[...]
    sin = sin_ref[...]

    def rot(src, dst):
        for h in range(src.shape[1] // LANES):
            x = src[:, h * LANES:(h + 1) * LANES]
            dst[:, h * LANES:(h + 1) * LANES] = x * cos + pltpu.roll(x, LANES // 2, 1) * sin

    rot(q_ref, qo_ref)
    rot(ks_ref, kso_ref)
    rot(kw_ref, kwo_ref)


def nsa_rope(proj, cos, sin, tm, tiles_per_seq, qw, kvw):
    m = proj.shape[0]
    ks_blk = (qw + 2 * kvw) // kvw
    kw_blk = (qw + 4 * kvw) // kvw
    return pl.pallas_call(
        _rope_body, grid=(m // tm,),
        in_specs=[pl.BlockSpec((tm, qw), lambda i: (i, 0)),
                  pl.BlockSpec((tm, kvw), lambda i: (i, ks_blk)),
                  pl.BlockSpec((tm, kvw), lambda i: (i, kw_blk)),
                  pl.BlockSpec((tm, LANES), lambda i: (i % tiles_per_seq, 0)),
                  pl.BlockSpec((tm, LANES), lambda i: (i % tiles_per_seq, 0))],
        out_specs=[pl.BlockSpec((tm, qw), lambda i: (i, 0)),
                   pl.BlockSpec((tm, kvw), lambda i: (i, 0)),
                   pl.BlockSpec((tm, kvw), lambda i: (i, 0))],
        out_shape=[jax.ShapeDtypeStruct((m, qw), F32), jax.ShapeDtypeStruct((m, kvw), F32),
                   jax.ShapeDtypeStruct((m, kvw), F32)],
        compiler_params=_cp("arbitrary"), name="nsa_rope")(proj, proj, proj, cos, sin)


def _pool_rows(x, pw):
    n = x.shape[0] // NSA_BLOCK
    return jnp.sum(x.reshape(n, NSA_BLOCK, x.shape[1]) * pw[None], axis=1)


def _pool_body(kc_ref, vc_ref, pk_ref, pv_ref, ko_ref, vo_ref, *, n_cb):
    ko_ref[...] = jnp.zeros(ko_ref.shape, F32)
    vo_ref[...] = jnp.zeros(vo_ref.shape, F32)
    kp = _pool_rows(kc_ref[0, 0:n_cb * NSA_BLOCK, :], pk_ref[...])
    vp = _pool_rows(vc_ref[0, 0:n_cb * NSA_BLOCK, :], pv_ref[...])
    for g in range(ko_ref.shape[1]):
        ko_ref[0, g, 0:n_cb, :] = kp[:, g * LANES:(g + 1) * LANES]
        vo_ref[0, g, 0:n_cb, :] = vp[:, g * LANES:(g + 1) * LANES]


def nsa_pool_prompt(proj3, pk, pv, qw, kvw, n_pad):
    b, t, _ = proj3.shape
    n_cb = t // NSA_BLOCK
    kc_blk = qw // kvw
    return pl.pallas_call(
        functools.partial(_pool_body, n_cb=n_cb), grid=(b,),
        in_specs=[pl.BlockSpec((1, t, kvw), lambda i: (i, 0, kc_blk)),
                  pl.BlockSpec((1, t, kvw), lambda i: (i, 0, kc_blk + 1)),
                  pl.BlockSpec((NSA_BLOCK, kvw), lambda i: (0, 0)),
                  pl.BlockSpec((NSA_BLOCK, kvw), lambda i: (0, 0))],
        out_specs=[pl.BlockSpec((1, kvw // LANES, n_pad, LANES), lambda i: (i, 0, 0, 0)),
                   pl.BlockSpec((1, kvw // LANES, n_pad, LANES), lambda i: (i, 0, 0, 0))],
        out_shape=[jax.ShapeDtypeStruct((b, kvw // LANES, n_pad, LANES), F32)] * 2,
        compiler_params=_cp("arbitrary"), name="nsa_pool")(proj3, proj3, pk, pv)


def _pool_pages_body(pt_ref, *refs, pg):
    k_refs = refs[:pg]
    v_refs = refs[pg:2 * pg]
    pk_ref, pv_ref, ko_ref, vo_ref = refs[2 * pg:]
    _, page, kv, dh = k_refs[0].shape
    per = page // NSA_BLOCK

    def pooled(x_ref, pw_ref):
        return jnp.sum(x_ref[0].reshape(per, NSA_BLOCK, kv, dh) * pw_ref[...][None], axis=1)

    for i in range(pg):
        kp, vp = pooled(k_refs[i], pk_ref), pooled(v_refs[i], pv_ref)
        for g in range(kv):
            ko_ref[0, g, i * per:(i + 1) * per, :] = kp[:, g, :]
            vo_ref[0, g, i * per:(i + 1) * per, :] = vp[:, g, :]


def nsa_pool_pages(pool_k, pool_v, page_table, pk, pv, pg):
    b, n_pages = page_table.shape
    _, page, kv, dh = pool_k.shape
    per = page // NSA_BLOCK
    page_spec = lambda i: pl.BlockSpec((1, page, kv, dh), lambda bi, s, pt, i=i: (pt[bi, s * pg + i], 0, 0, 0))
    wspec = pl.BlockSpec((NSA_BLOCK, kv, dh), lambda bi, s, pt: (0, 0, 0))
    ospec = pl.BlockSpec((1, kv, pg * per, dh), lambda bi, s, pt: (bi, 0, s, 0))
    gs = pltpu.PrefetchScalarGridSpec(
        num_scalar_prefetch=1, grid=(b, n_pages // pg),
        in_specs=[page_spec(i) for i in range(pg)] + [page_spec(i) for i in range(pg)] + [wspec, wspec],
        out_specs=[ospec, ospec])
    n_blk = n_pages * per
    return pl.pallas_call(
        functools.partial(_pool_pages_body, pg=pg), grid_spec=gs,
        out_shape=[jax.ShapeDtypeStruct((b, kv, n_blk, dh), F32)] * 2,
        compiler_params=_cp("arbitrary", "arbitrary"), name="nsa_pool_pages")(
            page_table, *([pool_k] * pg), *([pool_v] * pg), pk, pv)


def _cmp_body(q_ref, kc_ref, vc_ref, o_ref, sel_ref, *, q_start, n_cb, n_blk, nb_pad):
    tq = q_ref.shape[1]
    kvb, ncp = kc_ref.shape[1], kc_ref.shape[2]
    qt = pl.program_id(2)
    scale = NSA_DH ** -0.5
    colc = lax.broadcasted_iota(jnp.int32, (tq, ncp), 1)
    qposc = q_start + qt * tq + lax.broadcasted_iota(jnp.int32, (tq, ncp), 0)
    valid = ((colc + 1) * NSA_BLOCK - 1 <= qposc) & (colc < n_cb)
    imps = []
    for g in range(kvb):
        kc, vc = kc_ref[0, g], vc_ref[0, g]
        imp = jnp.zeros((tq, ncp), F32)
        for r in range(NSA_GROUP):
            hs = slice((g * NSA_GROUP + r) * LANES, (g * NSA_GROUP + r + 1) * LANES)
            s = jnp.where(valid, _mm_nt(q_ref[0, :, hs], kc) * scale, NEG)
            m = jnp.max(s, axis=-1, keepdims=True)
            e = jnp.where(valid, jnp.exp(s - m), 0.0)
            p = e / jnp.maximum(jnp.sum(e, axis=-1, keepdims=True), 1e-30)
            o_ref[0, :, hs] = _mm(p, vc)
            imp = imp + p
        imps.append(imp)
    imp = jnp.concatenate(imps, axis=0)
    rows = kvb * tq
    if nb_pad > ncp:
        imp = jnp.concatenate([imp, jnp.zeros((rows, nb_pad - ncp), F32)], axis=1)
    blk = lax.broadcasted_iota(jnp.int32, (rows, nb_pad), 1)
    qpos = q_start + qt * tq + lax.broadcasted_iota(jnp.int32, (rows, nb_pad), 0) % tq
    cur = qpos // NSA_BLOCK
    forced = (blk == cur) | (blk == 0)
    score = jnp.where(blk > cur, -1.0, jnp.where(forced, NSA_GROUP + 1.0, imp))
    score = jnp.where(blk < n_blk, score, -2.0)
    blkf = blk.astype(F32)

    def pick(_, carry):
        sc, sel = carry
        mx = jnp.max(sc, axis=-1, keepdims=True)
        first = jnp.min(jnp.where(sc == mx, blkf, 1e9), axis=-1, keepdims=True)
        hit = blkf == first
        return jnp.where(hit, -3.0, sc), jnp.where(hit, 1.0, sel)

    _, sel = lax.fori_loop(0, min(NSA_TOP_N, n_blk), pick, (score, jnp.zeros((rows, nb_pad), F32)))
    for g in range(kvb):
        sel_ref[0, g] = sel[g * tq:(g + 1) * tq, :]


def _cmp_t_body(q_ref, kc_ref, vc_ref, o_ref, sel_ref, *, q_start, n_cb, n_blk, nb_pad):
    tq = q_ref.shape[1]
    ncp = kc_ref.shape[2]
    qt = pl.program_id(2)
    scale = NSA_DH ** -0.5
    nr = min(ncp, -(-n_blk // SUBLANES) * SUBLANES)
    kc = kc_ref[0, 0, 0:nr, :]
    vc = vc_ref[0, 0]
    blk = lax.broadcasted_iota(jnp.int32, (nr, tq), 0)
    qpos = q_start + qt * tq + lax.broadcasted_iota(jnp.int32, (nr, tq), 1)
    valid = ((blk + 1) * NSA_BLOCK - 1 <= qpos) & (blk < n_cb)
    pad = jnp.zeros((ncp - nr, tq), F32)
    imp = jnp.zeros((nr, tq), F32)
    for r in range(NSA_GROUP):
        qr = q_ref[0, :, r * LANES:(r + 1) * LANES]
        s = jnp.where(valid, _mm_nt(kc, qr) * scale, NEG)
        m = jnp.max(s, axis=0, keepdims=True)
        e = jnp.where(valid, jnp.exp(s - m), 0.0)
        p = e / jnp.maximum(jnp.sum(e, axis=0, keepdims=True), 1e-30)
        p_rows = jnp.concatenate([p, pad], axis=0).T if ncp > nr else p.T
        o_ref[0, :, r * LANES:(r + 1) * LANES] = _mm(p_rows, vc)
        imp = imp + p
    cur = qpos // NSA_BLOCK
    forced = (blk == cur) | (blk == 0)
    score = jnp.where(blk > cur, -1.0, jnp.where(forced, NSA_GROUP + 1.0, imp))
    score = jnp.where(blk < n_blk, score, -2.0)
    blkf = blk.astype(F32)

    def pick(_, carry):
        sc, sel = carry
        mx = jnp.max(sc, axis=0, keepdims=True)
        first = jnp.min(jnp.where(sc == mx, blkf, 1e9), axis=0, keepdims=True)
        hit = blkf == first
        return jnp.where(hit, -3.0, sc), jnp.where(hit, 1.0, sel)

    _, sel = lax.fori_loop(0, min(NSA_TOP_N, n_blk), pick, (score, jnp.zeros((nr, tq), F32)))
    if nb_pad > nr:
        sel = jnp.concatenate([sel, jnp.zeros((nb_pad - nr, tq), F32)], axis=0)
    sel_ref[0, 0] = sel.T


def nsa_cmp(q3, kcmp, vcmp, tq, q_start, n_cb, n_blk, nb_pad, kvw):
    b, t = q3.shape[0], q3.shape[1]
    ncp = kcmp.shape[2]
    kv = kvw // LANES
    gw = NSA_GROUP * LANES
    transposed = tq % LANES == 0 and nb_pad % LANES == 0 and ncp % LANES == 0
    body = functools.partial(_cmp_t_body if transposed else _cmp_body,
                             q_start=q_start, n_cb=n_cb, n_blk=n_blk, nb_pad=nb_pad)
    kvb = 1 if transposed else kv
    return pl.pallas_call(
        body, grid=(b, kv // kvb, t // tq),
        in_specs=[pl.BlockSpec((1, tq, kvb * gw), lambda i, g, s: (i, s, g)),
                  pl.BlockSpec((1, kvb, ncp, LANES), lambda i, g, s: (i, g, 0, 0)),
                  pl.BlockSpec((1, kvb, ncp, LANES), lambda i, g, s: (i, g, 0, 0))],
        out_specs=[pl.BlockSpec((1, tq, kvb * gw), lambda i, g, s: (i, s, g)),
                   pl.BlockSpec((1, kvb, tq, nb_pad), lambda i, g, s: (i, g, s, 0))],
        out_shape=[jax.ShapeDtypeStruct((b, t, kv * gw), F32),
                   jax.ShapeDtypeStruct((b, kv, t, nb_pad), F32)],
        compiler_params=_cp("arbitrary", "arbitrary", "arbitrary"), name="nsa_cmp")(q3, kcmp, vcmp)


def _softmax_step(carry, s, valid, vv):
    m, l, acc = carry
    s = jnp.where(valid, s, NEG)
    m_new = jnp.maximum(m, jnp.max(s, axis=-1, keepdims=True))
    p = jnp.where(valid, jnp.exp(s - m_new), 0.0)
    alpha = jnp.exp(m - m_new)
    l = alpha * l + jnp.sum(p, axis=-1, keepdims=True)
    acc = alpha * acc + _mm(p, vv)
    return m_new, l, acc


def _softmax_init(rows):
    return (jnp.full((rows, 1), NEG, F32), jnp.zeros((rows, 1), F32), jnp.zeros((rows, LANES), F32))


def _attn_body(*refs, tq, tk, n_kt, q_start, k_start, do_sel, do_win):
    refs = list(refs)
    q_ref = refs.pop(0)
    if do_sel:
        ks_ref, vs_ref, sel_ref = refs[:3]
        refs = refs[3:]
    if do_win:
        kw_ref, vw_ref = refs[:2]
        refs = refs[2:]
    outs = refs
    qt = pl.program_id(2)
    scale = NSA_DH ** -0.5
    rows = NSA_GROUP * tq
    q4 = jnp.concatenate([q_ref[0, :, r * LANES:(r + 1) * LANES] for r in range(NSA_GROUP)], axis=0).astype(BF16)
    q0 = q_start + qt * tq
    qpos = q0 + lax.broadcasted_iota(jnp.int32, (rows, tk), 0) % tq
    kcol = lax.broadcasted_iota(jnp.int32, (rows, tk), 1)

    def finish(carry, o_ref):
        m, l, acc = carry
        o = acc / jnp.maximum(l, 1e-30)
        for r in range(NSA_GROUP):
            o_ref[0, :, r * LANES:(r + 1) * LANES] = o[r * tq:(r + 1) * tq, :]

    if do_sel:
        selb = sel_ref[0, 0].astype(BF16)
        nbp = selb.shape[1]
        en = lax.broadcasted_iota(jnp.int32, (nbp, tk), 0)
        es = lax.broadcasted_iota(jnp.int32, (nbp, tk), 1)

        def sel_step(kt, carry):
            off = pl.multiple_of(kt * tk, tk)
            kk = ks_ref[0, pl.ds(off, tk), :]
            vv = vs_ref[0, pl.ds(off, tk), :]
            s = _mm_nt(q4, kk) * scale
            kp0 = k_start + kt * tk
            expand = ((kp0 + es) // NSA_BLOCK == en).astype(BF16)
            chosen = jnp.dot(selb, expand, preferred_element_type=F32)
            chosen = jnp.concatenate([chosen] * NSA_GROUP, axis=0)
            valid = (chosen > 0.5) & (kp0 + kcol <= qpos)
            return _softmax_step(carry, s, valid, vv)

        hi = jnp.minimum(n_kt, (q0 + tq - 1 - k_start) // tk + 1)
        finish(lax.fori_loop(0, hi, sel_step, _softmax_init(rows)), outs.pop(0))

    if do_win:
        def win_step(kt, carry):
            off = pl.multiple_of(kt * tk, tk)
            kk = kw_ref[0, pl.ds(off, tk), :]
            vv = vw_ref[0, pl.ds(off, tk), :]
            s = _mm_nt(q4, kk) * scale
            kpos = k_start + kt * tk + kcol
            dist = qpos - kpos
            valid = (dist >= 0) & (dist < NSA_WINDOW) & (kpos >= 0)
            return _softmax_step(carry, s, valid, vv)

        lo = jnp.maximum(0, (q0 - (NSA_WINDOW - 1) - k_start) // tk)
        hi = jnp.minimum(n_kt, (q0 + tq - 1 - k_start) // tk + 1)
        finish(lax.fori_loop(lo, hi, win_step, _softmax_init(rows)), outs.pop(0))


def nsa_attend(q_rot3, tq, tk, q_start, k_start, sel_args=None, win_args=None):
    b, t, qw = q_rot3.shape
    kv = qw // (NSA_GROUP * LANES)
    gw = NSA_GROUP * LANES
    in_specs = [pl.BlockSpec((1, tq, gw), lambda i, g, s: (i, s, g))]
    args = [q_rot3]
    n_out = 0
    t_k = None
    if sel_args is not None:
        k, v, sel = sel_args
        t_k = k.shape[1]
        nbp = sel.shape[-1]
        in_specs += [pl.BlockSpec((1, t_k, LANES), lambda i, g, s: (i, 0, g)),
                     pl.BlockSpec((1, t_k, LANES), lambda i, g, s: (i, 0, g)),
                     pl.BlockSpec((1, 1, tq, nbp), lambda i, g, s: (i, g, s, 0))]
        args += [k, v, sel]
        n_out += 1
    if win_args is not None:
        k, v = win_args
        t_k = k.shape[1]
        in_specs += [pl.BlockSpec((1, t_k, LANES), lambda i, g, s: (i, 0, g)),
                     pl.BlockSpec((1, t_k, LANES), lambda i, g, s: (i, 0, g))]
        args += [k, v]
        n_out += 1
    body = functools.partial(_attn_body, tq=tq, tk=tk, n_kt=t_k // tk, q_start=q_start, k_start=k_start,
                             do_sel=sel_args is not None, do_win=win_args is not None)
    return pl.pallas_call(
        body, grid=(b, kv, t // tq), in_specs=in_specs,
        out_specs=[pl.BlockSpec((1, tq, gw), lambda i, g, s: (i, s, g))] * n_out,
        out_shape=[jax.ShapeDtypeStruct((b, t, qw), F32)] * n_out,
        compiler_params=_cp("arbitrary", "arbitrary", "arbitrary"), name="nsa_attend")(*args)


def _attn_t_body(q_ref, ks_ref, vs_ref, kw_ref, vw_ref, sel_ref, osel_ref, owin_ref, vst_ref, vwt_ref, selt_ref,
                 *, tq, tk, n_kt):
    qt = pl.program_id(2)
    scale = NSA_DH ** -0.5
    t_k = ks_ref.shape[1]

    @pl.when(qt == 0)
    def _():
        for j in range(t_k // LANES):
            sl = slice(j * LANES, (j + 1) * LANES)
            vst_ref[:, sl] = vs_ref[0, sl, :].T.astype(BF16)
            vwt_ref[:, sl] = vw_ref[0, sl, :].T.astype(BF16)

    q0 = qt * tq
    cols = NSA_GROUP * tq
    selt_ref[...] = sel_ref[0, 0].T
    q4 = jnp.concatenate([q_ref[0, :, r * LANES:(r + 1) * LANES] for r in range(NSA_GROUP)], axis=0).astype(BF16)

    def scores(k_ref, off, n, allowed):
        bias = jnp.where(allowed, 0.0, NEG)
        return _mm_nt(k_ref[0, pl.ds(off, n), :], q4) * scale + jnp.concatenate([bias] * NSA_GROUP, axis=1)

    def write(o_ref, acc, l):
        o = acc / jnp.maximum(l, 1e-30)
        for r in range(NSA_GROUP):
            o_ref[0, :, r * LANES:(r + 1) * LANES] = o[:, r * tq:(r + 1) * tq].T

    kpos_l = lax.broadcasted_iota(jnp.int32, (tk, tq), 0)
    qpos = q0 + lax.broadcasted_iota(jnp.int32, (tk, tq), 1)

    def sel_step(kt, carry):
        m, l, acc = carry
        off = pl.multiple_of(kt * tk, tk)
        bpc = tk // NSA_BLOCK
        chosen = jnp.concatenate(
            [jnp.broadcast_to(selt_ref[pl.ds(kt * bpc + j, 1), :], (NSA_BLOCK, tq)) for j in range(bpc)], axis=0)
        s = scores(ks_ref, off, tk, (chosen > 0.5) & (kt * tk + kpos_l <= qpos))
        m_new = jnp.maximum(m, jnp.max(s, axis=0, keepdims=True))
        p = jnp.exp(s - m_new)
        alpha = jnp.exp(m - m_new)
        l = alpha * l + jnp.sum(p, axis=0, keepdims=True)
        acc = alpha * acc + jnp.dot(vst_ref[:, pl.ds(off, tk)], p.astype(BF16), preferred_element_type=F32)
        return m_new, l, acc

    init = (jnp.full((1, cols), NEG, F32), jnp.zeros((1, cols), F32), jnp.zeros((NSA_DH, cols), F32))
    m, l, acc = lax.fori_loop(0, jnp.minimum(n_kt, (q0 + tq - 1) // tk + 1), sel_step, init)
    write(osel_ref, acc, l)

    wk = min(t_k, NSA_WINDOW + tq)
    ws = pl.multiple_of(jnp.clip(q0 - NSA_WINDOW, 0, t_k - wk), LANES)
    dist = (q0 + lax.broadcasted_iota(jnp.int32, (wk, tq), 1)) - (ws + lax.broadcasted_iota(jnp.int32, (wk, tq), 0))
    s = scores(kw_ref, ws, wk, (dist >= 0) & (dist < NSA_WINDOW))
    p = jnp.exp(s - jnp.max(s, axis=0, keepdims=True))
    acc = jnp.dot(vwt_ref[:, pl.ds(ws, wk)], p.astype(BF16), preferred_element_type=F32)
    write(owin_ref, acc, jnp.sum(p, axis=0, keepdims=True))


def nsa_attend_prompt(q_rot3, ks, vs, kw, vw, sel, tq, tk):
    b, t, qw = q_rot3.shape
    kv = qw // (NSA_GROUP * LANES)
    gw = NSA_GROUP * LANES
    kspec = pl.BlockSpec((1, t, LANES), lambda i, g, s: (i, 0, g))
    ospec = pl.BlockSpec((1, tq, gw), lambda i, g, s: (i, s, g))
    body = functools.partial(_attn_t_body, tq=tq, tk=tk, n_kt=t // tk)
    return pl.pallas_call(
        body, grid=(b, kv, t // tq),
        in_specs=[ospec, kspec, kspec, kspec, kspec,
                  pl.BlockSpec((1, 1, tq, sel.shape[-1]), lambda i, g, s: (i, g, s, 0))],
        out_specs=[ospec, ospec],
        out_shape=[jax.ShapeDtypeStruct((b, t, qw), F32)] * 2,
        scratch_shapes=[pltpu.VMEM((NSA_DH, t), BF16), pltpu.VMEM((NSA_DH, t), BF16),
                        pltpu.VMEM((sel.shape[-1], tq), F32)],
        compiler_params=_cp("arbitrary", "arbitrary", "arbitrary"), name="nsa_attend_prompt")(
            q_rot3, ks, vs, kw, vw, sel)


def _paged_sel_body(pt_ref, q_ref, sel_ref, kn_ref, vn_ref, k_hbm, v_hbm, o_ref,
                    kbuf, vbuf, sem, m_ref, l_ref, acc_ref, *, pg, past_len, t_new):
    bi, step = pl.program_id(0), pl.program_id(1)
    nsteps = pl.num_programs(1)
    total = pl.num_programs(0) * nsteps
    lin = bi * nsteps + step
    _, page, kv, _ = k_hbm.shape
    rows = q_ref.shape[1]
    per = rows // kv
    scale = NSA_DH ** -0.5
    nbp = sel_ref.shape[-1]
    qb = q_ref[0].astype(BF16)
    selb = sel_ref[0].astype(BF16)

    def page_copies(b_, s_, slot):
        out = []
        for i in range(pg):
            phys = pt_ref[b_, s_ * pg + i]
            for g in range(kv):
                dst = pl.ds(i * page, page)
                out.append(pltpu.make_async_copy(k_hbm.at[phys, :, g, :], kbuf.at[slot, g, dst, :], sem.at[slot, 0]))
                out.append(pltpu.make_async_copy(v_hbm.at[phys, :, g, :], vbuf.at[slot, g, dst, :], sem.at[slot, 1]))
        return out

    @pl.when(lin == 0)
    def _():
        for c in page_copies(0, 0, 0):
            c.start()

    @pl.when(lin + 1 < total)
    def _():
        nxt = lin + 1
        for c in page_copies(nxt // nsteps, nxt % nsteps, nxt % 2):
            c.start()

    @pl.when(step == 0)
    def _():
        m_ref[...] = jnp.full(m_ref.shape, NEG, F32)
        l_ref[...] = jnp.zeros(l_ref.shape, F32)
        acc_ref[...] = jnp.zeros(acc_ref.shape, F32)

    def update(key_of, val_of, n, kp0):
        tloc = lax.broadcasted_iota(jnp.int32, (rows, n), 0) % t_new
        kcol = lax.broadcasted_iota(jnp.int32, (rows, n), 1)
        en = lax.broadcasted_iota(jnp.int32, (nbp, n), 0)
        es = lax.broadcasted_iota(jnp.int32, (nbp, n), 1)
        s = jnp.concatenate([_mm_nt(qb[g * per:(g + 1) * per], key_of(g)) for g in range(kv)], axis=0) * scale
        expand = ((kp0 + es) // NSA_BLOCK == en).astype(BF16)
        chosen = jnp.dot(selb, expand, preferred_element_type=F32)
        valid = (chosen > 0.5) & (kp0 + kcol <= past_len + tloc)
        s = jnp.where(valid, s, NEG)
        m = m_ref[:, 0:1]
        m_new = jnp.maximum(m, jnp.max(s, axis=-1, keepdims=True))
        p = jnp.where(valid, jnp.exp(s - m_new), 0.0)
        alpha = jnp.exp(m - m_new)
        pv = jnp.concatenate([_mm(p[g * per:(g + 1) * per], val_of(g)) for g in range(kv)], axis=0)
        m_ref[...] = jnp.broadcast_to(m_new, m_ref.shape)
        l_ref[...] = jnp.broadcast_to(alpha * l_ref[:, 0:1] + jnp.sum(p, axis=-1, keepdims=True), l_ref.shape)
        acc_ref[...] = alpha * acc_ref[...] + pv

    slot = lin % 2
    for c in page_copies(bi, step, slot):
        c.wait()
    update(lambda g: kbuf[slot, g], lambda g: vbuf[slot, g], pg * page, step * (pg * page))

    @pl.when(step == nsteps - 1)
    def _():
        update(lambda g: kn_ref[0, :, g * LANES:(g + 1) * LANES],
               lambda g: vn_ref[0, :, g * LANES:(g + 1) * LANES], kn_ref.shape[1], past_len)
        o_ref[0] = acc_ref[...] / jnp.maximum(l_ref[:, 0:1], 1e-30)


def nsa_paged_sel(q_rot3, sel, pool_k, pool_v, page_table, k_new, v_new, pg):
    b, t_new, qw = q_rot3.shape
    n_pages = page_table.shape[1]
    _, page, kv, dh = pool_k.shape
    kvw = kv * dh
    nbp = sel.shape[-1]
    per = NSA_GROUP * t_new
    rows = kv * per
    q_rows = q_rot3.reshape(b, t_new, kv, NSA_GROUP, dh).transpose(0, 2, 3, 1, 4).reshape(b, rows, dh)
    sel_rows = jnp.broadcast_to(sel[:, :, None], (b, kv, NSA_GROUP, t_new, nbp)).reshape(b, rows, nbp)
    gs = pltpu.PrefetchScalarGridSpec(
        num_scalar_prefetch=1, grid=(b, n_pages // pg),
        in_specs=[pl.BlockSpec((1, rows, dh), lambda bi, s, pt: (bi, 0, 0)),
                  pl.BlockSpec((1, rows, nbp), lambda bi, s, pt: (bi, 0, 0)),
                  pl.BlockSpec((1, page, kvw), lambda bi, s, pt: (bi, 0, 0)),
                  pl.BlockSpec((1, page, kvw), lambda bi, s, pt: (bi, 0, 0)),
                  pl.BlockSpec(memory_space=pl.ANY), pl.BlockSpec(memory_space=pl.ANY)],
        out_specs=pl.BlockSpec((1, rows, dh), lambda bi, s, pt: (bi, 0, 0)),
        scratch_shapes=[pltpu.VMEM((2, kv, pg * page, dh), F32), pltpu.VMEM((2, kv, pg * page, dh), F32),
                        pltpu.SemaphoreType.DMA((2, 2)),
                        pltpu.VMEM((rows, LANES), F32), pltpu.VMEM((rows, LANES), F32),
                        pltpu.VMEM((rows, dh), F32)])
    body = functools.partial(_paged_sel_body, pg=pg, past_len=n_pages * page, t_new=t_new)
    o_rows = pl.pallas_call(
        body, grid_spec=gs, out_shape=jax.ShapeDtypeStruct((b, rows, dh), F32),
        compiler_params=_cp("arbitrary", "arbitrary"), name="nsa_paged_sel")(
            page_table, q_rows, sel_rows, k_new, v_new, pool_k, pool_v)
    return o_rows.reshape(b, kv, NSA_GROUP, t_new, NSA_DH).transpose(0, 3, 1, 2, 4).reshape(b, t_new, qw)


def _combine_body(oc_ref, os_ref, ow_ref, gt_ref, a_ref):
    gs = _sigmoid(gt_ref[...])
    for hh in range(oc_ref.shape[1] // LANES):
        sl = slice(hh * LANES, (hh + 1) * LANES)
        a = (gs[:, 3 * hh:3 * hh + 1] * oc_ref[:, sl] + gs[:, 3 * hh + 1:3 * hh + 2] * os_ref[:, sl]
             + gs[:, 3 * hh + 2:3 * hh + 3] * ow_ref[:, sl])
        a_ref[:, sl] = a.astype(BF16)


def nsa_combine(o_cmp, o_sel, o_win, gates, tm):
    m, qw = o_cmp.shape
    spec = pl.BlockSpec((tm, qw), lambda i: (i, 0))
    return pl.pallas_call(
        _combine_body, grid=(m // tm,),
        in_specs=[spec, spec, spec, pl.BlockSpec((tm, LANES), lambda i: (i, 0))],
        out_specs=spec, out_shape=jax.ShapeDtypeStruct((m, qw), BF16),
        compiler_params=_cp("arbitrary"), name="nsa_combine")(o_cmp, o_sel, o_win, gates)


def _s5_body(*refs, seg, seq_len, has_state):
    if has_state:
        (u_ref, a1_ref, a2_ref, dt_ref, b1_ref, b2_ref, cm_ref, d_ref, s0_ref,
         z_ref, st_ref, x_ref, y_ref, up_ref) = refs
    else:
        (u_ref, a1_ref, a2_ref, dt_ref, b1_ref, b2_ref, cm_ref, d_ref,
         z_ref, st_ref, x_ref, y_ref, up_ref) = refs
    gq = pl.program_id(1)
    m = u_ref.shape[0]
    nseg = m // seg
    nb = m // seq_len
    half = LANES // 2
    lane = lax.broadcasted_iota(jnp.int32, (1, LANES), 1)
    sgn = jnp.where(lane < half, -1.0, 1.0)

    gs = a1_ref.shape[0]
    abar, bcats = [], []
    for gi in range(gs):
        are, aim, dt = a1_ref[gi], a2_ref[gi], jnp.exp(dt_ref[gi])
        er = jnp.exp(are * dt)
        abr, abi = er * jnp.cos(aim * dt), er * jnp.sin(aim * dt)
        nr, ni, den = abr - 1.0, abi, are * are + aim * aim
        cr, cim = (nr * are + ni * aim) / den, (ni * are - nr * aim) / den
        abar.append((abr, abi))
        bcats.append((cr * b1_ref[gi] + cim * b2_ref[gi]).astype(BF16))

    def cmul(x, pr, pi):
        return x * pr + pltpu.roll(x, half, 1) * (pi * sgn)

    pb = SUBLANES * seg
    nblk = m // pb
    ri = lax.broadcasted_iota(jnp.int32, (pb, pb), 0)
    ci = lax.broadcasted_iota(jnp.int32, (pb, pb), 1)

    @pl.when(gq == 0)
    def _():
        perm = (ci == (ri % SUBLANES) * seg + ri // SUBLANES).astype(BF16)
        for k in range(nblk):
            uk = u_ref[k * pb:(k + 1) * pb, :].astype(BF16)
            up_ref[k] = jnp.dot(perm, uk, preferred_element_type=F32).astype(BF16)

    x_ref[...] = jnp.dot(up_ref[...].reshape(m, LANES), jnp.concatenate(bcats, axis=1),
                         preferred_element_type=F32).reshape(nblk, pb, gs * LANES)

    def scan_group(gi):
        abr, abi = abar[gi]
        gl = slice(gi * LANES, (gi + 1) * LANES)
        x = jnp.zeros((nseg, LANES), F32)
        for s in range(seg):
            sl = slice(s * SUBLANES, (s + 1) * SUBLANES)
            x = cmul(x, abr, abi) + x_ref[:, sl, gl].reshape(nseg, LANES)
            x_ref[:, sl, gl] = x.reshape(nblk, SUBLANES, LANES)
        if has_state:
            carry = s0_ref[gi]
        else:
            spb = seq_len // seg
            pr, pi = abr, abi
            for _ in range(int(math.log2(seg))):
                pr, pi = pr * pr - pi * pi, 2.0 * pr * pi
            rown = lax.broadcasted_iota(jnp.int32, (nseg, LANES), 0) % spb
            inc = x
            sh = 1
            while sh < spb:
                inc = inc + jnp.where(rown >= sh, cmul(pltpu.roll(inc, sh, 0), pr, pi), 0.0)
                pr, pi = pr * pr - pi * pi, 2.0 * pr * pi
                sh *= 2
            carry = jnp.where(rown >= 1, pltpu.roll(inc, 1, 0), 0.0)
        pr, pi = abr, abi
        for s in range(seg):
            sl = slice(s * SUBLANES, (s + 1) * SUBLANES)
            x_ref[:, sl, gl] = x_ref[:, sl, gl] + cmul(carry, pr, pi).reshape(nblk, SUBLANES, LANES)
            pr, pi = pr * abr - pi * abi, pr * abi + pi * abr
        finals = []
        for bi in range(nb):
            last_seg = (bi + 1) * (seq_len // seg) - 1
            row = (seg - 1) * SUBLANES + last_seg % SUBLANES
            finals.append(x_ref[last_seg // SUBLANES, row:row + 1, gl])
        st_ref[gi] = jnp.concatenate(finals, axis=0)

    for gi in range(gs):
        scan_group(gi)

    yg = _mm(x_ref[...].reshape(m, gs * LANES), cm_ref[...].reshape(gs * LANES, LANES))

    @pl.when(gq == 0)
    def _():
        y_ref[...] = yg

    @pl.when(gq > 0)
    def _():
        y_ref[...] = y_ref[...] + yg

    @pl.when(gq == pl.num_programs(1) - 1)
    def _():
        unperm = (ri == (ci % SUBLANES) * seg + ci // SUBLANES).astype(BF16)
        for k in range(nblk):
            rows = slice(k * pb, (k + 1) * pb)
            yk = y_ref[rows, :]
            hi = yk.astype(BF16)
            lo = (yk - hi.astype(F32)).astype(BF16)
            y = (jnp.dot(unperm, hi, preferred_element_type=F32) + jnp.dot(unperm, lo, preferred_element_type=F32)
                 + d_ref[...] * u_ref[rows, :])
            z = 0.5 * y * (1.0 + jnp.tanh(math.sqrt(2.0 / math.pi) * (y + 0.044715 * (y * y * y))))
            z_ref[rows, :] = z.astype(BF16)


def s5_scan(u, prm, seq_len, seg, s0=None):
    m, d = u.shape
    a1, a2, dtb, b1, b2, cm, dsk = prm
    groups = a1.shape[0]
    per_tile = LANES // S5_CH
    nb = m // seq_len
    has_state = s0 is not None
    gs = S5_GROUPS_PER_STEP
    steps = per_tile // gs
    gidx = lambda j, q: (j * steps + q, 0, 0)
    vspec = pl.BlockSpec((gs, 1, LANES), gidx)
    mspec = pl.BlockSpec((gs, LANES, LANES), gidx)
    in_specs = [pl.BlockSpec((m, LANES), lambda j, q: (0, j)), vspec, vspec, vspec, mspec, mspec, mspec,
                pl.BlockSpec((1, LANES), lambda j, q: (0, j))]
    args = [u, a1, a2, dtb, b1, b2, cm, dsk]
    if has_state:
        in_specs.append(pl.BlockSpec((gs, nb, LANES), gidx))
        args.append(s0)
    body = functools.partial(_s5_body, seg=seg, seq_len=seq_len, has_state=has_state)
    return pl.pallas_call(
        body, grid=(d // LANES, steps), in_specs=in_specs,
        out_specs=[pl.BlockSpec((m, LANES), lambda j, q: (0, j)),
                   pl.BlockSpec((gs, nb, LANES), gidx)],
        out_shape=[jax.ShapeDtypeStruct((m, d), BF16), jax.ShapeDtypeStruct((groups, nb, LANES), F32)],
        scratch_shapes=[pltpu.VMEM((m // (SUBLANES * seg), SUBLANES * seg, gs * LANES), F32),
                        pltpu.VMEM((m, LANES), F32),
                        pltpu.VMEM((m // (SUBLANES * seg), SUBLANES * seg, LANES), BF16)],
        compiler_params=_cp("arbitrary", "arbitrary"), name="s5_scan")(*args)


def _s5_params(a_re, a_im, log_dt, b_re, b_im, c_re, c_im, d_skip):
    groups, p = a_re.shape
    per_tile = LANES // S5_CH
    dup = lambda a: jnp.concatenate([a, a], axis=-1)[:, None, :]
    a1, a2 = dup(a_re), dup(a_im)
    dtb = jnp.broadcast_to(log_dt[:, None, None], (groups, 1, LANES))
    slot = jax.nn.one_hot(jnp.arange(groups) % per_tile, per_tile, dtype=F32)

    def rows_in_tile(w):
        return (slot[:, :, None, None] * w[:, None]).reshape(groups, LANES, w.shape[-1])

    bre_t, bim_t = b_re.transpose(0, 2, 1), b_im.transpose(0, 2, 1)
    b1 = rows_in_tile(jnp.concatenate([bre_t, bim_t], axis=-1))
    b2 = rows_in_tile(jnp.concatenate([-bim_t, bre_t], axis=-1))
    cmat = jnp.concatenate([c_re, -c_im], axis=-1)
    cm = rows_in_tile(cmat).transpose(0, 2, 1)
    return a1, a2, dtb, b1, b2, cm, d_skip.reshape(1, -1)


def _tiles(m):
    if m >= 1024:
        return 1024, 1024
    return m, m


def _gla_layer(h, b, t, gain, w_in, w_tail, w_alpha_pad, b_alpha, head_norm, w_out, s0, heads, dk, dv):
    tm, tmo = _tiles(h.shape[0])
    n_main = 2 * heads * dk + 2 * heads * dv
    proj = norm_proj(h, gain, w_in, n_main, tm, 512)
    lr = norm_proj(h, gain, w_tail, LANES, tm, LANES)
    c = math.gcd(t, CHUNK)
    tb = math.gcd(t, 128)
    og, st = recurrence("gla", proj.reshape(b, t, n_main), heads, dk, dv, head_norm, tb, c,
                        (lr.reshape(b, t, LANES), w_alpha_pad, b_alpha.reshape(1, -1)), s0, hb=4)
    return out_proj(og.reshape(b * t, heads * dv), w_out, h, tmo, 512), st


def _hgrn_layer(h, b, t, gain, w_in, lower_bound, layer, head_norm, w_out, s0, heads, dk):
    tm, tmo = _tiles(h.shape[0])
    n = 4 * heads * dk
    proj = norm_proj(h, gain, w_in, n, tm, 512)
    c = math.gcd(t, CHUNK)
    tb = math.gcd(t, 128)
    og, st = recurrence("hgrn", proj.reshape(b, t, n), heads, dk, dk, head_norm, tb, c,
                        (lower_bound,), s0, layer=layer, hb=16)
    return out_proj(og.reshape(b * t, heads * dk), w_out, h, tmo, 512), st


def _rope_tables(start, t):
    half = NSA_DH // 2
    inv = ROPE_THETA ** (-jnp.arange(half, dtype=F32) / half)
    ang = (start + jnp.arange(t, dtype=jnp.int32)).astype(F32)[:, None] * inv[None, :]
    cos, sin = jnp.cos(ang), jnp.sin(ang)
    return jnp.concatenate([cos, cos], axis=-1), jnp.concatenate([-sin, sin], axis=-1)


def _nsa_layer(h, b, t, start, gain, w_in, w_gates, pool_k, pool_v, w_out, past, heads, kv):
    tm, tmo = _tiles(h.shape[0])
    qw, kvw = heads * NSA_DH, kv * NSA_DH
    n_main = qw + 6 * kvw
    proj = norm_proj(h, gain, w_in, n_main, tm, 512)
    gates = norm_proj(h, gain, w_gates, LANES, tm, LANES)
    cos, sin = _rope_tables(start, t)
    pk = jnp.broadcast_to(pool_k[:, None], (NSA_BLOCK, kvw))
    pv = jnp.broadcast_to(pool_v[:, None], (NSA_BLOCK, kvw))
    proj3 = proj.reshape(b, t, n_main)
    col = lambda i: proj3[:, :, qw + i * kvw:qw + (i + 1) * kvw]
    kc, vc, vs, vw = col(0), col(1), col(3), col(5)

    if past is None:
        trope = min(256, t)
        q_rot, ks, kw = nsa_rope(proj, cos, sin, trope, t // trope, qw, kvw)
        q_rot3, ks3, kw3 = q_rot.reshape(b, t, qw), ks.reshape(b, t, kvw), kw.reshape(b, t, kvw)
        n_cb = t // NSA_BLOCK
        n_blk = -(-t // NSA_BLOCK)
        kcmp, vcmp = nsa_pool_prompt(proj3, pk, pv, qw, kvw, LANES)
        o_cmp, sel = nsa_cmp(proj3, kcmp, vcmp, min(t, 512), 0, n_cb, n_blk, LANES, kvw)
        tq = min(t, 128)
        o_sel, o_win = nsa_attend_prompt(q_rot3, ks3, vs, kw3, vw, sel, tq, min(t, 512))
        keep = min(NSA_WINDOW, t)
        win_k, win_v = kw3[:, t - keep:], vw[:, t - keep:]
    else:
        pool_ck, pool_cv, pool_sk, pool_sv, page_table, prev_kw, prev_vw = past
        n_pages = page_table.shape[1]
        page = pool_ck.shape[1]
        past_len = n_pages * page
        cos_r, sin_r = jnp.tile(cos, (b, 1)), jnp.tile(sin, (b, 1))
        q_rot, ks, kw = nsa_rope(proj, cos_r, sin_r, b * t, 1, qw, kvw)
        q_rot3, ks3, kw3 = q_rot.reshape(b, t, qw), ks.reshape(b, t, kvw), kw.reshape(b, t, kvw)
        pk4 = jnp.broadcast_to(pool_k[:, None, None], (NSA_BLOCK, kv, NSA_DH))
        pv4 = jnp.broadcast_to(pool_v[:, None, None], (NSA_BLOCK, kv, NSA_DH))
        kcmp, vcmp = nsa_pool_pages(pool_ck, pool_cv, page_table, pk4, pv4, 8)
        total = past_len + t
        n_cb = total // NSA_BLOCK
        n_blk = -(-total // NSA_BLOCK)
        nb_pad = -(-n_blk // LANES) * LANES
        o_cmp, sel = nsa_cmp(proj3, kcmp, vcmp, t, past_len, n_cb, n_blk, nb_pad, kvw)
        padp = lambda a: jnp.concatenate([a, jnp.zeros((b, page - t, kvw), F32)], axis=1)
        o_sel = nsa_paged_sel(q_rot3, sel, pool_sk, pool_sv, page_table, padp(ks3), padp(vs), 8)
        keep = prev_kw.shape[1]
        kw_ext = jnp.concatenate([prev_kw.reshape(b, keep, kvw), kw3], axis=1)
        vw_ext = jnp.concatenate([prev_vw.reshape(b, keep, kvw), vw], axis=1)
        t_ext = keep + t
        t_pad = -(-t_ext // LANES) * LANES
        pade = lambda a: jnp.concatenate([a, jnp.zeros((b, t_pad - t_ext, kvw), F32)], axis=1)
        (o_win,) = nsa_attend(q_rot3, t, LANES, past_len, past_len - keep, win_args=(pade(kw_ext), pade(vw_ext)))
        win_k, win_v = kw_ext[:, t_ext - keep:], vw_ext[:, t_ext - keep:]

    a = nsa_combine(o_cmp.reshape(b * t, qw), o_sel.reshape(b * t, qw), o_win.reshape(b * t, qw), gates,
                    min(b * t, 512))
    y = out_proj(a, w_out, h, tmo, 512)
    shp = lambda x, n: x.reshape(b, n, kv, NSA_DH)
    return y, (shp(kc, t), shp(vc, t), shp(ks3, t), shp(vs, t), shp(win_k, keep), shp(win_v, keep))


def _s5_layer(h, b, t, gain, prm, w_glu, s_re, s_im):
    tm, tmo = _tiles(h.shape[0])
    u = rmsnorm_rows(h, gain, min(h.shape[0], 512))
    groups = prm[0].shape[0]
    if s_re is None:
        z, st = s5_scan(u, prm, t, math.gcd(t, 32))
    else:
        s0 = jnp.concatenate([s_re, s_im], axis=-1).transpose(1, 0, 2)
        z, st = s5_scan(u, prm, t, t, s0)
    y = out_glu(z, w_glu, h, tmo, 512)
    st = st.transpose(1, 0, 2)
    return y, (st[..., :S5_STATE], st[..., S5_STATE:])


def _ffn_layer(h, b, t, gain, w_in, conv_w, conv_b, w_out, buf):
    tm, tmo = _tiles(h.shape[0])
    ff2 = w_in.shape[1]
    if buf is None:
        tm = min(1024, t)
        act, tg, tv = ffn_in(h, gain, w_in, conv_w, conv_b, tm, 512, t)
        per = t // tm
        last = lambda a: a[per - 1::per, SUBLANES - (CONV_W - 1):, :]
        state = jnp.concatenate([last(tg), last(tv)], axis=-1)
    else:
        zrow = jnp.zeros((b, t - 1, ff2), F32)
        p1 = jnp.concatenate([buf[:, 1:2], zrow], axis=1).reshape(b * t, ff2)
        p2 = jnp.concatenate([buf, zrow[:, 1:]], axis=1).reshape(b * t, ff2)
        act, tg, tv = ffn_in(h, gain, w_in, conv_w, conv_b, b * t, 512, t, hist=(p1, p2))
        up = jnp.concatenate([tg[0], tv[0]], axis=-1).reshape(b, t, ff2)
        state = jnp.concatenate([buf, up], axis=1)[:, t:]
    return out_proj(act, w_out, h, min(h.shape[0], 1024), 256), state


def kernel(x_prompt, x_sample, state_gla, state_hgrn, cache_nsa_cmp_k, cache_nsa_cmp_v, cache_nsa_sel_k, cache_nsa_sel_v, cache_nsa_win_k, cache_nsa_win_v, state_s5_re, state_s5_im, state_ffn_conv, page_table, norm_mix, norm_ffn, final_norm, gla_w_in, gla_w_alpha, gla_b_alpha, gla_head_norm, gla_w_out, hgrn_w_in, hgrn_lower_bound, hgrn_head_norm, hgrn_w_out, nsa_w_in, nsa_pool_k, nsa_pool_v, nsa_w_out, s5_a_re, s5_a_im, s5_log_dt, s5_b_re, s5_b_im, s5_c_re, s5_c_im, s5_d, s5_w_glu, ffn_w_in, ffn_conv_w, ffn_conv_b, ffn_w_out):
    bp, tp, d = x_prompt.shape
    bs, ts, _ = x_sample.shape
    depth = norm_mix.shape[0]
    n_mixers = 4
    gla_heads, gla_dk, gla_dv = state_gla.shape[2], state_gla.shape[3], state_gla.shape[4]
    hgrn_heads, hgrn_dk = state_hgrn.shape[2], state_hgrn.shape[3]
    nsa_kv = cache_nsa_cmp_k.shape[3]
    nsa_heads = d // NSA_DH
    hp = x_prompt.reshape(bp * tp, d)
    hs = x_sample.reshape(bs * ts, d)
    bf = lambda w: w.astype(BF16)

    def pad_cols(w, n):
        return jnp.concatenate([w, jnp.zeros((w.shape[0], n - w.shape[1]), w.dtype)], axis=1)

    outs = {k: [] for k in ("gla_p", "gla_s", "hgrn_p", "hgrn_s", "nsa_p", "nsa_s", "s5_p", "s5_s", "conv_p", "conv_s")}
    for i in range(depth):
        kind, j = i % n_mixers, i // n_mixers
        if kind == 0:
            n_main = 2 * gla_heads * gla_dk + 2 * gla_heads * gla_dv
            w_in = cast_bf16(gla_w_in, j)
            w_tail = bf(pad_cols(gla_w_in[j, :, n_main:], LANES))
            rank = gla_w_alpha.shape[1]
            wa = bf(jnp.concatenate([gla_w_alpha[j], jnp.zeros((LANES - rank, gla_w_alpha.shape[2]), F32)], axis=0))
            common = (norm_mix[i], w_in, w_tail, wa, gla_b_alpha[j], gla_head_norm[j], cast_bf16(gla_w_out, j))
            hp, st_p = _gla_layer(hp, bp, tp, *common, None, gla_heads, gla_dk, gla_dv)
            hs, st_s = _gla_layer(hs, bs, ts, *common, state_gla[j], gla_heads, gla_dk, gla_dv)
            outs["gla_p"].append(st_p)
            outs["gla_s"].append(st_s)
        elif kind == 1:
            common = (norm_mix[i], cast_bf16(hgrn_w_in, j), hgrn_lower_bound, i, hgrn_head_norm[j],
                      cast_bf16(hgrn_w_out, j))
            hp, st_p = _hgrn_layer(hp, bp, tp, *common, None, hgrn_heads, hgrn_dk)
            hs, st_s = _hgrn_layer(hs, bs, ts, *common, state_hgrn[j], hgrn_heads, hgrn_dk)
            outs["hgrn_p"].append(st_p)
            outs["hgrn_s"].append(st_s)
        elif kind == 2:
            n_main = nsa_heads * NSA_DH + 6 * nsa_kv * NSA_DH
            w_in = cast_bf16(nsa_w_in, j)
            w_gates = bf(pad_cols(nsa_w_in[j, :, n_main:], LANES))
            common = (norm_mix[i], w_in, w_gates, nsa_pool_k[j], nsa_pool_v[j], cast_bf16(nsa_w_out, j))
            hp, st_p = _nsa_layer(hp, bp, tp, 0, *common, None, nsa_heads, nsa_kv)
            past = (cache_nsa_cmp_k[j], cache_nsa_cmp_v[j], cache_nsa_sel_k[j], cache_nsa_sel_v[j],
                    page_table, cache_nsa_win_k[j], cache_nsa_win_v[j])
            hs, st_s = _nsa_layer(hs, bs, ts, page_table.shape[1] * cache_nsa_cmp_k.shape[2], *common, past,
                                  nsa_heads, nsa_kv)
            outs["nsa_p"].append(st_p)
            outs["nsa_s"].append(st_s)
        else:
            prm = _s5_params(s5_a_re[j], s5_a_im[j], s5_log_dt[j], s5_b_re[j], s5_b_im[j], s5_c_re[j],
                             s5_c_im[j], s5_d[j])
            w_glu = cast_bf16(s5_w_glu, j)
            hp, st_p = _s5_layer(hp, bp, tp, norm_mix[i], prm, w_glu, None, None)
            hs, st_s = _s5_layer(hs, bs, ts, norm_mix[i], prm, w_glu, state_s5_re[j], state_s5_im[j])
            outs["s5_p"].append(st_p)
            outs["s5_s"].append(st_s)
        fw = (norm_ffn[i], cast_bf16(ffn_w_in, i), ffn_conv_w[i], ffn_conv_b[i], cast_bf16(ffn_w_out, i))
        hp, cb_p = _ffn_layer(hp, bp, tp, *fw, None)
        hs, cb_s = _ffn_layer(hs, bs, ts, *fw, state_ffn_conv[i])
        outs["conv_p"].append(cb_p)
        outs["conv_s"].append(cb_s)

    y_prompt = rmsnorm_rows(hp, final_norm, min(hp.shape[0], 512)).reshape(bp, tp, d)
    y_sample = rmsnorm_rows(hs, final_norm, min(hs.shape[0], 512)).reshape(bs, ts, d)
    stack = lambda xs: jnp.stack(xs)
    pick = lambda key, r: stack([e[r] for e in outs[key]])
    res = [y_prompt, y_sample, stack(outs["gla_p"]), stack(outs["gla_s"]), stack(outs["hgrn_p"]), stack(outs["hgrn_s"])]
    for r in range(6):
        res += [pick("nsa_p", r), pick("nsa_s", r)]
    for r in range(2):
        res += [pick("s5_p", r), pick("s5_s", r)]
    res += [stack(outs["conv_p"]), stack(outs["conv_s"])]
    return tuple(res)
```

```python
import functools
import math

import jax
import jax.numpy as jnp
from jax import lax
from jax.experimental import pallas as pl
from jax.experimental.pallas import tpu as pltpu

F32 = jnp.float32
BF16 = jnp.bfloat16
HIGHEST = lax.Precision.HIGHEST

RMS_EPS = 1e-6
ROPE_THETA = 10000.0
NEG = -1e30
CHUNK = 128
SUBCHUNK = 8
GLA_TEMP = 16.0
NSA_BLOCK = 64
NSA_TOP_N = 16
NSA_WINDOW = 512
NSA_GROUP = 4
NSA_DH = 128
S5_CH = 16
S5_STATE = 64
S5_GROUPS_PER_STEP = 2
CONV_W = 3
LANES = 128
SUBLANES = 8
VMEM_LIMIT = 48 * 1024 * 1024


def _cp(*sem):
    return pltpu.CompilerParams(dimension_semantics=sem, vmem_limit_bytes=VMEM_LIMIT)


def _mm(a, b):
    return jnp.dot(a.astype(BF16), b.astype(BF16), preferred_element_type=F32)


def _mm_nt(a, b):
    return lax.dot_general(a.astype(BF16), b.astype(BF16), (((1,), (1,)), ((), ())),
                           preferred_element_type=F32)


def _sigmoid(x):
    return 1.0 / (1.0 + jnp.exp(-x))


def _rms(x, g):
    return x * lax.rsqrt(jnp.mean(x * x, axis=-1, keepdims=True) + RMS_EPS) * g


CAST_BLOCK_BYTES = 4 * 1024 * 1024


def _cast_body(w_ref, o_ref):
    o_ref[...] = w_ref[0].astype(BF16)


def cast_bf16(w_stack, layer):
    _, k, n = w_stack.shape
    tk = 16
    while k % (2 * tk) == 0 and 2 * tk * n * 4 <= CAST_BLOCK_BYTES:
        tk *= 2
    return pl.pallas_call(
        _cast_body, grid=(k // tk,),
        in_specs=[pl.BlockSpec((1, tk, n), lambda i: (layer, i, 0))],
        out_specs=pl.BlockSpec((tk, n), lambda i: (i, 0)),
        out_shape=jax.ShapeDtypeStruct((k, n), BF16),
        compiler_params=_cp("arbitrary"), name="cast_bf16")(w_stack)


def _norm_body(x_ref, g_ref, o_ref):
    o_ref[...] = _rms(x_ref[...], g_ref[...])


def rmsnorm_rows(x, gain, tm):
    m, d = x.shape
    return pl.pallas_call(
        _norm_body, grid=(m // tm,),
        in_specs=[pl.BlockSpec((tm, d), lambda i: (i, 0)), pl.BlockSpec((1, d), lambda i: (0, 0))],
        out_specs=pl.BlockSpec((tm, d), lambda i: (i, 0)),
        out_shape=jax.ShapeDtypeStruct((m, d), F32),
        compiler_params=_cp("arbitrary"), name="rmsnorm")(x, gain.reshape(1, d))


def _proj_body(x_ref, g_ref, w_ref, o_ref, xn_ref):
    @pl.when(pl.program_id(1) == 0)
    def _():
        xn_ref[...] = _rms(x_ref[...], g_ref[...]).astype(BF16)
    o_ref[...] = jnp.dot(xn_ref[...], w_ref[...], preferred_element_type=F32)


def norm_proj(x, gain, w, n_out, tm, tn):
    m, d = x.shape
    return pl.pallas_call(
        _proj_body, grid=(m // tm, n_out // tn),
        in_specs=[pl.BlockSpec((tm, d), lambda i, j: (i, 0)),
                  pl.BlockSpec((1, d), lambda i, j: (0, 0)),
                  pl.BlockSpec((d, tn), lambda i, j: (0, j))],
        out_specs=pl.BlockSpec((tm, tn), lambda i, j: (i, j)),
        out_shape=jax.ShapeDtypeStruct((m, n_out), F32),
        scratch_shapes=[pltpu.VMEM((tm, d), BF16)],
        compiler_params=_cp("arbitrary", "arbitrary"), name="norm_proj")(x, gain.reshape(1, d), w)


def _out_body(a_ref, w_ref, r_ref, o_ref):
    o_ref[...] = r_ref[...] + jnp.dot(a_ref[...].astype(BF16), w_ref[...], preferred_element_type=F32)


def out_proj(a, w, res, tm, tn):
    m, k = a.shape
    n = w.shape[1]
    return pl.pallas_call(
        _out_body, grid=(m // tm, n // tn),
        in_specs=[pl.BlockSpec((tm, k), lambda i, j: (i, 0)),
                  pl.BlockSpec((k, tn), lambda i, j: (0, j)),
                  pl.BlockSpec((tm, tn), lambda i, j: (i, j))],
        out_specs=pl.BlockSpec((tm, tn), lambda i, j: (i, j)),
        out_shape=jax.ShapeDtypeStruct((m, n), F32),
        compiler_params=_cp("arbitrary", "arbitrary"), name="out_proj")(a, w, res)


def _out_glu_body(a_ref, w1_ref, w2_ref, r_ref, o_ref):
    a = a_ref[...]
    g1 = jnp.dot(a, w1_ref[...], preferred_element_type=F32)
    g2 = jnp.dot(a, w2_ref[...], preferred_element_type=F32)
    o_ref[...] = r_ref[...] + g1 * _sigmoid(g2)


def out_glu(a, w, res, tm, tn):
    m, k = a.shape
    n = w.shape[1] // 2
    nj = n // tn
    return pl.pallas_call(
        _out_glu_body, grid=(m // tm, nj),
        in_specs=[pl.BlockSpec((tm, k), lambda i, j: (i, 0)),
                  pl.BlockSpec((k, tn), lambda i, j: (0, j)),
                  pl.BlockSpec((k, tn), lambda i, j: (0, nj + j)),
                  pl.BlockSpec((tm, tn), lambda i, j: (i, j))],
        out_specs=pl.BlockSpec((tm, tn), lambda i, j: (i, j)),
        out_shape=jax.ShapeDtypeStruct((m, n), F32),
        compiler_params=_cp("arbitrary", "arbitrary"), name="out_glu")(a, w, w, res)


def _ffn_in_body(*refs, seg, tiles_per_seq, tail_rows, has_state):
    if has_state:
        (x_ref, g_ref, wg_ref, wv_ref, cwg_ref, cwv_ref, cbg_ref, cbv_ref,
         p1g_ref, p2g_ref, p1v_ref, p2v_ref,
         act_ref, tg_ref, tv_ref, xn_ref, carry_ref) = refs
    else:
        (x_ref, g_ref, wg_ref, wv_ref, cwg_ref, cwv_ref, cbg_ref, cbv_ref,
         act_ref, tg_ref, tv_ref, xn_ref, carry_ref) = refs
    i = pl.program_id(0)
    f = pl.program_id(1)
    tm = x_ref.shape[0]

    @pl.when(f == 0)
    def _():
        xn_ref[...] = _rms(x_ref[...], g_ref[...]).astype(BF16)

    xn = xn_ref[...]
    tf = wg_ref.shape[1]
    sb = tf
    row = lax.broadcasted_iota(jnp.int32, (tm, sb), 0)
    rowm = row % seg
    fresh = (i % tiles_per_seq) == 0

    def conv(w_ref, cw_ref, cb_ref, kind, p1_ref, p2_ref, cs, t_ref):
        u = jnp.dot(xn, w_ref[:, cs], preferred_element_type=F32)
        if has_state:
            p1 = p1_ref[:, cs]
            p2 = p2_ref[:, cs]
        else:
            prev = carry_ref[kind, f, :, cs]
            prev = jnp.where(fresh, 0.0, prev)
            prev0 = prev[SUBLANES - 2:SUBLANES - 1, :]
            prev1 = prev[SUBLANES - 1:SUBLANES, :]
            p1 = jnp.broadcast_to(prev1, u.shape)
            p2 = jnp.where(row == 0, prev0, prev1)
            carry_ref[kind, f, :, cs] = u[tm - SUBLANES:, :]
        u1 = jnp.where(rowm < 1, p1, pltpu.roll(u, 1, 0))
        u2 = jnp.where(rowm < 2, p2, pltpu.roll(u, 2, 0))
        cw = cw_ref[:, cs]
        t_ref[0, :, cs] = u[tm - tail_rows:, :]
        return cw[0:1, :] * u2 + cw[1:2, :] * u1 + cw[2:3, :] * u + cb_ref[:, cs]

    for jb in range(tf // sb):
        cs = slice(jb * sb, (jb + 1) * sb)
        mg = conv(wg_ref, cwg_ref, cbg_ref, 0, p1g_ref if has_state else None, p2g_ref if has_state else None,
                  cs, tg_ref)
        mv = conv(wv_ref, cwv_ref, cbv_ref, 1, p1v_ref if has_state else None, p2v_ref if has_state else None,
                  cs, tv_ref)
        act_ref[:, cs] = (mg * _sigmoid(mg) * mv).astype(BF16)


def ffn_in(x, gain, w_in, conv_w, conv_b, tm, tf, seq_len, hist=None):
    m, d = x.shape
    ff = w_in.shape[1] // 2
    nf = ff // tf
    nb = m // tm
    has_state = hist is not None
    if has_state:
        seg, tiles_per_seq, tail_rows = seq_len, 1, tm
    else:
        seg, tiles_per_seq, tail_rows = tm, seq_len // tm, SUBLANES
    wspec_g = pl.BlockSpec((d, tf), lambda i, f: (0, f))
    wspec_v = pl.BlockSpec((d, tf), lambda i, f: (0, nf + f))
    cspec_g = lambda r: pl.BlockSpec((r, tf), lambda i, f: (0, f))
    cspec_v = lambda r: pl.BlockSpec((r, tf), lambda i, f: (0, nf + f))
    in_specs = [pl.BlockSpec((tm, d), lambda i, f: (i, 0)), pl.BlockSpec((1, d), lambda i, f: (0, 0)),
                wspec_g, wspec_v, cspec_g(CONV_W), cspec_v(CONV_W), cspec_g(1), cspec_v(1)]
    args = [x, gain.reshape(1, d), w_in, w_in, conv_w, conv_w, conv_b.reshape(1, -1), conv_b.reshape(1, -1)]
    if has_state:
        p1, p2 = hist
        in_specs += [pl.BlockSpec((tm, tf), lambda i, f: (i, f)), pl.BlockSpec((tm, tf), lambda i, f: (i, f)),
                     pl.BlockSpec((tm, tf), lambda i, f: (i, nf + f)), pl.BlockSpec((tm, tf), lambda i, f: (i, nf + f))]
        args += [p1, p2, p1, p2]
    body = functools.partial(_ffn_in_body, seg=seg, tiles_per_seq=tiles_per_seq,
                             tail_rows=tail_rows, has_state=has_state)
    return pl.pallas_call(
        body, grid=(nb, nf), in_specs=in_specs,
        out_specs=[pl.BlockSpec((tm, tf), lambda i, f: (i, f)),
                   pl.BlockSpec((1, tail_rows, tf), lambda i, f: (i, 0, f)),
                   pl.BlockSpec((1, tail_rows, tf), lambda i, f: (i, 0, f))],
        out_shape=[jax.ShapeDtypeStruct((m, ff), BF16),
                   jax.ShapeDtypeStruct((nb, tail_rows, ff), F32),
                   jax.ShapeDtypeStruct((nb, tail_rows, ff), F32)],
        scratch_shapes=[pltpu.VMEM((tm, d), BF16), pltpu.VMEM((2, nf, SUBLANES, tf), F32)],
        compiler_params=_cp("arbitrary", "arbitrary"), name="ffn_in")(*args)


def _pad_rows(a, rows):
    if a.shape[0] == rows:
        return a
    return jnp.concatenate([a, jnp.zeros((rows - a.shape[0], a.shape[1]), a.dtype)], axis=0)


def _roll_in_tiles(x, d):
    c, n = x.shape
    return pltpu.roll(x.reshape(c // SUBLANES, SUBLANES, n), d, 1).reshape(c, n)


def _glr_chunk(q, k, v, g, st, c, sub):
    dk = q.shape[1]
    row = lax.broadcasted_iota(jnp.int32, (c, LANES), 0)
    col = lax.broadcasted_iota(jnp.int32, (c, LANES), 1)
    trow = lax.broadcasted_iota(jnp.int32, (c, c), 0)
    tcol = lax.broadcasted_iota(jnp.int32, (c, c), 1)
    tri = (trow >= tcol).astype(F32)
    cum = jnp.dot(tri, g, preferred_element_type=F32, precision=HIGHEST)
    last = cum[c - 1:c, :]
    inter = _mm_nt(q * jnp.exp(cum), st)

    rowk = lax.broadcasted_iota(jnp.int32, (c, dk), 0)
    rm = rowk % sub
    ones = jnp.ones((dk, LANES), BF16)
    att = jnp.zeros((c, LANES), F32)
    for d in range(sub):
        if d == 0:
            p = q * k
        else:
            ks = _roll_in_tiles(k, d)
            cs = _roll_in_tiles(cum, d)
            p = q * ks * jnp.exp(jnp.where(rm >= d, cum - cs, NEG))
        a = jnp.dot(p.astype(BF16), ones, preferred_element_type=F32)
        att = att + jnp.where(col == row - d, a, 0.0)
    if c > sub:
        blocks = [jnp.zeros((sub, LANES), F32)]
        for i in range(1, c // sub):
            cs = cum[i * sub - 1:i * sub, :]
            qi = q[i * sub:(i + 1) * sub, :] * jnp.exp(cum[i * sub:(i + 1) * sub, :] - cs)
            kj = k * jnp.exp(jnp.where(rowk < i * sub, cs - cum, NEG))
            blocks.append(_mm_nt(qi, _pad_rows(kj, LANES)))
        att = att + jnp.concatenate(blocks, axis=0)
    vpad = _pad_rows(v, LANES)
    intra = _mm(att, vpad)
    kd = _pad_rows(k * jnp.exp(last - cum), LANES)
    st_new = st * jnp.exp(last) + _mm(vpad.T, kd)
    return inter + intra, st_new


def _rec_body(*refs, mode, c, sub, n_chunks, dk, dv, hb, layer, has_state):
    refs = list(refs)
    if mode == "gla":
        q_ref, k_ref, v_ref, r_ref, lr_ref, wa_ref, ba_ref, hn_ref = refs[:8]
        rest = refs[8:]
    else:
        q_ref, k_ref, v_ref, r_ref, lb_ref, hn_ref = refs[:6]
        rest = refs[6:]
    if has_state:
        s0_ref, og_ref, sout_ref, st_ref = rest
    else:
        og_ref, sout_ref, st_ref = rest
    tstep = pl.program_id(2)

    @pl.when(tstep == 0)
    def _():
        for hh in range(hb):
            if has_state:
                st_ref[hh] = s0_ref[0, hh].T
            else:
                st_ref[hh] = jnp.zeros(st_ref.shape[1:], F32)

    if mode == "hgrn":
        lbx = lb_ref[...]
        e = jnp.exp(lbx - jnp.max(lbx, axis=0, keepdims=True))
        sm = e / jnp.sum(e, axis=0, keepdims=True)
        lb_all = jnp.zeros((1, hb * dk), F32)
        for li in range(1, layer + 1):
            lb_all = lb_all + sm[li:li + 1, :]

    for ci in range(n_chunks):
        sl = slice(ci * c, (ci + 1) * c)
        for hh in range(hb):
            hk = slice(hh * dk, (hh + 1) * dk)
            hv = slice(hh * dv, (hh + 1) * dv)
            if mode == "gla":
                q = q_ref[0, sl, hk] * (dk ** -0.5)
                k = k_ref[0, sl, hk]
                z = _mm(lr_ref[0, sl, :], wa_ref[:, hk]) + ba_ref[:, hk]
                g = -(jnp.maximum(-z, 0.0) + jnp.log1p(jnp.exp(-jnp.abs(z)))) / GLA_TEMP
            else:
                qz = q_ref[0, sl, hk]
                q = qz * _sigmoid(qz)
                lbv = lb_all[:, hk]
                fg = lbv + (1.0 - lbv) * _sigmoid(k_ref[0, sl, hk])
                k = 1.0 - fg
                g = jnp.log(fg)
            v = v_ref[0, sl, hv]
            o, st_new = _glr_chunk(q, k, v, g, st_ref[hh], c, sub)
            st_ref[hh] = st_new
            of = o * lax.rsqrt(jnp.mean(o * o, axis=-1, keepdims=True) + RMS_EPS) * hn_ref[...]
            gate = r_ref[0, sl, hv]
            og_ref[0, sl, hv] = (of * (gate * _sigmoid(gate))).astype(BF16)

    @pl.when(tstep == pl.num_programs(2) - 1)
    def _():
        for hh in range(hb):
            sout_ref[0, hh] = st_ref[hh].T


def recurrence(mode, proj, heads, dk, dv, hn, tb, c, extra, s0=None, layer=0, hb=1):
    b, t, _ = proj.shape
    sub = min(SUBCHUNK, c)
    has_state = s0 is not None
    wk, wv = hb * dk, hb * dv
    nh = heads // hb
    if mode == "gla":
        lr, wa, ba = extra
        koff, voff = nh, (2 * heads * dk) // wv
        roff = voff + nh
        in_specs = [pl.BlockSpec((1, tb, wk), lambda i, h, s: (i, s, h)),
                    pl.BlockSpec((1, tb, wk), lambda i, h, s: (i, s, koff + h)),
                    pl.BlockSpec((1, tb, wv), lambda i, h, s: (i, s, voff + h)),
                    pl.BlockSpec((1, tb, wv), lambda i, h, s: (i, s, roff + h)),
                    pl.BlockSpec((1, tb, LANES), lambda i, h, s: (i, s, 0)),
                    pl.BlockSpec((LANES, wk), lambda i, h, s: (0, h)),
                    pl.BlockSpec((1, wk), lambda i, h, s: (0, h)),
                    pl.BlockSpec((1, dv), lambda i, h, s: (0, 0))]
        args = [proj, proj, proj, proj, lr, wa, ba, hn.reshape(1, dv)]
    else:
        (lb,) = extra
        in_specs = [pl.BlockSpec((1, tb, wk), lambda i, h, s: (i, s, h)),
                    pl.BlockSpec((1, tb, wk), lambda i, h, s: (i, s, nh + h)),
                    pl.BlockSpec((1, tb, wv), lambda i, h, s: (i, s, 2 * nh + h)),
                    pl.BlockSpec((1, tb, wv), lambda i, h, s: (i, s, 3 * nh + h)),
                    pl.BlockSpec((lb.shape[0], wk), lambda i, h, s: (0, h)),
                    pl.BlockSpec((1, dv), lambda i, h, s: (0, 0))]
        args = [proj, proj, proj, proj, lb, hn.reshape(1, dv)]
    if has_state:
        in_specs.append(pl.BlockSpec((1, hb, dk, dv), lambda i, h, s: (i, h, 0, 0)))
        args.append(s0)
    body = functools.partial(_rec_body, mode=mode, c=c, sub=sub, n_chunks=tb // c, dk=dk, dv=dv, hb=hb,
                             layer=layer, has_state=has_state)
    return pl.pallas_call(
        body, grid=(b, nh, t // tb), in_specs=in_specs,
        out_specs=[pl.BlockSpec((1, tb, wv), lambda i, h, s: (i, s, h)),
                   pl.BlockSpec((1, hb, dk, dv), lambda i, h, s: (i, h, 0, 0))],
        out_shape=[jax.ShapeDtypeStruct((b, t, heads * dv), BF16),
                   jax.ShapeDtypeStruct((b, heads, dk, dv), F32)],
        scratch_shapes=[pltpu.VMEM((hb, dv, dk), F32)],
        compiler_params=_cp("arbitrary", "arbitrary", "arbitrary"), name="recurrence_" + mode)(*args)


def _rope_body(q_ref, ks_ref, kw_ref, cos_ref, sin_ref, qo_ref, kso_ref, kwo_ref):
    cos = cos_ref[...]
    sin = sin_ref[...]

    def rot(src, dst):
        for h in range(src.shape[1] // LANES):
            x = src[:, h * LANES:(h + 1) * LANES]
            dst[:, h * LANES:(h + 1) * LANES] = x * cos + pltpu.roll(x, LANES // 2, 1) * sin

    rot(q_ref, qo_ref)
    rot(ks_ref, kso_ref)
    rot(kw_ref, kwo_ref)


def nsa_rope(proj, cos, sin, tm, tiles_per_seq, qw, kvw):
    m = proj.shape[0]
    ks_blk = (qw + 2 * kvw) // kvw
    kw_blk = (qw + 4 * kvw) // kvw
    return pl.pallas_call(
        _rope_body, grid=(m // tm,),
        in_specs=[pl.BlockSpec((tm, qw), lambda i: (i, 0)),
                  pl.BlockSpec((tm, kvw), lambda i: (i, ks_blk)),
                  pl.BlockSpec((tm, kvw), lambda i: (i, kw_blk)),
                  pl.BlockSpec((tm, LANES), lambda i: (i % tiles_per_seq, 0)),
                  pl.BlockSpec((tm, LANES), lambda i: (i % tiles_per_seq, 0))],
        out_specs=[pl.BlockSpec((tm, qw), lambda i: (i, 0)),
                   pl.BlockSpec((tm, kvw), lambda i: (i, 0)),
                   pl.BlockSpec((tm, kvw), lambda i: (i, 0))],
        out_shape=[jax.ShapeDtypeStruct((m, qw), F32), jax.ShapeDtypeStruct((m, kvw), F32),
                   jax.ShapeDtypeStruct((m, kvw), F32)],
        compiler_params=_cp("arbitrary"), name="nsa_rope")(proj, proj, proj, cos, sin)


def _pool_rows(x, pw):
    n = x.shape[0] // NSA_BLOCK
    return jnp.sum(x.reshape(n, NSA_BLOCK, x.shape[1]) * pw[None], axis=1)


def _pool_body(kc_ref, vc_ref, pk_ref, pv_ref, ko_ref, vo_ref, *, n_cb):
    ko_ref[...] = jnp.zeros(ko_ref.shape, F32)
    vo_ref[...] = jnp.zeros(vo_ref.shape, F32)
    kp = _pool_rows(kc_ref[0, 0:n_cb * NSA_BLOCK, :], pk_ref[...])
    vp = _pool_rows(vc_ref[0, 0:n_cb * NSA_BLOCK, :], pv_ref[...])
    for g in range(ko_ref.shape[1]):
        ko_ref[0, g, 0:n_cb, :] = kp[:, g * LANES:(g + 1) * LANES]
        vo_ref[0, g, 0:n_cb, :] = vp[:, g * LANES:(g + 1) * LANES]


def nsa_pool_prompt(proj3, pk, pv, qw, kvw, n_pad):
    b, t, _ = proj3.shape
    n_cb = t // NSA_BLOCK
    kc_blk = qw // kvw
    return pl.pallas_call(
        functools.partial(_pool_body, n_cb=n_cb), grid=(b,),
        in_specs=[pl.BlockSpec((1, t, kvw), lambda i: (i, 0, kc_blk)),
                  pl.BlockSpec((1, t, kvw), lambda i: (i, 0, kc_blk + 1)),
                  pl.BlockSpec((NSA_BLOCK, kvw), lambda i: (0, 0)),
                  pl.BlockSpec((NSA_BLOCK, kvw), lambda i: (0, 0))],
        out_specs=[pl.BlockSpec((1, kvw // LANES, n_pad, LANES), lambda i: (i, 0, 0, 0)),
                   pl.BlockSpec((1, kvw // LANES, n_pad, LANES), lambda i: (i, 0, 0, 0))],
        out_shape=[jax.ShapeDtypeStruct((b, kvw // LANES, n_pad, LANES), F32)] * 2,
        compiler_params=_cp("arbitrary"), name="nsa_pool")(proj3, proj3, pk, pv)


def _pool_pages_body(pt_ref, *refs, pg):
    k_refs = refs[:pg]
    v_refs = refs[pg:2 * pg]
    pk_ref, pv_ref, ko_ref, vo_ref = refs[2 * pg:]
    _, page, kv, dh = k_refs[0].shape
    per = page // NSA_BLOCK

    def pooled(x_ref, pw_ref):
        return jnp.sum(x_ref[0].reshape(per, NSA_BLOCK, kv, dh) * pw_ref[...][None], axis=1)

    for i in range(pg):
        kp, vp = pooled(k_refs[i], pk_ref), pooled(v_refs[i], pv_ref)
        for g in range(kv):
            ko_ref[0, g, i * per:(i + 1) * per, :] = kp[:, g, :]
            vo_ref[0, g, i * per:(i + 1) * per, :] = vp[:, g, :]


def nsa_pool_pages(pool_k, pool_v, page_table, pk, pv, pg):
    b, n_pages = page_table.shape
    _, page, kv, dh = pool_k.shape
    per = page // NSA_BLOCK
    page_spec = lambda i: pl.BlockSpec((1, page, kv, dh), lambda bi, s, pt, i=i: (pt[bi, s * pg + i], 0, 0, 0))
    wspec = pl.BlockSpec((NSA_BLOCK, kv, dh), lambda bi, s, pt: (0, 0, 0))
    ospec = pl.BlockSpec((1, kv, pg * per, dh), lambda bi, s, pt: (bi, 0, s, 0))
    gs = pltpu.PrefetchScalarGridSpec(
        num_scalar_prefetch=1, grid=(b, n_pages // pg),
        in_specs=[page_spec(i) for i in range(pg)] + [page_spec(i) for i in range(pg)] + [wspec, wspec],
        out_specs=[ospec, ospec])
    n_blk = n_pages * per
    return pl.pallas_call(
        functools.partial(_pool_pages_body, pg=pg), grid_spec=gs,
        out_shape=[jax.ShapeDtypeStruct((b, kv, n_blk, dh), F32)] * 2,
        compiler_params=_cp("arbitrary", "arbitrary"), name="nsa_pool_pages")(
            page_table, *([pool_k] * pg), *([pool_v] * pg), pk, pv)


def _cmp_body(q_ref, kc_ref, vc_ref, o_ref, sel_ref, *, q_start, n_cb, n_blk, nb_pad):
    tq = q_ref.shape[1]
    kvb, ncp = kc_ref.shape[1], kc_ref.shape[2]
    qt = pl.program_id(2)
    scale = NSA_DH ** -0.5
    colc = lax.broadcasted_iota(jnp.int32, (tq, ncp), 1)
    qposc = q_start + qt * tq + lax.broadcasted_iota(jnp.int32, (tq, ncp), 0)
    valid = ((colc + 1) * NSA_BLOCK - 1 <= qposc) & (colc < n_cb)
    imps = []
    for g in range(kvb):
        kc, vc = kc_ref[0, g], vc_ref[0, g]
        imp = jnp.zeros((tq, ncp), F32)
        for r in range(NSA_GROUP):
            hs = slice((g * NSA_GROUP + r) * LANES, (g * NSA_GROUP + r + 1) * LANES)
            s = jnp.where(valid, _mm_nt(q_ref[0, :, hs], kc) * scale, NEG)
            m = jnp.max(s, axis=-1, keepdims=True)
            e = jnp.where(valid, jnp.exp(s - m), 0.0)
            p = e / jnp.maximum(jnp.sum(e, axis=-1, keepdims=True), 1e-30)
            o_ref[0, :, hs] = _mm(p, vc)
            imp = imp + p
        imps.append(imp)
    imp = jnp.concatenate(imps, axis=0)
    rows = kvb * tq
    if nb_pad > ncp:
        imp = jnp.concatenate([imp, jnp.zeros((rows, nb_pad - ncp), F32)], axis=1)
    blk = lax.broadcasted_iota(jnp.int32, (rows, nb_pad), 1)
    qpos = q_start + qt * tq + lax.broadcasted_iota(jnp.int32, (rows, nb_pad), 0) % tq
    cur = qpos // NSA_BLOCK
    forced = (blk == cur) | (blk == 0)
    score = jnp.where(blk > cur, -1.0, jnp.where(forced, NSA_GROUP + 1.0, imp))
    score = jnp.where(blk < n_blk, score, -2.0)
    blkf = blk.astype(F32)

    def pick(_, carry):
        sc, sel = carry
        mx = jnp.max(sc, axis=-1, keepdims=True)
        first = jnp.min(jnp.where(sc == mx, blkf, 1e9), axis=-1, keepdims=True)
        hit = blkf == first
        return jnp.where(hit, -3.0, sc), jnp.where(hit, 1.0, sel)

    _, sel = lax.fori_loop(0, min(NSA_TOP_N, n_blk), pick, (score, jnp.zeros((rows, nb_pad), F32)))
    for g in range(kvb):
        sel_ref[0, g] = sel[g * tq:(g + 1) * tq, :]


def _cmp_t_body(q_ref, kc_ref, vc_ref, o_ref, sel_ref, *, q_start, n_cb, n_blk, nb_pad):
    tq = q_ref.shape[1]
    ncp = kc_ref.shape[2]
    qt = pl.program_id(2)
    scale = NSA_DH ** -0.5
    nr = min(ncp, -(-n_blk // SUBLANES) * SUBLANES)
    kc = kc_ref[0, 0, 0:nr, :]
    vc = vc_ref[0, 0]
    blk = lax.broadcasted_iota(jnp.int32, (nr, tq), 0)
    qpos = q_start + qt * tq + lax.broadcasted_iota(jnp.int32, (nr, tq), 1)
    valid = ((blk + 1) * NSA_BLOCK - 1 <= qpos) & (blk < n_cb)
    pad = jnp.zeros((ncp - nr, tq), F32)
    imp = jnp.zeros((nr, tq), F32)
    for r in range(NSA_GROUP):
        qr = q_ref[0, :, r * LANES:(r + 1) * LANES]
        s = jnp.where(valid, _mm_nt(kc, qr) * scale, NEG)
        m = jnp.max(s, axis=0, keepdims=True)
        e = jnp.where(valid, jnp.exp(s - m), 0.0)
        p = e / jnp.maximum(jnp.sum(e, axis=0, keepdims=True), 1e-30)
        p_rows = jnp.concatenate([p, pad], axis=0).T if ncp > nr else p.T
        o_ref[0, :, r * LANES:(r + 1) * LANES] = _mm(p_rows, vc)
        imp = imp + p
    cur = qpos // NSA_BLOCK
    forced = (blk == cur) | (blk == 0)
    score = jnp.where(blk > cur, -1.0, jnp.where(forced, NSA_GROUP + 1.0, imp))
    score = jnp.where(blk < n_blk, score, -2.0)
    blkf = blk.astype(F32)

    def pick(_, carry):
        sc, sel = carry
        mx = jnp.max(sc, axis=0, keepdims=True)
        first = jnp.min(jnp.where(sc == mx, blkf, 1e9), axis=0, keepdims=True)
        hit = blkf == first
        return jnp.where(hit, -3.0, sc), jnp.where(hit, 1.0, sel)

    _, sel = lax.fori_loop(0, min(NSA_TOP_N, n_blk), pick, (score, jnp.zeros((nr, tq), F32)))
    if nb_pad > nr:
        sel = jnp.concatenate([sel, jnp.zeros((nb_pad - nr, tq), F32)], axis=0)
    sel_ref[0, 0] = sel.T


def nsa_cmp(q3, kcmp, vcmp, tq, q_start, n_cb, n_blk, nb_pad, kvw):
    b, t = q3.shape[0], q3.shape[1]
    ncp = kcmp.shape[2]
    kv = kvw // LANES
    gw = NSA_GROUP * LANES
    transposed = tq % LANES == 0 and nb_pad % LANES == 0 and ncp % LANES == 0
    body = functools.partial(_cmp_t_body if transposed else _cmp_body,
                             q_start=q_start, n_cb=n_cb, n_blk=n_blk, nb_pad=nb_pad)
    kvb = 1 if transposed else kv
    return pl.pallas_call(
        body, grid=(b, kv // kvb, t // tq),
        in_specs=[pl.BlockSpec((1, tq, kvb * gw), lambda i, g, s: (i, s, g)),
                  pl.BlockSpec((1, kvb, ncp, LANES), lambda i, g, s: (i, g, 0, 0)),
                  pl.BlockSpec((1, kvb, ncp, LANES), lambda i, g, s: (i, g, 0, 0))],
        out_specs=[pl.BlockSpec((1, tq, kvb * gw), lambda i, g, s: (i, s, g)),
                   pl.BlockSpec((1, kvb, tq, nb_pad), lambda i, g, s: (i, g, s, 0))],
        out_shape=[jax.ShapeDtypeStruct((b, t, kv * gw), F32),
                   jax.ShapeDtypeStruct((b, kv, t, nb_pad), F32)],
        compiler_params=_cp("arbitrary", "arbitrary", "arbitrary"), name="nsa_cmp")(q3, kcmp, vcmp)


def _softmax_step(carry, s, valid, vv):
    m, l, acc = carry
    s = jnp.where(valid, s, NEG)
    m_new = jnp.maximum(m, jnp.max(s, axis=-1, keepdims=True))
    p = jnp.where(valid, jnp.exp(s - m_new), 0.0)
    alpha = jnp.exp(m - m_new)
    l = alpha * l + jnp.sum(p, axis=-1, keepdims=True)
    acc = alpha * acc + _mm(p, vv)
    return m_new, l, acc


def _softmax_init(rows):
    return (jnp.full((rows, 1), NEG, F32), jnp.zeros((rows, 1), F32), jnp.zeros((rows, LANES), F32))


def _attn_body(*refs, tq, tk, n_kt, q_start, k_start, do_sel, do_win):
    refs = list(refs)
    q_ref = refs.pop(0)
    if do_sel:
        ks_ref, vs_ref, sel_ref = refs[:3]
        refs = refs[3:]
    if do_win:
        kw_ref, vw_ref = refs[:2]
        refs = refs[2:]
    outs = refs
    qt = pl.program_id(2)
    scale = NSA_DH ** -0.5
    rows = NSA_GROUP * tq
    q4 = jnp.concatenate([q_ref[0, :, r * LANES:(r + 1) * LANES] for r in range(NSA_GROUP)], axis=0).astype(BF16)
    q0 = q_start + qt * tq
    qpos = q0 + lax.broadcasted_iota(jnp.int32, (rows, tk), 0) % tq
    kcol = lax.broadcasted_iota(jnp.int32, (rows, tk), 1)

    def finish(carry, o_ref):
        m, l, acc = carry
        o = acc / jnp.maximum(l, 1e-30)
        for r in range(NSA_GROUP):
            o_ref[0, :, r * LANES:(r + 1) * LANES] = o[r * tq:(r + 1) * tq, :]

    if do_sel:
        selb = sel_ref[0, 0].astype(BF16)
        nbp = selb.shape[1]
        en = lax.broadcasted_iota(jnp.int32, (nbp, tk), 0)
        es = lax.broadcasted_iota(jnp.int32, (nbp, tk), 1)

        def sel_step(kt, carry):
            off = pl.multiple_of(kt * tk, tk)
            kk = ks_ref[0, pl.ds(off, tk), :]
            vv = vs_ref[0, pl.ds(off, tk), :]
            s = _mm_nt(q4, kk) * scale
            kp0 = k_start + kt * tk
            expand = ((kp0 + es) // NSA_BLOCK == en).astype(BF16)
            chosen = jnp.dot(selb, expand, preferred_element_type=F32)
            chosen = jnp.concatenate([chosen] * NSA_GROUP, axis=0)
            valid = (chosen > 0.5) & (kp0 + kcol <= qpos)
            return _softmax_step(carry, s, valid, vv)

        hi = jnp.minimum(n_kt, (q0 + tq - 1 - k_start) // tk + 1)
        finish(lax.fori_loop(0, hi, sel_step, _softmax_init(rows)), outs.pop(0))

    if do_win:
        def win_step(kt, carry):
            off = pl.multiple_of(kt * tk, tk)
            kk = kw_ref[0, pl.ds(off, tk), :]
            vv = vw_ref[0, pl.ds(off, tk), :]
            s = _mm_nt(q4, kk) * scale
            kpos = k_start + kt * tk + kcol
            dist = qpos - kpos
            valid = (dist >= 0) & (dist < NSA_WINDOW) & (kpos >= 0)
            return _softmax_step(carry, s, valid, vv)

        lo = jnp.maximum(0, (q0 - (NSA_WINDOW - 1) - k_start) // tk)
        hi = jnp.minimum(n_kt, (q0 + tq - 1 - k_start) // tk + 1)
        finish(lax.fori_loop(lo, hi, win_step, _softmax_init(rows)), outs.pop(0))


def nsa_attend(q_rot3, tq, tk, q_start, k_start, sel_args=None, win_args=None):
    b, t, qw = q_rot3.shape
    kv = qw // (NSA_GROUP * LANES)
    gw = NSA_GROUP * LANES
    in_specs = [pl.BlockSpec((1, tq, gw), lambda i, g, s: (i, s, g))]
    args = [q_rot3]
    n_out = 0
    t_k = None
    if sel_args is not None:
        k, v, sel = sel_args
        t_k = k.shape[1]
        nbp = sel.shape[-1]
        in_specs += [pl.BlockSpec((1, t_k, LANES), lambda i, g, s: (i, 0, g)),
                     pl.BlockSpec((1, t_k, LANES), lambda i, g, s: (i, 0, g)),
                     pl.BlockSpec((1, 1, tq, nbp), lambda i, g, s: (i, g, s, 0))]
        args += [k, v, sel]
        n_out += 1
    if win_args is not None:
        k, v = win_args
        t_k = k.shape[1]
        in_specs += [pl.BlockSpec((1, t_k, LANES), lambda i, g, s: (i, 0, g)),
                     pl.BlockSpec((1, t_k, LANES), lambda i, g, s: (i, 0, g))]
        args += [k, v]
        n_out += 1
    body = functools.partial(_attn_body, tq=tq, tk=tk, n_kt=t_k // tk, q_start=q_start, k_start=k_start,
                             do_sel=sel_args is not None, do_win=win_args is not None)
    return pl.pallas_call(
        body, grid=(b, kv, t // tq), in_specs=in_specs,
        out_specs=[pl.BlockSpec((1, tq, gw), lambda i, g, s: (i, s, g))] * n_out,
        out_shape=[jax.ShapeDtypeStruct((b, t, qw), F32)] * n_out,
        compiler_params=_cp("arbitrary", "arbitrary", "arbitrary"), name="nsa_attend")(*args)


def _attn_t_body(q_ref, ks_ref, vs_ref, kw_ref, vw_ref, sel_ref, osel_ref, owin_ref, vst_ref, vwt_ref, selt_ref,
                 *, tq, tk, n_kt):
    qt = pl.program_id(2)
    scale = NSA_DH ** -0.5
    t_k = ks_ref.shape[1]

    @pl.when(qt == 0)
    def _():
        for j in range(t_k // LANES):
            sl = slice(j * LANES, (j + 1) * LANES)
            vst_ref[:, sl] = vs_ref[0, sl, :].T.astype(BF16)
            vwt_ref[:, sl] = vw_ref[0, sl, :].T.astype(BF16)

    q0 = qt * tq
    cols = NSA_GROUP * tq
    selt_ref[...] = sel_ref[0, 0].T
    q4 = jnp.concatenate([q_ref[0, :, r * LANES:(r + 1) * LANES] for r in range(NSA_GROUP)], axis=0).astype(BF16)

    def scores(k_ref, off, n, allowed):
        bias = jnp.where(allowed, 0.0, NEG)
        return _mm_nt(k_ref[0, pl.ds(off, n), :], q4) * scale + jnp.concatenate([bias] * NSA_GROUP, axis=1)

    def write(o_ref, acc, l):
        o = acc / jnp.maximum(l, 1e-30)
        for r in range(NSA_GROUP):
            o_ref[0, :, r * LANES:(r + 1) * LANES] = o[:, r * tq:(r + 1) * tq].T

    kpos_l = lax.broadcasted_iota(jnp.int32, (tk, tq), 0)
    qpos = q0 + lax.broadcasted_iota(jnp.int32, (tk, tq), 1)

    def sel_step(kt, carry):
        m, l, acc = carry
        off = pl.multiple_of(kt * tk, tk)
        bpc = tk // NSA_BLOCK
        chosen = jnp.concatenate(
            [jnp.broadcast_to(selt_ref[pl.ds(kt * bpc + j, 1), :], (NSA_BLOCK, tq)) for j in range(bpc)], axis=0)
        s = scores(ks_ref, off, tk, (chosen > 0.5) & (kt * tk + kpos_l <= qpos))
        m_new = jnp.maximum(m, jnp.max(s, axis=0, keepdims=True))
        p = jnp.exp(s - m_new)
        alpha = jnp.exp(m - m_new)
        l = alpha * l + jnp.sum(p, axis=0, keepdims=True)
        acc = alpha * acc + jnp.dot(vst_ref[:, pl.ds(off, tk)], p.astype(BF16), preferred_element_type=F32)
        return m_new, l, acc

    init = (jnp.full((1, cols), NEG, F32), jnp.zeros((1, cols), F32), jnp.zeros((NSA_DH, cols), F32))
    m, l, acc = lax.fori_loop(0, jnp.minimum(n_kt, (q0 + tq - 1) // tk + 1), sel_step, init)
    write(osel_ref, acc, l)

    wk = min(t_k, NSA_WINDOW + tq)
    ws = pl.multiple_of(jnp.clip(q0 - NSA_WINDOW, 0, t_k - wk), LANES)
    dist = (q0 + lax.broadcasted_iota(jnp.int32, (wk, tq), 1)) - (ws + lax.broadcasted_iota(jnp.int32, (wk, tq), 0))
    s = scores(kw_ref, ws, wk, (dist >= 0) & (dist < NSA_WINDOW))
    p = jnp.exp(s - jnp.max(s, axis=0, keepdims=True))
    acc = jnp.dot(vwt_ref[:, pl.ds(ws, wk)], p.astype(BF16), preferred_element_type=F32)
    write(owin_ref, acc, jnp.sum(p, axis=0, keepdims=True))


def nsa_attend_prompt(q_rot3, ks, vs, kw, vw, sel, tq, tk):
    b, t, qw = q_rot3.shape
    kv = qw // (NSA_GROUP * LANES)
    gw = NSA_GROUP * LANES
    kspec = pl.BlockSpec((1, t, LANES), lambda i, g, s: (i, 0, g))
    ospec = pl.BlockSpec((1, tq, gw), lambda i, g, s: (i, s, g))
    body = functools.partial(_attn_t_body, tq=tq, tk=tk, n_kt=t // tk)
    return pl.pallas_call(
        body, grid=(b, kv, t // tq),
        in_specs=[ospec, kspec, kspec, kspec, kspec,
                  pl.BlockSpec((1, 1, tq, sel.shape[-1]), lambda i, g, s: (i, g, s, 0))],
        out_specs=[ospec, ospec],
        out_shape=[jax.ShapeDtypeStruct((b, t, qw), F32)] * 2,
        scratch_shapes=[pltpu.VMEM((NSA_DH, t), BF16), pltpu.VMEM((NSA_DH, t), BF16),
                        pltpu.VMEM((sel.shape[-1], tq), F32)],
        compiler_params=_cp("arbitrary", "arbitrary", "arbitrary"), name="nsa_attend_prompt")(
            q_rot3, ks, vs, kw, vw, sel)


def _paged_sel_body(pt_ref, q_ref, sel_ref, kn_ref, vn_ref, k_hbm, v_hbm, o_ref,
                    kbuf, vbuf, sem, m_ref, l_ref, acc_ref, *, pg, past_len, t_new):
    bi, step = pl.program_id(0), pl.program_id(1)
    nsteps = pl.num_programs(1)
    total = pl.num_programs(0) * nsteps
    lin = bi * nsteps + step
    _, page, kv, _ = k_hbm.shape
    rows = q_ref.shape[1]
    per = rows // kv
    scale = NSA_DH ** -0.5
    nbp = sel_ref.shape[-1]
    qb = q_ref[0].astype(BF16)
    selb = sel_ref[0].astype(BF16)

    def page_copies(b_, s_, slot):
        out = []
        for i in range(pg):
            phys = pt_ref[b_, s_ * pg + i]
            for g in range(kv):
                dst = pl.ds(i * page, page)
                out.append(pltpu.make_async_copy(k_hbm.at[phys, :, g, :], kbuf.at[slot, g, dst, :], sem.at[slot, 0]))
                out.append(pltpu.make_async_copy(v_hbm.at[phys, :, g, :], vbuf.at[slot, g, dst, :], sem.at[slot, 1]))
        return out

    @pl.when(lin == 0)
    def _():
        for c in page_copies(0, 0, 0):
            c.start()

    @pl.when(lin + 1 < total)
    def _():
        nxt = lin + 1
        for c in page_copies(nxt // nsteps, nxt % nsteps, nxt % 2):
            c.start()

    @pl.when(step == 0)
    def _():
        m_ref[...] = jnp.full(m_ref.shape, NEG, F32)
        l_ref[...] = jnp.zeros(l_ref.shape, F32)
        acc_ref[...] = jnp.zeros(acc_ref.shape, F32)

    def update(key_of, val_of, n, kp0):
        tloc = lax.broadcasted_iota(jnp.int32, (rows, n), 0) % t_new
        kcol = lax.broadcasted_iota(jnp.int32, (rows, n), 1)
        en = lax.broadcasted_iota(jnp.int32, (nbp, n), 0)
        es = lax.broadcasted_iota(jnp.int32, (nbp, n), 1)
        s = jnp.concatenate([_mm_nt(qb[g * per:(g + 1) * per], key_of(g)) for g in range(kv)], axis=0) * scale
        expand = ((kp0 + es) // NSA_BLOCK == en).astype(BF16)
        chosen = jnp.dot(selb, expand, preferred_element_type=F32)
        valid = (chosen > 0.5) & (kp0 + kcol <= past_len + tloc)
        s = jnp.where(valid, s, NEG)
        m = m_ref[:, 0:1]
        m_new = jnp.maximum(m, jnp.max(s, axis=-1, keepdims=True))
        p = jnp.where(valid, jnp.exp(s - m_new), 0.0)
        alpha = jnp.exp(m - m_new)
        pv = jnp.concatenate([_mm(p[g * per:(g + 1) * per], val_of(g)) for g in range(kv)], axis=0)
        m_ref[...] = jnp.broadcast_to(m_new, m_ref.shape)
        l_ref[...] = jnp.broadcast_to(alpha * l_ref[:, 0:1] + jnp.sum(p, axis=-1, keepdims=True), l_ref.shape)
        acc_ref[...] = alpha * acc_ref[...] + pv

    slot = lin % 2
    for c in page_copies(bi, step, slot):
        c.wait()
    update(lambda g: kbuf[slot, g], lambda g: vbuf[slot, g], pg * page, step * (pg * page))

    @pl.when(step == nsteps - 1)
    def _():
        update(lambda g: kn_ref[0, :, g * LANES:(g + 1) * LANES],
               lambda g: vn_ref[0, :, g * LANES:(g + 1) * LANES], kn_ref.shape[1], past_len)
        o_ref[0] = acc_ref[...] / jnp.maximum(l_ref[:, 0:1], 1e-30)


def nsa_paged_sel(q_rot3, sel, pool_k, pool_v, page_table, k_new, v_new, pg):
    b, t_new, qw = q_rot3.shape
    n_pages = page_table.shape[1]
    _, page, kv, dh = pool_k.shape
    kvw = kv * dh
    nbp = sel.shape[-1]
    per = NSA_GROUP * t_new
    rows = kv * per
    q_rows = q_rot3.reshape(b, t_new, kv, NSA_GROUP, dh).transpose(0, 2, 3, 1, 4).reshape(b, rows, dh)
    sel_rows = jnp.broadcast_to(sel[:, :, None], (b, kv, NSA_GROUP, t_new, nbp)).reshape(b, rows, nbp)
    gs = pltpu.PrefetchScalarGridSpec(
        num_scalar_prefetch=1, grid=(b, n_pages // pg),
        in_specs=[pl.BlockSpec((1, rows, dh), lambda bi, s, pt: (bi, 0, 0)),
                  pl.BlockSpec((1, rows, nbp), lambda bi, s, pt: (bi, 0, 0)),
                  pl.BlockSpec((1, page, kvw), lambda bi, s, pt: (bi, 0, 0)),
                  pl.BlockSpec((1, page, kvw), lambda bi, s, pt: (bi, 0, 0)),
                  pl.BlockSpec(memory_space=pl.ANY), pl.BlockSpec(memory_space=pl.ANY)],
        out_specs=pl.BlockSpec((1, rows, dh), lambda bi, s, pt: (bi, 0, 0)),
        scratch_shapes=[pltpu.VMEM((2, kv, pg * page, dh), F32), pltpu.VMEM((2, kv, pg * page, dh), F32),
                        pltpu.SemaphoreType.DMA((2, 2)),
                        pltpu.VMEM((rows, LANES), F32), pltpu.VMEM((rows, LANES), F32),
                        pltpu.VMEM((rows, dh), F32)])
    body = functools.partial(_paged_sel_body, pg=pg, past_len=n_pages * page, t_new=t_new)
    o_rows = pl.pallas_call(
        body, grid_spec=gs, out_shape=jax.ShapeDtypeStruct((b, rows, dh), F32),
        compiler_params=_cp("arbitrary", "arbitrary"), name="nsa_paged_sel")(
            page_table, q_rows, sel_rows, k_new, v_new, pool_k, pool_v)
    return o_rows.reshape(b, kv, NSA_GROUP, t_new, NSA_DH).transpose(0, 3, 1, 2, 4).reshape(b, t_new, qw)


def _combine_body(oc_ref, os_ref, ow_ref, gt_ref, a_ref):
    gs = _sigmoid(gt_ref[...])
    for hh in range(oc_ref.shape[1] // LANES):
        sl = slice(hh * LANES, (hh + 1) * LANES)
        a = (gs[:, 3 * hh:3 * hh + 1] * oc_ref[:, sl] + gs[:, 3 * hh + 1:3 * hh + 2] * os_ref[:, sl]
             + gs[:, 3 * hh + 2:3 * hh + 3] * ow_ref[:, sl])
        a_ref[:, sl] = a.astype(BF16)


def nsa_combine(o_cmp, o_sel, o_win, gates, tm):
    m, qw = o_cmp.shape
    spec = pl.BlockSpec((tm, qw), lambda i: (i, 0))
    return pl.pallas_call(
        _combine_body, grid=(m // tm,),
        in_specs=[spec, spec, spec, pl.BlockSpec((tm, LANES), lambda i: (i, 0))],
        out_specs=spec, out_shape=jax.ShapeDtypeStruct((m, qw), BF16),
        compiler_params=_cp("arbitrary"), name="nsa_combine")(o_cmp, o_sel, o_win, gates)


def _s5_body(*refs, seg, seq_len, has_state):
    if has_state:
        (u_ref, a1_ref, a2_ref, dt_ref, b1_ref, b2_ref, cm_ref, d_ref, s0_ref,
         z_ref, st_ref, x_ref, y_ref, up_ref) = refs
    else:
        (u_ref, a1_ref, a2_ref, dt_ref, b1_ref, b2_ref, cm_ref, d_ref,
         z_ref, st_ref, x_ref, y_ref, up_ref) = refs
    gq = pl.program_id(1)
    m = u_ref.shape[0]
    nseg = m // seg
    nb = m // seq_len
    half = LANES // 2
    lane = lax.broadcasted_iota(jnp.int32, (1, LANES), 1)
    sgn = jnp.where(lane < half, -1.0, 1.0)

    gs = a1_ref.shape[0]
    abar, bcats = [], []
    for gi in range(gs):
        are, aim, dt = a1_ref[gi], a2_ref[gi], jnp.exp(dt_ref[gi])
        er = jnp.exp(are * dt)
        abr, abi = er * jnp.cos(aim * dt), er * jnp.sin(aim * dt)
        nr, ni, den = abr - 1.0, abi, are * are + aim * aim
        cr, cim = (nr * are + ni * aim) / den, (ni * are - nr * aim) / den
        abar.append((abr, abi))
        bcats.append((cr * b1_ref[gi] + cim * b2_ref[gi]).astype(BF16))

    def cmul(x, pr, pi):
        return x * pr + pltpu.roll(x, half, 1) * (pi * sgn)

    pb = SUBLANES * seg
    nblk = m // pb
    ri = lax.broadcasted_iota(jnp.int32, (pb, pb), 0)
    ci = lax.broadcasted_iota(jnp.int32, (pb, pb), 1)

    @pl.when(gq == 0)
    def _():
        perm = (ci == (ri % SUBLANES) * seg + ri // SUBLANES).astype(BF16)
        for k in range(nblk):
            uk = u_ref[k * pb:(k + 1) * pb, :].astype(BF16)
            up_ref[k] = jnp.dot(perm, uk, preferred_element_type=F32).astype(BF16)

    x_ref[...] = jnp.dot(up_ref[...].reshape(m, LANES), jnp.concatenate(bcats, axis=1),
                         preferred_element_type=F32).reshape(nblk, pb, gs * LANES)

    def scan_group(gi):
        abr, abi = abar[gi]
        gl = slice(gi * LANES, (gi + 1) * LANES)
        x = jnp.zeros((nseg, LANES), F32)
        for s in range(seg):
            sl = slice(s * SUBLANES, (s + 1) * SUBLANES)
            x = cmul(x, abr, abi) + x_ref[:, sl, gl].reshape(nseg, LANES)
            x_ref[:, sl, gl] = x.reshape(nblk, SUBLANES, LANES)
        if has_state:
            carry = s0_ref[gi]
        else:
            spb = seq_len // seg
            pr, pi = abr, abi
            for _ in range(int(math.log2(seg))):
                pr, pi = pr * pr - pi * pi, 2.0 * pr * pi
            rown = lax.broadcasted_iota(jnp.int32, (nseg, LANES), 0) % spb
            inc = x
            sh = 1
            while sh < spb:
                inc = inc + jnp.where(rown >= sh, cmul(pltpu.roll(inc, sh, 0), pr, pi), 0.0)
                pr, pi = pr * pr - pi * pi, 2.0 * pr * pi
                sh *= 2
            carry = jnp.where(rown >= 1, pltpu.roll(inc, 1, 0), 0.0)
        pr, pi = abr, abi
        for s in range(seg):
            sl = slice(s * SUBLANES, (s + 1) * SUBLANES)
            x_ref[:, sl, gl] = x_ref[:, sl, gl] + cmul(carry, pr, pi).reshape(nblk, SUBLANES, LANES)
            pr, pi = pr * abr - pi * abi, pr * abi + pi * abr
        finals = []
        for bi in range(nb):
            last_seg = (bi + 1) * (seq_len // seg) - 1
            row = (seg - 1) * SUBLANES + last_seg % SUBLANES
            finals.append(x_ref[last_seg // SUBLANES, row:row + 1, gl])
        st_ref[gi] = jnp.concatenate(finals, axis=0)

    for gi in range(gs):
        scan_group(gi)

    yg = _mm(x_ref[...].reshape(m, gs * LANES), cm_ref[...].reshape(gs * LANES, LANES))

    @pl.when(gq == 0)
    def _():
        y_ref[...] = yg

    @pl.when(gq > 0)
    def _():
        y_ref[...] = y_ref[...] + yg

    @pl.when(gq == pl.num_programs(1) - 1)
    def _():
        unperm = (ri == (ci % SUBLANES) * seg + ci // SUBLANES).astype(BF16)
        for k in range(nblk):
            rows = slice(k * pb, (k + 1) * pb)
            yk = y_ref[rows, :]
            hi = yk.astype(BF16)
            lo = (yk - hi.astype(F32)).astype(BF16)
            y = (jnp.dot(unperm, hi, preferred_element_type=F32) + jnp.dot(unperm, lo, preferred_element_type=F32)
                 + d_ref[...] * u_ref[rows, :])
            z = 0.5 * y * (1.0 + jnp.tanh(math.sqrt(2.0 / math.pi) * (y + 0.044715 * (y * y * y))))
            z_ref[rows, :] = z.astype(BF16)


def s5_scan(u, prm, seq_len, seg, s0=None):
    m, d = u.shape
    a1, a2, dtb, b1, b2, cm, dsk = prm
    groups = a1.shape[0]
    per_tile = LANES // S5_CH
    nb = m // seq_len
    has_state = s0 is not None
    gs = S5_GROUPS_PER_STEP
    steps = per_tile // gs
    gidx = lambda j, q: (j * steps + q, 0, 0)
    vspec = pl.BlockSpec((gs, 1, LANES), gidx)
    mspec = pl.BlockSpec((gs, LANES, LANES), gidx)
    in_specs = [pl.BlockSpec((m, LANES), lambda j, q: (0, j)), vspec, vspec, vspec, mspec, mspec, mspec,
                pl.BlockSpec((1, LANES), lambda j, q: (0, j))]
    args = [u, a1, a2, dtb, b1, b2, cm, dsk]
    if has_state:
        in_specs.append(pl.BlockSpec((gs, nb, LANES), gidx))
        args.append(s0)
    body = functools.partial(_s5_body, seg=seg, seq_len=seq_len, has_state=has_state)
    return pl.pallas_call(
        body, grid=(d // LANES, steps), in_specs=in_specs,
        out_specs=[pl.BlockSpec((m, LANES), lambda j, q: (0, j)),
                   pl.BlockSpec((gs, nb, LANES), gidx)],
        out_shape=[jax.ShapeDtypeStruct((m, d), BF16), jax.ShapeDtypeStruct((groups, nb, LANES), F32)],
        scratch_shapes=[pltpu.VMEM((m // (SUBLANES * seg), SUBLANES * seg, gs * LANES), F32),
                        pltpu.VMEM((m, LANES), F32),
                        pltpu.VMEM((m // (SUBLANES * seg), SUBLANES * seg, LANES), BF16)],
        compiler_params=_cp("arbitrary", "arbitrary"), name="s5_scan")(*args)


def _s5_params(a_re, a_im, log_dt, b_re, b_im, c_re, c_im, d_skip):
    groups, p = a_re.shape
    per_tile = LANES // S5_CH
    dup = lambda a: jnp.concatenate([a, a], axis=-1)[:, None, :]
    a1, a2 = dup(a_re), dup(a_im)
    dtb = jnp.broadcast_to(log_dt[:, None, None], (groups, 1, LANES))
    slot = jax.nn.one_hot(jnp.arange(groups) % per_tile, per_tile, dtype=F32)

    def rows_in_tile(w):
        return (slot[:, :, None, None] * w[:, None]).reshape(groups, LANES, w.shape[-1])

    bre_t, bim_t = b_re.transpose(0, 2, 1), b_im.transpose(0, 2, 1)
    b1 = rows_in_tile(jnp.concatenate([bre_t, bim_t], axis=-1))
    b2 = rows_in_tile(jnp.concatenate([-bim_t, bre_t], axis=-1))
    cmat = jnp.concatenate([c_re, -c_im], axis=-1)
    cm = rows_in_tile(cmat).transpose(0, 2, 1)
    return a1, a2, dtb, b1, b2, cm, d_skip.reshape(1, -1)


def _tiles(m):
    if m >= 1024:
        return 1024, 1024
    return m, m


def _gla_layer(h, b, t, gain, w_in, w_tail, w_alpha_pad, b_alpha, head_norm, w_out, s0, heads, dk, dv):
    tm, tmo = _tiles(h.shape[0])
    n_main = 2 * heads * dk + 2 * heads * dv
    proj = norm_proj(h, gain, w_in, n_main, tm, 1024)
    lr = norm_proj(h, gain, w_tail, LANES, tm, LANES)
    c = math.gcd(t, CHUNK)
    tb = math.gcd(t, 128)
    og, st = recurrence("gla", proj.reshape(b, t, n_main), heads, dk, dv, head_norm, tb, c,
                        (lr.reshape(b, t, LANES), w_alpha_pad, b_alpha.reshape(1, -1)), s0, hb=4)
    return out_proj(og.reshape(b * t, heads * dv), w_out, h, tmo, 512), st


def _hgrn_layer(h, b, t, gain, w_in, lower_bound, layer, head_norm, w_out, s0, heads, dk):
    tm, tmo = _tiles(h.shape[0])
    n = 4 * heads * dk
    proj = norm_proj(h, gain, w_in, n, tm, 1024)
    c = math.gcd(t, CHUNK)
    tb = math.gcd(t, 128)
    og, st = recurrence("hgrn", proj.reshape(b, t, n), heads, dk, dk, head_norm, tb, c,
                        (lower_bound,), s0, layer=layer, hb=16)
    return out_proj(og.reshape(b * t, heads * dk), w_out, h, tmo, 512), st


def _rope_tables(start, t):
    half = NSA_DH // 2
    inv = ROPE_THETA ** (-jnp.arange(half, dtype=F32) / half)
    ang = (start + jnp.arange(t, dtype=jnp.int32)).astype(F32)[:, None] * inv[None, :]
    cos, sin = jnp.cos(ang), jnp.sin(ang)
    return jnp.concatenate([cos, cos], axis=-1), jnp.concatenate([-sin, sin], axis=-1)


def _nsa_layer(h, b, t, start, gain, w_in, w_gates, pool_k, pool_v, w_out, past, heads, kv):
    tm, tmo = _tiles(h.shape[0])
    qw, kvw = heads * NSA_DH, kv * NSA_DH
    n_main = qw + 6 * kvw
    proj = norm_proj(h, gain, w_in, n_main, tm, 1024)
    gates = norm_proj(h, gain, w_gates, LANES, tm, LANES)
    cos, sin = _rope_tables(start, t)
    pk = jnp.broadcast_to(pool_k[:, None], (NSA_BLOCK, kvw))
    pv = jnp.broadcast_to(pool_v[:, None], (NSA_BLOCK, kvw))
    proj3 = proj.reshape(b, t, n_main)
    col = lambda i: proj3[:, :, qw + i * kvw:qw + (i + 1) * kvw]
    kc, vc, vs, vw = col(0), col(1), col(3), col(5)

    if past is None:
        trope = min(256, t)
        q_rot, ks, kw = nsa_rope(proj, cos, sin, trope, t // trope, qw, kvw)
        q_rot3, ks3, kw3 = q_rot.reshape(b, t, qw), ks.reshape(b, t, kvw), kw.reshape(b, t, kvw)
        n_cb = t // NSA_BLOCK
        n_blk = -(-t // NSA_BLOCK)
        kcmp, vcmp = nsa_pool_prompt(proj3, pk, pv, qw, kvw, LANES)
        o_cmp, sel = nsa_cmp(proj3, kcmp, vcmp, min(t, 512), 0, n_cb, n_blk, LANES, kvw)
        tq = min(t, 128)
        o_sel, o_win = nsa_attend_prompt(q_rot3, ks3, vs, kw3, vw, sel, tq, min(t, 512))
        keep = min(NSA_WINDOW, t)
        win_k, win_v = kw3[:, t - keep:], vw[:, t - keep:]
    else:
        pool_ck, pool_cv, pool_sk, pool_sv, page_table, prev_kw, prev_vw = past
        n_pages = page_table.shape[1]
        page = pool_ck.shape[1]
        past_len = n_pages * page
        cos_r, sin_r = jnp.tile(cos, (b, 1)), jnp.tile(sin, (b, 1))
        q_rot, ks, kw = nsa_rope(proj, cos_r, sin_r, b * t, 1, qw, kvw)
        q_rot3, ks3, kw3 = q_rot.reshape(b, t, qw), ks.reshape(b, t, kvw), kw.reshape(b, t, kvw)
        pk4 = jnp.broadcast_to(pool_k[:, None, None], (NSA_BLOCK, kv, NSA_DH))
        pv4 = jnp.broadcast_to(pool_v[:, None, None], (NSA_BLOCK, kv, NSA_DH))
        kcmp, vcmp = nsa_pool_pages(pool_ck, pool_cv, page_table, pk4, pv4, 8)
        total = past_len + t
        n_cb = total // NSA_BLOCK
        n_blk = -(-total // NSA_BLOCK)
        nb_pad = -(-n_blk // LANES) * LANES
        o_cmp, sel = nsa_cmp(proj3, kcmp, vcmp, t, past_len, n_cb, n_blk, nb_pad, kvw)
        padp = lambda a: jnp.concatenate([a, jnp.zeros((b, page - t, kvw), F32)], axis=1)
        o_sel = nsa_paged_sel(q_rot3, sel, pool_sk, pool_sv, page_table, padp(ks3), padp(vs), 8)
        keep = prev_kw.shape[1]
        kw_ext = jnp.concatenate([prev_kw.reshape(b, keep, kvw), kw3], axis=1)
        vw_ext = jnp.concatenate([prev_vw.reshape(b, keep, kvw), vw], axis=1)
        t_ext = keep + t
        t_pad = -(-t_ext // LANES) * LANES
        pade = lambda a: jnp.concatenate([a, jnp.zeros((b, t_pad - t_ext, kvw), F32)], axis=1)
        (o_win,) = nsa_attend(q_rot3, t, LANES, past_len, past_len - keep, win_args=(pade(kw_ext), pade(vw_ext)))
        win_k, win_v = kw_ext[:, t_ext - keep:], vw_ext[:, t_ext - keep:]

    a = nsa_combine(o_cmp.reshape(b * t, qw), o_sel.reshape(b * t, qw), o_win.reshape(b * t, qw), gates,
                    min(b * t, 512))
    y = out_proj(a, w_out, h, tmo, 512)
    shp = lambda x, n: x.reshape(b, n, kv, NSA_DH)
    return y, (shp(kc, t), shp(vc, t), shp(ks3, t), shp(vs, t), shp(win_k, keep), shp(win_v, keep))


def _s5_layer(h, b, t, gain, prm, w_glu, s_re, s_im):
    tm, tmo = _tiles(h.shape[0])
    u = rmsnorm_rows(h, gain, min(h.shape[0], 512))
    groups = prm[0].shape[0]
    if s_re is None:
        z, st = s5_scan(u, prm, t, math.gcd(t, 32))
    else:
        s0 = jnp.concatenate([s_re, s_im], axis=-1).transpose(1, 0, 2)
        z, st = s5_scan(u, prm, t, t, s0)
    y = out_glu(z, w_glu, h, tmo, 512)
    st = st.transpose(1, 0, 2)
    return y, (st[..., :S5_STATE], st[..., S5_STATE:])


def _ffn_layer(h, b, t, gain, w_in, conv_w, conv_b, w_out, buf):
    tm, tmo = _tiles(h.shape[0])
    ff2 = w_in.shape[1]
    if buf is None:
        tm = min(1024, t)
        act, tg, tv = ffn_in(h, gain, w_in, conv_w, conv_b, tm, 512, t)
        per = t // tm
        last = lambda a: a[per - 1::per, SUBLANES - (CONV_W - 1):, :]
        state = jnp.concatenate([last(tg), last(tv)], axis=-1)
    else:
        zrow = jnp.zeros((b, t - 1, ff2), F32)
        p1 = jnp.concatenate([buf[:, 1:2], zrow], axis=1).reshape(b * t, ff2)
        p2 = jnp.concatenate([buf, zrow[:, 1:]], axis=1).reshape(b * t, ff2)
        act, tg, tv = ffn_in(h, gain, w_in, conv_w, conv_b, b * t, 512, t, hist=(p1, p2))
        up = jnp.concatenate([tg[0], tv[0]], axis=-1).reshape(b, t, ff2)
        state = jnp.concatenate([buf, up], axis=1)[:, t:]
    return out_proj(act, w_out, h, min(h.shape[0], 1024), 512), state


def kernel(x_prompt, x_sample, state_gla, state_hgrn, cache_nsa_cmp_k, cache_nsa_cmp_v, cache_nsa_sel_k, cache_nsa_sel_v, cache_nsa_win_k, cache_nsa_win_v, state_s5_re, state_s5_im, state_ffn_conv, page_table, norm_mix, norm_ffn, final_norm, gla_w_in, gla_w_alpha, gla_b_alpha, gla_head_norm, gla_w_out, hgrn_w_in, hgrn_lower_bound, hgrn_head_norm, hgrn_w_out, nsa_w_in, nsa_pool_k, nsa_pool_v, nsa_w_out, s5_a_re, s5_a_im, s5_log_dt, s5_b_re, s5_b_im, s5_c_re, s5_c_im, s5_d, s5_w_glu, ffn_w_in, ffn_conv_w, ffn_conv_b, ffn_w_out):
    bp, tp, d = x_prompt.shape
    bs, ts, _ = x_sample.shape
    depth = norm_mix.shape[0]
    n_mixers = 4
    gla_heads, gla_dk, gla_dv = state_gla.shape[2], state_gla.shape[3], state_gla.shape[4]
    hgrn_heads, hgrn_dk = state_hgrn.shape[2], state_hgrn.shape[3]
    nsa_kv = cache_nsa_cmp_k.shape[3]
    nsa_heads = d // NSA_DH
    hp = x_prompt.reshape(bp * tp, d)
    hs = x_sample.reshape(bs * ts, d)
    bf = lambda w: w.astype(BF16)

    def pad_cols(w, n):
        return jnp.concatenate([w, jnp.zeros((w.shape[0], n - w.shape[1]), w.dtype)], axis=1)

    outs = {k: [] for k in ("gla_p", "gla_s", "hgrn_p", "hgrn_s", "nsa_p", "nsa_s", "s5_p", "s5_s", "conv_p", "conv_s")}
    for i in range(depth):
        kind, j = i % n_mixers, i // n_mixers
        if kind == 0:
            n_main = 2 * gla_heads * gla_dk + 2 * gla_heads * gla_dv
            w_in = cast_bf16(gla_w_in, j)
            w_tail = bf(pad_cols(gla_w_in[j, :, n_main:], LANES))
            rank = gla_w_alpha.shape[1]
            wa = bf(jnp.concatenate([gla_w_alpha[j], jnp.zeros((LANES - rank, gla_w_alpha.shape[2]), F32)], axis=0))
            common = (norm_mix[i], w_in, w_tail, wa, gla_b_alpha[j], gla_head_norm[j], cast_bf16(gla_w_out, j))
            hp, st_p = _gla_layer(hp, bp, tp, *common, None, gla_heads, gla_dk, gla_dv)
            hs, st_s = _gla_layer(hs, bs, ts, *common, state_gla[j], gla_heads, gla_dk, gla_dv)
            outs["gla_p"].append(st_p)
            outs["gla_s"].append(st_s)
        elif kind == 1:
            common = (norm_mix[i], cast_bf16(hgrn_w_in, j), hgrn_lower_bound, i, hgrn_head_norm[j],
                      cast_bf16(hgrn_w_out, j))
            hp, st_p = _hgrn_layer(hp, bp, tp, *common, None, hgrn_heads, hgrn_dk)
            hs, st_s = _hgrn_layer(hs, bs, ts, *common, state_hgrn[j], hgrn_heads, hgrn_dk)
            outs["hgrn_p"].append(st_p)
            outs["hgrn_s"].append(st_s)
        elif kind == 2:
            n_main = nsa_heads * NSA_DH + 6 * nsa_kv * NSA_DH
            w_in = cast_bf16(nsa_w_in, j)
            w_gates = bf(pad_cols(nsa_w_in[j, :, n_main:], LANES))
            common = (norm_mix[i], w_in, w_gates, nsa_pool_k[j], nsa_pool_v[j], cast_bf16(nsa_w_out, j))
            hp, st_p = _nsa_layer(hp, bp, tp, 0, *common, None, nsa_heads, nsa_kv)
            past = (cache_nsa_cmp_k[j], cache_nsa_cmp_v[j], cache_nsa_sel_k[j], cache_nsa_sel_v[j],
                    page_table, cache_nsa_win_k[j], cache_nsa_win_v[j])
            hs, st_s = _nsa_layer(hs, bs, ts, page_table.shape[1] * cache_nsa_cmp_k.shape[2], *common, past,
                                  nsa_heads, nsa_kv)
            outs["nsa_p"].append(st_p)
            outs["nsa_s"].append(st_s)
        else:
            prm = _s5_params(s5_a_re[j], s5_a_im[j], s5_log_dt[j], s5_b_re[j], s5_b_im[j], s5_c_re[j],
                             s5_c_im[j], s5_d[j])
            w_glu = cast_bf16(s5_w_glu, j)
            hp, st_p = _s5_layer(hp, bp, tp, norm_mix[i], prm, w_glu, None, None)
            hs, st_s = _s5_layer(hs, bs, ts, norm_mix[i], prm, w_glu, state_s5_re[j], state_s5_im[j])
            outs["s5_p"].append(st_p)
            outs["s5_s"].append(st_s)
        fw = (norm_ffn[i], cast_bf16(ffn_w_in, i), ffn_conv_w[i], ffn_conv_b[i], cast_bf16(ffn_w_out, i))
        hp, cb_p = _ffn_layer(hp, bp, tp, *fw, None)
        hs, cb_s = _ffn_layer(hs, bs, ts, *fw, state_ffn_conv[i])
        outs["conv_p"].append(cb_p)
        outs["conv_s"].append(cb_s)

    y_prompt = rmsnorm_rows(hp, final_norm, min(hp.shape[0], 512)).reshape(bp, tp, d)
    y_sample = rmsnorm_rows(hs, final_norm, min(hs.shape[0], 512)).reshape(bs, ts, d)
    stack = lambda xs: jnp.stack(xs)
    pick = lambda key, r: stack([e[r] for e in outs[key]])
    res = [y_prompt, y_sample, stack(outs["gla_p"]), stack(outs["gla_s"]), stack(outs["hgrn_p"]), stack(outs["hgrn_s"])]
    for r in range(6):
        res += [pick("nsa_p", r), pick("nsa_s", r)]
    for r in range(2):
        res += [pick("s5_p", r), pick("s5_s", r)]
    res += [stack(outs["conv_p"]), stack(outs["conv_s"])]
    return tuple(res)
```

```python
import functools
import math

import jax
import jax.numpy as jnp
from jax import lax
from jax.experimental import pallas as pl
from jax.experimental.pallas import tpu as pltpu

F32 = jnp.float32
BF16 = jnp.bfloat16
HIGHEST = lax.Precision.HIGHEST

RMS_EPS = 1e-6
ROPE_THETA = 10000.0
NEG = -1e30
CHUNK = 128
SUBCHUNK = 8
GLA_TEMP = 16.0
NSA_BLOCK = 64
NSA_TOP_N = 16
NSA_WINDOW = 512
NSA_GROUP = 4
NSA_DH = 128
S5_CH = 16
S5_STATE = 64
S5_GROUPS_PER_STEP = 2
CONV_W = 3
LANES = 128
SUBLANES = 8
VMEM_LIMIT = 48 * 1024 * 1024


def _cp(*sem):
    return pltpu.CompilerParams(dimension_semantics=sem, vmem_limit_bytes=VMEM_LIMIT)


def _mm(a, b):
    return jnp.dot(a.astype(BF16), b.astype(BF16), preferred_element_type=F32)


def _mm_nt(a, b):
    return lax.dot_general(a.astype(BF16), b.astype(BF16), (((1,), (1,)), ((), ())),
                           preferred_element_type=F32)


def _sigmoid(x):
    return 1.0 / (1.0 + jnp.exp(-x))


def _rms(x, g):
    return x * lax.rsqrt(jnp.mean(x * x, axis=-1, keepdims=True) + RMS_EPS) * g


CAST_BLOCK_BYTES = 4 * 1024 * 1024


def _cast_body(w_ref, o_ref):
    o_ref[...] = w_ref[0].astype(BF16)


def cast_bf16(w_stack, layer):
    _, k, n = w_stack.shape
    tk = 16
    while k % (2 * tk) == 0 and 2 * tk * n * 4 <= CAST_BLOCK_BYTES:
        tk *= 2
    return pl.pallas_call(
        _cast_body, grid=(k // tk,),
        in_specs=[pl.BlockSpec((1, tk, n), lambda i: (layer, i, 0))],
        out_specs=pl.BlockSpec((tk, n), lambda i: (i, 0)),
        out_shape=jax.ShapeDtypeStruct((k, n), BF16),
        compiler_params=_cp("arbitrary"), name="cast_bf16")(w_stack)


def _norm_body(x_ref, g_ref, o_ref):
    o_ref[...] = _rms(x_ref[...], g_ref[...])


def rmsnorm_rows(x, gain, tm):
    m, d = x.shape
    return pl.pallas_call(
        _norm_body, grid=(m // tm,),
        in_specs=[pl.BlockSpec((tm, d), lambda i: (i, 0)), pl.BlockSpec((1, d), lambda i: (0, 0))],
        out_specs=pl.BlockSpec((tm, d), lambda i: (i, 0)),
        out_shape=jax.ShapeDtypeStruct((m, d), F32),
        compiler_params=_cp("arbitrary"), name="rmsnorm")(x, gain.reshape(1, d))


def _proj_body(x_ref, g_ref, w_ref, o_ref, xn_ref):
    @pl.when(pl.program_id(1) == 0)
    def _():
        xn_ref[...] = _rms(x_ref[...], g_ref[...]).astype(BF16)
    o_ref[...] = jnp.dot(xn_ref[...], w_ref[...], preferred_element_type=F32)


def norm_proj(x, gain, w, n_out, tm, tn):
    m, d = x.shape
    return pl.pallas_call(
        _proj_body, grid=(m // tm, n_out // tn),
        in_specs=[pl.BlockSpec((tm, d), lambda i, j: (i, 0)),
                  pl.BlockSpec((1, d), lambda i, j: (0, 0)),
                  pl.BlockSpec((d, tn), lambda i, j: (0, j))],
        out_specs=pl.BlockSpec((tm, tn), lambda i, j: (i, j)),
        out_shape=jax.ShapeDtypeStruct((m, n_out), F32),
        scratch_shapes=[pltpu.VMEM((tm, d), BF16)],
        compiler_params=_cp("arbitrary", "arbitrary"), name="norm_proj")(x, gain.reshape(1, d), w)


def _out_body(a_ref, w_ref, r_ref, o_ref):
    o_ref[...] = r_ref[...] + jnp.dot(a_ref[...].astype(BF16), w_ref[...], preferred_element_type=F32)


def out_proj(a, w, res, tm, tn):
    m, k = a.shape
    n = w.shape[1]
    return pl.pallas_call(
        _out_body, grid=(m // tm, n // tn),
        in_specs=[pl.BlockSpec((tm, k), lambda i, j: (i, 0)),
                  pl.BlockSpec((k, tn), lambda i, j: (0, j)),
                  pl.BlockSpec((tm, tn), lambda i, j: (i, j))],
        out_specs=pl.BlockSpec((tm, tn), lambda i, j: (i, j)),
        out_shape=jax.ShapeDtypeStruct((m, n), F32),
        compiler_params=_cp("arbitrary", "arbitrary"), name="out_proj")(a, w, res)


def _out_glu_body(a_ref, w1_ref, w2_ref, r_ref, o_ref):
    a = a_ref[...]
    g1 = jnp.dot(a, w1_ref[...], preferred_element_type=F32)
    g2 = jnp.dot(a, w2_ref[...], preferred_element_type=F32)
    o_ref[...] = r_ref[...] + g1 * _sigmoid(g2)


def out_glu(a, w, res, tm, tn):
    m, k = a.shape
    n = w.shape[1] // 2
    nj = n // tn
    return pl.pallas_call(
        _out_glu_body, grid=(m // tm, nj),
        in_specs=[pl.BlockSpec((tm, k), lambda i, j: (i, 0)),
                  pl.BlockSpec((k, tn), lambda i, j: (0, j)),
                  pl.BlockSpec((k, tn), lambda i, j: (0, nj + j)),
                  pl.BlockSpec((tm, tn), lambda i, j: (i, j))],
        out_specs=pl.BlockSpec((tm, tn), lambda i, j: (i, j)),
        out_shape=jax.ShapeDtypeStruct((m, n), F32),
        compiler_params=_cp("arbitrary", "arbitrary"), name="out_glu")(a, w, w, res)


def _ffn_in_body(*refs, seg, tiles_per_seq, tail_rows, has_state):
    if has_state:
        (x_ref, g_ref, wg_ref, wv_ref, cwg_ref, cwv_ref, cbg_ref, cbv_ref,
         p1g_ref, p2g_ref, p1v_ref, p2v_ref,
         act_ref, tg_ref, tv_ref, xn_ref, carry_ref) = refs
    else:
        (x_ref, g_ref, wg_ref, wv_ref, cwg_ref, cwv_ref, cbg_ref, cbv_ref,
         act_ref, tg_ref, tv_ref, xn_ref, carry_ref) = refs
    i = pl.program_id(0)
    f = pl.program_id(1)
    tm = x_ref.shape[0]

    @pl.when(f == 0)
    def _():
        xn_ref[...] = _rms(x_ref[...], g_ref[...]).astype(BF16)

    xn = xn_ref[...]
    tf = wg_ref.shape[1]
    sb = tf
    row = lax.broadcasted_iota(jnp.int32, (tm, sb), 0)
    rowm = row % seg
    fresh = (i % tiles_per_seq) == 0

    def conv(w_ref, cw_ref, cb_ref, kind, p1_ref, p2_ref, cs, t_ref):
        u = jnp.dot(xn, w_ref[:, cs], preferred_element_type=F32)
        if has_state:
            p1 = p1_ref[:, cs]
            p2 = p2_ref[:, cs]
        else:
            prev = carry_ref[kind, f, :, cs]
            prev = jnp.where(fresh, 0.0, prev)
            prev0 = prev[SUBLANES - 2:SUBLANES - 1, :]
            prev1 = prev[SUBLANES - 1:SUBLANES, :]
            p1 = jnp.broadcast_to(prev1, u.shape)
            p2 = jnp.where(row == 0, prev0, prev1)
            carry_ref[kind, f, :, cs] = u[tm - SUBLANES:, :]
        u1 = jnp.where(rowm < 1, p1, pltpu.roll(u, 1, 0))
        u2 = jnp.where(rowm < 2, p2, pltpu.roll(u, 2, 0))
        cw = cw_ref[:, cs]
        t_ref[0, :, cs] = u[tm - tail_rows:, :]
        return cw[0:1, :] * u2 + cw[1:2, :] * u1 + cw[2:3, :] * u + cb_ref[:, cs]

    for jb in range(tf // sb):
        cs = slice(jb * sb, (jb + 1) * sb)
        mg = conv(wg_ref, cwg_ref, cbg_ref, 0, p1g_ref if has_state else None, p2g_ref if has_state else None,
                  cs, tg_ref)
        mv = conv(wv_ref, cwv_ref, cbv_ref, 1, p1v_ref if has_state else None, p2v_ref if has_state else None,
                  cs, tv_ref)
        act_ref[:, cs] = (mg * _sigmoid(mg) * mv).astype(BF16)


def ffn_in(x, gain, w_in, conv_w, conv_b, tm, tf, seq_len, hist=None):
    m, d = x.shape
    ff = w_in.shape[1] // 2
    nf = ff // tf
    nb = m // tm
    has_state = hist is not None
    if has_state:
        seg, tiles_per_seq, tail_rows = seq_len, 1, tm
    else:
        seg, tiles_per_seq, tail_rows = tm, seq_len // tm, SUBLANES
    wspec_g = pl.BlockSpec((d, tf), lambda i, f: (0, f))
    wspec_v = pl.BlockSpec((d, tf), lambda i, f: (0, nf + f))
    cspec_g = lambda r: pl.BlockSpec((r, tf), lambda i, f: (0, f))
    cspec_v = lambda r: pl.BlockSpec((r, tf), lambda i, f: (0, nf + f))
    in_specs = [pl.BlockSpec((tm, d), lambda i, f: (i, 0)), pl.BlockSpec((1, d), lambda i, f: (0, 0)),
                wspec_g, wspec_v, cspec_g(CONV_W), cspec_v(CONV_W), cspec_g(1), cspec_v(1)]
    args = [x, gain.reshape(1, d), w_in, w_in, conv_w, conv_w, conv_b.reshape(1, -1), conv_b.reshape(1, -1)]
    if has_state:
        p1, p2 = hist
        in_specs += [pl.BlockSpec((tm, tf), lambda i, f: (i, f)), pl.BlockSpec((tm, tf), lambda i, f: (i, f)),
                     pl.BlockSpec((tm, tf), lambda i, f: (i, nf + f)), pl.BlockSpec((tm, tf), lambda i, f: (i, nf + f))]
        args += [p1, p2, p1, p2]
    body = functools.partial(_ffn_in_body, seg=seg, tiles_per_seq=tiles_per_seq,
                             tail_rows=tail_rows, has_state=has_state)
    return pl.pallas_call(
        body, grid=(nb, nf), in_specs=in_specs,
        out_specs=[pl.BlockSpec((tm, tf), lambda i, f: (i, f)),
                   pl.BlockSpec((1, tail_rows, tf), lambda i, f: (i, 0, f)),
                   pl.BlockSpec((1, tail_rows, tf), lambda i, f: (i, 0, f))],
        out_shape=[jax.ShapeDtypeStruct((m, ff), BF16),
                   jax.ShapeDtypeStruct((nb, tail_rows, ff), F32),
                   jax.ShapeDtypeStruct((nb, tail_rows, ff), F32)],
        scratch_shapes=[pltpu.VMEM((tm, d), BF16), pltpu.VMEM((2, nf, SUBLANES, tf), F32)],
        compiler_params=_cp("arbitrary", "arbitrary"), name="ffn_in")(*args)


def _pad_rows(a, rows):
    if a.shape[0] == rows:
        return a
    return jnp.concatenate([a, jnp.zeros((rows - a.shape[0], a.shape[1]), a.dtype)], axis=0)


def _roll_in_tiles(x, d):
    c, n = x.shape
    return pltpu.roll(x.reshape(c // SUBLANES, SUBLANES, n), d, 1).reshape(c, n)


def _glr_chunk(q, k, v, g, st, c, sub):
    dk = q.shape[1]
    row = lax.broadcasted_iota(jnp.int32, (c, LANES), 0)
    col = lax.broadcasted_iota(jnp.int32, (c, LANES), 1)
    trow = lax.broadcasted_iota(jnp.int32, (c, c), 0)
    tcol = lax.broadcasted_iota(jnp.int32, (c, c), 1)
    tri = (trow >= tcol).astype(F32)
    cum = jnp.dot(tri, g, preferred_element_type=F32, precision=HIGHEST)
    last = cum[c - 1:c, :]
    inter = _mm_nt(q * jnp.exp(cum), st)

    rowk = lax.broadcasted_iota(jnp.int32, (c, dk), 0)
    rm = rowk % sub
    ones = jnp.ones((dk, LANES), BF16)
    att = jnp.zeros((c, LANES), F32)
    for d in range(sub):
        if d == 0:
            p = q * k
        else:
            ks = _roll_in_tiles(k, d)
            cs = _roll_in_tiles(cum, d)
            p = q * ks * jnp.exp(jnp.where(rm >= d, cum - cs, NEG))
        a = jnp.dot(p.astype(BF16), ones, preferred_element_type=F32)
        att = att + jnp.where(col == row - d, a, 0.0)
    if c > sub:
        blocks = [jnp.zeros((sub, LANES), F32)]
        for i in range(1, c // sub):
            cs = cum[i * sub - 1:i * sub, :]
            qi = q[i * sub:(i + 1) * sub, :] * jnp.exp(cum[i * sub:(i + 1) * sub, :] - cs)
            kj = k * jnp.exp(jnp.where(rowk < i * sub, cs - cum, NEG))
            blocks.append(_mm_nt(qi, _pad_rows(kj, LANES)))
        att = att + jnp.concatenate(blocks, axis=0)
    vpad = _pad_rows(v, LANES)
    intra = _mm(att, vpad)
    kd = _pad_rows(k * jnp.exp(last - cum), LANES)
    st_new = st * jnp.exp(last) + _mm(vpad.T, kd)
    return inter + intra, st_new


def _rec_body(*refs, mode, c, sub, n_chunks, dk, dv, hb, layer, has_state):
    refs = list(refs)
    if mode == "gla":
        q_ref, k_ref, v_ref, r_ref, lr_ref, wa_ref, ba_ref, hn_ref = refs[:8]
        rest = refs[8:]
    else:
        q_ref, k_ref, v_ref, r_ref, lb_ref, hn_ref = refs[:6]
        rest = refs[6:]
    if has_state:
        s0_ref, og_ref, sout_ref, st_ref = rest
    else:
        og_ref, sout_ref, st_ref = rest
    tstep = pl.program_id(2)

    @pl.when(tstep == 0)
    def _():
        for hh in range(hb):
            if has_state:
                st_ref[hh] = s0_ref[0, hh].T
            else:
                st_ref[hh] = jnp.zeros(st_ref.shape[1:], F32)

    if mode == "hgrn":
        lbx = lb_ref[...]
        e = jnp.exp(lbx - jnp.max(lbx, axis=0, keepdims=True))
        sm = e / jnp.sum(e, axis=0, keepdims=True)
        lb_all = jnp.zeros((1, hb * dk), F32)
        for li in range(1, layer + 1):
            lb_all = lb_all + sm[li:li + 1, :]

    for ci in range(n_chunks):
        sl = slice(ci * c, (ci + 1) * c)
        for hh in range(hb):
            hk = slice(hh * dk, (hh + 1) * dk)
            hv = slice(hh * dv, (hh + 1) * dv)
            if mode == "gla":
                q = q_ref[0, sl, hk] * (dk ** -0.5)
                k = k_ref[0, sl, hk]
                z = _mm(lr_ref[0, sl, :], wa_ref[:, hk]) + ba_ref[:, hk]
                g = -(jnp.maximum(-z, 0.0) + jnp.log1p(jnp.exp(-jnp.abs(z)))) / GLA_TEMP
            else:
                qz = q_ref[0, sl, hk]
                q = qz * _sigmoid(qz)
                lbv = lb_all[:, hk]
                fg = lbv + (1.0 - lbv) * _sigmoid(k_ref[0, sl, hk])
                k = 1.0 - fg
                g = jnp.log(fg)
            v = v_ref[0, sl, hv]
            o, st_new = _glr_chunk(q, k, v, g, st_ref[hh], c, sub)
            st_ref[hh] = st_new
            of = o * lax.rsqrt(jnp.mean(o * o, axis=-1, keepdims=True) + RMS_EPS) * hn_ref[...]
            gate = r_ref[0, sl, hv]
            og_ref[0, sl, hv] = (of * (gate * _sigmoid(gate))).astype(BF16)

    @pl.when(tstep == pl.num_programs(2) - 1)
    def _():
        for hh in range(hb):
            sout_ref[0, hh] = st_ref[hh].T


def recurrence(mode, proj, heads, dk, dv, hn, tb, c, extra, s0=None, layer=0, hb=1):
    b, t, _ = proj.shape
    sub = min(SUBCHUNK, c)
    has_state = s0 is not None
    wk, wv = hb * dk, hb * dv
    nh = heads // hb
    if mode == "gla":
        lr, wa, ba = extra
        koff, voff = nh, (2 * heads * dk) // wv
        roff = voff + nh
        in_specs = [pl.BlockSpec((1, tb, wk), lambda i, h, s: (i, s, h)),
                    pl.BlockSpec((1, tb, wk), lambda i, h, s: (i, s, koff + h)),
                    pl.BlockSpec((1, tb, wv), lambda i, h, s: (i, s, voff + h)),
                    pl.BlockSpec((1, tb, wv), lambda i, h, s: (i, s, roff + h)),
                    pl.BlockSpec((1, tb, LANES), lambda i, h, s: (i, s, 0)),
                    pl.BlockSpec((LANES, wk), lambda i, h, s: (0, h)),
                    pl.BlockSpec((1, wk), lambda i, h, s: (0, h)),
                    pl.BlockSpec((1, dv), lambda i, h, s: (0, 0))]
        args = [proj, proj, proj, proj, lr, wa, ba, hn.reshape(1, dv)]
    else:
        (lb,) = extra
        in_specs = [pl.BlockSpec((1, tb, wk), lambda i, h, s: (i, s, h)),
                    pl.BlockSpec((1, tb, wk), lambda i, h, s: (i, s, nh + h)),
                    pl.BlockSpec((1, tb, wv), lambda i, h, s: (i, s, 2 * nh + h)),
                    pl.BlockSpec((1, tb, wv), lambda i, h, s: (i, s, 3 * nh + h)),
                    pl.BlockSpec((lb.shape[0], wk), lambda i, h, s: (0, h)),
                    pl.BlockSpec((1, dv), lambda i, h, s: (0, 0))]
        args = [proj, proj, proj, proj, lb, hn.reshape(1, dv)]
    if has_state:
        in_specs.append(pl.BlockSpec((1, hb, dk, dv), lambda i, h, s: (i, h, 0, 0)))
        args.append(s0)
    body = functools.partial(_rec_body, mode=mode, c=c, sub=sub, n_chunks=tb // c, dk=dk, dv=dv, hb=hb,
                             layer=layer, has_state=has_state)
    return pl.pallas_call(
        body, grid=(b, nh, t // tb), in_specs=in_specs,
        out_specs=[pl.BlockSpec((1, tb, wv), lambda i, h, s: (i, s, h)),
                   pl.BlockSpec((1, hb, dk, dv), lambda i, h, s: (i, h, 0, 0))],
        out_shape=[jax.ShapeDtypeStruct((b, t, heads * dv), BF16),
                   jax.ShapeDtypeStruct((b, heads, dk, dv), F32)],
        scratch_shapes=[pltpu.VMEM((hb, dv, dk), F32)],
        compiler_params=_cp("arbitrary", "arbitrary", "arbitrary"), name="recurrence_" + mode)(*args)


def _rope_body(q_ref, ks_ref, kw_ref, cos_ref, sin_ref, qo_ref, kso_ref, kwo_ref):
    cos = cos_ref[...]
    sin = sin_ref[...]

    def rot(src, dst):
        for h in range(src.shape[1] // LANES):
            x = src[:, h * LANES:(h + 1) * LANES]
            dst[:, h * LANES:(h + 1) * LANES] = x * cos + pltpu.roll(x, LANES // 2, 1) * sin

    rot(q_ref, qo_ref)
    rot(ks_ref, kso_ref)
    rot(kw_ref, kwo_ref)


def nsa_rope(proj, cos, sin, tm, tiles_per_seq, qw, kvw):
    m = proj.shape[0]
    ks_blk = (qw + 2 * kvw) // kvw
    kw_blk = (qw + 4 * kvw) // kvw
    return pl.pallas_call(
        _rope_body, grid=(m // tm,),
        in_specs=[pl.BlockSpec((tm, qw), lambda i: (i, 0)),
                  pl.BlockSpec((tm, kvw), lambda i: (i, ks_blk)),
                  pl.BlockSpec((tm, kvw), lambda i: (i, kw_blk)),
                  pl.BlockSpec((tm, LANES), lambda i: (i % tiles_per_seq, 0)),
                  pl.BlockSpec((tm, LANES), lambda i: (i % tiles_per_seq, 0))],
        out_specs=[pl.BlockSpec((tm, qw), lambda i: (i, 0)),
                   pl.BlockSpec((tm, kvw), lambda i: (i, 0)),
                   pl.BlockSpec((tm, kvw), lambda i: (i, 0))],
        out_shape=[jax.ShapeDtypeStruct((m, qw), F32), jax.ShapeDtypeStruct((m, kvw), F32),
                   jax.ShapeDtypeStruct((m, kvw), F32)],
        compiler_params=_cp("arbitrary"), name="nsa_rope")(proj, proj, proj, cos, sin)


def _pool_rows(x, pw):
    n = x.shape[0] // NSA_BLOCK
    return jnp.sum(x.reshape(n, NSA_BLOCK, x.shape[1]) * pw[None], axis=1)


def _pool_body(kc_ref, vc_ref, pk_ref, pv_ref, ko_ref, vo_ref, *, n_cb):
    ko_ref[...] = jnp.zeros(ko_ref.shape, F32)
    vo_ref[...] = jnp.zeros(vo_ref.shape, F32)
    kp = _pool_rows(kc_ref[0, 0:n_cb * NSA_BLOCK, :], pk_ref[...])
    vp = _pool_rows(vc_ref[0, 0:n_cb * NSA_BLOCK, :], pv_ref[...])
    for g in range(ko_ref.shape[1]):
        ko_ref[0, g, 0:n_cb, :] = kp[:, g * LANES:(g + 1) * LANES]
        vo_ref[0, g, 0:n_cb, :] = vp[:, g * LANES:(g + 1) * LANES]


def nsa_pool_prompt(proj3, pk, pv, qw, kvw, n_pad):
    b, t, _ = proj3.shape
    n_cb = t // NSA_BLOCK
    kc_blk = qw // kvw
    return pl.pallas_call(
        functools.partial(_pool_body, n_cb=n_cb), grid=(b,),
        in_specs=[pl.BlockSpec((1, t, kvw), lambda i: (i, 0, kc_blk)),
                  pl.BlockSpec((1, t, kvw), lambda i: (i, 0, kc_blk + 1)),
                  pl.BlockSpec((NSA_BLOCK, kvw), lambda i: (0, 0)),
                  pl.BlockSpec((NSA_BLOCK, kvw), lambda i: (0, 0))],
        out_specs=[pl.BlockSpec((1, kvw // LANES, n_pad, LANES), lambda i: (i, 0, 0, 0)),
                   pl.BlockSpec((1, kvw // LANES, n_pad, LANES), lambda i: (i, 0, 0, 0))],
        out_shape=[jax.ShapeDtypeStruct((b, kvw // LANES, n_pad, LANES), F32)] * 2,
        compiler_params=_cp("arbitrary"), name="nsa_pool")(proj3, proj3, pk, pv)


def _pool_pages_body(pt_ref, *refs, pg):
    k_refs = refs[:pg]
    v_refs = refs[pg:2 * pg]
    pk_ref, pv_ref, ko_ref, vo_ref = refs[2 * pg:]
    _, page, kv, dh = k_refs[0].shape
    per = page // NSA_BLOCK

    def pooled(x_ref, pw_ref):
        return jnp.sum(x_ref[0].reshape(per, NSA_BLOCK, kv, dh) * pw_ref[...][None], axis=1)

    for i in range(pg):
        kp, vp = pooled(k_refs[i], pk_ref), pooled(v_refs[i], pv_ref)
        for g in range(kv):
            ko_ref[0, g, i * per:(i + 1) * per, :] = kp[:, g, :]
            vo_ref[0, g, i * per:(i + 1) * per, :] = vp[:, g, :]


def nsa_pool_pages(pool_k, pool_v, page_table, pk, pv, pg):
    b, n_pages = page_table.shape
    _, page, kv, dh = pool_k.shape
    per = page // NSA_BLOCK
    page_spec = lambda i: pl.BlockSpec((1, page, kv, dh), lambda bi, s, pt, i=i: (pt[bi, s * pg + i], 0, 0, 0))
    wspec = pl.BlockSpec((NSA_BLOCK, kv, dh), lambda bi, s, pt: (0, 0, 0))
    ospec = pl.BlockSpec((1, kv, pg * per, dh), lambda bi, s, pt: (bi, 0, s, 0))
    gs = pltpu.PrefetchScalarGridSpec(
        num_scalar_prefetch=1, grid=(b, n_pages // pg),
        in_specs=[page_spec(i) for i in range(pg)] + [page_spec(i) for i in range(pg)] + [wspec, wspec],
        out_specs=[ospec, ospec])
    n_blk = n_pages * per
    return pl.pallas_call(
        functools.partial(_pool_pages_body, pg=pg), grid_spec=gs,
        out_shape=[jax.ShapeDtypeStruct((b, kv, n_blk, dh), F32)] * 2,
        compiler_params=_cp("arbitrary", "arbitrary"), name="nsa_pool_pages")(
            page_table, *([pool_k] * pg), *([pool_v] * pg), pk, pv)


def _cmp_body(q_ref, kc_ref, vc_ref, o_ref, sel_ref, *, q_start, n_cb, n_blk, nb_pad):
    tq = q_ref.shape[1]
    kvb, ncp = kc_ref.shape[1], kc_ref.shape[2]
    qt = pl.program_id(2)
    scale = NSA_DH ** -0.5
    colc = lax.broadcasted_iota(jnp.int32, (tq, ncp), 1)
    qposc = q_start + qt * tq + lax.broadcasted_iota(jnp.int32, (tq, ncp), 0)
    valid = ((colc + 1) * NSA_BLOCK - 1 <= qposc) & (colc < n_cb)
    imps = []
    for g in range(kvb):
        kc, vc = kc_ref[0, g], vc_ref[0, g]
        imp = jnp.zeros((tq, ncp), F32)
        for r in range(NSA_GROUP):
            hs = slice((g * NSA_GROUP + r) * LANES, (g * NSA_GROUP + r + 1) * LANES)
            s = jnp.where(valid, _mm_nt(q_ref[0, :, hs], kc) * scale, NEG)
            m = jnp.max(s, axis=-1, keepdims=True)
            e = jnp.where(valid, jnp.exp(s - m), 0.0)
            p = e / jnp.maximum(jnp.sum(e, axis=-1, keepdims=True), 1e-30)
            o_ref[0, :, hs] = _mm(p, vc)
            imp = imp + p
        imps.append(imp)
    imp = jnp.concatenate(imps, axis=0)
    rows = kvb * tq
    if nb_pad > ncp:
        imp = jnp.concatenate([imp, jnp.zeros((rows, nb_pad - ncp), F32)], axis=1)
    blk = lax.broadcasted_iota(jnp.int32, (rows, nb_pad), 1)
    qpos = q_start + qt * tq + lax.broadcasted_iota(jnp.int32, (rows, nb_pad), 0) % tq
    cur = qpos // NSA_BLOCK
    forced = (blk == cur) | (blk == 0)
    score = jnp.where(blk > cur, -1.0, jnp.where(forced, NSA_GROUP + 1.0, imp))
    score = jnp.where(blk < n_blk, score, -2.0)
    blkf = blk.astype(F32)

    def pick(_, carry):
        sc, sel = carry
        mx = jnp.max(sc, axis=-1, keepdims=True)
        first = jnp.min(jnp.where(sc == mx, blkf, 1e9), axis=-1, keepdims=True)
        hit = blkf == first
        return jnp.where(hit, -3.0, sc), jnp.where(hit, 1.0, sel)

    _, sel = lax.fori_loop(0, min(NSA_TOP_N, n_blk), pick, (score, jnp.zeros((rows, nb_pad), F32)))
    for g in range(kvb):
        sel_ref[0, g] = sel[g * tq:(g + 1) * tq, :]


def _cmp_t_body(q_ref, kc_ref, vc_ref, o_ref, sel_ref, *, q_start, n_cb, n_blk, nb_pad):
    tq = q_ref.shape[1]
    ncp = kc_ref.shape[2]
    qt = pl.program_id(2)
    scale = NSA_DH ** -0.5
    nr = min(ncp, -(-n_blk // SUBLANES) * SUBLANES)
    kc = kc_ref[0, 0, 0:nr, :]
    vc = vc_ref[0, 0]
    blk = lax.broadcasted_iota(jnp.int32, (nr, tq), 0)
    qpos = q_start + qt * tq + lax.broadcasted_iota(jnp.int32, (nr, tq), 1)
    valid = ((blk + 1) * NSA_BLOCK - 1 <= qpos) & (blk < n_cb)
    pad = jnp.zeros((ncp - nr, tq), F32)
    imp = jnp.zeros((nr, tq), F32)
    for r in range(NSA_GROUP):
        qr = q_ref[0, :, r * LANES:(r + 1) * LANES]
        s = jnp.where(valid, _mm_nt(kc, qr) * scale, NEG)
        m = jnp.max(s, axis=0, keepdims=True)
        e = jnp.where(valid, jnp.exp(s - m), 0.0)
        p = e / jnp.maximum(jnp.sum(e, axis=0, keepdims=True), 1e-30)
        p_rows = jnp.concatenate([p, pad], axis=0).T if ncp > nr else p.T
        o_ref[0, :, r * LANES:(r + 1) * LANES] = _mm(p_rows, vc)
        imp = imp + p
    cur = qpos // NSA_BLOCK
    forced = (blk == cur) | (blk == 0)
    score = jnp.where(blk > cur, -1.0, jnp.where(forced, NSA_GROUP + 1.0, imp))
    score = jnp.where(blk < n_blk, score, -2.0)
    blkf = blk.astype(F32)

    def pick(_, carry):
        sc, sel = carry
        mx = jnp.max(sc, axis=0, keepdims=True)
        first = jnp.min(jnp.where(sc == mx, blkf, 1e9), axis=0, keepdims=True)
        hit = blkf == first
        return jnp.where(hit, -3.0, sc), jnp.where(hit, 1.0, sel)

    _, sel = lax.fori_loop(0, min(NSA_TOP_N, n_blk), pick, (score, jnp.zeros((nr, tq), F32)))
    if nb_pad > nr:
        sel = jnp.concatenate([sel, jnp.zeros((nb_pad - nr, tq), F32)], axis=0)
    sel_ref[0, 0] = sel.T


def nsa_cmp(q3, kcmp, vcmp, tq, q_start, n_cb, n_blk, nb_pad, kvw):
    b, t = q3.shape[0], q3.shape[1]
    ncp = kcmp.shape[2]
    kv = kvw // LANES
    gw = NSA_GROUP * LANES
    transposed = tq % LANES == 0 and nb_pad % LANES == 0 and ncp % LANES == 0
    body = functools.partial(_cmp_t_body if transposed else _cmp_body,
                             q_start=q_start, n_cb=n_cb, n_blk=n_blk, nb_pad=nb_pad)
    kvb = 1 if transposed else kv
    return pl.pallas_call(
        body, grid=(b, kv // kvb, t // tq),
        in_specs=[pl.BlockSpec((1, tq, kvb * gw), lambda i, g, s: (i, s, g)),
                  pl.BlockSpec((1, kvb, ncp, LANES), lambda i, g, s: (i, g, 0, 0)),
                  pl.BlockSpec((1, kvb, ncp, LANES), lambda i, g, s: (i, g, 0, 0))],
        out_specs=[pl.BlockSpec((1, tq, kvb * gw), lambda i, g, s: (i, s, g)),
                   pl.BlockSpec((1, kvb, tq, nb_pad), lambda i, g, s: (i, g, s, 0))],
        out_shape=[jax.ShapeDtypeStruct((b, t, kv * gw), F32),
                   jax.ShapeDtypeStruct((b, kv, t, nb_pad), F32)],
        compiler_params=_cp("arbitrary", "arbitrary", "arbitrary"), name="nsa_cmp")(q3, kcmp, vcmp)


def _softmax_step(carry, s, valid, vv):
    m, l, acc = carry
    s = jnp.where(valid, s, NEG)
    m_new = jnp.maximum(m, jnp.max(s, axis=-1, keepdims=True))
    p = jnp.where(valid, jnp.exp(s - m_new), 0.0)
    alpha = jnp.exp(m - m_new)
    l = alpha * l + jnp.sum(p, axis=-1, keepdims=True)
    acc = alpha * acc + _mm(p, vv)
    return m_new, l, acc


def _softmax_init(rows):
    return (jnp.full((rows, 1), NEG, F32), jnp.zeros((rows, 1), F32), jnp.zeros((rows, LANES), F32))


def _attn_body(*refs, tq, tk, n_kt, q_start, k_start, do_sel, do_win):
    refs = list(refs)
    q_ref = refs.pop(0)
    if do_sel:
        ks_ref, vs_ref, sel_ref = refs[:3]
        refs = refs[3:]
    if do_win:
        kw_ref, vw_ref = refs[:2]
        refs = refs[2:]
    outs = refs
    qt = pl.program_id(2)
    scale = NSA_DH ** -0.5
    rows = NSA_GROUP * tq
    q4 = jnp.concatenate([q_ref[0, :, r * LANES:(r + 1) * LANES] for r in range(NSA_GROUP)], axis=0).astype(BF16)
    q0 = q_start + qt * tq
    qpos = q0 + lax.broadcasted_iota(jnp.int32, (rows, tk), 0) % tq
    kcol = lax.broadcasted_iota(jnp.int32, (rows, tk), 1)

    def finish(carry, o_ref):
        m, l, acc = carry
        o = acc / jnp.maximum(l, 1e-30)
        for r in range(NSA_GROUP):
            o_ref[0, :, r * LANES:(r + 1) * LANES] = o[r * tq:(r + 1) * tq, :]

    if do_sel:
        selb = sel_ref[0, 0].astype(BF16)
        nbp = selb.shape[1]
        en = lax.broadcasted_iota(jnp.int32, (nbp, tk), 0)
        es = lax.broadcasted_iota(jnp.int32, (nbp, tk), 1)

        def sel_step(kt, carry):
            off = pl.multiple_of(kt * tk, tk)
            kk = ks_ref[0, pl.ds(off, tk), :]
            vv = vs_ref[0, pl.ds(off, tk), :]
            s = _mm_nt(q4, kk) * scale
            kp0 = k_start + kt * tk
            expand = ((kp0 + es) // NSA_BLOCK == en).astype(BF16)
            chosen = jnp.dot(selb, expand, preferred_element_type=F32)
            chosen = jnp.concatenate([chosen] * NSA_GROUP, axis=0)
            valid = (chosen > 0.5) & (kp0 + kcol <= qpos)
            return _softmax_step(carry, s, valid, vv)

        hi = jnp.minimum(n_kt, (q0 + tq - 1 - k_start) // tk + 1)
        finish(lax.fori_loop(0, hi, sel_step, _softmax_init(rows)), outs.pop(0))

    if do_win:
        def win_step(kt, carry):
            off = pl.multiple_of(kt * tk, tk)
            kk = kw_ref[0, pl.ds(off, tk), :]
            vv = vw_ref[0, pl.ds(off, tk), :]
            s = _mm_nt(q4, kk) * scale
            kpos = k_start + kt * tk + kcol
            dist = qpos - kpos
            valid = (dist >= 0) & (dist < NSA_WINDOW) & (kpos >= 0)
            return _softmax_step(carry, s, valid, vv)

        lo = jnp.maximum(0, (q0 - (NSA_WINDOW - 1) - k_start) // tk)
        hi = jnp.minimum(n_kt, (q0 + tq - 1 - k_start) // tk + 1)
        finish(lax.fori_loop(lo, hi, win_step, _softmax_init(rows)), outs.pop(0))


def nsa_attend(q_rot3, tq, tk, q_start, k_start, sel_args=None, win_args=None):
    b, t, qw = q_rot3.shape
    kv = qw // (NSA_GROUP * LANES)
    gw = NSA_GROUP * LANES
    in_specs = [pl.BlockSpec((1, tq, gw), lambda i, g, s: (i, s, g))]
    args = [q_rot3]
    n_out = 0
    t_k = None
    if sel_args is not None:
        k, v, sel = sel_args
        t_k = k.shape[1]
        nbp = sel.shape[-1]
        in_specs += [pl.BlockSpec((1, t_k, LANES), lambda i, g, s: (i, 0, g)),
                     pl.BlockSpec((1, t_k, LANES), lambda i, g, s: (i, 0, g)),
                     pl.BlockSpec((1, 1, tq, nbp), lambda i, g, s: (i, g, s, 0))]
        args += [k, v, sel]
        n_out += 1
    if win_args is not None:
        k, v = win_args
        t_k = k.shape[1]
        in_specs += [pl.BlockSpec((1, t_k, LANES), lambda i, g, s: (i, 0, g)),
                     pl.BlockSpec((1, t_k, LANES), lambda i, g, s: (i, 0, g))]
        args += [k, v]
        n_out += 1
    body = functools.partial(_attn_body, tq=tq, tk=tk, n_kt=t_k // tk, q_start=q_start, k_start=k_start,
                             do_sel=sel_args is not None, do_win=win_args is not None)
    return pl.pallas_call(
        body, grid=(b, kv, t // tq), in_specs=in_specs,
        out_specs=[pl.BlockSpec((1, tq, gw), lambda i, g, s: (i, s, g))] * n_out,
        out_shape=[jax.ShapeDtypeStruct((b, t, qw), F32)] * n_out,
        compiler_params=_cp("arbitrary", "arbitrary", "arbitrary"), name="nsa_attend")(*args)


def _attn_t_body(q_ref, ks_ref, vs_ref, kw_ref, vw_ref, sel_ref, osel_ref, owin_ref, vst_ref, vwt_ref, selt_ref,
                 *, tq, tk, n_kt):
    qt = pl.program_id(2)
    scale = NSA_DH ** -0.5
    t_k = ks_ref.shape[1]

    @pl.when(qt == 0)
    def _():
        for j in range(t_k // LANES):
            sl = slice(j * LANES, (j + 1) * LANES)
            vst_ref[:, sl] = vs_ref[0, sl, :].T.astype(BF16)
            vwt_ref[:, sl] = vw_ref[0, sl, :].T.astype(BF16)

    q0 = qt * tq
    cols = NSA_GROUP * tq
    selt_ref[...] = sel_ref[0, 0].T
    q4 = jnp.concatenate([q_ref[0, :, r * LANES:(r + 1) * LANES] for r in range(NSA_GROUP)], axis=0).astype(BF16)

    def scores(k_ref, off, n, allowed):
        bias = jnp.where(allowed, 0.0, NEG)
        return _mm_nt(k_ref[0, pl.ds(off, n), :], q4) * scale + jnp.concatenate([bias] * NSA_GROUP, axis=1)

    def write(o_ref, acc, l):
        o = acc / jnp.maximum(l, 1e-30)
        for r in range(NSA_GROUP):
            o_ref[0, :, r * LANES:(r + 1) * LANES] = o[:, r * tq:(r + 1) * tq].T

    kpos_l = lax.broadcasted_iota(jnp.int32, (tk, tq), 0)
    qpos = q0 + lax.broadcasted_iota(jnp.int32, (tk, tq), 1)

    def sel_step(kt, carry):
        m, l, acc = carry
        off = pl.multiple_of(kt * tk, tk)
        bpc = tk // NSA_BLOCK
        chosen = jnp.concatenate(
            [jnp.broadcast_to(selt_ref[pl.ds(kt * bpc + j, 1), :], (NSA_BLOCK, tq)) for j in range(bpc)], axis=0)
        s = scores(ks_ref, off, tk, (chosen > 0.5) & (kt * tk + kpos_l <= qpos))
        m_new = jnp.maximum(m, jnp.max(s, axis=0, keepdims=True))
        p = jnp.exp(s - m_new)
        alpha = jnp.exp(m - m_new)
        l = alpha * l + jnp.sum(p, axis=0, keepdims=True)
        acc = alpha * acc + jnp.dot(vst_ref[:, pl.ds(off, tk)], p.astype(BF16), preferred_element_type=F32)
        return m_new, l, acc

    init = (jnp.full((1, cols), NEG, F32), jnp.zeros((1, cols), F32), jnp.zeros((NSA_DH, cols), F32))
    m, l, acc = lax.fori_loop(0, jnp.minimum(n_kt, (q0 + tq - 1) // tk + 1), sel_step, init)
    write(osel_ref, acc, l)

    wk = min(t_k, NSA_WINDOW + tq)
    ws = pl.multiple_of(jnp.clip(q0 - NSA_WINDOW, 0, t_k - wk), LANES)
    dist = (q0 + lax.broadcasted_iota(jnp.int32, (wk, tq), 1)) - (ws + lax.broadcasted_iota(jnp.int32, (wk, tq), 0))
    s = scores(kw_ref, ws, wk, (dist >= 0) & (dist < NSA_WINDOW))
    p = jnp.exp(s - jnp.max(s, axis=0, keepdims=True))
    acc = jnp.dot(vwt_ref[:, pl.ds(ws, wk)], p.astype(BF16), preferred_element_type=F32)
    write(owin_ref, acc, jnp.sum(p, axis=0, keepdims=True))


def nsa_attend_prompt(q_rot3, ks, vs, kw, vw, sel, tq, tk):
    b, t, qw = q_rot3.shape
    kv = qw // (NSA_GROUP * LANES)
    gw = NSA_GROUP * LANES
    kspec = pl.BlockSpec((1, t, LANES), lambda i, g, s: (i, 0, g))
    ospec = pl.BlockSpec((1, tq, gw), lambda i, g, s: (i, s, g))
    body = functools.partial(_attn_t_body, tq=tq, tk=tk, n_kt=t // tk)
    return pl.pallas_call(
        body, grid=(b, kv, t // tq),
        in_specs=[ospec, kspec, kspec, kspec, kspec,
                  pl.BlockSpec((1, 1, tq, sel.shape[-1]), lambda i, g, s: (i, g, s, 0))],
        out_specs=[ospec, ospec],
        out_shape=[jax.ShapeDtypeStruct((b, t, qw), F32)] * 2,
        scratch_shapes=[pltpu.VMEM((NSA_DH, t), BF16), pltpu.VMEM((NSA_DH, t), BF16),
                        pltpu.VMEM((sel.shape[-1], tq), F32)],
        compiler_params=_cp("arbitrary", "arbitrary", "arbitrary"), name="nsa_attend_prompt")(
            q_rot3, ks, vs, kw, vw, sel)


def _paged_sel_body(pt_ref, q_ref, sel_ref, kn_ref, vn_ref, k_hbm, v_hbm, o_ref,
                    kbuf, vbuf, sem, m_ref, l_ref, acc_ref, *, pg, past_len, t_new):
    bi, step = pl.program_id(0), pl.program_id(1)
    nsteps = pl.num_programs(1)
    total = pl.num_programs(0) * nsteps
    lin = bi * nsteps + step
    _, page, kv, _ = k_hbm.shape
    rows = q_ref.shape[1]
    per = rows // kv
    scale = NSA_DH ** -0.5
    nbp = sel_ref.shape[-1]
    qb = q_ref[0].astype(BF16)
    selb = sel_ref[0].astype(BF16)

    def page_copies(b_, s_, slot):
        out = []
        for i in range(pg):
            phys = pt_ref[b_, s_ * pg + i]
            for g in range(kv):
                dst = pl.ds(i * page, page)
                out.append(pltpu.make_async_copy(k_hbm.at[phys, :, g, :], kbuf.at[slot, g, dst, :], sem.at[slot, 0]))
                out.append(pltpu.make_async_copy(v_hbm.at[phys, :, g, :], vbuf.at[slot, g, dst, :], sem.at[slot, 1]))
        return out

    @pl.when(lin == 0)
    def _():
        for c in page_copies(0, 0, 0):
            c.start()

    @pl.when(lin + 1 < total)
    def _():
        nxt = lin + 1
        for c in page_copies(nxt // nsteps, nxt % nsteps, nxt % 2):
            c.start()

    @pl.when(step == 0)
    def _():
        m_ref[...] = jnp.full(m_ref.shape, NEG, F32)
        l_ref[...] = jnp.zeros(l_ref.shape, F32)
        acc_ref[...] = jnp.zeros(acc_ref.shape, F32)

    def update(key_of, val_of, n, kp0):
        tloc = lax.broadcasted_iota(jnp.int32, (rows, n), 0) % t_new
        kcol = lax.broadcasted_iota(jnp.int32, (rows, n), 1)
        en = lax.broadcasted_iota(jnp.int32, (nbp, n), 0)
        es = lax.broadcasted_iota(jnp.int32, (nbp, n), 1)
        s = jnp.concatenate([_mm_nt(qb[g * per:(g + 1) * per], key_of(g)) for g in range(kv)], axis=0) * scale
        expand = ((kp0 + es) // NSA_BLOCK == en).astype(BF16)
        chosen = jnp.dot(selb, expand, preferred_element_type=F32)
        valid = (chosen > 0.5) & (kp0 + kcol <= past_len + tloc)
        s = jnp.where(valid, s, NEG)
        m = m_ref[:, 0:1]
        m_new = jnp.maximum(m, jnp.max(s, axis=-1, keepdims=True))
        p = jnp.where(valid, jnp.exp(s - m_new), 0.0)
        alpha = jnp.exp(m - m_new)
        pv = jnp.concatenate([_mm(p[g * per:(g + 1) * per], val_of(g)) for g in range(kv)], axis=0)
        m_ref[...] = jnp.broadcast_to(m_new, m_ref.shape)
        l_ref[...] = jnp.broadcast_to(alpha * l_ref[:, 0:1] + jnp.sum(p, axis=-1, keepdims=True), l_ref.shape)
        acc_ref[...] = alpha * acc_ref[...] + pv

    slot = lin % 2
    for c in page_copies(bi, step, slot):
        c.wait()
    update(lambda g: kbuf[slot, g], lambda g: vbuf[slot, g], pg * page, step * (pg * page))

    @pl.when(step == nsteps - 1)
    def _():
        update(lambda g: kn_ref[0, :, g * LANES:(g + 1) * LANES],
               lambda g: vn_ref[0, :, g * LANES:(g + 1) * LANES], kn_ref.shape[1], past_len)
        o_ref[0] = acc_ref[...] / jnp.maximum(l_ref[:, 0:1], 1e-30)


def nsa_paged_sel(q_rot3, sel, pool_k, pool_v, page_table, k_new, v_new, pg):
    b, t_new, qw = q_rot3.shape
    n_pages = page_table.shape[1]
    _, page, kv, dh = pool_k.shape
    kvw = kv * dh
    nbp = sel.shape[-1]
    per = NSA_GROUP * t_new
    rows = kv * per
    q_rows = q_rot3.reshape(b, t_new, kv, NSA_GROUP, dh).transpose(0, 2, 3, 1, 4).reshape(b, rows, dh)
    sel_rows = jnp.broadcast_to(sel[:, :, None], (b, kv, NSA_GROUP, t_new, nbp)).reshape(b, rows, nbp)
    gs = pltpu.PrefetchScalarGridSpec(
        num_scalar_prefetch=1, grid=(b, n_pages // pg),
        in_specs=[pl.BlockSpec((1, rows, dh), lambda bi, s, pt: (bi, 0, 0)),
                  pl.BlockSpec((1, rows, nbp), lambda bi, s, pt: (bi, 0, 0)),
                  pl.BlockSpec((1, page, kvw), lambda bi, s, pt: (bi, 0, 0)),
                  pl.BlockSpec((1, page, kvw), lambda bi, s, pt: (bi, 0, 0)),
                  pl.BlockSpec(memory_space=pl.ANY), pl.BlockSpec(memory_space=pl.ANY)],
        out_specs=pl.BlockSpec((1, rows, dh), lambda bi, s, pt: (bi, 0, 0)),
        scratch_shapes=[pltpu.VMEM((2, kv, pg * page, dh), F32), pltpu.VMEM((2, kv, pg * page, dh), F32),
                        pltpu.SemaphoreType.DMA((2, 2)),
                        pltpu.VMEM((rows, LANES), F32), pltpu.VMEM((rows, LANES), F32),
                        pltpu.VMEM((rows, dh), F32)])
    body = functools.partial(_paged_sel_body, pg=pg, past_len=n_pages * page, t_new=t_new)
    o_rows = pl.pallas_call(
        body, grid_spec=gs, out_shape=jax.ShapeDtypeStruct((b, rows, dh), F32),
        compiler_params=_cp("arbitrary", "arbitrary"), name="nsa_paged_sel")(
            page_table, q_rows, sel_rows, k_new, v_new, pool_k, pool_v)
    return o_rows.reshape(b, kv, NSA_GROUP, t_new, NSA_DH).transpose(0, 3, 1, 2, 4).reshape(b, t_new, qw)


def _combine_body(oc_ref, os_ref, ow_ref, gt_ref, a_ref):
    gs = _sigmoid(gt_ref[...])
    for hh in range(oc_ref.shape[1] // LANES):
        sl = slice(hh * LANES, (hh + 1) * LANES)
        a = (gs[:, 3 * hh:3 * hh + 1] * oc_ref[:, sl] + gs[:, 3 * hh + 1:3 * hh + 2] * os_ref[:, sl]
             + gs[:, 3 * hh + 2:3 * hh + 3] * ow_ref[:, sl])
        a_ref[:, sl] = a.astype(BF16)


def nsa_combine(o_cmp, o_sel, o_win, gates, tm):
    m, qw = o_cmp.shape
    spec = pl.BlockSpec((tm, qw), lambda i: (i, 0))
    return pl.pallas_call(
        _combine_body, grid=(m // tm,),
        in_specs=[spec, spec, spec, pl.BlockSpec((tm, LANES), lambda i: (i, 0))],
        out_specs=spec, out_shape=jax.ShapeDtypeStruct((m, qw), BF16),
        compiler_params=_cp("arbitrary"), name="nsa_combine")(o_cmp, o_sel, o_win, gates)


def _s5_body(*refs, seg, seq_len, has_state):
    if has_state:
        (u_ref, a1_ref, a2_ref, dt_ref, b1_ref, b2_ref, cm_ref, d_ref, s0_ref,
         z_ref, st_ref, x_ref, y_ref, up_ref) = refs
    else:
        (u_ref, a1_ref, a2_ref, dt_ref, b1_ref, b2_ref, cm_ref, d_ref,
         z_ref, st_ref, x_ref, y_ref, up_ref) = refs
    gq = pl.program_id(1)
    m = u_ref.shape[0]
    nseg = m // seg
    nb = m // seq_len
    half = LANES // 2
    lane = lax.broadcasted_iota(jnp.int32, (1, LANES), 1)
    sgn = jnp.where(lane < half, -1.0, 1.0)

    gs = a1_ref.shape[0]
    abar, bcats = [], []
    for gi in range(gs):
        are, aim, dt = a1_ref[gi], a2_ref[gi], jnp.exp(dt_ref[gi])
        er = jnp.exp(are * dt)
        abr, abi = er * jnp.cos(aim * dt), er * jnp.sin(aim * dt)
        nr, ni, den = abr - 1.0, abi, are * are + aim * aim
        cr, cim = (nr * are + ni * aim) / den, (ni * are - nr * aim) / den
        abar.append((abr, abi))
        bcats.append((cr * b1_ref[gi] + cim * b2_ref[gi]).astype(BF16))

    def cmul(x, pr, pi):
        return x * pr + pltpu.roll(x, half, 1) * (pi * sgn)

    pb = SUBLANES * seg
    nblk = m // pb
    ri = lax.broadcasted_iota(jnp.int32, (pb, pb), 0)
    ci = lax.broadcasted_iota(jnp.int32, (pb, pb), 1)

    @pl.when(gq == 0)
    def _():
        perm = (ci == (ri % SUBLANES) * seg + ri // SUBLANES).astype(BF16)
        for k in range(nblk):
            uk = u_ref[k * pb:(k + 1) * pb, :].astype(BF16)
            up_ref[k] = jnp.dot(perm, uk, preferred_element_type=F32).astype(BF16)

    x_ref[...] = jnp.dot(up_ref[...].reshape(m, LANES), jnp.concatenate(bcats, axis=1),
                         preferred_element_type=F32).reshape(nblk, pb, gs * LANES)

    def scan_group(gi):
        abr, abi = abar[gi]
        gl = slice(gi * LANES, (gi + 1) * LANES)
        x = jnp.zeros((nseg, LANES), F32)
        for s in range(seg):
            sl = slice(s * SUBLANES, (s + 1) * SUBLANES)
            x = cmul(x, abr, abi) + x_ref[:, sl, gl].reshape(nseg, LANES)
            x_ref[:, sl, gl] = x.reshape(nblk, SUBLANES, LANES)
        if has_state:
            carry = s0_ref[gi]
        else:
            spb = seq_len // seg
            pr, pi = abr, abi
            for _ in range(int(math.log2(seg))):
                pr, pi = pr * pr - pi * pi, 2.0 * pr * pi
            rown = lax.broadcasted_iota(jnp.int32, (nseg, LANES), 0) % spb
            inc = x
            sh = 1
            while sh < spb:
                inc = inc + jnp.where(rown >= sh, cmul(pltpu.roll(inc, sh, 0), pr, pi), 0.0)
                pr, pi = pr * pr - pi * pi, 2.0 * pr * pi
                sh *= 2
            carry = jnp.where(rown >= 1, pltpu.roll(inc, 1, 0), 0.0)
        pr, pi = abr, abi
        for s in range(seg):
            sl = slice(s * SUBLANES, (s + 1) * SUBLANES)
            x_ref[:, sl, gl] = x_ref[:, sl, gl] + cmul(carry, pr, pi).reshape(nblk, SUBLANES, LANES)
            pr, pi = pr * abr - pi * abi, pr * abi + pi * abr
        finals = []
        for bi in range(nb):
            last_seg = (bi + 1) * (seq_len // seg) - 1
            row = (seg - 1) * SUBLANES + last_seg % SUBLANES
            finals.append(x_ref[last_seg // SUBLANES, row:row + 1, gl])
        st_ref[gi] = jnp.concatenate(finals, axis=0)

    for gi in range(gs):
        scan_group(gi)

    yg = _mm(x_ref[...].reshape(m, gs * LANES), cm_ref[...].reshape(gs * LANES, LANES))

    @pl.when(gq == 0)
    def _():
        y_ref[...] = yg

    @pl.when(gq > 0)
    def _():
        y_ref[...] = y_ref[...] + yg

    @pl.when(gq == pl.num_programs(1) - 1)
    def _():
        unperm = (ri == (ci % SUBLANES) * seg + ci // SUBLANES).astype(BF16)
        for k in range(nblk):
            rows = slice(k * pb, (k + 1) * pb)
            yk = y_ref[rows, :]
            hi = yk.astype(BF16)
            lo = (yk - hi.astype(F32)).astype(BF16)
            y = (jnp.dot(unperm, hi, preferred_element_type=F32) + jnp.dot(unperm, lo, preferred_element_type=F32)
                 + d_ref[...] * u_ref[rows, :])
            z = 0.5 * y * (1.0 + jnp.tanh(math.sqrt(2.0 / math.pi) * (y + 0.044715 * (y * y * y))))
            z_ref[rows, :] = z.astype(BF16)


def s5_scan(u, prm, seq_len, seg, s0=None):
    m, d = u.shape
    a1, a2, dtb, b1, b2, cm, dsk = prm
    groups = a1.shape[0]
    per_tile = LANES // S5_CH
    nb = m // seq_len
    has_state = s0 is not None
    gs = S5_GROUPS_PER_STEP
    steps = per_tile // gs
    gidx = lambda j, q: (j * steps + q, 0, 0)
    vspec = pl.BlockSpec((gs, 1, LANES), gidx)
    mspec = pl.BlockSpec((gs, LANES, LANES), gidx)
    in_specs = [pl.BlockSpec((m, LANES), lambda j, q: (0, j)), vspec, vspec, vspec, mspec, mspec, mspec,
                pl.BlockSpec((1, LANES), lambda j, q: (0, j))]
    args = [u, a1, a2, dtb, b1, b2, cm, dsk]
    if has_state:
        in_specs.append(pl.BlockSpec((gs, nb, LANES), gidx))
        args.append(s0)
    body = functools.partial(_s5_body, seg=seg, seq_len=seq_len, has_state=has_state)
    return pl.pallas_call(
        body, grid=(d // LANES, steps), in_specs=in_specs,
        out_specs=[pl.BlockSpec((m, LANES), lambda j, q: (0, j)),
                   pl.BlockSpec((gs, nb, LANES), gidx)],
        out_shape=[jax.ShapeDtypeStruct((m, d), BF16), jax.ShapeDtypeStruct((groups, nb, LANES), F32)],
        scratch_shapes=[pltpu.VMEM((m // (SUBLANES * seg), SUBLANES * seg, gs * LANES), F32),
                        pltpu.VMEM((m, LANES), F32),
                        pltpu.VMEM((m // (SUBLANES * seg), SUBLANES * seg, LANES), BF16)],
        compiler_params=_cp("arbitrary", "arbitrary"), name="s5_scan")(*args)


def _s5_params(a_re, a_im, log_dt, b_re, b_im, c_re, c_im, d_skip):
    groups, p = a_re.shape
    per_tile = LANES // S5_CH
    dup = lambda a: jnp.concatenate([a, a], axis=-1)[:, None, :]
    a1, a2 = dup(a_re), dup(a_im)
    dtb = jnp.broadcast_to(log_dt[:, None, None], (groups, 1, LANES))
    slot = jax.nn.one_hot(jnp.arange(groups) % per_tile, per_tile, dtype=F32)

    def rows_in_tile(w):
        return (slot[:, :, None, None] * w[:, None]).reshape(groups, LANES, w.shape[-1])

    bre_t, bim_t = b_re.transpose(0, 2, 1), b_im.transpose(0, 2, 1)
    b1 = rows_in_tile(jnp.concatenate([bre_t, bim_t], axis=-1))
    b2 = rows_in_tile(jnp.concatenate([-bim_t, bre_t], axis=-1))
    cmat = jnp.concatenate([c_re, -c_im], axis=-1)
    cm = rows_in_tile(cmat).transpose(0, 2, 1)
    return a1, a2, dtb, b1, b2, cm, d_skip.reshape(1, -1)


def _tiles(m):
    if m >= 1024:
        return 1024, 1024
    return m, m


def _gla_layer(h, b, t, gain, w_in, w_tail, w_alpha_pad, b_alpha, head_norm, w_out, s0, heads, dk, dv):
    tm, tmo = _tiles(h.shape[0])
    n_main = 2 * heads * dk + 2 * heads * dv
    proj = norm_proj(h, gain, w_in, n_main, tm, 1024)
    lr = norm_proj(h, gain, w_tail, LANES, tm, LANES)
    c = math.gcd(t, CHUNK)
    tb = math.gcd(t, 256)
    og, st = recurrence("gla", proj.reshape(b, t, n_main), heads, dk, dv, head_norm, tb, c,
                        (lr.reshape(b, t, LANES), w_alpha_pad, b_alpha.reshape(1, -1)), s0, hb=4)
    return out_proj(og.reshape(b * t, heads * dv), w_out, h, tmo, 512), st


def _hgrn_layer(h, b, t, gain, w_in, lower_bound, layer, head_norm, w_out, s0, heads, dk):
    tm, tmo = _tiles(h.shape[0])
    n = 4 * heads * dk
    proj = norm_proj(h, gain, w_in, n, tm, 1024)
    c = math.gcd(t, CHUNK)
    tb = math.gcd(t, 256)
    og, st = recurrence("hgrn", proj.reshape(b, t, n), heads, dk, dk, head_norm, tb, c,
                        (lower_bound,), s0, layer=layer, hb=16)
    return out_proj(og.reshape(b * t, heads * dk), w_out, h, tmo, 512), st


def _rope_tables(start, t):
    half = NSA_DH // 2
    inv = ROPE_THETA ** (-jnp.arange(half, dtype=F32) / half)
    ang = (start + jnp.arange(t, dtype=jnp.int32)).astype(F32)[:, None] * inv[None, :]
    cos, sin = jnp.cos(ang), jnp.sin(ang)
    return jnp.concatenate([cos, cos], axis=-1), jnp.concatenate([-sin, sin], axis=-1)


def _nsa_layer(h, b, t, start, gain, w_in, w_gates, pool_k, pool_v, w_out, past, heads, kv):
    tm, tmo = _tiles(h.shape[0])
    qw, kvw = heads * NSA_DH, kv * NSA_DH
    n_main = qw + 6 * kvw
    proj = norm_proj(h, gain, w_in, n_main, tm, 1024)
    gates = norm_proj(h, gain, w_gates, LANES, tm, LANES)
    cos, sin = _rope_tables(start, t)
    pk = jnp.broadcast_to(pool_k[:, None], (NSA_BLOCK, kvw))
    pv = jnp.broadcast_to(pool_v[:, None], (NSA_BLOCK, kvw))
    proj3 = proj.reshape(b, t, n_main)
    col = lambda i: proj3[:, :, qw + i * kvw:qw + (i + 1) * kvw]
    kc, vc, vs, vw = col(0), col(1), col(3), col(5)

    if past is None:
        trope = min(256, t)
        q_rot, ks, kw = nsa_rope(proj, cos, sin, trope, t // trope, qw, kvw)
        q_rot3, ks3, kw3 = q_rot.reshape(b, t, qw), ks.reshape(b, t, kvw), kw.reshape(b, t, kvw)
        n_cb = t // NSA_BLOCK
        n_blk = -(-t // NSA_BLOCK)
        kcmp, vcmp = nsa_pool_prompt(proj3, pk, pv, qw, kvw, LANES)
        o_cmp, sel = nsa_cmp(proj3, kcmp, vcmp, min(t, 512), 0, n_cb, n_blk, LANES, kvw)
        tq = min(t, 128)
        o_sel, o_win = nsa_attend_prompt(q_rot3, ks3, vs, kw3, vw, sel, tq, min(t, 512))
        keep = min(NSA_WINDOW, t)
        win_k, win_v = kw3[:, t - keep:], vw[:, t - keep:]
    else:
        pool_ck, pool_cv, pool_sk, pool_sv, page_table, prev_kw, prev_vw = past
        n_pages = page_table.shape[1]
        page = pool_ck.shape[1]
        past_len = n_pages * page
        cos_r, sin_r = jnp.tile(cos, (b, 1)), jnp.tile(sin, (b, 1))
        q_rot, ks, kw = nsa_rope(proj, cos_r, sin_r, b * t, 1, qw, kvw)
        q_rot3, ks3, kw3 = q_rot.reshape(b, t, qw), ks.reshape(b, t, kvw), kw.reshape(b, t, kvw)
        pk4 = jnp.broadcast_to(pool_k[:, None, None], (NSA_BLOCK, kv, NSA_DH))
        pv4 = jnp.broadcast_to(pool_v[:, None, None], (NSA_BLOCK, kv, NSA_DH))
        kcmp, vcmp = nsa_pool_pages(pool_ck, pool_cv, page_table, pk4, pv4, 8)
        total = past_len + t
        n_cb = total // NSA_BLOCK
        n_blk = -(-total // NSA_BLOCK)
        nb_pad = -(-n_blk // LANES) * LANES
        o_cmp, sel = nsa_cmp(proj3, kcmp, vcmp, t, past_len, n_cb, n_blk, nb_pad, kvw)
        padp = lambda a: jnp.concatenate([a, jnp.zeros((b, page - t, kvw), F32)], axis=1)
        o_sel = nsa_paged_sel(q_rot3, sel, pool_sk, pool_sv, page_table, padp(ks3), padp(vs), 8)
        keep = prev_kw.shape[1]
        kw_ext = jnp.concatenate([prev_kw.reshape(b, keep, kvw), kw3], axis=1)
        vw_ext = jnp.concatenate([prev_vw.reshape(b, keep, kvw), vw], axis=1)
        t_ext = keep + t
        t_pad = -(-t_ext // LANES) * LANES
        pade = lambda a: jnp.concatenate([a, jnp.zeros((b, t_pad - t_ext, kvw), F32)], axis=1)
        (o_win,) = nsa_attend(q_rot3, t, LANES, past_len, past_len - keep, win_args=(pade(kw_ext), pade(vw_ext)))
        win_k, win_v = kw_ext[:, t_ext - keep:], vw_ext[:, t_ext - keep:]

    a = nsa_combine(o_cmp.reshape(b * t, qw), o_sel.reshape(b * t, qw), o_win.reshape(b * t, qw), gates,
                    min(b * t, 512))
    y = out_proj(a, w_out, h, tmo, 512)
    shp = lambda x, n: x.reshape(b, n, kv, NSA_DH)
    return y, (shp(kc, t), shp(vc, t), shp(ks3, t), shp(vs, t), shp(win_k, keep), shp(win_v, keep))


def _s5_layer(h, b, t, gain, prm, w_glu, s_re, s_im):
    tm, tmo = _tiles(h.shape[0])
    u = rmsnorm_rows(h, gain, min(h.shape[0], 512))
    groups = prm[0].shape[0]
    if s_re is None:
        z, st = s5_scan(u, prm, t, math.gcd(t, 32))
    else:
        s0 = jnp.concatenate([s_re, s_im], axis=-1).transpose(1, 0, 2)
        z, st = s5_scan(u, prm, t, t, s0)
    y = out_glu(z, w_glu, h, tmo, 512)
    st = st.transpose(1, 0, 2)
    return y, (st[..., :S5_STATE], st[..., S5_STATE:])


def _ffn_layer(h, b, t, gain, w_in, conv_w, conv_b, w_out, buf):
    tm, tmo = _tiles(h.shape[0])
    ff2 = w_in.shape[1]
    if buf is None:
        tm = min(1024, t)
        act, tg, tv = ffn_in(h, gain, w_in, conv_w, conv_b, tm, 512, t)
        per = t // tm
        last = lambda a: a[per - 1::per, SUBLANES - (CONV_W - 1):, :]
        state = jnp.concatenate([last(tg), last(tv)], axis=-1)
    else:
        zrow = jnp.zeros((b, t - 1, ff2), F32)
        p1 = jnp.concatenate([buf[:, 1:2], zrow], axis=1).reshape(b * t, ff2)
        p2 = jnp.concatenate([buf, zrow[:, 1:]], axis=1).reshape(b * t, ff2)
        act, tg, tv = ffn_in(h, gain, w_in, conv_w, conv_b, b * t, 512, t, hist=(p1, p2))
        up = jnp.concatenate([tg[0], tv[0]], axis=-1).reshape(b, t, ff2)
        state = jnp.concatenate([buf, up], axis=1)[:, t:]
    return out_proj(act, w_out, h, min(h.shape[0], 1024), 512), state


def kernel(x_prompt, x_sample, state_gla, state_hgrn, cache_nsa_cmp_k, cache_nsa_cmp_v, cache_nsa_sel_k, cache_nsa_sel_v, cache_nsa_win_k, cache_nsa_win_v, state_s5_re, state_s5_im, state_ffn_conv, page_table, norm_mix, norm_ffn, final_norm, gla_w_in, gla_w_alpha, gla_b_alpha, gla_head_norm, gla_w_out, hgrn_w_in, hgrn_lower_bound, hgrn_head_norm, hgrn_w_out, nsa_w_in, nsa_pool_k, nsa_pool_v, nsa_w_out, s5_a_re, s5_a_im, s5_log_dt, s5_b_re, s5_b_im, s5_c_re, s5_c_im, s5_d, s5_w_glu, ffn_w_in, ffn_conv_w, ffn_conv_b, ffn_w_out):
    bp, tp, d = x_prompt.shape
    bs, ts, _ = x_sample.shape
    depth = norm_mix.shape[0]
    n_mixers = 4
    gla_heads, gla_dk, gla_dv = state_gla.shape[2], state_gla.shape[3], state_gla.shape[4]
    hgrn_heads, hgrn_dk = state_hgrn.shape[2], state_hgrn.shape[3]
    nsa_kv = cache_nsa_cmp_k.shape[3]
    nsa_heads = d // NSA_DH
    hp = x_prompt.reshape(bp * tp, d)
    hs = x_sample.reshape(bs * ts, d)
    bf = lambda w: w.astype(BF16)

    def pad_cols(w, n):
        return jnp.concatenate([w, jnp.zeros((w.shape[0], n - w.shape[1]), w.dtype)], axis=1)

    outs = {k: [] for k in ("gla_p", "gla_s", "hgrn_p", "hgrn_s", "nsa_p", "nsa_s", "s5_p", "s5_s", "conv_p", "conv_s")}
    for i in range(depth):
        kind, j = i % n_mixers, i // n_mixers
        if kind == 0:
            n_main = 2 * gla_heads * gla_dk + 2 * gla_heads * gla_dv
            w_in = cast_bf16(gla_w_in, j)
            w_tail = bf(pad_cols(gla_w_in[j, :, n_main:], LANES))
            rank = gla_w_alpha.shape[1]
            wa = bf(jnp.concatenate([gla_w_alpha[j], jnp.zeros((LANES - rank, gla_w_alpha.shape[2]), F32)], axis=0))
            common = (norm_mix[i], w_in, w_tail, wa, gla_b_alpha[j], gla_head_norm[j], cast_bf16(gla_w_out, j))
            hp, st_p = _gla_layer(hp, bp, tp, *common, None, gla_heads, gla_dk, gla_dv)
            hs, st_s = _gla_layer(hs, bs, ts, *common, state_gla[j], gla_heads, gla_dk, gla_dv)
            outs["gla_p"].append(st_p)
            outs["gla_s"].append(st_s)
        elif kind == 1:
            common = (norm_mix[i], cast_bf16(hgrn_w_in, j), hgrn_lower_bound, i, hgrn_head_norm[j],
                      cast_bf16(hgrn_w_out, j))
            hp, st_p = _hgrn_layer(hp, bp, tp, *common, None, hgrn_heads, hgrn_dk)
            hs, st_s = _hgrn_layer(hs, bs, ts, *common, state_hgrn[j], hgrn_heads, hgrn_dk)
            outs["hgrn_p"].append(st_p)
            outs["hgrn_s"].append(st_s)
        elif kind == 2:
            n_main = nsa_heads * NSA_DH + 6 * nsa_kv * NSA_DH
            w_in = cast_bf16(nsa_w_in, j)
            w_gates = bf(pad_cols(nsa_w_in[j, :, n_main:], LANES))
            common = (norm_mix[i], w_in, w_gates, nsa_pool_k[j], nsa_pool_v[j], cast_bf16(nsa_w_out, j))
            hp, st_p = _nsa_layer(hp, bp, tp, 0, *common, None, nsa_heads, nsa_kv)
            past = (cache_nsa_cmp_k[j], cache_nsa_cmp_v[j], cache_nsa_sel_k[j], cache_nsa_sel_v[j],
                    page_table, cache_nsa_win_k[j], cache_nsa_win_v[j])
            hs, st_s = _nsa_layer(hs, bs, ts, page_table.shape[1] * cache_nsa_cmp_k.shape[2], *common, past,
                                  nsa_heads, nsa_kv)
            outs["nsa_p"].append(st_p)
            outs["nsa_s"].append(st_s)
        else:
            prm = _s5_params(s5_a_re[j], s5_a_im[j], s5_log_dt[j], s5_b_re[j], s5_b_im[j], s5_c_re[j],
                             s5_c_im[j], s5_d[j])
            w_glu = cast_bf16(s5_w_glu, j)
            hp, st_p = _s5_layer(hp, bp, tp, norm_mix[i], prm, w_glu, None, None)
            hs, st_s = _s5_layer(hs, bs, ts, norm_mix[i], prm, w_glu, state_s5_re[j], state_s5_im[j])
            outs["s5_p"].append(st_p)
            outs["s5_s"].append(st_s)
        fw = (norm_ffn[i], cast_bf16(ffn_w_in, i), ffn_conv_w[i], ffn_conv_b[i], cast_bf16(ffn_w_out, i))
        hp, cb_p = _ffn_layer(hp, bp, tp, *fw, None)
        hs, cb_s = _ffn_layer(hs, bs, ts, *fw, state_ffn_conv[i])
        outs["conv_p"].append(cb_p)
        outs["conv_s"].append(cb_s)

    y_prompt = rmsnorm_rows(hp, final_norm, min(hp.shape[0], 512)).reshape(bp, tp, d)
    y_sample = rmsnorm_rows(hs, final_norm, min(hs.shape[0], 512)).reshape(bs, ts, d)
    stack = lambda xs: jnp.stack(xs)
    pick = lambda key, r: stack([e[r] for e in outs[key]])
    res = [y_prompt, y_sample, stack(outs["gla_p"]), stack(outs["gla_s"]), stack(outs["hgrn_p"]), stack(outs["hgrn_s"])]
    for r in range(6):
        res += [pick("nsa_p", r), pick("nsa_s", r)]
    for r in range(2):
        res += [pick("s5_p", r), pick("s5_s", r)]
    res += [stack(outs["conv_p"]), stack(outs["conv_s"])]
    return tuple(res)
```
